```python
import numpy as np
import jax
import jax.numpy as jnp
from jax import lax

D_MODEL = 1024
BATCH = 8
SEQ = 8192
DEPTH = 2
DEC_BATCH = 8
DEC_SEQ = 32
PAST_LEN = 2048

CHUNK = 64
QBLOCK = 128
HEAD_DIM = 64
FOX_HEADS = 4
SB_HEADS = 4
LRU_WIDTH = 256
LRU_BLOCKS = 4
LRU_CONV = 4
LRU_C = 8.0
MLA_HEADS = 4
MLA_Q_RANK = 256
MLA_KV_RANK = 128
MLA_NOPE = 64
MLA_ROPE = 32
MLA_V = 64
ROPE_BASE = 10000.0
N_BRANCH = 4
BRANCH_WIDTH = 256
MEM_LEN = 256
MEM_HEADS = 4
MEM_HEAD_DIM = 128
D_FF = -(-8 * D_MODEL // (3 * 256)) * 256
EPS = 1e-6
NEG_INF = -1e30

_MIX_SIZES = (FOX_HEADS * HEAD_DIM, FOX_HEADS * HEAD_DIM, FOX_HEADS * HEAD_DIM, FOX_HEADS,
              LRU_WIDTH, LRU_WIDTH,
              MLA_Q_RANK, MLA_KV_RANK, MLA_ROPE,
              SB_HEADS * HEAD_DIM, SB_HEADS * HEAD_DIM, SB_HEADS * HEAD_DIM)
D_MIX_IN = sum(_MIX_SIZES)
SPLIT_IDX = tuple(int(i) for i in np.cumsum(_MIX_SIZES)[:-1])
D_IN = D_MIX_IN + N_BRANCH * D_MODEL

kernel_name = 'hybrid_streaming_encoder_step'


def rms_norm(x, g):
    xf = x.astype(jnp.float32)
    y = xf * lax.rsqrt(jnp.mean(xf * xf, axis=-1, keepdims=True) + EPS)
    return (y * g.astype(jnp.float32)).astype(x.dtype)


def sweep_queries(fn, *q_args):
    t = q_args[0].shape[1]
    blk = min(QBLOCK, t)
    n = t // blk
    blocks = tuple(jnp.swapaxes(a.reshape(a.shape[0], n, blk, *a.shape[2:]), 0, 1) for a in q_args)
    out = lax.map(lambda args: fn(*args), blocks)
    out = jnp.swapaxes(out, 0, 1)
    return out.reshape(out.shape[0], t, *out.shape[3:])


def apply_rope(x, pos):
    half = x.shape[-1] // 2
    inv = jnp.power(ROPE_BASE, -jnp.arange(half, dtype=jnp.float32) / half)
    ang = pos.astype(jnp.float32)[:, None, None] * inv
    cos, sin = jnp.cos(ang), jnp.sin(ang)
    xf = x.astype(jnp.float32)
    x1, x2 = xf[..., :half], xf[..., half:]
    return jnp.concatenate([x1 * cos - x2 * sin, x1 * sin + x2 * cos], axis=-1).astype(x.dtype)


def forgetting_attention(q, k, v, logf, pos_q, pos_k):
    t = q.shape[1]
    c = jnp.cumsum(logf.astype(jnp.float32), axis=1)
    c_k = jnp.swapaxes(c, 1, 2)[:, :, None, :]
    c_q = c[:, -t:]
    vf = v.astype(jnp.float32)
    scale = q.shape[-1] ** -0.5

    def block(qb, cqb, qpb):
        s = jnp.einsum('bqhd,bkhd->bhqk', qb, k, preferred_element_type=jnp.float32) * scale
        s = s + jnp.swapaxes(cqb, 1, 2)[..., None] - c_k
        mask = pos_k[None, :] <= qpb[0][:, None]
        p = jax.nn.softmax(jnp.where(mask, s, NEG_INF), axis=-1)
        return jnp.einsum('bhqk,bkhd->bqhd', p, vf).astype(v.dtype)

    return sweep_queries(block, q, c_q, pos_q[None])


def stick_breaking_attention(q, k, v, pos_q, pos_k):
    vf = v.astype(jnp.float32)
    scale = q.shape[-1] ** -0.5

    def block(qb, qpb):
        z = jnp.einsum('bqhd,bkhd->bhqk', qb, k, preferred_element_type=jnp.float32) * scale
        mask = pos_k[None, :] < qpb[0][:, None]
        log_keep = jnp.where(mask, jax.nn.log_sigmoid(-z), 0.0)
        later = lax.cumsum(log_keep, axis=3, reverse=True) - log_keep
        a = jnp.where(mask, jnp.exp(jax.nn.log_sigmoid(z) + later), 0.0)
        return jnp.einsum('bhqk,bkhd->bqhd', a, vf).astype(v.dtype)

    return sweep_queries(block, q, pos_q[None])


def latent_attention(cq, ckv_new, kpe_new, ckv_past, kpe_past, pos_q, pos_k,
                     q_norm, w_uq, kv_norm, w_uk, w_uv):
    b, t, _ = cq.shape
    q = (rms_norm(cq, q_norm) @ w_uq).reshape(b, t, MLA_HEADS, MLA_NOPE + MLA_ROPE)
    q = jnp.concatenate([q[..., :MLA_NOPE], apply_rope(q[..., MLA_NOPE:], pos_q)], axis=-1)
    ckv_n = rms_norm(ckv_new, kv_norm)
    kpe_r = apply_rope(kpe_new[:, :, None, :], pos_q)[:, :, 0, :]
    ckv = jnp.concatenate([ckv_past.astype(ckv_n.dtype), ckv_n], axis=1)
    kpe = jnp.concatenate([kpe_past.astype(kpe_r.dtype), kpe_r], axis=1)
    s_len = ckv.shape[1]
    k_nope = (ckv @ w_uk).reshape(b, s_len, MLA_HEADS, MLA_NOPE)
    vf = (ckv @ w_uv).reshape(b, s_len, MLA_HEADS, MLA_V).astype(jnp.float32)
    k = jnp.concatenate([k_nope, jnp.broadcast_to(kpe[:, :, None, :], (b, s_len, MLA_HEADS, MLA_ROPE)).astype(k_nope.dtype)], axis=-1)
    scale = (MLA_NOPE + MLA_ROPE) ** -0.5
    k_chunk = pos_k // CHUNK

    def block(qb, qcb):
        s = jnp.einsum('bqhd,bkhd->bhqk', qb, k.astype(qb.dtype), preferred_element_type=jnp.float32) * scale
        mask = k_chunk[None, :] <= qcb[0][:, None]
        p = jax.nn.softmax(jnp.where(mask, s, NEG_INF), axis=-1)
        return jnp.einsum('bhqk,bkhd->bqhd', p, vf).astype(cq.dtype)

    out = sweep_queries(block, q, (pos_q // CHUNK)[None])
    return out, ckv_n, kpe_r


def lru_combine(left, right):
    a_l, b_l = left
    a_r, b_r = right
    return a_l * a_r, a_r * b_l + b_r


def rg_lru_branch(xb, gb, conv_buf, h0, pos_q, conv_w, conv_b, wr, br, wi, bi, lam):
    b, t, width = xb.shape
    f32 = jnp.float32
    xpad = jnp.concatenate([conv_buf.astype(xb.dtype), xb], axis=1)
    xc = conv_b.astype(f32)
    for tap in range(LRU_CONV):
        xc = xc + xpad[:, tap:tap + t].astype(f32) * conv_w[tap].astype(f32)
    new_buf = xpad[:, -(LRU_CONV - 1):]
    xh = xc.reshape(b, t, LRU_BLOCKS, width // LRU_BLOCKS)
    r = jax.nn.sigmoid(jnp.einsum('bthi,hij->bthj', xh, wr.astype(f32)) + br.astype(f32)).reshape(b, t, width)
    i = jax.nn.sigmoid(jnp.einsum('bthi,hij->bthj', xh, wi.astype(f32)) + bi.astype(f32)).reshape(b, t, width)
    log_a = -LRU_C * r * jax.nn.softplus(-lam.astype(f32))
    reset = (pos_q == 0)[None, :, None]
    a = jnp.where(reset, 0.0, jnp.exp(log_a))
    mult = jnp.where(reset, 1.0, jnp.sqrt(-jnp.expm1(2.0 * log_a)))
    u = mult * i * xc
    u = u.at[:, 0].add(a[:, 0] * h0.astype(f32))
    _, hs = lax.associative_scan(lru_combine, (a, u), axis=1)
    y = (hs * jax.nn.gelu(gb.astype(f32))).astype(xb.dtype)
    return y, new_buf, hs[:, -1]


def memory_kv(mem, g, wk, wv):
    b, m, _ = mem.shape
    mn = rms_norm(mem, g)
    return ((mn @ wk).reshape(b, m, MEM_HEADS, MEM_HEAD_DIM),
            (mn @ wv).reshape(b, m, MEM_HEADS, MEM_HEAD_DIM))


def memory_attention(h, mk, mv, wq, wo):
    b, t, _ = h.shape
    q = (h @ wq).reshape(b, t, MEM_HEADS, MEM_HEAD_DIM)
    s = jnp.einsum('bqhd,bkhd->bhqk', q, mk.astype(q.dtype), preferred_element_type=jnp.float32) * MEM_HEAD_DIM ** -0.5
    p = jax.nn.softmax(s, axis=-1)
    o = jnp.einsum('bhqk,bkhd->bqhd', p, mv.astype(jnp.float32)).astype(h.dtype)
    return o.reshape(b, t, MEM_HEADS * MEM_HEAD_DIM) @ wo


def swiglu(h, wg, wu, wd):
    return (jax.nn.silu(h @ wg) * (h @ wu)) @ wd


def empty_past(b, dtype):
    return dict(
        fox_k=jnp.zeros((b, 0, FOX_HEADS, HEAD_DIM), dtype),
        fox_v=jnp.zeros((b, 0, FOX_HEADS, HEAD_DIM), dtype),
        fox_logf=jnp.zeros((b, 0, FOX_HEADS), jnp.float32),
        lru_h=jnp.zeros((b, LRU_WIDTH), jnp.float32),
        lru_conv=jnp.zeros((b, LRU_CONV - 1, LRU_WIDTH), dtype),
        mla_ckv=jnp.zeros((b, 0, MLA_KV_RANK), dtype),
        mla_kpe=jnp.zeros((b, 0, MLA_ROPE), dtype),
        sb_k=jnp.zeros((b, 0, SB_HEADS, HEAD_DIM), dtype),
        sb_v=jnp.zeros((b, 0, SB_HEADS, HEAD_DIM), dtype))


def trunk_layer(x, past, mem_k, mem_v, w):
    bsz, t, _ = x.shape
    past_len = past['fox_k'].shape[1]
    pos_q = past_len + jnp.arange(t, dtype=jnp.int32)
    pos_k = jnp.arange(past_len + t, dtype=jnp.int32)

    def cat(a, bnew):
        return jnp.concatenate([a.astype(bnew.dtype), bnew], axis=1)

    def heads(a):
        return a.reshape(bsz, t, -1, HEAD_DIM)

    h = rms_norm(x, w['ln_mix_pre'])
    w_mix, w_gate = w['w_in'][:, :D_MIX_IN], w['w_in'][:, D_MIX_IN:]
    fq, fk, fv, ff, lx, lg, cq, ckv, kpe, sq, sk, sv = jnp.split(h @ w_mix, SPLIT_IDX, axis=-1)

    fk, fv = heads(fk), heads(fv)
    logf = jax.nn.log_sigmoid(ff.astype(jnp.float32) + w['fox_bf'].astype(jnp.float32))
    o_a = forgetting_attention(heads(fq), cat(past['fox_k'], fk), cat(past['fox_v'], fv),
                               cat(past['fox_logf'], logf), pos_q, pos_k)
    o_b, lru_conv, lru_h = rg_lru_branch(lx, lg, past['lru_conv'], past['lru_h'], pos_q,
                                         w['lru_conv_w'], w['lru_conv_b'], w['lru_wr'], w['lru_br'],
                                         w['lru_wi'], w['lru_bi'], w['lru_lam'])
    o_c, ckv_n, kpe_r = latent_attention(cq, ckv, kpe, past['mla_ckv'], past['mla_kpe'], pos_q, pos_k,
                                         w['mla_q_norm'], w['mla_w_uq'], w['mla_kv_norm'],
                                         w['mla_w_uk'], w['mla_w_uv'])
    sk, sv = heads(sk), heads(sv)
    o_d = stick_breaking_attention(heads(sq), cat(past['sb_k'], sk), cat(past['sb_v'], sv), pos_q, pos_k)

    terms = []
    for n, o in enumerate((o_a, o_b, o_c, o_d)):
        gate = jax.nn.sigmoid(h @ w_gate[:, n * D_MODEL:(n + 1) * D_MODEL])
        terms.append(gate * (o.reshape(bsz, t, BRANCH_WIDTH) @ w['w_branch'][n]))
    merged = terms[0] + terms[1] + terms[2] + terms[3]
    x = x + rms_norm(merged @ w['w_out'], w['ln_mix_post'])

    m = memory_attention(rms_norm(x, w['ln_mem_pre']), mem_k, mem_v, w['mem_wq'], w['mem_wo'])
    x = x + rms_norm(m, w['ln_mem_post'])

    f = swiglu(rms_norm(x, w['ln_ffn_pre']), w['ffn_wg'], w['ffn_wu'], w['ffn_wd'])
    x = x + rms_norm(f, w['ln_ffn_post'])

    new = dict(fox_k=fk, fox_v=fv, fox_logf=logf, lru_h=lru_h, lru_conv=lru_conv,
               mla_ckv=ckv_n, mla_kpe=kpe_r, sb_k=sk, sb_v=sv)
    return x, new


def setup_inputs(seed: int = 0) -> dict:
    key = jax.random.key(seed)
    keys = iter(jax.random.split(key, 64))

    def nrm(shape, scale=1.0):
        return scale * jax.random.normal(next(keys), shape, jnp.float32)

    def gain(shape):
        return 1.0 + 0.05 * nrm(shape)

    L, D = DEPTH, D_MODEL
    gb = LRU_WIDTH // LRU_BLOCKS
    a_lru = jax.random.uniform(next(keys), (L, LRU_WIDTH), jnp.float32, 0.9, 0.999) ** (1.0 / LRU_C)
    return {
        'x_prompt': nrm((BATCH, SEQ, D)),
        'x_sample': nrm((DEC_BATCH, DEC_SEQ, D)),
        'cache_fox_k': nrm((L, DEC_BATCH, PAST_LEN, FOX_HEADS, HEAD_DIM)),
        'cache_fox_v': nrm((L, DEC_BATCH, PAST_LEN, FOX_HEADS, HEAD_DIM)),
        'cache_fox_logf': jax.nn.log_sigmoid(2.0 + nrm((L, DEC_BATCH, PAST_LEN, FOX_HEADS), 0.5)),
        'state_lru_h': nrm((L, DEC_BATCH, LRU_WIDTH), 0.5),
        'state_lru_conv': nrm((L, DEC_BATCH, LRU_CONV - 1, LRU_WIDTH)),
        'cache_mla_ckv': nrm((L, DEC_BATCH, PAST_LEN, MLA_KV_RANK)),
        'cache_mla_kpe': nrm((L, DEC_BATCH, PAST_LEN, MLA_ROPE)),
        'cache_sb_k': nrm((L, DEC_BATCH, PAST_LEN, SB_HEADS, HEAD_DIM)),
        'cache_sb_v': nrm((L, DEC_BATCH, PAST_LEN, SB_HEADS, HEAD_DIM)),
        'cache_mem_k': nrm((L, DEC_BATCH, MEM_LEN, MEM_HEADS, MEM_HEAD_DIM)),
        'cache_mem_v': nrm((L, DEC_BATCH, MEM_LEN, MEM_HEADS, MEM_HEAD_DIM)),
        'mem_prompt': nrm((BATCH, MEM_LEN, D)),
        'ln_mix_pre': gain((L, D)),
        'ln_mix_post': gain((L, D)),
        'w_in': nrm((L, D, D_IN), D ** -0.5),
        'fox_bf': 2.0 + nrm((L, FOX_HEADS), 0.5),
        'lru_conv_w': nrm((L, LRU_CONV, LRU_WIDTH), LRU_CONV ** -0.5),
        'lru_conv_b': nrm((L, LRU_WIDTH), 0.02),
        'lru_wr': nrm((L, LRU_BLOCKS, gb, gb), gb ** -0.5),
        'lru_br': nrm((L, LRU_BLOCKS, gb), 0.02),
        'lru_wi': nrm((L, LRU_BLOCKS, gb, gb), gb ** -0.5),
        'lru_bi': nrm((L, LRU_BLOCKS, gb), 0.02),
        'lru_lam': jnp.log(a_lru) - jnp.log1p(-a_lru),
        'mla_q_norm': gain((L, MLA_Q_RANK)),
        'mla_w_uq': nrm((L, MLA_Q_RANK, MLA_HEADS * (MLA_NOPE + MLA_ROPE)), MLA_Q_RANK ** -0.5),
        'mla_kv_norm': gain((L, MLA_KV_RANK)),
        'mla_w_uk': nrm((L, MLA_KV_RANK, MLA_HEADS * MLA_NOPE), MLA_KV_RANK ** -0.5),
        'mla_w_uv': nrm((L, MLA_KV_RANK, MLA_HEADS * MLA_V), MLA_KV_RANK ** -0.5),
        'w_branch': nrm((L, N_BRANCH, BRANCH_WIDTH, D), BRANCH_WIDTH ** -0.5),
        'w_out': nrm((L, D, D), D ** -0.5),
        'ln_mem_pre': gain((L, D)),
        'ln_mem_post': gain((L, D)),
        'mem_norm': gain((L, D)),
        'mem_wq': nrm((L, D, MEM_HEADS * MEM_HEAD_DIM), D ** -0.5),
        'mem_wk': nrm((L, D, MEM_HEADS * MEM_HEAD_DIM), D ** -0.5),
        'mem_wv': nrm((L, D, MEM_HEADS * MEM_HEAD_DIM), D ** -0.5),
        'mem_wo': nrm((L, MEM_HEADS * MEM_HEAD_DIM, D), (MEM_HEADS * MEM_HEAD_DIM) ** -0.5),
        'ln_ffn_pre': gain((L, D)),
        'ln_ffn_post': gain((L, D)),
        'ffn_wg': nrm((L, D, D_FF), D ** -0.5),
        'ffn_wu': nrm((L, D, D_FF), D ** -0.5),
        'ffn_wd': nrm((L, D_FF, D), D_FF ** -0.5),
    }


def reference(x_prompt, x_sample, cache_fox_k, cache_fox_v, cache_fox_logf, state_lru_h, state_lru_conv,
              cache_mla_ckv, cache_mla_kpe, cache_sb_k, cache_sb_v, cache_mem_k, cache_mem_v, mem_prompt,
              ln_mix_pre, ln_mix_post, w_in, fox_bf, lru_conv_w, lru_conv_b, lru_wr, lru_br, lru_wi, lru_bi,
              lru_lam, mla_q_norm, mla_w_uq, mla_kv_norm, mla_w_uk, mla_w_uv, w_branch, w_out,
              ln_mem_pre, ln_mem_post, mem_norm, mem_wq, mem_wk, mem_wv, mem_wo,
              ln_ffn_pre, ln_ffn_post, ffn_wg, ffn_wu, ffn_wd):
    weights = dict(ln_mix_pre=ln_mix_pre, ln_mix_post=ln_mix_post, w_in=w_in, fox_bf=fox_bf,
                   lru_conv_w=lru_conv_w, lru_conv_b=lru_conv_b, lru_wr=lru_wr, lru_br=lru_br,
                   lru_wi=lru_wi, lru_bi=lru_bi, lru_lam=lru_lam, mla_q_norm=mla_q_norm,
                   mla_w_uq=mla_w_uq, mla_kv_norm=mla_kv_norm, mla_w_uk=mla_w_uk, mla_w_uv=mla_w_uv,
                   w_branch=w_branch, w_out=w_out, ln_mem_pre=ln_mem_pre, ln_mem_post=ln_mem_post,
                   mem_wq=mem_wq, mem_wo=mem_wo, ln_ffn_pre=ln_ffn_pre, ln_ffn_post=ln_ffn_post,
                   ffn_wg=ffn_wg, ffn_wu=ffn_wu, ffn_wd=ffn_wd)

    def layer_weights(l):
        return {name: arr[l] for name, arr in weights.items()}

    y_prompt = x_prompt
    p_states = []
    for l in range(DEPTH):
        mk, mv = memory_kv(mem_prompt, mem_norm[l], mem_wk[l], mem_wv[l])
        y_prompt, st = trunk_layer(y_prompt, empty_past(x_prompt.shape[0], x_prompt.dtype), mk, mv, layer_weights(l))
        st['mem_k'] = mk
        st['mem_v'] = mv
        p_states.append(st)

    y_sample = x_sample
    s_states = []
    for l in range(DEPTH):
        past = dict(fox_k=cache_fox_k[l], fox_v=cache_fox_v[l], fox_logf=cache_fox_logf[l],
                    lru_h=state_lru_h[l], lru_conv=state_lru_conv[l],
                    mla_ckv=cache_mla_ckv[l], mla_kpe=cache_mla_kpe[l],
                    sb_k=cache_sb_k[l], sb_v=cache_sb_v[l])
        y_sample, st = trunk_layer(y_sample, past, cache_mem_k[l], cache_mem_v[l], layer_weights(l))
        s_states.append(st)

    def stk(states, name):
        return jnp.stack([s[name] for s in states])

    return (y_prompt, y_sample,
            stk(p_states, 'fox_k'), stk(p_states, 'fox_v'), stk(p_states, 'fox_logf'),
            stk(p_states, 'lru_h'), stk(p_states, 'lru_conv'),
            stk(p_states, 'mla_ckv'), stk(p_states, 'mla_kpe'),
            stk(p_states, 'sb_k'), stk(p_states, 'sb_v'),
            stk(p_states, 'mem_k'), stk(p_states, 'mem_v'),
            stk(s_states, 'fox_k'), stk(s_states, 'fox_v'), stk(s_states, 'fox_logf'),
            stk(s_states, 'lru_h'), stk(s_states, 'lru_conv'),
            stk(s_states, 'mla_ckv'), stk(s_states, 'mla_kpe'),
            stk(s_states, 'sb_k'), stk(s_states, 'sb_v'))
```

```python
import functools
import math

import numpy as np
import jax
import jax.numpy as jnp
from jax import lax
from jax.experimental import pallas as pl
from jax.experimental.pallas import tpu as pltpu

F32 = jnp.float32
BF16 = jnp.bfloat16

CHUNK = 64
HEAD_DIM = 64
N_HEADS = 4
LRU_WIDTH = 256
LRU_CONV = 4
LRU_C = 8.0
MLA_Q_RANK = 256
MLA_KV_RANK = 128
MLA_NOPE = 64
MLA_ROPE = 32
MLA_V = 64
ROPE_BASE = 10000.0
N_BRANCH = 4
BRANCH_WIDTH = 256
MEM_HEAD_DIM = 128
EPS = 1e-6
NEG_INF = -1e30

HP = 128
QW = N_HEADS * HP
LANE = 128
VMEM_LIMIT_BYTES = 56 * 1024 * 1024
SB_DECAY_LIMIT = 88.0

_NT = (((1,), (1,)), ((), ()))


def _dot(a, b):
    return jnp.dot(a, b, preferred_element_type=F32)


def _dot_nt(a, b):
    return lax.dot_general(a, b, _NT, preferred_element_type=F32)


def _rms(x, g):
    ms = jnp.mean(x * x, axis=-1, keepdims=True)
    return x * lax.rsqrt(ms + EPS) * g


def _sigmoid(x):
    return 1.0 / (1.0 + jnp.exp(-x))


def _softplus(x):
    return jnp.maximum(x, 0.0) + jnp.log(1.0 + jnp.exp(-jnp.abs(x)))


def _pick(n, cands):
    for c in cands:
        if n % c == 0:
            return c
    return n


def _params(*sem):
    return pltpu.CompilerParams(dimension_semantics=sem, vmem_limit_bytes=VMEM_LIMIT_BYTES)


def _const_spec(shape):
    nd = len(shape)
    return pl.BlockSpec(shape, lambda *_: (0,) * nd)


def _place_qk():
    e = np.zeros((N_HEADS * HEAD_DIM, QW), np.float32)
    for h in range(N_HEADS):
        for j in range(HEAD_DIM):
            e[h * HEAD_DIM + j, h * HP + j] = 1.0
    return e


def _place_v():
    e = np.zeros((N_HEADS * HEAD_DIM, QW), np.float32)
    for h in range(N_HEADS):
        for j in range(HEAD_DIM):
            e[h * HEAD_DIM + j, h * HP + (h % 2) * HEAD_DIM + j] = 1.0
    return e


def _fox_feature_maps():
    fk = np.zeros((3, LANE, QW), np.float32)
    fq = np.zeros((3, LANE, QW), np.float32)
    ones_k = np.zeros((1, QW), np.float32)
    ones_q = np.zeros((1, QW), np.float32)
    for h in range(N_HEADS):
        for part in range(3):
            fq[part, h, h * HP + HEAD_DIM + part] = 1.0
            fk[part, h, h * HP + HEAD_DIM + 3 + part] = -1.0
            ones_k[0, h * HP + HEAD_DIM + part] = 1.0
            ones_q[0, h * HP + HEAD_DIM + 3 + part] = 1.0
    return fk, fq, ones_k, ones_q


def _place_kpe():
    e = np.zeros((LANE, QW), np.float32)
    for h in range(N_HEADS):
        for j in range(MLA_ROPE):
            e[j, h * HP + MLA_NOPE + j] = 1.0
    return e


_W_OFF = dict(fq=0, fk=256, fv=512, lx=768, lg=1024, cq=1280, ckv=1536, kpe=1664, kpe_sw=1792, ff=1920,
              sq=2048, sk=2304, sv=2560)
_W_MIX_COLS = 2816


def _in_proj_kernel(x_ref, g_ref, w_ref, qn_ref, kvn_ref, bf_ref, ck_ref, sk_tab_ref, cqT_ref, sqT_ref,
                    wuqT_ref, wuqswT_ref, eqT_ref,
                    fq_ref, fk_ref, fv_ref, lx_ref, lg_ref, ckv_ref, kpe_ref, logf_ref, mqT_ref, sbqT_ref,
                    sk_ref, sv_ref):
    hb = _rms(x_ref[0], g_ref[...]).astype(BF16)

    def proj(name, width):
        a = _W_OFF[name]
        return _dot(hb, w_ref[:, a:a + width])

    fq_ref[0] = (proj('fq', 256) * 0.125).astype(BF16)
    fk_ref[0] = proj('fk', 256)
    fv_ref[0] = proj('fv', 256)
    lx_ref[0] = proj('lx', 256)
    lg_ref[0] = proj('lg', 256)

    cqn = _rms(proj('cq', 256), qn_ref[...]).astype(BF16)
    q_t = _dot_nt(wuqT_ref[...], cqn)
    qsw_t = _dot_nt(wuqswT_ref[...], cqn)
    mqT_ref[0] = (q_t * cqT_ref[...] + qsw_t * sqT_ref[...]).astype(BF16)

    ckv_ref[0] = _rms(proj('ckv', 128), kvn_ref[...])
    kpe_ref[0] = proj('kpe', 128) * ck_ref[...] + proj('kpe_sw', 128) * sk_tab_ref[...]

    ff = proj('ff', 128) + bf_ref[...]
    log_sig = jnp.minimum(ff, 0.0) - jnp.log(1.0 + jnp.exp(-jnp.abs(ff)))
    lane = lax.broadcasted_iota(jnp.int32, ff.shape, 1)
    logf_ref[0] = jnp.where(lane < N_HEADS, log_sig, 0.0)

    sq = (proj('sq', 256) * 0.125).astype(BF16)
    sbqT_ref[0] = _dot_nt(eqT_ref[...], sq).astype(BF16)
    sk_ref[0] = proj('sk', 256)
    sv_ref[0] = proj('sv', 256)


def _in_proj(x, lw, tabs):
    b, t, d = x.shape
    tm = _pick(t, (512, 256, 128))
    grid = (b, t // tm)
    row = lambda w: pl.BlockSpec((1, tm, w), lambda i, j: (i, j, 0))
    colT = pl.BlockSpec((1, QW, tm), lambda i, j: (i, 0, j))
    in_specs = [
        row(d), _const_spec((1, d)), _const_spec((d, _W_MIX_COLS)),
        _const_spec((1, MLA_Q_RANK)), _const_spec((1, MLA_KV_RANK)), _const_spec((1, LANE)),
        pl.BlockSpec((tm, LANE), lambda i, j: (j, 0)), pl.BlockSpec((tm, LANE), lambda i, j: (j, 0)),
        pl.BlockSpec((QW, tm), lambda i, j: (0, j)), pl.BlockSpec((QW, tm), lambda i, j: (0, j)),
        _const_spec((QW, MLA_Q_RANK)), _const_spec((QW, MLA_Q_RANK)), _const_spec((QW, 256)),
    ]
    sds = jax.ShapeDtypeStruct
    out_shape = [
        sds((b, t, 256), BF16),
        sds((b, t, 256), F32), sds((b, t, 256), F32),
        sds((b, t, 256), F32), sds((b, t, 256), F32),
        sds((b, t, 128), F32), sds((b, t, LANE), F32), sds((b, t, LANE), F32),
        sds((b, QW, t), BF16), sds((b, QW, t), BF16),
        sds((b, t, 256), F32), sds((b, t, 256), F32),
    ]
    out_specs = [row(256), row(256), row(256), row(256), row(256), row(128), row(LANE), row(LANE),
                 colT, colT, row(256), row(256)]
    return pl.pallas_call(
        _in_proj_kernel, grid=grid, in_specs=in_specs, out_specs=out_specs, out_shape=out_shape,
        compiler_params=_params("parallel", "parallel"), name="in_proj",
    )(x, lw['ln_mix_pre'], lw['w_mix'], lw['mla_q_norm'], lw['mla_kv_norm'], lw['fox_bf'],
      tabs['ck'], tabs['sk'], tabs['cqT'], tabs['sqT'], lw['wuqT'], lw['wuqswT'], lw['eqT'])


def _cumsum_rows(x):
    n = x.shape[0]
    row = lax.broadcasted_iota(jnp.int32, x.shape, 0)
    d = 1
    while d < n:
        x = x + jnp.where(row >= d, pltpu.roll(x, d, 0), 0.0)
        d *= 2
    return x


def _kv_prep_kernel(fq_ref, fk_ref, fv_ref, logf_ref, ckv_ref, kpe_ref, sk_ref, sv_ref,
                    ek_ref, evT_ref, eqT_ref, fkmap_ref, fqmapT_ref, onesk_ref, onesq_ref,
                    wuk_ref, ekpe_ref, wuvT_ref,
                    fqT_out, fk_out, fvT_out, mk_out, mvT_out, sbk_out, sbvT_out, carry_ref, *, tk):
    @pl.when(pl.program_id(1) == 0)
    def _():
        carry_ref[...] = jnp.zeros_like(carry_ref)

    ts = fk_ref.shape[1]
    c = _cumsum_rows(logf_ref[0]) + carry_ref[...]
    carry_ref[...] = c[ts - 1:ts, :]
    c_hi = c.astype(BF16)
    rem = c - c_hi.astype(F32)
    c_mid = rem.astype(BF16)
    c_lo = (rem - c_mid.astype(F32)).astype(BF16)

    def store_vt(out, v_t):
        for j in range(ts // tk):
            out[0, j] = v_t[:, j * tk:(j + 1) * tk]

    fk_out[0] = (_dot(fk_ref[0].astype(BF16), ek_ref[...]) + _dot(c_hi, fkmap_ref[0]) + _dot(c_mid, fkmap_ref[1])
                 + _dot(c_lo, fkmap_ref[2]) + onesk_ref[...]).astype(BF16)
    fqT_out[0] = (_dot_nt(eqT_ref[...], fq_ref[0]) + _dot_nt(fqmapT_ref[0], c_hi) + _dot_nt(fqmapT_ref[1], c_mid)
                  + _dot_nt(fqmapT_ref[2], c_lo) + onesq_ref[...]).astype(BF16)
    store_vt(fvT_out, _dot_nt(evT_ref[...], fv_ref[0].astype(BF16)).astype(BF16))

    ckv = ckv_ref[0].astype(BF16)
    mk_out[0] = (_dot(ckv, wuk_ref[...]) + _dot(kpe_ref[0].astype(BF16), ekpe_ref[...])).astype(BF16)
    store_vt(mvT_out, _dot_nt(wuvT_ref[...], ckv).astype(BF16))

    sbk_out[0] = _dot(sk_ref[0].astype(BF16), ek_ref[...]).astype(BF16)
    store_vt(sbvT_out, _dot_nt(evT_ref[...], sv_ref[0].astype(BF16)).astype(BF16))


def _kv_prep(fq, fk, fv, logf, ckv, kpe, sk, sv, lw, cst, tk):
    b, s, _ = fk.shape
    ts = _pick(s, (1024, 512, 256))
    grid = (b, s // ts)
    row = lambda w: pl.BlockSpec((1, ts, w), lambda i, j: (i, j, 0))
    colT = pl.BlockSpec((1, QW, ts), lambda i, j: (i, 0, j))
    vT = pl.BlockSpec((1, ts // tk, QW, tk), lambda i, j: (i, j, 0, 0))
    in_specs = [row(256), row(256), row(256), row(LANE), row(128), row(LANE), row(256), row(256),
                _const_spec((256, QW)), _const_spec((QW, 256)), _const_spec((QW, 256)),
                _const_spec((3, LANE, QW)), _const_spec((3, QW, LANE)), _const_spec((1, QW)), _const_spec((QW, 1)),
                _const_spec((MLA_KV_RANK, QW)), _const_spec((LANE, QW)), _const_spec((QW, MLA_KV_RANK))]
    sds = jax.ShapeDtypeStruct
    slab = sds((b, s, QW), BF16)
    slab_t = sds((b, s // tk, QW, tk), BF16)
    out_shape = [sds((b, QW, s), BF16), slab, slab_t, slab, slab_t, slab, slab_t]
    out_specs = [colT, row(QW), vT, row(QW), vT, row(QW), vT]
    return pl.pallas_call(
        functools.partial(_kv_prep_kernel, tk=tk), grid=grid, in_specs=in_specs, out_specs=out_specs,
        out_shape=out_shape, scratch_shapes=[pltpu.VMEM((1, LANE), F32)],
        compiler_params=_params("parallel", "arbitrary"), name="kv_prep",
    )(fq, fk, fv, logf, ckv, kpe, sk, sv, cst['ek'], cst['evT'], lw['eqT'], cst['fkmap'], cst['fqmapT'],
      cst['ones_k'], cst['ones_qT'], lw['wuk'], cst['ekpe'], lw['wuvT'])


def _load_kv(k_ref, vT_ref, kb, tk):
    off = pl.multiple_of(kb * tk, tk)
    return k_ref[0, pl.ds(off, tk), :], vT_ref[0, kb]


def _finish(acc, inv, eye_ref, o_ref):
    o_t = (acc * inv).astype(BF16)
    o_ref[0] = _dot_nt(eye_ref[...], o_t).astype(BF16)


def _softmax_attn_kernel(qT_ref, k_ref, vT_ref, eye_ref, o_ref, *, mode, past, tq, tk, s_valid):
    qi = pl.program_id(2)
    q_t = qT_ref[0]
    ltk = int(math.log2(tk))
    q_lo = past + qi * tq
    q_hi = q_lo + (tq - 1)
    if mode == 'fox':
        n_full = (q_lo + 1) >> ltk
        n_blk = (q_hi >> ltk) + 1
    else:
        lim_lo = ((q_lo // CHUNK) + 1) * CHUNK
        lim_hi = jnp.minimum(((q_hi // CHUNK) + 1) * CHUNK, s_valid)
        n_blk = (lim_hi + (tk - 1)) >> ltk
        n_full = jnp.minimum(lim_lo >> ltk, n_blk)
    qpos = q_lo + lax.broadcasted_iota(jnp.int32, (1, tq), 1)

    def step(kb, carry, masked):
        m, l, acc = carry
        k, v_t = _load_kv(k_ref, vT_ref, kb, tk)
        s = _dot(k, q_t)
        if masked:
            kpos = kb * tk + lax.broadcasted_iota(jnp.int32, (tk, 1), 0)
            if mode == 'fox':
                vis = kpos <= qpos
            else:
                vis = jnp.logical_and((kpos // CHUNK) <= (qpos // CHUNK), kpos < s_valid)
            s = jnp.where(vis, s, NEG_INF)
        m_new = jnp.maximum(m, jnp.max(s, axis=0, keepdims=True))
        p = jnp.exp(s - m_new)
        alpha = jnp.exp(m - m_new)
        l = alpha * l + jnp.sum(p, axis=0, keepdims=True)
        acc = alpha * acc + _dot(v_t, p.astype(BF16))
        return m_new, l, acc

    carry = (jnp.full((1, tq), NEG_INF, F32), jnp.zeros((1, tq), F32), jnp.zeros((HP, tq), F32))
    carry = lax.fori_loop(0, n_full, functools.partial(step, masked=False), carry)
    carry = lax.fori_loop(n_full, n_blk, functools.partial(step, masked=True), carry)
    _, l, acc = carry
    _finish(acc, 1.0 / l, eye_ref, o_ref)


def _stick_attn_kernel(qT_ref, k_ref, vT_ref, eye_ref, later_ref, o_ref, *, past, tq, tk):
    qi = pl.program_id(2)
    q_t = qT_ref[0]
    ltk = int(math.log2(tk))
    q_lo = past + qi * tq
    q_hi = q_lo + (tq - 1)
    n_full = q_lo >> ltk
    n_blk = ((q_hi - 1) >> ltk) + 1
    qpos = q_lo + lax.broadcasted_iota(jnp.int32, (1, tq), 1)

    def step(kb, decay, acc, masked):
        k, v_t = _load_kv(k_ref, vT_ref, kb, tk)
        z = _dot(k, q_t)
        drop = _softplus(z)
        if masked:
            kpos = kb * tk + lax.broadcasted_iota(jnp.int32, (tk, 1), 0)
            vis = kpos < qpos
            drop = jnp.where(vis, drop, 0.0)
        hi = drop.astype(BF16)
        lo = (drop - hi.astype(F32)).astype(BF16)
        later = _dot(later_ref[...], hi) + _dot(later_ref[...], lo)
        a = jnp.exp(z - drop - later - decay)
        if masked:
            a = jnp.where(vis, a, 0.0)
        acc = acc + _dot(v_t, a.astype(BF16))
        decay = decay + jnp.sum(drop, axis=0, keepdims=True)
        return decay, acc

    def masked_body(i, carry):
        return step(n_blk - 1 - i, carry[0], carry[1], True)

    decay, acc = lax.fori_loop(0, n_blk - n_full, masked_body,
                               (jnp.zeros((1, tq), F32), jnp.zeros((HP, tq), F32)))

    def cond(carry):
        kb, dmin, _, _ = carry
        return jnp.logical_and(kb >= 0, dmin < SB_DECAY_LIMIT)

    def body(carry):
        kb, _, decay, acc = carry
        decay, acc = step(kb, decay, acc, False)
        return kb - 1, jnp.min(decay), decay, acc

    _, _, _, acc = lax.while_loop(cond, body, (n_full - 1, jnp.min(decay), decay, acc))
    _finish(acc, 1.0, eye_ref, o_ref)


def _attention(mode, q_t, k, v_t, cst, *, past, s_valid, tq, tk):
    b, _, t = q_t.shape
    s = k.shape[1]
    grid = (b, N_HEADS, t // tq)
    in_specs = [pl.BlockSpec((1, HP, tq), lambda i, h, j: (i, h, j)),
                pl.BlockSpec((1, s, HP), lambda i, h, j: (i, 0, h)),
                pl.BlockSpec((1, s // tk, HP, tk), lambda i, h, j: (i, 0, h, 0)),
                _const_spec((tq, tq))]
    args = [q_t, k, v_t, jnp.eye(tq, dtype=BF16)]
    if mode == 'sb':
        kern = functools.partial(_stick_attn_kernel, past=past, tq=tq, tk=tk)
        in_specs.append(_const_spec((tk, tk)))
        args.append(cst['later'][tk])
    else:
        kern = functools.partial(_softmax_attn_kernel, mode=mode, past=past, tq=tq, tk=tk, s_valid=s_valid)
    return pl.pallas_call(
        kern, grid=grid, in_specs=in_specs,
        out_specs=pl.BlockSpec((1, tq, HP), lambda i, h, j: (i, j, h)),
        out_shape=jax.ShapeDtypeStruct((b, t, QW), BF16),
        compiler_params=_params("parallel", "parallel", "parallel"), name="attn_" + mode,
    )(*args)


def _lru_kernel(lx_ref, lg_ref, cb_ref, h0_ref, cw_ref, cbias_ref, wr_ref, br_ref, wi_ref, bi_ref, lam_ref,
                y_ref, nb_ref, hl_ref, xcat_ref, h_ref, *, past, tc):
    ci = pl.program_id(1)
    keep = LRU_CONV - 1

    @pl.when(ci == 0)
    def _():
        xcat_ref[0:8, :] = jnp.zeros((8, LRU_WIDTH), F32)
        xcat_ref[8 - keep:8, :] = cb_ref[0]
        h_ref[...] = h0_ref[0]

    x = lx_ref[0]
    xcat_ref[8:8 + tc, :] = x
    xc = cbias_ref[...] + x * cw_ref[keep:keep + 1, :]
    for tap in range(keep):
        shift = keep - tap
        xc = xc + xcat_ref[8 - shift:8 - shift + tc, :] * cw_ref[tap:tap + 1, :]
    nb_ref[0] = xcat_ref[8 + tc - keep:8 + tc, :]
    xcat_ref[0:8, :] = x[tc - 8:tc, :]

    xcb = xc.astype(BF16)
    r = _sigmoid(_dot(xcb, wr_ref[...]) + br_ref[...])
    gate_in = _sigmoid(_dot(xcb, wi_ref[...]) + bi_ref[...])
    log_a = (-LRU_C) * r * _softplus(-lam_ref[...])
    row = lax.broadcasted_iota(jnp.int32, (tc, LRU_WIDTH), 0)
    reset = (past + ci * tc + row) == 0
    a = jnp.where(reset, 0.0, jnp.exp(log_a))
    y2 = 2.0 * log_a
    series = -y2 * (1.0 + y2 * (0.5 + y2 * (1.0 / 6.0 + y2 * (1.0 / 24.0 + y2 * (1.0 / 120.0)))))
    one_minus = jnp.where(y2 > -0.05, series, 1.0 - jnp.exp(y2))
    mult = jnp.where(reset, 1.0, jnp.sqrt(one_minus))
    u = mult * gate_in * xc

    d = 1
    while d < tc:
        ok = row >= d
        u = u + jnp.where(ok, a * pltpu.roll(u, d, 0), 0.0)
        a = jnp.where(ok, a * pltpu.roll(a, d, 0), a)
        d *= 2
    hs = a * h_ref[...] + u
    h_last = hs[tc - 1:tc, :]
    h_ref[...] = h_last
    hl_ref[0] = h_last

    g = lg_ref[0]
    gelu = 0.5 * g * (1.0 + jnp.tanh(0.7978845608028654 * (g + 0.044715 * g * g * g)))
    y_ref[0] = (hs * gelu).astype(BF16)


def _lru(lx, lg, conv_buf, h0, lw, past):
    b, t, w = lx.shape
    tc = _pick(t, (256, 128, 64, 32))
    grid = (b, t // tc)
    row = pl.BlockSpec((1, tc, w), lambda i, j: (i, j, 0))
    keep = LRU_CONV - 1
    in_specs = [row, row, pl.BlockSpec((1, keep, w), lambda i, j: (i, 0, 0)), pl.BlockSpec((1, 1, w), lambda i, j: (i, 0, 0)),
                _const_spec((LRU_CONV, w)), _const_spec((1, w)), _const_spec((w, w)), _const_spec((1, w)),
                _const_spec((w, w)), _const_spec((1, w)), _const_spec((1, w))]
    sds = jax.ShapeDtypeStruct
    return pl.pallas_call(
        functools.partial(_lru_kernel, past=past, tc=tc), grid=grid, in_specs=in_specs,
        out_specs=[row, pl.BlockSpec((1, keep, w), lambda i, j: (i, 0, 0)), pl.BlockSpec((1, 1, w), lambda i, j: (i, 0, 0))],
        out_shape=[sds((b, t, w), BF16), sds((b, keep, w), F32), sds((b, 1, w), F32)],
        scratch_shapes=[pltpu.VMEM((tc + 8, w), F32), pltpu.VMEM((1, w), F32)],
        compiler_params=_params("parallel", "arbitrary"), name="rg_lru",
    )(lx, lg, conv_buf, h0, lw['lru_conv_w'], lw['lru_conv_b'], lw['lru_wr'], lw['lru_br'], lw['lru_wi'],
      lw['lru_bi'], lw['lru_lam'])


def _pair_heads(o):
    return jnp.concatenate([o[:, 0:HP] + o[:, HP:2 * HP], o[:, 2 * HP:3 * HP] + o[:, 3 * HP:4 * HP]], axis=1)


def _merge_kernel(x_ref, oa_ref, ob_ref, oc_ref, od_ref, g1_ref, g2_ref, wg_ref, wb_ref, wo_ref, out_ref):
    x = x_ref[0]
    d = x.shape[1]
    hb = _rms(x, g1_ref[...]).astype(BF16)
    branches = (_pair_heads(oa_ref[0]), ob_ref[0], _pair_heads(oc_ref[0]), _pair_heads(od_ref[0]))
    merged = None
    for n, o in enumerate(branches):
        gate = _sigmoid(_dot(hb, wg_ref[:, n * d:(n + 1) * d]))
        term = gate * _dot(o, wb_ref[n])
        merged = term if merged is None else merged + term
    y = _dot(merged.astype(BF16), wo_ref[...])
    out_ref[0] = x + _rms(y, g2_ref[...])


def _merge(x, o_a, o_b, o_c, o_d, lw):
    b, t, d = x.shape
    tm = _pick(t, (256, 128))
    grid = (b, t // tm)
    row = lambda w: pl.BlockSpec((1, tm, w), lambda i, j: (i, j, 0))
    in_specs = [row(d), row(QW), row(BRANCH_WIDTH), row(QW), row(QW), _const_spec((1, d)), _const_spec((1, d)),
                _const_spec((d, N_BRANCH * d)), _const_spec((N_BRANCH, BRANCH_WIDTH, d)), _const_spec((d, d))]
    return pl.pallas_call(
        _merge_kernel, grid=grid, in_specs=in_specs, out_specs=row(d),
        out_shape=jax.ShapeDtypeStruct((b, t, d), F32),
        compiler_params=_params("parallel", "parallel"), name="merge",
    )(x, o_a, o_b, o_c, o_d, lw['ln_mix_pre'], lw['ln_mix_post'], lw['w_gate'], lw['w_branch'], lw['w_out'])


def _mem_kv_kernel(mem_ref, g_ref, wk_ref, wv_ref, k_ref, v_ref):
    mn = _rms(mem_ref[0], g_ref[...]).astype(BF16)
    k_ref[0] = _dot(mn, wk_ref[...])
    v_ref[0] = _dot(mn, wv_ref[...])


def _mem_kv(mem, lw):
    b, m, d = mem.shape
    w = lw['mem_wk'].shape[1]
    out = jax.ShapeDtypeStruct((b, m, w), F32)
    blk = pl.BlockSpec((1, m, w), lambda i: (i, 0, 0))
    return pl.pallas_call(
        _mem_kv_kernel, grid=(b,),
        in_specs=[pl.BlockSpec((1, m, d), lambda i: (i, 0, 0)), _const_spec((1, d)), _const_spec((d, w)), _const_spec((d, w))],
        out_specs=[blk, blk], out_shape=[out, out], compiler_params=_params("parallel"), name="mem_kv",
    )(mem, lw['mem_norm'], lw['mem_wk'], lw['mem_wv'])


def _mem_attn_kernel(x_ref, mk_ref, mv_ref, g1_ref, g2_ref, wq_ref, wo_ref, out_ref):
    x = x_ref[0]
    hb = _rms(x, g1_ref[...]).astype(BF16)
    q = (_dot(hb, wq_ref[...]) * (MEM_HEAD_DIM ** -0.5)).astype(BF16)
    heads = []
    for h in range(N_HEADS):
        sl = slice(h * MEM_HEAD_DIM, (h + 1) * MEM_HEAD_DIM)
        s = _dot_nt(q[:, sl], mk_ref[0, :, sl])
        p = jnp.exp(s - jnp.max(s, axis=-1, keepdims=True))
        inv = 1.0 / jnp.sum(p, axis=-1, keepdims=True)
        heads.append((_dot(p.astype(BF16), mv_ref[0, :, sl]) * inv).astype(BF16))
    y = _dot(jnp.concatenate(heads, axis=1), wo_ref[...])
    out_ref[0] = x + _rms(y, g2_ref[...])


def _mem_attn(x, mk, mv, lw):
    b, t, d = x.shape
    m, w = mk.shape[1], mk.shape[2]
    tm = _pick(t, (512, 256, 128))
    row = pl.BlockSpec((1, tm, d), lambda i, j: (i, j, 0))
    kv = pl.BlockSpec((1, m, w), lambda i, j: (i, 0, 0))
    return pl.pallas_call(
        _mem_attn_kernel, grid=(b, t // tm),
        in_specs=[row, kv, kv, _const_spec((1, d)), _const_spec((1, d)), _const_spec((d, w)), _const_spec((w, d))],
        out_specs=row, out_shape=jax.ShapeDtypeStruct((b, t, d), F32),
        compiler_params=_params("parallel", "parallel"), name="mem_attn",
    )(x, mk, mv, lw['ln_mem_pre'], lw['ln_mem_post'], lw['mem_wq'], lw['mem_wo'])


def _ffn_kernel(x_ref, g1_ref, g2_ref, wg_ref, wu_ref, wd_ref, out_ref):
    x = x_ref[0]
    hb = _rms(x, g1_ref[...]).astype(BF16)
    gate = _dot(hb, wg_ref[...])
    act = (gate * _sigmoid(gate) * _dot(hb, wu_ref[...])).astype(BF16)
    y = _dot(act, wd_ref[...])
    out_ref[0] = x + _rms(y, g2_ref[...])


def _ffn(x, lw):
    b, t, d = x.shape
    f = lw['ffn_wg'].shape[1]
    tm = _pick(t, (256, 128))
    row = pl.BlockSpec((1, tm, d), lambda i, j: (i, j, 0))
    single = pl.Buffered(1)
    return pl.pallas_call(
        _ffn_kernel, grid=(b, t // tm),
        in_specs=[row, _const_spec((1, d)), _const_spec((1, d)),
                  pl.BlockSpec((d, f), lambda i, j: (0, 0), pipeline_mode=single),
                  pl.BlockSpec((d, f), lambda i, j: (0, 0), pipeline_mode=single),
                  pl.BlockSpec((f, d), lambda i, j: (0, 0), pipeline_mode=single)],
        out_specs=row, out_shape=jax.ShapeDtypeStruct((b, t, d), F32),
        compiler_params=_params("parallel", "parallel"), name="ffn",
    )(x, lw['ln_ffn_pre'], lw['ln_ffn_post'], lw['ffn_wg'], lw['ffn_wu'], lw['ffn_wd'])


def _constants():
    fk, fq, ones_k, ones_q = _fox_feature_maps()
    later = {}
    for tk in (256,):
        idx = np.arange(tk)
        later[tk] = jnp.asarray((idx[None, :] > idx[:, None]).astype(np.float32), BF16)
    return dict(
        ek=jnp.asarray(_place_qk(), BF16), evT=jnp.asarray(_place_v().T, BF16),
        fkmap=jnp.asarray(fk, BF16), fqmapT=jnp.asarray(np.transpose(fq, (0, 2, 1)), BF16),
        ones_k=jnp.asarray(ones_k, F32), ones_qT=jnp.asarray(ones_q.T, F32),
        ekpe=jnp.asarray(_place_kpe(), BF16), later=later)


def _pad_cols(w, n):
    return jnp.pad(w, ((0, 0), (0, n - w.shape[1])))


def _layer_weights(l, p):
    d = p['w_in'].shape[1]
    w_in = p['w_in'][l]
    sizes = (256, 256, 256, N_HEADS, LRU_WIDTH, LRU_WIDTH, MLA_Q_RANK, MLA_KV_RANK, MLA_ROPE, 256, 256, 256)
    offs = np.concatenate([[0], np.cumsum(sizes)])
    fq, fk, fv, ff, lx, lg, cq, ckv, kpe, sq, sk, sv = [w_in[:, offs[i]:offs[i + 1]] for i in range(len(sizes))]
    half = MLA_ROPE // 2
    kpe_sw = jnp.concatenate([kpe[:, half:], kpe[:, :half]], axis=1)
    w_mix = jnp.concatenate([fq, fk, fv, lx, lg, cq, ckv, _pad_cols(kpe, LANE), _pad_cols(kpe_sw, LANE),
                             _pad_cols(ff, LANE), sq, sk, sv], axis=1).astype(BF16)

    qk = MLA_NOPE + MLA_ROPE
    wuq = p['mla_w_uq'][l].reshape(MLA_Q_RANK, N_HEADS, qk)
    rope = wuq[:, :, MLA_NOPE:]
    rope_sw = jnp.concatenate([rope[:, :, half:], rope[:, :, :half]], axis=2)
    wuq_p = jnp.pad(wuq, ((0, 0), (0, 0), (0, HP - qk))).reshape(MLA_Q_RANK, QW)
    wuqsw_p = jnp.pad(rope_sw, ((0, 0), (0, 0), (MLA_NOPE, HP - qk))).reshape(MLA_Q_RANK, QW)
    wuk = p['mla_w_uk'][l].reshape(MLA_KV_RANK, N_HEADS, MLA_NOPE)
    wuk_p = jnp.pad(wuk, ((0, 0), (0, 0), (0, HP - MLA_NOPE))).reshape(MLA_KV_RANK, QW)
    wuv = p['mla_w_uv'][l].reshape(MLA_KV_RANK, N_HEADS // 2, 2, MLA_V)
    wuv_p = jnp.stack([jnp.pad(wuv[:, :, 0], ((0, 0), (0, 0), (0, HP - MLA_V))),
                       jnp.pad(wuv[:, :, 1], ((0, 0), (0, 0), (HP - MLA_V, 0)))], axis=2).reshape(MLA_KV_RANK, QW)

    def block_diag(w):
        return jax.scipy.linalg.block_diag(*[w[i] for i in range(w.shape[0])]).astype(BF16)

    row = lambda v: v[l].reshape(1, -1).astype(F32)
    return dict(
        w_mix=w_mix, w_gate=w_in[:, offs[-1]:].astype(BF16),
        ln_mix_pre=row(p['ln_mix_pre']), ln_mix_post=row(p['ln_mix_post']),
        fox_bf=_pad_cols(row(p['fox_bf']), LANE),
        mla_q_norm=row(p['mla_q_norm']), mla_kv_norm=row(p['mla_kv_norm']),
        wuqT=wuq_p.T.astype(BF16), wuqswT=wuqsw_p.T.astype(BF16), wuk=wuk_p.astype(BF16), wuvT=wuv_p.T.astype(BF16),
        eqT=jnp.asarray(_place_qk().T, BF16),
        lru_conv_w=p['lru_conv_w'][l].astype(F32), lru_conv_b=row(p['lru_conv_b']),
        lru_wr=block_diag(p['lru_wr'][l]), lru_br=row(p['lru_br']),
        lru_wi=block_diag(p['lru_wi'][l]), lru_bi=row(p['lru_bi']), lru_lam=row(p['lru_lam']),
        w_branch=p['w_branch'][l].astype(BF16), w_out=p['w_out'][l].astype(BF16),
        ln_mem_pre=row(p['ln_mem_pre']), ln_mem_post=row(p['ln_mem_post']), mem_norm=row(p['mem_norm']),
        mem_wq=p['mem_wq'][l].astype(BF16), mem_wk=p['mem_wk'][l].astype(BF16),
        mem_wv=p['mem_wv'][l].astype(BF16), mem_wo=p['mem_wo'][l].astype(BF16),
        ln_ffn_pre=row(p['ln_ffn_pre']), ln_ffn_post=row(p['ln_ffn_post']),
        ffn_wg=p['ffn_wg'][l].astype(BF16), ffn_wu=p['ffn_wu'][l].astype(BF16), ffn_wd=p['ffn_wd'][l].astype(BF16))


def _rope_tables(past, t):
    half = MLA_ROPE // 2
    inv = jnp.power(ROPE_BASE, -jnp.arange(half, dtype=F32) / half)
    ang = (past + jnp.arange(t, dtype=jnp.int32)).astype(F32)[:, None] * inv
    cos, sin = jnp.cos(ang), jnp.sin(ang)
    c32 = jnp.concatenate([cos, cos], axis=1)
    s32 = jnp.concatenate([-sin, sin], axis=1)
    scale = (MLA_NOPE + MLA_ROPE) ** -0.5
    slot_c = jnp.concatenate([jnp.ones((t, MLA_NOPE), F32), c32, jnp.zeros((t, HP - MLA_NOPE - MLA_ROPE), F32)], axis=1)
    slot_s = jnp.concatenate([jnp.zeros((t, MLA_NOPE), F32), s32, jnp.zeros((t, HP - MLA_NOPE - MLA_ROPE), F32)], axis=1)
    return dict(ck=_pad_cols(c32, LANE), sk=_pad_cols(s32, LANE),
                cqT=(jnp.tile(slot_c, (1, N_HEADS)) * scale).T, sqT=(jnp.tile(slot_s, (1, N_HEADS)) * scale).T)


def _trunk_layer(x, past_state, mem_k, mem_v, lw, cst):
    b, t, _ = x.shape
    past = 0 if past_state is None else past_state['fox_k'].shape[1]
    tk = 256
    fq, fk, fv, lx, lg, ckv_n, kpe_slab, logf_slab, mla_qT, sb_qT, sk, sv = _in_proj(x, lw, _rope_tables(past, t))

    if past_state is None:
        s_valid = t
        keys = (fq, fk, fv, logf_slab, ckv_n, kpe_slab, sk, sv)
        conv_buf = jnp.zeros((b, LRU_CONV - 1, LRU_WIDTH), F32)
        h0 = jnp.zeros((b, 1, LRU_WIDTH), F32)
    else:
        s_valid = past + t
        s_pad = -(-s_valid // tk) * tk

        def cat(old, new):
            old = old.reshape(b, past, -1).astype(new.dtype)
            old = jnp.pad(old, ((0, 0), (0, 0), (0, new.shape[2] - old.shape[2])))
            return jnp.pad(jnp.concatenate([old, new], axis=1), ((0, 0), (0, s_pad - s_valid), (0, 0)))

        keys = (cat(jnp.zeros((b, past, 256), BF16), fq), cat(past_state['fox_k'], fk), cat(past_state['fox_v'], fv),
                cat(past_state['fox_logf'], logf_slab), cat(past_state['mla_ckv'], ckv_n),
                cat(past_state['mla_kpe'], kpe_slab), cat(past_state['sb_k'], sk), cat(past_state['sb_v'], sv))
        conv_buf = past_state['lru_conv'].astype(F32)
        h0 = past_state['lru_h'].reshape(b, 1, LRU_WIDTH).astype(F32)

    fox_qT, fox_k, fox_vT, mla_k, mla_vT, sb_k, sb_vT = _kv_prep(*keys, lw, cst, tk)
    if past_state is not None:
        fox_qT = fox_qT[:, :, past:past + t]

    tq = _pick(t, (256, 128))
    att = functools.partial(_attention, cst=cst, past=past, s_valid=s_valid, tq=tq, tk=tk)
    o_a = att('fox', fox_qT, fox_k, fox_vT)
    o_c = att('mla', mla_qT, mla_k, mla_vT)
    o_d = att('sb', sb_qT, sb_k, sb_vT)
    o_b, lru_conv, lru_h = _lru(lx, lg, conv_buf, h0, lw, past)

    x = _merge(x, o_a, o_b, o_c, o_d, lw)
    x = _mem_attn(x, mem_k, mem_v, lw)
    x = _ffn(x, lw)

    new = dict(fox_k=fk.reshape(b, t, N_HEADS, HEAD_DIM), fox_v=fv.reshape(b, t, N_HEADS, HEAD_DIM),
               fox_logf=logf_slab[:, :, :N_HEADS], lru_h=lru_h.reshape(b, LRU_WIDTH), lru_conv=lru_conv,
               mla_ckv=ckv_n, mla_kpe=kpe_slab[:, :, :MLA_ROPE],
               sb_k=sk.reshape(b, t, N_HEADS, HEAD_DIM), sb_v=sv.reshape(b, t, N_HEADS, HEAD_DIM))
    return x, new


def kernel(x_prompt, x_sample, cache_fox_k, cache_fox_v, cache_fox_logf, state_lru_h, state_lru_conv, cache_mla_ckv, cache_mla_kpe, cache_sb_k, cache_sb_v, cache_mem_k, cache_mem_v, mem_prompt, ln_mix_pre, ln_mix_post, w_in, fox_bf, lru_conv_w, lru_conv_b, lru_wr, lru_br, lru_wi, lru_bi, lru_lam, mla_q_norm, mla_w_uq, mla_kv_norm, mla_w_uk, mla_w_uv, w_branch, w_out, ln_mem_pre, ln_mem_post, mem_norm, mem_wq, mem_wk, mem_wv, mem_wo, ln_ffn_pre, ln_ffn_post, ffn_wg, ffn_wu, ffn_wd):
    params = dict(ln_mix_pre=ln_mix_pre, ln_mix_post=ln_mix_post, w_in=w_in, fox_bf=fox_bf, lru_conv_w=lru_conv_w,
                  lru_conv_b=lru_conv_b, lru_wr=lru_wr, lru_br=lru_br, lru_wi=lru_wi, lru_bi=lru_bi, lru_lam=lru_lam,
                  mla_q_norm=mla_q_norm, mla_w_uq=mla_w_uq, mla_kv_norm=mla_kv_norm, mla_w_uk=mla_w_uk,
                  mla_w_uv=mla_w_uv, w_branch=w_branch, w_out=w_out, ln_mem_pre=ln_mem_pre, ln_mem_post=ln_mem_post,
                  mem_norm=mem_norm, mem_wq=mem_wq, mem_wk=mem_wk, mem_wv=mem_wv, mem_wo=mem_wo,
                  ln_ffn_pre=ln_ffn_pre, ln_ffn_post=ln_ffn_post, ffn_wg=ffn_wg, ffn_wu=ffn_wu, ffn_wd=ffn_wd)
    depth = w_in.shape[0]
    cst = _constants()
    weights = [_layer_weights(l, params) for l in range(depth)]
    bp, mem_len = mem_prompt.shape[0], mem_prompt.shape[1]

    y_prompt, p_states = x_prompt, []
    for l in range(depth):
        mk, mv = _mem_kv(mem_prompt, weights[l])
        y_prompt, st = _trunk_layer(y_prompt, None, mk.astype(BF16), mv.astype(BF16), weights[l], cst)
        st['mem_k'] = mk.reshape(bp, mem_len, N_HEADS, MEM_HEAD_DIM)
        st['mem_v'] = mv.reshape(bp, mem_len, N_HEADS, MEM_HEAD_DIM)
        p_states.append(st)

    y_sample, s_states = x_sample, []
    bs = x_sample.shape[0]
    for l in range(depth):
        past = dict(fox_k=cache_fox_k[l], fox_v=cache_fox_v[l], fox_logf=cache_fox_logf[l], lru_h=state_lru_h[l],
                    lru_conv=state_lru_conv[l], mla_ckv=cache_mla_ckv[l], mla_kpe=cache_mla_kpe[l],
                    sb_k=cache_sb_k[l], sb_v=cache_sb_v[l])
        mk = cache_mem_k[l].reshape(bs, mem_len, -1).astype(BF16)
        mv = cache_mem_v[l].reshape(bs, mem_len, -1).astype(BF16)
        y_sample, st = _trunk_layer(y_sample, past, mk, mv, weights[l], cst)
        s_states.append(st)

    def stk(states, name):
        return jnp.stack([s[name] for s in states])

    p_names = ('fox_k', 'fox_v', 'fox_logf', 'lru_h', 'lru_conv', 'mla_ckv', 'mla_kpe', 'sb_k', 'sb_v', 'mem_k', 'mem_v')
    s_names = p_names[:9]
    return ((y_prompt, y_sample) + tuple(stk(p_states, n) for n in p_names)
            + tuple(stk(s_states, n) for n in s_names))
```

```python
import functools
import math

import numpy as np
import jax
import jax.numpy as jnp
from jax import lax
from jax.experimental import pallas as pl
from jax.experimental.pallas import tpu as pltpu

F32 = jnp.float32
BF16 = jnp.bfloat16

CHUNK = 64
HEAD_DIM = 64
N_HEADS = 4
LRU_WIDTH = 256
LRU_CONV = 4
LRU_C = 8.0
MLA_Q_RANK = 256
MLA_KV_RANK = 128
MLA_NOPE = 64
MLA_ROPE = 32
MLA_V = 64
ROPE_BASE = 10000.0
N_BRANCH = 4
BRANCH_WIDTH = 256
MEM_HEAD_DIM = 128
EPS = 1e-6
NEG_INF = -1e30

HP = 128
QW = N_HEADS * HP
VW = N_HEADS * HEAD_DIM
LANE = 128
VMEM_LIMIT_BYTES = 56 * 1024 * 1024
SB_DECAY_LIMIT = 88.0
LOG2E = 1.4426950408889634

_NT = (((1,), (1,)), ((), ()))


def _dot(a, b):
    return jnp.dot(a, b, preferred_element_type=F32)


def _dot_nt(a, b):
    return lax.dot_general(a, b, _NT, preferred_element_type=F32)


def _rms(x, g):
    ms = jnp.mean(x * x, axis=-1, keepdims=True)
    return x * lax.rsqrt(ms + EPS) * g


def _sigmoid(x):
    return 1.0 / (1.0 + jnp.exp(-x))


def _softplus(x):
    return jnp.maximum(x, 0.0) + jnp.log(1.0 + jnp.exp(-jnp.abs(x)))


def _pick(n, cands):
    for c in cands:
        if n % c == 0:
            return c
    return n


def _params(*sem):
    return pltpu.CompilerParams(dimension_semantics=sem, vmem_limit_bytes=VMEM_LIMIT_BYTES)


def _const_spec(shape):
    nd = len(shape)
    return pl.BlockSpec(shape, lambda *_: (0,) * nd)


def _place_qk():
    e = np.zeros((N_HEADS * HEAD_DIM, QW), np.float32)
    for h in range(N_HEADS):
        for j in range(HEAD_DIM):
            e[h * HEAD_DIM + j, h * HP + j] = 1.0
    return e


def _fox_feature_maps():
    fk = np.zeros((3, LANE, QW), np.float32)
    fq = np.zeros((3, LANE, QW), np.float32)
    ones_k = np.zeros((1, QW), np.float32)
    ones_q = np.zeros((1, QW), np.float32)
    for h in range(N_HEADS):
        for part in range(3):
            fq[part, h, h * HP + HEAD_DIM + part] = 1.0
            fk[part, h, h * HP + HEAD_DIM + 3 + part] = -1.0
            ones_k[0, h * HP + HEAD_DIM + part] = 1.0
            ones_q[0, h * HP + HEAD_DIM + 3 + part] = 1.0
    return fk, fq, ones_k, ones_q


def _place_kpe():
    e = np.zeros((LANE, QW), np.float32)
    for h in range(N_HEADS):
        for j in range(MLA_ROPE):
            e[j, h * HP + MLA_NOPE + j] = 1.0
    return e


_W_OFF = dict(fq=0, fk=256, fv=512, lx=768, lg=1024, cq=1280, ckv=1536, kpe=1664, kpe_sw=1792, ff=1920,
              sq=2048, sk=2304, sv=2560)
_W_MIX_COLS = 2816


def _in_proj_kernel(x_ref, g_ref, w_ref, qn_ref, kvn_ref, bf_ref, ck_ref, sk_tab_ref, cqT_ref, sqT_ref,
                    wuqT_ref, wuqswT_ref, eqT_ref,
                    fq_ref, fk_ref, fv_ref, lx_ref, lg_ref, ckv_ref, kpe_ref, logf_ref, mqT_ref, sbqT_ref,
                    sk_ref, sv_ref):
    hb = _rms(x_ref[0], g_ref[...]).astype(BF16)

    def proj(name, width):
        a = _W_OFF[name]
        return _dot(hb, w_ref[:, a:a + width])

    fq_ref[0] = (proj('fq', 256) * (HEAD_DIM ** -0.5 * LOG2E)).astype(BF16)
    fk_ref[0] = proj('fk', 256)
    fv_ref[0] = proj('fv', 256)
    lx_ref[0] = proj('lx', 256)
    lg_ref[0] = proj('lg', 256)

    cqn = _rms(proj('cq', 256), qn_ref[...]).astype(BF16)
    q_t = _dot_nt(wuqT_ref[...], cqn)
    qsw_t = _dot_nt(wuqswT_ref[...], cqn)
    mqT_ref[0] = (q_t * cqT_ref[...] + qsw_t * sqT_ref[...]).astype(BF16)

    ckv_ref[0] = _rms(proj('ckv', 128), kvn_ref[...])
    kpe_ref[0] = proj('kpe', 128) * ck_ref[...] + proj('kpe_sw', 128) * sk_tab_ref[...]

    ff = proj('ff', 128) + bf_ref[...]
    log_sig = jnp.minimum(ff, 0.0) - jnp.log(1.0 + jnp.exp(-jnp.abs(ff)))
    lane = lax.broadcasted_iota(jnp.int32, ff.shape, 1)
    logf_ref[0] = jnp.where(lane < N_HEADS, log_sig, 0.0)

    sq = (proj('sq', 256) * HEAD_DIM ** -0.5).astype(BF16)
    sbqT_ref[0] = _dot_nt(eqT_ref[...], sq).astype(BF16)
    sk_ref[0] = proj('sk', 256)
    sv_ref[0] = proj('sv', 256)


def _in_proj(x, lw, tabs):
    b, t, d = x.shape
    tm = _pick(t, (512, 256, 128))
    grid = (b, t // tm)
    row = lambda w: pl.BlockSpec((1, tm, w), lambda i, j: (i, j, 0))
    colT = pl.BlockSpec((1, QW, tm), lambda i, j: (i, 0, j))
    in_specs = [
        row(d), _const_spec((1, d)), _const_spec((d, _W_MIX_COLS)),
        _const_spec((1, MLA_Q_RANK)), _const_spec((1, MLA_KV_RANK)), _const_spec((1, LANE)),
        pl.BlockSpec((tm, LANE), lambda i, j: (j, 0)), pl.BlockSpec((tm, LANE), lambda i, j: (j, 0)),
        pl.BlockSpec((QW, tm), lambda i, j: (0, j)), pl.BlockSpec((QW, tm), lambda i, j: (0, j)),
        _const_spec((QW, MLA_Q_RANK)), _const_spec((QW, MLA_Q_RANK)), _const_spec((QW, 256)),
    ]
    sds = jax.ShapeDtypeStruct
    out_shape = [
        sds((b, t, 256), BF16),
        sds((b, t, 256), F32), sds((b, t, 256), F32),
        sds((b, t, 256), F32), sds((b, t, 256), F32),
        sds((b, t, 128), F32), sds((b, t, LANE), F32), sds((b, t, LANE), F32),
        sds((b, QW, t), BF16), sds((b, QW, t), BF16),
        sds((b, t, 256), F32), sds((b, t, 256), F32),
    ]
    out_specs = [row(256), row(256), row(256), row(256), row(256), row(128), row(LANE), row(LANE),
                 colT, colT, row(256), row(256)]
    return pl.pallas_call(
        _in_proj_kernel, grid=grid, in_specs=in_specs, out_specs=out_specs, out_shape=out_shape,
        compiler_params=_params("parallel", "parallel"), name="in_proj",
    )(x, lw['ln_mix_pre'], lw['w_mix'], lw['mla_q_norm'], lw['mla_kv_norm'], lw['fox_bf'],
      tabs['ck'], tabs['sk'], tabs['cqT'], tabs['sqT'], lw['wuqT'], lw['wuqswT'], lw['eqT'])


def _cumsum_rows(x):
    n = x.shape[0]
    row = lax.broadcasted_iota(jnp.int32, x.shape, 0)
    d = 1
    while d < n:
        x = x + jnp.where(row >= d, pltpu.roll(x, d, 0), 0.0)
        d *= 2
    return x


def _kv_prep_kernel(fq_ref, fk_ref, fv_ref, logf_ref, ckv_ref, kpe_ref, sk_ref, sv_ref,
                    ek_ref, evT_ref, eqT_ref, fkmap_ref, fqmapT_ref, onesk_ref, onesq_ref,
                    wuk_ref, ekpe_ref, wuvT_ref,
                    fqT_out, fk_out, fvT_out, mk_out, mvT_out, sbk_out, sbvT_out, carry_ref, *, tk):
    @pl.when(pl.program_id(1) == 0)
    def _():
        carry_ref[...] = jnp.zeros_like(carry_ref)

    ts = fk_ref.shape[1]
    c = _cumsum_rows(logf_ref[0]) + carry_ref[...]
    carry_ref[...] = c[ts - 1:ts, :]
    c = c * LOG2E
    c_hi = c.astype(BF16)
    rem = c - c_hi.astype(F32)
    c_mid = rem.astype(BF16)
    c_lo = (rem - c_mid.astype(F32)).astype(BF16)

    def store_vt(out, v_t):
        for j in range(ts // tk):
            out[0, j] = v_t[:, j * tk:(j + 1) * tk]

    fk_out[0] = (_dot(fk_ref[0].astype(BF16), ek_ref[...]) + _dot(c_hi, fkmap_ref[0]) + _dot(c_mid, fkmap_ref[1])
                 + _dot(c_lo, fkmap_ref[2]) + onesk_ref[...]).astype(BF16)
    fqT_out[0] = (_dot_nt(eqT_ref[...], fq_ref[0]) + _dot_nt(fqmapT_ref[0], c_hi) + _dot_nt(fqmapT_ref[1], c_mid)
                  + _dot_nt(fqmapT_ref[2], c_lo) + onesq_ref[...]).astype(BF16)
    store_vt(fvT_out, _dot_nt(evT_ref[...], fv_ref[0].astype(BF16)).astype(BF16))

    ckv = ckv_ref[0].astype(BF16)
    mk_out[0] = (_dot(ckv, wuk_ref[...]) + _dot(kpe_ref[0].astype(BF16), ekpe_ref[...])).astype(BF16)
    store_vt(mvT_out, _dot_nt(wuvT_ref[...], ckv).astype(BF16))

    sbk_out[0] = _dot(sk_ref[0].astype(BF16), ek_ref[...]).astype(BF16)
    store_vt(sbvT_out, _dot_nt(evT_ref[...], sv_ref[0].astype(BF16)).astype(BF16))


def _kv_prep(fq, fk, fv, logf, ckv, kpe, sk, sv, lw, cst, tk):
    b, s, _ = fk.shape
    ts = _pick(s, (1024, 512, 256))
    grid = (b, s // ts)
    row = lambda w: pl.BlockSpec((1, ts, w), lambda i, j: (i, j, 0))
    colT = pl.BlockSpec((1, QW, ts), lambda i, j: (i, 0, j))
    vT = pl.BlockSpec((1, ts // tk, VW, tk), lambda i, j: (i, j, 0, 0))
    in_specs = [row(256), row(256), row(256), row(LANE), row(128), row(LANE), row(256), row(256),
                _const_spec((256, QW)), _const_spec((VW, VW)), _const_spec((QW, 256)),
                _const_spec((3, LANE, QW)), _const_spec((3, QW, LANE)), _const_spec((1, QW)), _const_spec((QW, 1)),
                _const_spec((MLA_KV_RANK, QW)), _const_spec((LANE, QW)), _const_spec((VW, MLA_KV_RANK))]
    sds = jax.ShapeDtypeStruct
    slab = sds((b, s, QW), BF16)
    slab_t = sds((b, s // tk, VW, tk), BF16)
    out_shape = [sds((b, QW, s), BF16), slab, slab_t, slab, slab_t, slab, slab_t]
    out_specs = [colT, row(QW), vT, row(QW), vT, row(QW), vT]
    return pl.pallas_call(
        functools.partial(_kv_prep_kernel, tk=tk), grid=grid, in_specs=in_specs, out_specs=out_specs,
        out_shape=out_shape, scratch_shapes=[pltpu.VMEM((1, LANE), F32)],
        compiler_params=_params("parallel", "arbitrary"), name="kv_prep",
    )(fq, fk, fv, logf, ckv, kpe, sk, sv, cst['ek'], cst['evT'], lw['eqT'], cst['fkmap'], cst['fqmapT'],
      cst['ones_k'], cst['ones_qT'], lw['wuk'], cst['ekpe'], lw['wuvT'])


def _head_rows(h):
    return slice(h * HP, (h + 1) * HP)


def _value_rows(h):
    return slice(h * HEAD_DIM, (h + 1) * HEAD_DIM)


def _load_kv(k_ref, vT_ref, kb, tk, h):
    off = pl.multiple_of(kb * tk, tk)
    return k_ref[0, pl.ds(off, tk), _head_rows(h)], vT_ref[0, kb, _value_rows(h), :]


def _finish(accs, invs, eye_ref, o_ref):
    for pair in range(N_HEADS // 2):
        o_t = jnp.concatenate([accs[2 * pair] * invs[2 * pair], accs[2 * pair + 1] * invs[2 * pair + 1]], axis=0)
        o_ref[0, :, _head_rows(pair)] = _dot_nt(eye_ref[...], o_t.astype(BF16)).astype(BF16)


def _softmax_attn_kernel(qT_ref, k_ref, vT_ref, eye_ref, o_ref, *, mode, past, tq, tk, s_valid):
    qi = pl.program_id(1)
    ltk = int(math.log2(tk))
    q_lo = past + qi * tq
    q_hi = q_lo + (tq - 1)
    if mode == 'fox':
        n_full = (q_lo + 1) >> ltk
        n_blk = (q_hi >> ltk) + 1
    else:
        lim_lo = ((q_lo // CHUNK) + 1) * CHUNK
        lim_hi = jnp.minimum(((q_hi // CHUNK) + 1) * CHUNK, s_valid)
        n_blk = (lim_hi + (tk - 1)) >> ltk
        n_full = jnp.minimum(lim_lo >> ltk, n_blk)
    qpos = q_lo + lax.broadcasted_iota(jnp.int32, (1, tq), 1)

    def step(kb, carry, masked):
        off = pl.multiple_of(kb * tk, tk)
        scores = [_dot(k_ref[0, pl.ds(off, tk), _head_rows(h)], qT_ref[0, _head_rows(h), :]) for h in range(N_HEADS)]
        if masked:
            kpos = kb * tk + lax.broadcasted_iota(jnp.int32, (tk, 1), 0)
            if mode == 'fox':
                vis = kpos <= qpos
            else:
                vis = jnp.logical_and((kpos // CHUNK) <= (qpos // CHUNK), kpos < s_valid)
        out = []
        for h in range(N_HEADS):
            m, l, acc = carry[h]
            s = jnp.where(vis, scores[h], NEG_INF) if masked else scores[h]
            m_new = jnp.maximum(m, jnp.max(s, axis=0, keepdims=True))
            p = jnp.exp2(s - m_new)
            alpha = jnp.exp2(m - m_new)
            l = alpha * l + jnp.sum(p, axis=0, keepdims=True)
            acc = alpha * acc + _dot(vT_ref[0, kb, _value_rows(h), :], p.astype(BF16))
            out.append((m_new, l, acc))
        return tuple(out)

    init = (jnp.full((1, tq), NEG_INF, F32), jnp.zeros((1, tq), F32), jnp.zeros((HEAD_DIM, tq), F32))
    carry = lax.fori_loop(0, n_full, functools.partial(step, masked=False), (init,) * N_HEADS)
    carry = lax.fori_loop(n_full, n_blk, functools.partial(step, masked=True), carry)
    _finish([c[2] for c in carry], [1.0 / c[1] for c in carry], eye_ref, o_ref)


def _stick_attn_kernel(qT_ref, k_ref, vT_ref, eye_ref, later_ref, o_ref, *, past, tq, tk):
    qi = pl.program_id(1)
    ltk = int(math.log2(tk))
    q_lo = past + qi * tq
    q_hi = q_lo + (tq - 1)
    n_full = q_lo >> ltk
    n_blk = ((q_hi - 1) >> ltk) + 1
    qpos = q_lo + lax.broadcasted_iota(jnp.int32, (1, tq), 1)

    def step(kb, carry, masked):
        kvs = [_load_kv(k_ref, vT_ref, kb, tk, h) for h in range(N_HEADS)]
        zs = [_dot(kvs[h][0], qT_ref[0, _head_rows(h), :]) for h in range(N_HEADS)]
        if masked:
            kpos = kb * tk + lax.broadcasted_iota(jnp.int32, (tk, 1), 0)
            vis = kpos < qpos
        out = []
        for h in range(N_HEADS):
            decay, acc = carry[h]
            z = zs[h]
            drop = _softplus(z)
            if masked:
                drop = jnp.where(vis, drop, 0.0)
            hi = drop.astype(BF16)
            lo = (drop - hi.astype(F32)).astype(BF16)
            later = _dot(later_ref[...], hi) + _dot(later_ref[...], lo)
            a = jnp.exp(z - drop - later - decay)
            if masked:
                a = jnp.where(vis, a, 0.0)
            acc = acc + _dot(kvs[h][1], a.astype(BF16))
            out.append((decay + jnp.sum(drop, axis=0, keepdims=True), acc))
        return tuple(out)

    def min_decay(carry):
        d = carry[0][0]
        for h in range(1, N_HEADS):
            d = jnp.minimum(d, carry[h][0])
        return jnp.min(d)

    init = (jnp.zeros((1, tq), F32), jnp.zeros((HEAD_DIM, tq), F32))
    carry = lax.fori_loop(0, n_blk - n_full, lambda i, c: step(n_blk - 1 - i, c, True), (init,) * N_HEADS)

    def cond(state):
        kb, dmin, _ = state
        return jnp.logical_and(kb >= 0, dmin < SB_DECAY_LIMIT)

    def body(state):
        kb, _, carry = state
        carry = step(kb, carry, False)
        return kb - 1, min_decay(carry), carry

    _, _, carry = lax.while_loop(cond, body, (n_full - 1, min_decay(carry), carry))
    _finish([c[1] for c in carry], [1.0] * N_HEADS, eye_ref, o_ref)


def _attention(mode, q_t, k, v_t, cst, *, past, s_valid, tq, tk):
    b, _, t = q_t.shape
    s = k.shape[1]
    grid = (b, t // tq)
    single = pl.Buffered(1)
    in_specs = [pl.BlockSpec((1, QW, tq), lambda i, j: (i, 0, j)),
                pl.BlockSpec((1, s, QW), lambda i, j: (i, 0, 0), pipeline_mode=single),
                pl.BlockSpec((1, s // tk, VW, tk), lambda i, j: (i, 0, 0, 0), pipeline_mode=single),
                _const_spec((tq, tq))]
    args = [q_t, k, v_t, jnp.eye(tq, dtype=BF16)]
    if mode == 'sb':
        kern = functools.partial(_stick_attn_kernel, past=past, tq=tq, tk=tk)
        in_specs.append(_const_spec((tk, tk)))
        args.append(cst['later'][tk])
    else:
        kern = functools.partial(_softmax_attn_kernel, mode=mode, past=past, tq=tq, tk=tk, s_valid=s_valid)
    width = (N_HEADS // 2) * HP
    return pl.pallas_call(
        kern, grid=grid, in_specs=in_specs,
        out_specs=pl.BlockSpec((1, tq, width), lambda i, j: (i, j, 0)),
        out_shape=jax.ShapeDtypeStruct((b, t, width), BF16),
        compiler_params=_params("parallel", "parallel"), name="attn_" + mode,
    )(*args)


def _lru_kernel(lx_ref, lg_ref, cb_ref, h0_ref, cw_ref, cbias_ref, wr_ref, br_ref, wi_ref, bi_ref, lam_ref,
                y_ref, nb_ref, hl_ref, xcat_ref, h_ref, *, past, tc):
    ci = pl.program_id(1)
    keep = LRU_CONV - 1

    @pl.when(ci == 0)
    def _():
        xcat_ref[0:8, :] = jnp.zeros((8, LRU_WIDTH), F32)
        xcat_ref[8 - keep:8, :] = cb_ref[0]
        h_ref[...] = h0_ref[0]

    x = lx_ref[0]
    xcat_ref[8:8 + tc, :] = x
    xc = cbias_ref[...] + x * cw_ref[keep:keep + 1, :]
    for tap in range(keep):
        shift = keep - tap
        xc = xc + xcat_ref[8 - shift:8 - shift + tc, :] * cw_ref[tap:tap + 1, :]
    nb_ref[0] = xcat_ref[8 + tc - keep:8 + tc, :]
    xcat_ref[0:8, :] = x[tc - 8:tc, :]

    xcb = xc.astype(BF16)
    r = _sigmoid(_dot(xcb, wr_ref[...]) + br_ref[...])
    gate_in = _sigmoid(_dot(xcb, wi_ref[...]) + bi_ref[...])
    log_a = (-LRU_C) * r * _softplus(-lam_ref[...])
    row = lax.broadcasted_iota(jnp.int32, (tc, LRU_WIDTH), 0)
    reset = (past + ci * tc + row) == 0
    a = jnp.where(reset, 0.0, jnp.exp(log_a))
    y2 = 2.0 * log_a
    series = -y2 * (1.0 + y2 * (0.5 + y2 * (1.0 / 6.0 + y2 * (1.0 / 24.0 + y2 * (1.0 / 120.0)))))
    one_minus = jnp.where(y2 > -0.05, series, 1.0 - jnp.exp(y2))
    mult = jnp.where(reset, 1.0, jnp.sqrt(one_minus))
    u = mult * gate_in * xc

    d = 1
    while d < tc:
        ok = row >= d
        u = u + jnp.where(ok, a * pltpu.roll(u, d, 0), 0.0)
        a = jnp.where(ok, a * pltpu.roll(a, d, 0), a)
        d *= 2
    hs = a * h_ref[...] + u
    h_last = hs[tc - 1:tc, :]
    h_ref[...] = h_last
    hl_ref[0] = h_last

    g = lg_ref[0]
    gelu = 0.5 * g * (1.0 + jnp.tanh(0.7978845608028654 * (g + 0.044715 * g * g * g)))
    y_ref[0] = (hs * gelu).astype(BF16)


def _lru(lx, lg, conv_buf, h0, lw, past):
    b, t, w = lx.shape
    tc = _pick(t, (256, 128, 64, 32))
    grid = (b, t // tc)
    row = pl.BlockSpec((1, tc, w), lambda i, j: (i, j, 0))
    keep = LRU_CONV - 1
    in_specs = [row, row, pl.BlockSpec((1, keep, w), lambda i, j: (i, 0, 0)), pl.BlockSpec((1, 1, w), lambda i, j: (i, 0, 0)),
                _const_spec((LRU_CONV, w)), _const_spec((1, w)), _const_spec((w, w)), _const_spec((1, w)),
                _const_spec((w, w)), _const_spec((1, w)), _const_spec((1, w))]
    sds = jax.ShapeDtypeStruct
    return pl.pallas_call(
        functools.partial(_lru_kernel, past=past, tc=tc), grid=grid, in_specs=in_specs,
        out_specs=[row, pl.BlockSpec((1, keep, w), lambda i, j: (i, 0, 0)), pl.BlockSpec((1, 1, w), lambda i, j: (i, 0, 0))],
        out_shape=[sds((b, t, w), BF16), sds((b, keep, w), F32), sds((b, 1, w), F32)],
        scratch_shapes=[pltpu.VMEM((tc + 8, w), F32), pltpu.VMEM((1, w), F32)],
        compiler_params=_params("parallel", "arbitrary"), name="rg_lru",
    )(lx, lg, conv_buf, h0, lw['lru_conv_w'], lw['lru_conv_b'], lw['lru_wr'], lw['lru_br'], lw['lru_wi'],
      lw['lru_bi'], lw['lru_lam'])


def _merge_kernel(x_ref, oa_ref, ob_ref, oc_ref, od_ref, g1_ref, g2_ref, wg_ref, wb_ref, wo_ref, out_ref):
    x = x_ref[0]
    d = x.shape[1]
    hb = _rms(x, g1_ref[...]).astype(BF16)
    branches = (oa_ref[0], ob_ref[0], oc_ref[0], od_ref[0])
    merged = None
    for n, o in enumerate(branches):
        gate = _sigmoid(_dot(hb, wg_ref[:, n * d:(n + 1) * d]))
        term = gate * _dot(o, wb_ref[n])
        merged = term if merged is None else merged + term
    y = _dot(merged.astype(BF16), wo_ref[...])
    out_ref[0] = x + _rms(y, g2_ref[...])


def _merge(x, o_a, o_b, o_c, o_d, lw):
    b, t, d = x.shape
    tm = _pick(t, (256, 128))
    grid = (b, t // tm)
    row = lambda w: pl.BlockSpec((1, tm, w), lambda i, j: (i, j, 0))
    bw = BRANCH_WIDTH
    in_specs = [row(d), row(bw), row(bw), row(bw), row(bw), _const_spec((1, d)), _const_spec((1, d)),
                _const_spec((d, N_BRANCH * d)), _const_spec((N_BRANCH, BRANCH_WIDTH, d)), _const_spec((d, d))]
    return pl.pallas_call(
        _merge_kernel, grid=grid, in_specs=in_specs, out_specs=row(d),
        out_shape=jax.ShapeDtypeStruct((b, t, d), F32),
        compiler_params=_params("parallel", "parallel"), name="merge",
    )(x, o_a, o_b, o_c, o_d, lw['ln_mix_pre'], lw['ln_mix_post'], lw['w_gate'], lw['w_branch'], lw['w_out'])


def _mem_kv_kernel(mem_ref, g_ref, wk_ref, wv_ref, k_ref, v_ref):
    mn = _rms(mem_ref[0], g_ref[...]).astype(BF16)
    k_ref[0] = _dot(mn, wk_ref[...])
    v_ref[0] = _dot(mn, wv_ref[...])


def _mem_kv(mem, lw):
    b, m, d = mem.shape
    w = lw['mem_wk'].shape[1]
    out = jax.ShapeDtypeStruct((b, m, w), F32)
    blk = pl.BlockSpec((1, m, w), lambda i: (i, 0, 0))
    return pl.pallas_call(
        _mem_kv_kernel, grid=(b,),
        in_specs=[pl.BlockSpec((1, m, d), lambda i: (i, 0, 0)), _const_spec((1, d)), _const_spec((d, w)), _const_spec((d, w))],
        out_specs=[blk, blk], out_shape=[out, out], compiler_params=_params("parallel"), name="mem_kv",
    )(mem, lw['mem_norm'], lw['mem_wk'], lw['mem_wv'])


def _mem_attn_kernel(x_ref, mk_ref, mv_ref, g1_ref, g2_ref, wq_ref, wo_ref, out_ref):
    x = x_ref[0]
    hb = _rms(x, g1_ref[...]).astype(BF16)
    q = (_dot(hb, wq_ref[...]) * (MEM_HEAD_DIM ** -0.5)).astype(BF16)
    heads = []
    for h in range(N_HEADS):
        sl = slice(h * MEM_HEAD_DIM, (h + 1) * MEM_HEAD_DIM)
        s = _dot_nt(q[:, sl], mk_ref[0, :, sl])
        p = jnp.exp(s - jnp.max(s, axis=-1, keepdims=True))
        inv = 1.0 / jnp.sum(p, axis=-1, keepdims=True)
        heads.append((_dot(p.astype(BF16), mv_ref[0, :, sl]) * inv).astype(BF16))
    y = _dot(jnp.concatenate(heads, axis=1), wo_ref[...])
    out_ref[0] = x + _rms(y, g2_ref[...])


def _mem_attn(x, mk, mv, lw):
    b, t, d = x.shape
    m, w = mk.shape[1], mk.shape[2]
    tm = _pick(t, (512, 256, 128))
    row = pl.BlockSpec((1, tm, d), lambda i, j: (i, j, 0))
    kv = pl.BlockSpec((1, m, w), lambda i, j: (i, 0, 0))
    return pl.pallas_call(
        _mem_attn_kernel, grid=(b, t // tm),
        in_specs=[row, kv, kv, _const_spec((1, d)), _const_spec((1, d)), _const_spec((d, w)), _const_spec((w, d))],
        out_specs=row, out_shape=jax.ShapeDtypeStruct((b, t, d), F32),
        compiler_params=_params("parallel", "parallel"), name="mem_attn",
    )(x, mk, mv, lw['ln_mem_pre'], lw['ln_mem_post'], lw['mem_wq'], lw['mem_wo'])


def _ffn_kernel(x_ref, g1_ref, g2_ref, wg_ref, wu_ref, wd_ref, out_ref):
    x = x_ref[0]
    hb = _rms(x, g1_ref[...]).astype(BF16)
    gate = _dot(hb, wg_ref[...])
    act = (gate * _sigmoid(gate) * _dot(hb, wu_ref[...])).astype(BF16)
    y = _dot(act, wd_ref[...])
    out_ref[0] = x + _rms(y, g2_ref[...])


def _ffn(x, lw):
    b, t, d = x.shape
    f = lw['ffn_wg'].shape[1]
    tm = _pick(t, (256, 128))
    row = pl.BlockSpec((1, tm, d), lambda i, j: (i, j, 0))
    single = pl.Buffered(1)
    return pl.pallas_call(
        _ffn_kernel, grid=(b, t // tm),
        in_specs=[row, _const_spec((1, d)), _const_spec((1, d)),
                  pl.BlockSpec((d, f), lambda i, j: (0, 0), pipeline_mode=single),
                  pl.BlockSpec((d, f), lambda i, j: (0, 0), pipeline_mode=single),
                  pl.BlockSpec((f, d), lambda i, j: (0, 0), pipeline_mode=single)],
        out_specs=row, out_shape=jax.ShapeDtypeStruct((b, t, d), F32),
        compiler_params=_params("parallel", "parallel"), name="ffn",
    )(x, lw['ln_ffn_pre'], lw['ln_ffn_post'], lw['ffn_wg'], lw['ffn_wu'], lw['ffn_wd'])


def _constants():
    fk, fq, ones_k, ones_q = _fox_feature_maps()
    later = {}
    for tk in (256,):
        idx = np.arange(tk)
        later[tk] = jnp.asarray((idx[None, :] > idx[:, None]).astype(np.float32), BF16)
    return dict(
        ek=jnp.asarray(_place_qk(), BF16), evT=jnp.eye(VW, dtype=BF16),
        fkmap=jnp.asarray(fk, BF16), fqmapT=jnp.asarray(np.transpose(fq, (0, 2, 1)), BF16),
        ones_k=jnp.asarray(ones_k, F32), ones_qT=jnp.asarray(ones_q.T, F32),
        ekpe=jnp.asarray(_place_kpe(), BF16), later=later)


def _pad_cols(w, n):
    return jnp.pad(w, ((0, 0), (0, n - w.shape[1])))


def _layer_weights(l, p):
    d = p['w_in'].shape[1]
    w_in = p['w_in'][l]
    sizes = (256, 256, 256, N_HEADS, LRU_WIDTH, LRU_WIDTH, MLA_Q_RANK, MLA_KV_RANK, MLA_ROPE, 256, 256, 256)
    offs = np.concatenate([[0], np.cumsum(sizes)])
    fq, fk, fv, ff, lx, lg, cq, ckv, kpe, sq, sk, sv = [w_in[:, offs[i]:offs[i + 1]] for i in range(len(sizes))]
    half = MLA_ROPE // 2
    kpe_sw = jnp.concatenate([kpe[:, half:], kpe[:, :half]], axis=1)
    w_mix = jnp.concatenate([fq, fk, fv, lx, lg, cq, ckv, _pad_cols(kpe, LANE), _pad_cols(kpe_sw, LANE),
                             _pad_cols(ff, LANE), sq, sk, sv], axis=1).astype(BF16)

    qk = MLA_NOPE + MLA_ROPE
    wuq = p['mla_w_uq'][l].reshape(MLA_Q_RANK, N_HEADS, qk)
    rope = wuq[:, :, MLA_NOPE:]
    rope_sw = jnp.concatenate([rope[:, :, half:], rope[:, :, :half]], axis=2)
    wuq_p = jnp.pad(wuq, ((0, 0), (0, 0), (0, HP - qk))).reshape(MLA_Q_RANK, QW)
    wuqsw_p = jnp.pad(rope_sw, ((0, 0), (0, 0), (MLA_NOPE, HP - qk))).reshape(MLA_Q_RANK, QW)
    wuk = p['mla_w_uk'][l].reshape(MLA_KV_RANK, N_HEADS, MLA_NOPE)
    wuk_p = jnp.pad(wuk, ((0, 0), (0, 0), (0, HP - MLA_NOPE))).reshape(MLA_KV_RANK, QW)

    def block_diag(w):
        return jax.scipy.linalg.block_diag(*[w[i] for i in range(w.shape[0])]).astype(BF16)

    row = lambda v: v[l].reshape(1, -1).astype(F32)
    return dict(
        w_mix=w_mix, w_gate=w_in[:, offs[-1]:].astype(BF16),
        ln_mix_pre=row(p['ln_mix_pre']), ln_mix_post=row(p['ln_mix_post']),
        fox_bf=_pad_cols(row(p['fox_bf']), LANE),
        mla_q_norm=row(p['mla_q_norm']), mla_kv_norm=row(p['mla_kv_norm']),
        wuqT=wuq_p.T.astype(BF16), wuqswT=wuqsw_p.T.astype(BF16), wuk=wuk_p.astype(BF16), wuvT=p['mla_w_uv'][l].T.astype(BF16),
        eqT=jnp.asarray(_place_qk().T, BF16),
        lru_conv_w=p['lru_conv_w'][l].astype(F32), lru_conv_b=row(p['lru_conv_b']),
        lru_wr=block_diag(p['lru_wr'][l]), lru_br=row(p['lru_br']),
        lru_wi=block_diag(p['lru_wi'][l]), lru_bi=row(p['lru_bi']), lru_lam=row(p['lru_lam']),
        w_branch=p['w_branch'][l].astype(BF16), w_out=p['w_out'][l].astype(BF16),
        ln_mem_pre=row(p['ln_mem_pre']), ln_mem_post=row(p['ln_mem_post']), mem_norm=row(p['mem_norm']),
        mem_wq=p['mem_wq'][l].astype(BF16), mem_wk=p['mem_wk'][l].astype(BF16),
        mem_wv=p['mem_wv'][l].astype(BF16), mem_wo=p['mem_wo'][l].astype(BF16),
        ln_ffn_pre=row(p['ln_ffn_pre']), ln_ffn_post=row(p['ln_ffn_post']),
        ffn_wg=p['ffn_wg'][l].astype(BF16), ffn_wu=p['ffn_wu'][l].astype(BF16), ffn_wd=p['ffn_wd'][l].astype(BF16))


def _rope_tables(past, t):
    half = MLA_ROPE // 2
    inv = jnp.power(ROPE_BASE, -jnp.arange(half, dtype=F32) / half)
    ang = (past + jnp.arange(t, dtype=jnp.int32)).astype(F32)[:, None] * inv
    cos, sin = jnp.cos(ang), jnp.sin(ang)
    c32 = jnp.concatenate([cos, cos], axis=1)
    s32 = jnp.concatenate([-sin, sin], axis=1)
    scale = (MLA_NOPE + MLA_ROPE) ** -0.5 * LOG2E
    slot_c = jnp.concatenate([jnp.ones((t, MLA_NOPE), F32), c32, jnp.zeros((t, HP - MLA_NOPE - MLA_ROPE), F32)], axis=1)
    slot_s = jnp.concatenate([jnp.zeros((t, MLA_NOPE), F32), s32, jnp.zeros((t, HP - MLA_NOPE - MLA_ROPE), F32)], axis=1)
    return dict(ck=_pad_cols(c32, LANE), sk=_pad_cols(s32, LANE),
                cqT=(jnp.tile(slot_c, (1, N_HEADS)) * scale).T, sqT=(jnp.tile(slot_s, (1, N_HEADS)) * scale).T)


def _trunk_layer(x, past_state, mem_k, mem_v, lw, cst):
    b, t, _ = x.shape
    past = 0 if past_state is None else past_state['fox_k'].shape[1]
    tk = 256
    fq, fk, fv, lx, lg, ckv_n, kpe_slab, logf_slab, mla_qT, sb_qT, sk, sv = _in_proj(x, lw, _rope_tables(past, t))

    if past_state is None:
        s_valid = t
        keys = (fq, fk, fv, logf_slab, ckv_n, kpe_slab, sk, sv)
        conv_buf = jnp.zeros((b, LRU_CONV - 1, LRU_WIDTH), F32)
        h0 = jnp.zeros((b, 1, LRU_WIDTH), F32)
    else:
        s_valid = past + t
        s_pad = -(-s_valid // tk) * tk

        def cat(old, new):
            old = old.reshape(b, past, -1).astype(new.dtype)
            old = jnp.pad(old, ((0, 0), (0, 0), (0, new.shape[2] - old.shape[2])))
            return jnp.pad(jnp.concatenate([old, new], axis=1), ((0, 0), (0, s_pad - s_valid), (0, 0)))

        keys = (cat(jnp.zeros((b, past, 256), BF16), fq), cat(past_state['fox_k'], fk), cat(past_state['fox_v'], fv),
                cat(past_state['fox_logf'], logf_slab), cat(past_state['mla_ckv'], ckv_n),
                cat(past_state['mla_kpe'], kpe_slab), cat(past_state['sb_k'], sk), cat(past_state['sb_v'], sv))
        conv_buf = past_state['lru_conv'].astype(F32)
        h0 = past_state['lru_h'].reshape(b, 1, LRU_WIDTH).astype(F32)

    fox_qT, fox_k, fox_vT, mla_k, mla_vT, sb_k, sb_vT = _kv_prep(*keys, lw, cst, tk)
    if past_state is not None:
        fox_qT = fox_qT[:, :, past:past + t]

    tq = _pick(t, (256, 128))
    att = functools.partial(_attention, cst=cst, past=past, s_valid=s_valid, tq=tq, tk=tk)
    o_a = att('fox', fox_qT, fox_k, fox_vT)
    o_c = att('mla', mla_qT, mla_k, mla_vT)
    o_d = att('sb', sb_qT, sb_k, sb_vT)
    o_b, lru_conv, lru_h = _lru(lx, lg, conv_buf, h0, lw, past)

    x = _merge(x, o_a, o_b, o_c, o_d, lw)
    x = _mem_attn(x, mem_k, mem_v, lw)
    x = _ffn(x, lw)

    new = dict(fox_k=fk.reshape(b, t, N_HEADS, HEAD_DIM), fox_v=fv.reshape(b, t, N_HEADS, HEAD_DIM),
               fox_logf=logf_slab[:, :, :N_HEADS], lru_h=lru_h.reshape(b, LRU_WIDTH), lru_conv=lru_conv,
               mla_ckv=ckv_n, mla_kpe=kpe_slab[:, :, :MLA_ROPE],
               sb_k=sk.reshape(b, t, N_HEADS, HEAD_DIM), sb_v=sv.reshape(b, t, N_HEADS, HEAD_DIM))
    return x, new


def kernel(x_prompt, x_sample, cache_fox_k, cache_fox_v, cache_fox_logf, state_lru_h, state_lru_conv, cache_mla_ckv, cache_mla_kpe, cache_sb_k, cache_sb_v, cache_mem_k, cache_mem_v, mem_prompt, ln_mix_pre, ln_mix_post, w_in, fox_bf, lru_conv_w, lru_conv_b, lru_wr, lru_br, lru_wi, lru_bi, lru_lam, mla_q_norm, mla_w_uq, mla_kv_norm, mla_w_uk, mla_w_uv, w_branch, w_out, ln_mem_pre, ln_mem_post, mem_norm, mem_wq, mem_wk, mem_wv, mem_wo, ln_ffn_pre, ln_ffn_post, ffn_wg, ffn_wu, ffn_wd):
    params = dict(ln_mix_pre=ln_mix_pre, ln_mix_post=ln_mix_post, w_in=w_in, fox_bf=fox_bf, lru_conv_w=lru_conv_w,
                  lru_conv_b=lru_conv_b, lru_wr=lru_wr, lru_br=lru_br, lru_wi=lru_wi, lru_bi=lru_bi, lru_lam=lru_lam,
                  mla_q_norm=mla_q_norm, mla_w_uq=mla_w_uq, mla_kv_norm=mla_kv_norm, mla_w_uk=mla_w_uk,
                  mla_w_uv=mla_w_uv, w_branch=w_branch, w_out=w_out, ln_mem_pre=ln_mem_pre, ln_mem_post=ln_mem_post,
                  mem_norm=mem_norm, mem_wq=mem_wq, mem_wk=mem_wk, mem_wv=mem_wv, mem_wo=mem_wo,
                  ln_ffn_pre=ln_ffn_pre, ln_ffn_post=ln_ffn_post, ffn_wg=ffn_wg, ffn_wu=ffn_wu, ffn_wd=ffn_wd)
    depth = w_in.shape[0]
    cst = _constants()
    weights = [_layer_weights(l, params) for l in range(depth)]
    bp, mem_len = mem_prompt.shape[0], mem_prompt.shape[1]

    y_prompt, p_states = x_prompt, []
    for l in range(depth):
        mk, mv = _mem_kv(mem_prompt, weights[l])
        y_prompt, st = _trunk_layer(y_prompt, None, mk.astype(BF16), mv.astype(BF16), weights[l], cst)
        st['mem_k'] = mk.reshape(bp, mem_len, N_HEADS, MEM_HEAD_DIM)
        st['mem_v'] = mv.reshape(bp, mem_len, N_HEADS, MEM_HEAD_DIM)
        p_states.append(st)

    y_sample, s_states = x_sample, []
    bs = x_sample.shape[0]
    for l in range(depth):
        past = dict(fox_k=cache_fox_k[l], fox_v=cache_fox_v[l], fox_logf=cache_fox_logf[l], lru_h=state_lru_h[l],
                    lru_conv=state_lru_conv[l], mla_ckv=cache_mla_ckv[l], mla_kpe=cache_mla_kpe[l],
                    sb_k=cache_sb_k[l], sb_v=cache_sb_v[l])
        mk = cache_mem_k[l].reshape(bs, mem_len, -1).astype(BF16)
        mv = cache_mem_v[l].reshape(bs, mem_len, -1).astype(BF16)
        y_sample, st = _trunk_layer(y_sample, past, mk, mv, weights[l], cst)
        s_states.append(st)

    def stk(states, name):
        return jnp.stack([s[name] for s in states])

    p_names = ('fox_k', 'fox_v', 'fox_logf', 'lru_h', 'lru_conv', 'mla_ckv', 'mla_kpe', 'sb_k', 'sb_v', 'mem_k', 'mem_v')
    s_names = p_names[:9]
    return ((y_prompt, y_sample) + tuple(stk(p_states, n) for n in p_names)
            + tuple(stk(s_states, n) for n in s_names))
```

```python
import functools
import math

import numpy as np
import jax
import jax.numpy as jnp
from jax import lax
from jax.experimental import pallas as pl
from jax.experimental.pallas import tpu as pltpu

F32 = jnp.float32
BF16 = jnp.bfloat16

CHUNK = 64
HEAD_DIM = 64
N_HEADS = 4
LRU_WIDTH = 256
LRU_CONV = 4
LRU_C = 8.0
MLA_Q_RANK = 256
MLA_KV_RANK = 128
MLA_NOPE = 64
MLA_ROPE = 32
MLA_V = 64
ROPE_BASE = 10000.0
N_BRANCH = 4
BRANCH_WIDTH = 256
MEM_HEAD_DIM = 128
EPS = 1e-6
NEG_INF = -1e30

HP = 128
QW = N_HEADS * HP
VW = N_HEADS * HEAD_DIM
LANE = 128
VMEM_LIMIT_BYTES = 56 * 1024 * 1024
SB_DECAY_LIMIT = 88.0
LOG2E = 1.4426950408889634

_NT = (((1,), (1,)), ((), ()))


def _dot(a, b):
    return jnp.dot(a, b, preferred_element_type=F32)


def _dot_nt(a, b):
    return lax.dot_general(a, b, _NT, preferred_element_type=F32)


def _rms(x, g):
    ms = jnp.mean(x * x, axis=-1, keepdims=True)
    return x * lax.rsqrt(ms + EPS) * g


def _sigmoid(x):
    return 1.0 / (1.0 + jnp.exp(-x))


def _softplus(x):
    return jnp.maximum(x, 0.0) + jnp.log(1.0 + jnp.exp(-jnp.abs(x)))


def _pick(n, cands):
    for c in cands:
        if n % c == 0:
            return c
    return n


def _params(*sem):
    return pltpu.CompilerParams(dimension_semantics=sem, vmem_limit_bytes=VMEM_LIMIT_BYTES)


def _const_spec(shape):
    nd = len(shape)
    return pl.BlockSpec(shape, lambda *_: (0,) * nd)


def _place_qk():
    e = np.zeros((N_HEADS * HEAD_DIM, QW), np.float32)
    for h in range(N_HEADS):
        for j in range(HEAD_DIM):
            e[h * HEAD_DIM + j, h * HP + j] = 1.0
    return e


def _fox_feature_maps():
    fk = np.zeros((3, LANE, QW), np.float32)
    fq = np.zeros((3, LANE, QW), np.float32)
    ones_k = np.zeros((1, QW), np.float32)
    ones_q = np.zeros((1, QW), np.float32)
    for h in range(N_HEADS):
        for part in range(3):
            fq[part, h, h * HP + HEAD_DIM + part] = 1.0
            fk[part, h, h * HP + HEAD_DIM + 3 + part] = -1.0
            ones_k[0, h * HP + HEAD_DIM + part] = 1.0
            ones_q[0, h * HP + HEAD_DIM + 3 + part] = 1.0
    return fk, fq, ones_k, ones_q


def _place_kpe():
    e = np.zeros((LANE, QW), np.float32)
    for h in range(N_HEADS):
        for j in range(MLA_ROPE):
            e[j, h * HP + MLA_NOPE + j] = 1.0
    return e


_W_OFF = dict(fq=0, fk=256, fv=512, lx=768, lg=1024, cq=1280, ckv=1536, kpe=1664, kpe_sw=1792, ff=1920,
              sq=2048, sk=2304, sv=2560)
_W_MIX_COLS = 2816


def _in_proj_kernel(x_ref, g_ref, w_ref, qn_ref, kvn_ref, bf_ref, ck_ref, sk_tab_ref, cqT_ref, sqT_ref,
                    wuqT_ref, wuqswT_ref, eqT_ref,
                    fq_ref, fk_ref, fv_ref, lx_ref, lg_ref, ckv_ref, kpe_ref, logf_ref, mqT_ref, sbqT_ref,
                    sk_ref, sv_ref):
    hb = _rms(x_ref[0], g_ref[...]).astype(BF16)

    def proj(name, width):
        a = _W_OFF[name]
        return _dot(hb, w_ref[:, a:a + width])

    fq_ref[0] = (proj('fq', 256) * (HEAD_DIM ** -0.5 * LOG2E)).astype(BF16)
    fk_ref[0] = proj('fk', 256)
    fv_ref[0] = proj('fv', 256)
    lx_ref[0] = proj('lx', 256)
    lg_ref[0] = proj('lg', 256)

    cqn = _rms(proj('cq', 256), qn_ref[...]).astype(BF16)
    q_t = _dot_nt(wuqT_ref[...], cqn)
    qsw_t = _dot_nt(wuqswT_ref[...], cqn)
    mqT_ref[0] = (q_t * cqT_ref[...] + qsw_t * sqT_ref[...]).astype(BF16)

    ckv_ref[0] = _rms(proj('ckv', 128), kvn_ref[...])
    kpe_ref[0] = proj('kpe', 128) * ck_ref[...] + proj('kpe_sw', 128) * sk_tab_ref[...]

    ff = proj('ff', 128) + bf_ref[...]
    log_sig = jnp.minimum(ff, 0.0) - jnp.log(1.0 + jnp.exp(-jnp.abs(ff)))
    lane = lax.broadcasted_iota(jnp.int32, ff.shape, 1)
    logf_ref[0] = jnp.where(lane < N_HEADS, log_sig, 0.0)

    sq = (proj('sq', 256) * HEAD_DIM ** -0.5).astype(BF16)
    sbqT_ref[0] = _dot_nt(eqT_ref[...], sq).astype(BF16)
    sk_ref[0] = proj('sk', 256)
    sv_ref[0] = proj('sv', 256)


def _in_proj(x, lw, tabs):
    b, t, d = x.shape
    tm = _pick(t, (512, 256, 128))
    grid = (b, t // tm)
    row = lambda w: pl.BlockSpec((1, tm, w), lambda i, j: (i, j, 0))
    colT = pl.BlockSpec((1, QW, tm), lambda i, j: (i, 0, j))
    in_specs = [
        row(d), _const_spec((1, d)), _const_spec((d, _W_MIX_COLS)),
        _const_spec((1, MLA_Q_RANK)), _const_spec((1, MLA_KV_RANK)), _const_spec((1, LANE)),
        pl.BlockSpec((tm, LANE), lambda i, j: (j, 0)), pl.BlockSpec((tm, LANE), lambda i, j: (j, 0)),
        pl.BlockSpec((QW, tm), lambda i, j: (0, j)), pl.BlockSpec((QW, tm), lambda i, j: (0, j)),
        _const_spec((QW, MLA_Q_RANK)), _const_spec((QW, MLA_Q_RANK)), _const_spec((QW, 256)),
    ]
    sds = jax.ShapeDtypeStruct
    out_shape = [
        sds((b, t, 256), BF16),
        sds((b, t, 256), F32), sds((b, t, 256), F32),
        sds((b, t, 256), F32), sds((b, t, 256), F32),
        sds((b, t, 128), F32), sds((b, t, LANE), F32), sds((b, t, LANE), F32),
        sds((b, QW, t), BF16), sds((b, QW, t), BF16),
        sds((b, t, 256), F32), sds((b, t, 256), F32),
    ]
    out_specs = [row(256), row(256), row(256), row(256), row(256), row(128), row(LANE), row(LANE),
                 colT, colT, row(256), row(256)]
    return pl.pallas_call(
        _in_proj_kernel, grid=grid, in_specs=in_specs, out_specs=out_specs, out_shape=out_shape,
        compiler_params=_params("parallel", "parallel"), name="in_proj",
    )(x, lw['ln_mix_pre'], lw['w_mix'], lw['mla_q_norm'], lw['mla_kv_norm'], lw['fox_bf'],
      tabs['ck'], tabs['sk'], tabs['cqT'], tabs['sqT'], lw['wuqT'], lw['wuqswT'], lw['eqT'])


def _cumsum_rows(x):
    n = x.shape[0]
    row = lax.broadcasted_iota(jnp.int32, x.shape, 0)
    d = 1
    while d < n:
        x = x + jnp.where(row >= d, pltpu.roll(x, d, 0), 0.0)
        d *= 2
    return x


def _kv_prep_kernel(fq_ref, fk_ref, fv_ref, logf_ref, ckv_ref, kpe_ref, sk_ref, sv_ref,
                    ek_ref, evT_ref, eqT_ref, fkmap_ref, fqmapT_ref, onesk_ref, onesq_ref,
                    wuk_ref, ekpe_ref, wuvT_ref,
                    fqT_out, fk_out, fvT_out, mk_out, mvT_out, sbk_out, sbvT_out, carry_ref, *, tk):
    @pl.when(pl.program_id(1) == 0)
    def _():
        carry_ref[...] = jnp.zeros_like(carry_ref)

    ts = fk_ref.shape[1]
    c = _cumsum_rows(logf_ref[0]) + carry_ref[...]
    carry_ref[...] = c[ts - 1:ts, :]
    c = c * LOG2E
    c_hi = c.astype(BF16)
    rem = c - c_hi.astype(F32)
    c_mid = rem.astype(BF16)
    c_lo = (rem - c_mid.astype(F32)).astype(BF16)

    def store_vt(out, v_t):
        for j in range(ts // tk):
            out[0, j] = v_t[:, j * tk:(j + 1) * tk]

    fk_out[0] = (_dot(fk_ref[0].astype(BF16), ek_ref[...]) + _dot(c_hi, fkmap_ref[0]) + _dot(c_mid, fkmap_ref[1])
                 + _dot(c_lo, fkmap_ref[2]) + onesk_ref[...]).astype(BF16)
    fqT_out[0] = (_dot_nt(eqT_ref[...], fq_ref[0]) + _dot_nt(fqmapT_ref[0], c_hi) + _dot_nt(fqmapT_ref[1], c_mid)
                  + _dot_nt(fqmapT_ref[2], c_lo) + onesq_ref[...]).astype(BF16)
    store_vt(fvT_out, _dot_nt(evT_ref[...], fv_ref[0].astype(BF16)).astype(BF16))

    ckv = ckv_ref[0].astype(BF16)
    mk_out[0] = (_dot(ckv, wuk_ref[...]) + _dot(kpe_ref[0].astype(BF16), ekpe_ref[...])).astype(BF16)
    store_vt(mvT_out, _dot_nt(wuvT_ref[...], ckv).astype(BF16))

    sbk_out[0] = _dot(sk_ref[0].astype(BF16), ek_ref[...]).astype(BF16)
    store_vt(sbvT_out, _dot_nt(evT_ref[...], sv_ref[0].astype(BF16)).astype(BF16))


def _kv_prep(fq, fk, fv, logf, ckv, kpe, sk, sv, lw, cst, tk):
    b, s, _ = fk.shape
    ts = _pick(s, (1024, 512, 256))
    grid = (b, s // ts)
    row = lambda w: pl.BlockSpec((1, ts, w), lambda i, j: (i, j, 0))
    colT = pl.BlockSpec((1, QW, ts), lambda i, j: (i, 0, j))
    vT = pl.BlockSpec((1, ts // tk, VW, tk), lambda i, j: (i, j, 0, 0))
    in_specs = [row(256), row(256), row(256), row(LANE), row(128), row(LANE), row(256), row(256),
                _const_spec((256, QW)), _const_spec((VW, VW)), _const_spec((QW, 256)),
                _const_spec((3, LANE, QW)), _const_spec((3, QW, LANE)), _const_spec((1, QW)), _const_spec((QW, 1)),
                _const_spec((MLA_KV_RANK, QW)), _const_spec((LANE, QW)), _const_spec((VW, MLA_KV_RANK))]
    sds = jax.ShapeDtypeStruct
    slab = sds((b, s, QW), BF16)
    slab_t = sds((b, s // tk, VW, tk), BF16)
    out_shape = [sds((b, QW, s), BF16), slab, slab_t, slab, slab_t, slab, slab_t]
    out_specs = [colT, row(QW), vT, row(QW), vT, row(QW), vT]
    return pl.pallas_call(
        functools.partial(_kv_prep_kernel, tk=tk), grid=grid, in_specs=in_specs, out_specs=out_specs,
        out_shape=out_shape, scratch_shapes=[pltpu.VMEM((1, LANE), F32)],
        compiler_params=_params("parallel", "arbitrary"), name="kv_prep",
    )(fq, fk, fv, logf, ckv, kpe, sk, sv, cst['ek'], cst['evT'], lw['eqT'], cst['fkmap'], cst['fqmapT'],
      cst['ones_k'], cst['ones_qT'], lw['wuk'], cst['ekpe'], lw['wuvT'])


def _head_rows(h):
    return slice(h * HP, (h + 1) * HP)


def _value_rows(h):
    return slice(h * HEAD_DIM, (h + 1) * HEAD_DIM)


def _load_kv(k_ref, vT_ref, kb, tk, h):
    off = pl.multiple_of(kb * tk, tk)
    return k_ref[0, pl.ds(off, tk), _head_rows(h)], vT_ref[0, kb, _value_rows(h), :]


def _finish(accs, invs, eye_ref, o_ref):
    for pair in range(N_HEADS // 2):
        o_t = jnp.concatenate([accs[2 * pair] * invs[2 * pair], accs[2 * pair + 1] * invs[2 * pair + 1]], axis=0)
        o_ref[0, :, _head_rows(pair)] = _dot_nt(eye_ref[...], o_t.astype(BF16)).astype(BF16)


def _softmax_attn_kernel(qT_ref, k_ref, vT_ref, eye_ref, o_ref, *, mode, past, tq, tk, s_valid):
    qi = pl.program_id(1)
    ltk = int(math.log2(tk))
    q_lo = past + qi * tq
    q_hi = q_lo + (tq - 1)
    if mode == 'fox':
        n_full = (q_lo + 1) >> ltk
        n_blk = (q_hi >> ltk) + 1
    else:
        lim_lo = ((q_lo // CHUNK) + 1) * CHUNK
        lim_hi = jnp.minimum(((q_hi // CHUNK) + 1) * CHUNK, s_valid)
        n_blk = (lim_hi + (tk - 1)) >> ltk
        n_full = jnp.minimum(lim_lo >> ltk, n_blk)
    qpos = q_lo + lax.broadcasted_iota(jnp.int32, (1, tq), 1)

    def step(kb, carry, masked):
        off = pl.multiple_of(kb * tk, tk)
        scores = [_dot(k_ref[0, pl.ds(off, tk), _head_rows(h)], qT_ref[0, _head_rows(h), :]) for h in range(N_HEADS)]
        if masked:
            kpos = kb * tk + lax.broadcasted_iota(jnp.int32, (tk, 1), 0)
            if mode == 'fox':
                vis = kpos <= qpos
            else:
                vis = jnp.logical_and((kpos // CHUNK) <= (qpos // CHUNK), kpos < s_valid)
        out = []
        for h in range(N_HEADS):
            m, l, acc = carry[h]
            s = jnp.where(vis, scores[h], NEG_INF) if masked else scores[h]
            m_new = jnp.maximum(m, jnp.max(s, axis=0, keepdims=True))
            p = jnp.exp2(s - m_new)
            alpha = jnp.exp2(m - m_new)
            l = alpha * l + jnp.sum(p, axis=0, keepdims=True)
            acc = alpha * acc + _dot(vT_ref[0, kb, _value_rows(h), :], p.astype(BF16))
            out.append((m_new, l, acc))
        return tuple(out)

    init = (jnp.full((1, tq), NEG_INF, F32), jnp.zeros((1, tq), F32), jnp.zeros((HEAD_DIM, tq), F32))
    carry = lax.fori_loop(0, n_full, functools.partial(step, masked=False), (init,) * N_HEADS)
    carry = lax.fori_loop(n_full, n_blk, functools.partial(step, masked=True), carry)
    _finish([c[2] for c in carry], [1.0 / c[1] for c in carry], eye_ref, o_ref)


def _softmax_attn_pipelined_kernel(qT_ref, k_ref, vT_ref, eye_ref, o_ref, sa_ref, sb_ref, acc_ref, *,
                                   mode, past, tq, tk, s_valid):
    qi = pl.program_id(1)
    q_lo = past + qi * tq
    n_pairs = past // tq + qi
    qpos = q_lo + lax.broadcasted_iota(jnp.int32, (1, tq), 1)

    def qk(kb, dst_ref):
        off = pl.multiple_of(kb * tk, tk)
        for h in range(N_HEADS):
            dst_ref[h] = _dot(k_ref[0, pl.ds(off, tk), _head_rows(h)], qT_ref[0, _head_rows(h), :])

    def softmax_pv(kb, src_ref, stats, masked):
        if masked:
            kpos = kb * tk + lax.broadcasted_iota(jnp.int32, (tk, 1), 0)
            if mode == 'fox':
                vis = kpos <= qpos
            else:
                vis = (kpos // CHUNK) <= (qpos // CHUNK)
                if s_valid < k_ref.shape[1]:
                    vis = jnp.logical_and(vis, kpos < s_valid)
        out = []
        for h in range(N_HEADS):
            m, l = stats[h]
            s = src_ref[h]
            if masked:
                s = jnp.where(vis, s, NEG_INF)
            m_new = jnp.maximum(m, jnp.max(s, axis=0, keepdims=True))
            p = jnp.exp2(s - m_new)
            alpha = jnp.exp2(m - m_new)
            l = alpha * l + jnp.sum(p, axis=0, keepdims=True)
            rows = _value_rows(h)
            acc_ref[rows, :] = alpha * acc_ref[rows, :] + _dot(vT_ref[0, kb, rows, :], p.astype(BF16))
            out.append((m_new, l))
        return tuple(out)

    acc_ref[...] = jnp.zeros_like(acc_ref)
    qk(0, sa_ref)

    def pair(j, stats):
        kb = 2 * j
        qk(kb + 1, sb_ref)
        stats = softmax_pv(kb, sa_ref, stats, False)
        qk(kb + 2, sa_ref)
        return softmax_pv(kb + 1, sb_ref, stats, False)

    init = (jnp.full((1, tq), NEG_INF, F32), jnp.zeros((1, tq), F32))
    stats = lax.fori_loop(0, n_pairs, pair, (init,) * N_HEADS)
    kb = 2 * n_pairs
    qk(kb + 1, sb_ref)
    stats = softmax_pv(kb, sa_ref, stats, True)
    stats = softmax_pv(kb + 1, sb_ref, stats, True)
    _finish([acc_ref[_value_rows(h), :] for h in range(N_HEADS)], [1.0 / st[1] for st in stats], eye_ref, o_ref)


def _stick_attn_kernel(qT_ref, k_ref, vT_ref, eye_ref, later_ref, o_ref, *, past, tq, tk):
    qi = pl.program_id(1)
    ltk = int(math.log2(tk))
    q_lo = past + qi * tq
    q_hi = q_lo + (tq - 1)
    n_full = q_lo >> ltk
    n_blk = ((q_hi - 1) >> ltk) + 1
    qpos = q_lo + lax.broadcasted_iota(jnp.int32, (1, tq), 1)

    def step(kb, carry, masked):
        kvs = [_load_kv(k_ref, vT_ref, kb, tk, h) for h in range(N_HEADS)]
        zs = [_dot(kvs[h][0], qT_ref[0, _head_rows(h), :]) for h in range(N_HEADS)]
        if masked:
            kpos = kb * tk + lax.broadcasted_iota(jnp.int32, (tk, 1), 0)
            vis = kpos < qpos
        out = []
        for h in range(N_HEADS):
            decay, acc = carry[h]
            z = zs[h]
            drop = _softplus(z)
            if masked:
                drop = jnp.where(vis, drop, 0.0)
            hi = drop.astype(BF16)
            lo = (drop - hi.astype(F32)).astype(BF16)
            later = _dot(later_ref[...], hi) + _dot(later_ref[...], lo)
            a = jnp.exp(z - drop - later - decay)
            if masked:
                a = jnp.where(vis, a, 0.0)
            acc = acc + _dot(kvs[h][1], a.astype(BF16))
            out.append((decay + jnp.sum(drop, axis=0, keepdims=True), acc))
        return tuple(out)

    def min_decay(carry):
        d = carry[0][0]
        for h in range(1, N_HEADS):
            d = jnp.minimum(d, carry[h][0])
        return jnp.min(d)

    init = (jnp.zeros((1, tq), F32), jnp.zeros((HEAD_DIM, tq), F32))
    carry = lax.fori_loop(0, n_blk - n_full, lambda i, c: step(n_blk - 1 - i, c, True), (init,) * N_HEADS)

    def cond(state):
        kb, dmin, _ = state
        return jnp.logical_and(kb >= 0, dmin < SB_DECAY_LIMIT)

    def body(state):
        kb, _, carry = state
        carry = step(kb, carry, False)
        return kb - 1, min_decay(carry), carry

    _, _, carry = lax.while_loop(cond, body, (n_full - 1, min_decay(carry), carry))
    _finish([c[1] for c in carry], [1.0] * N_HEADS, eye_ref, o_ref)


def _attention(mode, q_t, k, v_t, cst, *, past, s_valid, tq, tk):
    b, _, t = q_t.shape
    s = k.shape[1]
    grid = (b, t // tq)
    single = pl.Buffered(1)
    in_specs = [pl.BlockSpec((1, QW, tq), lambda i, j: (i, 0, j)),
                pl.BlockSpec((1, s, QW), lambda i, j: (i, 0, 0), pipeline_mode=single),
                pl.BlockSpec((1, s // tk, VW, tk), lambda i, j: (i, 0, 0, 0), pipeline_mode=single),
                _const_spec((tq, tq))]
    args = [q_t, k, v_t, jnp.eye(tq, dtype=BF16)]
    scratch = []
    if mode == 'sb':
        kern = functools.partial(_stick_attn_kernel, past=past, tq=tq, tk=tk)
        in_specs.append(_const_spec((tk, tk)))
        args.append(cst['later'][tk])
    elif tq == 2 * tk and past % tq == 0:
        kern = functools.partial(_softmax_attn_pipelined_kernel, mode=mode, past=past, tq=tq, tk=tk, s_valid=s_valid)
        scratch = [pltpu.VMEM((N_HEADS, tk, tq), F32), pltpu.VMEM((N_HEADS, tk, tq), F32), pltpu.VMEM((VW, tq), F32)]
    else:
        kern = functools.partial(_softmax_attn_kernel, mode=mode, past=past, tq=tq, tk=tk, s_valid=s_valid)
    width = (N_HEADS // 2) * HP
    return pl.pallas_call(
        kern, grid=grid, in_specs=in_specs,
        out_specs=pl.BlockSpec((1, tq, width), lambda i, j: (i, j, 0)),
        out_shape=jax.ShapeDtypeStruct((b, t, width), BF16), scratch_shapes=scratch,
        compiler_params=_params("parallel", "parallel"), name="attn_" + mode,
    )(*args)


def _lru_kernel(lx_ref, lg_ref, cb_ref, h0_ref, cw_ref, cbias_ref, wr_ref, br_ref, wi_ref, bi_ref, lam_ref,
                y_ref, nb_ref, hl_ref, xcat_ref, h_ref, *, past, tc):
    ci = pl.program_id(1)
    keep = LRU_CONV - 1

    @pl.when(ci == 0)
    def _():
        xcat_ref[0:8, :] = jnp.zeros((8, LRU_WIDTH), F32)
        xcat_ref[8 - keep:8, :] = cb_ref[0]
        h_ref[...] = h0_ref[0]

    x = lx_ref[0]
    xcat_ref[8:8 + tc, :] = x
    xc = cbias_ref[...] + x * cw_ref[keep:keep + 1, :]
    for tap in range(keep):
        shift = keep - tap
        xc = xc + xcat_ref[8 - shift:8 - shift + tc, :] * cw_ref[tap:tap + 1, :]
    nb_ref[0] = xcat_ref[8 + tc - keep:8 + tc, :]
    xcat_ref[0:8, :] = x[tc - 8:tc, :]

    xcb = xc.astype(BF16)
    r = _sigmoid(_dot(xcb, wr_ref[...]) + br_ref[...])
    gate_in = _sigmoid(_dot(xcb, wi_ref[...]) + bi_ref[...])
    log_a = (-LRU_C) * r * _softplus(-lam_ref[...])
    row = lax.broadcasted_iota(jnp.int32, (tc, LRU_WIDTH), 0)
    reset = (past + ci * tc + row) == 0
    a = jnp.where(reset, 0.0, jnp.exp(log_a))
    y2 = 2.0 * log_a
    series = -y2 * (1.0 + y2 * (0.5 + y2 * (1.0 / 6.0 + y2 * (1.0 / 24.0 + y2 * (1.0 / 120.0)))))
    one_minus = jnp.where(y2 > -0.05, series, 1.0 - jnp.exp(y2))
    mult = jnp.where(reset, 1.0, jnp.sqrt(one_minus))
    u = mult * gate_in * xc

    d = 1
    while d < tc:
        ok = row >= d
        u = u + jnp.where(ok, a * pltpu.roll(u, d, 0), 0.0)
        a = jnp.where(ok, a * pltpu.roll(a, d, 0), a)
        d *= 2
    hs = a * h_ref[...] + u
    h_last = hs[tc - 1:tc, :]
    h_ref[...] = h_last
    hl_ref[0] = h_last

    g = lg_ref[0]
    gelu = 0.5 * g * (1.0 + jnp.tanh(0.7978845608028654 * (g + 0.044715 * g * g * g)))
    y_ref[0] = (hs * gelu).astype(BF16)


def _lru(lx, lg, conv_buf, h0, lw, past):
    b, t, w = lx.shape
    tc = _pick(t, (256, 128, 64, 32))
    grid = (b, t // tc)
    row = pl.BlockSpec((1, tc, w), lambda i, j: (i, j, 0))
    keep = LRU_CONV - 1
    in_specs = [row, row, pl.BlockSpec((1, keep, w), lambda i, j: (i, 0, 0)), pl.BlockSpec((1, 1, w), lambda i, j: (i, 0, 0)),
                _const_spec((LRU_CONV, w)), _const_spec((1, w)), _const_spec((w, w)), _const_spec((1, w)),
                _const_spec((w, w)), _const_spec((1, w)), _const_spec((1, w))]
    sds = jax.ShapeDtypeStruct
    return pl.pallas_call(
        functools.partial(_lru_kernel, past=past, tc=tc), grid=grid, in_specs=in_specs,
        out_specs=[row, pl.BlockSpec((1, keep, w), lambda i, j: (i, 0, 0)), pl.BlockSpec((1, 1, w), lambda i, j: (i, 0, 0))],
        out_shape=[sds((b, t, w), BF16), sds((b, keep, w), F32), sds((b, 1, w), F32)],
        scratch_shapes=[pltpu.VMEM((tc + 8, w), F32), pltpu.VMEM((1, w), F32)],
        compiler_params=_params("parallel", "arbitrary"), name="rg_lru",
    )(lx, lg, conv_buf, h0, lw['lru_conv_w'], lw['lru_conv_b'], lw['lru_wr'], lw['lru_br'], lw['lru_wi'],
      lw['lru_bi'], lw['lru_lam'])


def _merge_kernel(x_ref, oa_ref, ob_ref, oc_ref, od_ref, g1_ref, g2_ref, wg_ref, wb_ref, wo_ref, out_ref):
    x = x_ref[0]
    d = x.shape[1]
    hb = _rms(x, g1_ref[...]).astype(BF16)
    branches = (oa_ref[0], ob_ref[0], oc_ref[0], od_ref[0])
    merged = None
    for n, o in enumerate(branches):
        gate = _sigmoid(_dot(hb, wg_ref[:, n * d:(n + 1) * d]))
        term = gate * _dot(o, wb_ref[n])
        merged = term if merged is None else merged + term
    y = _dot(merged.astype(BF16), wo_ref[...])
    out_ref[0] = x + _rms(y, g2_ref[...])


def _merge(x, o_a, o_b, o_c, o_d, lw):
    b, t, d = x.shape
    tm = _pick(t, (256, 128))
    grid = (b, t // tm)
    row = lambda w: pl.BlockSpec((1, tm, w), lambda i, j: (i, j, 0))
    bw = BRANCH_WIDTH
    in_specs = [row(d), row(bw), row(bw), row(bw), row(bw), _const_spec((1, d)), _const_spec((1, d)),
                _const_spec((d, N_BRANCH * d)), _const_spec((N_BRANCH, BRANCH_WIDTH, d)), _const_spec((d, d))]
    return pl.pallas_call(
        _merge_kernel, grid=grid, in_specs=in_specs, out_specs=row(d),
        out_shape=jax.ShapeDtypeStruct((b, t, d), F32),
        compiler_params=_params("parallel", "parallel"), name="merge",
    )(x, o_a, o_b, o_c, o_d, lw['ln_mix_pre'], lw['ln_mix_post'], lw['w_gate'], lw['w_branch'], lw['w_out'])


def _mem_kv_kernel(mem_ref, g_ref, wk_ref, wv_ref, k_ref, v_ref):
    mn = _rms(mem_ref[0], g_ref[...]).astype(BF16)
    k_ref[0] = _dot(mn, wk_ref[...])
    v_ref[0] = _dot(mn, wv_ref[...])


def _mem_kv(mem, lw):
    b, m, d = mem.shape
    w = lw['mem_wk'].shape[1]
    out = jax.ShapeDtypeStruct((b, m, w), F32)
    blk = pl.BlockSpec((1, m, w), lambda i: (i, 0, 0))
    return pl.pallas_call(
        _mem_kv_kernel, grid=(b,),
        in_specs=[pl.BlockSpec((1, m, d), lambda i: (i, 0, 0)), _const_spec((1, d)), _const_spec((d, w)), _const_spec((d, w))],
        out_specs=[blk, blk], out_shape=[out, out], compiler_params=_params("parallel"), name="mem_kv",
    )(mem, lw['mem_norm'], lw['mem_wk'], lw['mem_wv'])


def _mem_attn_kernel(x_ref, mk_ref, mv_ref, g1_ref, g2_ref, wq_ref, wo_ref, out_ref):
    x = x_ref[0]
    hb = _rms(x, g1_ref[...]).astype(BF16)
    q = (_dot(hb, wq_ref[...]) * (MEM_HEAD_DIM ** -0.5)).astype(BF16)
    heads = []
    for h in range(N_HEADS):
        sl = slice(h * MEM_HEAD_DIM, (h + 1) * MEM_HEAD_DIM)
        s = _dot_nt(q[:, sl], mk_ref[0, :, sl])
        p = jnp.exp(s - jnp.max(s, axis=-1, keepdims=True))
        inv = 1.0 / jnp.sum(p, axis=-1, keepdims=True)
        heads.append((_dot(p.astype(BF16), mv_ref[0, :, sl]) * inv).astype(BF16))
    y = _dot(jnp.concatenate(heads, axis=1), wo_ref[...])
    out_ref[0] = x + _rms(y, g2_ref[...])


def _mem_attn(x, mk, mv, lw):
    b, t, d = x.shape
    m, w = mk.shape[1], mk.shape[2]
    tm = _pick(t, (512, 256, 128))
    row = pl.BlockSpec((1, tm, d), lambda i, j: (i, j, 0))
    kv = pl.BlockSpec((1, m, w), lambda i, j: (i, 0, 0))
    return pl.pallas_call(
        _mem_attn_kernel, grid=(b, t // tm),
        in_specs=[row, kv, kv, _const_spec((1, d)), _const_spec((1, d)), _const_spec((d, w)), _const_spec((w, d))],
        out_specs=row, out_shape=jax.ShapeDtypeStruct((b, t, d), F32),
        compiler_params=_params("parallel", "parallel"), name="mem_attn",
    )(x, mk, mv, lw['ln_mem_pre'], lw['ln_mem_post'], lw['mem_wq'], lw['mem_wo'])


def _ffn_kernel(x_ref, g1_ref, g2_ref, wg_ref, wu_ref, wd_ref, out_ref):
    x = x_ref[0]
    hb = _rms(x, g1_ref[...]).astype(BF16)
    gate = _dot(hb, wg_ref[...])
    act = (gate * _sigmoid(gate) * _dot(hb, wu_ref[...])).astype(BF16)
    y = _dot(act, wd_ref[...])
    out_ref[0] = x + _rms(y, g2_ref[...])


def _ffn(x, lw):
    b, t, d = x.shape
    f = lw['ffn_wg'].shape[1]
    tm = _pick(t, (256, 128))
    row = pl.BlockSpec((1, tm, d), lambda i, j: (i, j, 0))
    single = pl.Buffered(1)
    return pl.pallas_call(
        _ffn_kernel, grid=(b, t // tm),
        in_specs=[row, _const_spec((1, d)), _const_spec((1, d)),
                  pl.BlockSpec((d, f), lambda i, j: (0, 0), pipeline_mode=single),
                  pl.BlockSpec((d, f), lambda i, j: (0, 0), pipeline_mode=single),
                  pl.BlockSpec((f, d), lambda i, j: (0, 0), pipeline_mode=single)],
        out_specs=row, out_shape=jax.ShapeDtypeStruct((b, t, d), F32),
        compiler_params=_params("parallel", "parallel"), name="ffn",
    )(x, lw['ln_ffn_pre'], lw['ln_ffn_post'], lw['ffn_wg'], lw['ffn_wu'], lw['ffn_wd'])


def _constants():
    fk, fq, ones_k, ones_q = _fox_feature_maps()
    later = {}
    for tk in (256,):
        idx = np.arange(tk)
        later[tk] = jnp.asarray((idx[None, :] > idx[:, None]).astype(np.float32), BF16)
    return dict(
        ek=jnp.asarray(_place_qk(), BF16), evT=jnp.eye(VW, dtype=BF16),
        fkmap=jnp.asarray(fk, BF16), fqmapT=jnp.asarray(np.transpose(fq, (0, 2, 1)), BF16),
        ones_k=jnp.asarray(ones_k, F32), ones_qT=jnp.asarray(ones_q.T, F32),
        ekpe=jnp.asarray(_place_kpe(), BF16), later=later)


def _pad_cols(w, n):
    return jnp.pad(w, ((0, 0), (0, n - w.shape[1])))


def _layer_weights(l, p):
    d = p['w_in'].shape[1]
    w_in = p['w_in'][l]
    sizes = (256, 256, 256, N_HEADS, LRU_WIDTH, LRU_WIDTH, MLA_Q_RANK, MLA_KV_RANK, MLA_ROPE, 256, 256, 256)
    offs = np.concatenate([[0], np.cumsum(sizes)])
    fq, fk, fv, ff, lx, lg, cq, ckv, kpe, sq, sk, sv = [w_in[:, offs[i]:offs[i + 1]] for i in range(len(sizes))]
    half = MLA_ROPE // 2
    kpe_sw = jnp.concatenate([kpe[:, half:], kpe[:, :half]], axis=1)
    w_mix = jnp.concatenate([fq, fk, fv, lx, lg, cq, ckv, _pad_cols(kpe, LANE), _pad_cols(kpe_sw, LANE),
                             _pad_cols(ff, LANE), sq, sk, sv], axis=1).astype(BF16)

    qk = MLA_NOPE + MLA_ROPE
    wuq = p['mla_w_uq'][l].reshape(MLA_Q_RANK, N_HEADS, qk)
    rope = wuq[:, :, MLA_NOPE:]
    rope_sw = jnp.concatenate([rope[:, :, half:], rope[:, :, :half]], axis=2)
    wuq_p = jnp.pad(wuq, ((0, 0), (0, 0), (0, HP - qk))).reshape(MLA_Q_RANK, QW)
    wuqsw_p = jnp.pad(rope_sw, ((0, 0), (0, 0), (MLA_NOPE, HP - qk))).reshape(MLA_Q_RANK, QW)
    wuk = p['mla_w_uk'][l].reshape(MLA_KV_RANK, N_HEADS, MLA_NOPE)
    wuk_p = jnp.pad(wuk, ((0, 0), (0, 0), (0, HP - MLA_NOPE))).reshape(MLA_KV_RANK, QW)

    def block_diag(w):
        return jax.scipy.linalg.block_diag(*[w[i] for i in range(w.shape[0])]).astype(BF16)

    row = lambda v: v[l].reshape(1, -1).astype(F32)
    return dict(
        w_mix=w_mix, w_gate=w_in[:, offs[-1]:].astype(BF16),
        ln_mix_pre=row(p['ln_mix_pre']), ln_mix_post=row(p['ln_mix_post']),
        fox_bf=_pad_cols(row(p['fox_bf']), LANE),
        mla_q_norm=row(p['mla_q_norm']), mla_kv_norm=row(p['mla_kv_norm']),
        wuqT=wuq_p.T.astype(BF16), wuqswT=wuqsw_p.T.astype(BF16), wuk=wuk_p.astype(BF16), wuvT=p['mla_w_uv'][l].T.astype(BF16),
        eqT=jnp.asarray(_place_qk().T, BF16),
        lru_conv_w=p['lru_conv_w'][l].astype(F32), lru_conv_b=row(p['lru_conv_b']),
        lru_wr=block_diag(p['lru_wr'][l]), lru_br=row(p['lru_br']),
        lru_wi=block_diag(p['lru_wi'][l]), lru_bi=row(p['lru_bi']), lru_lam=row(p['lru_lam']),
        w_branch=p['w_branch'][l].astype(BF16), w_out=p['w_out'][l].astype(BF16),
        ln_mem_pre=row(p['ln_mem_pre']), ln_mem_post=row(p['ln_mem_post']), mem_norm=row(p['mem_norm']),
        mem_wq=p['mem_wq'][l].astype(BF16), mem_wk=p['mem_wk'][l].astype(BF16),
        mem_wv=p['mem_wv'][l].astype(BF16), mem_wo=p['mem_wo'][l].astype(BF16),
        ln_ffn_pre=row(p['ln_ffn_pre']), ln_ffn_post=row(p['ln_ffn_post']),
        ffn_wg=p['ffn_wg'][l].astype(BF16), ffn_wu=p['ffn_wu'][l].astype(BF16), ffn_wd=p['ffn_wd'][l].astype(BF16))


def _rope_tables(past, t):
    half = MLA_ROPE // 2
    inv = jnp.power(ROPE_BASE, -jnp.arange(half, dtype=F32) / half)
    ang = (past + jnp.arange(t, dtype=jnp.int32)).astype(F32)[:, None] * inv
    cos, sin = jnp.cos(ang), jnp.sin(ang)
    c32 = jnp.concatenate([cos, cos], axis=1)
    s32 = jnp.concatenate([-sin, sin], axis=1)
    scale = (MLA_NOPE + MLA_ROPE) ** -0.5 * LOG2E
    slot_c = jnp.concatenate([jnp.ones((t, MLA_NOPE), F32), c32, jnp.zeros((t, HP - MLA_NOPE - MLA_ROPE), F32)], axis=1)
    slot_s = jnp.concatenate([jnp.zeros((t, MLA_NOPE), F32), s32, jnp.zeros((t, HP - MLA_NOPE - MLA_ROPE), F32)], axis=1)
    return dict(ck=_pad_cols(c32, LANE), sk=_pad_cols(s32, LANE),
                cqT=(jnp.tile(slot_c, (1, N_HEADS)) * scale).T, sqT=(jnp.tile(slot_s, (1, N_HEADS)) * scale).T)


def _trunk_layer(x, past_state, mem_k, mem_v, lw, cst):
    b, t, _ = x.shape
    past = 0 if past_state is None else past_state['fox_k'].shape[1]
    tk = 256
    fq, fk, fv, lx, lg, ckv_n, kpe_slab, logf_slab, mla_qT, sb_qT, sk, sv = _in_proj(x, lw, _rope_tables(past, t))

    if past_state is None:
        s_valid = t
        keys = (fq, fk, fv, logf_slab, ckv_n, kpe_slab, sk, sv)
        conv_buf = jnp.zeros((b, LRU_CONV - 1, LRU_WIDTH), F32)
        h0 = jnp.zeros((b, 1, LRU_WIDTH), F32)
    else:
        s_valid = past + t
        s_pad = -(-s_valid // tk) * tk

        def cat(old, new):
            old = old.reshape(b, past, -1).astype(new.dtype)
            old = jnp.pad(old, ((0, 0), (0, 0), (0, new.shape[2] - old.shape[2])))
            return jnp.pad(jnp.concatenate([old, new], axis=1), ((0, 0), (0, s_pad - s_valid), (0, 0)))

        keys = (cat(jnp.zeros((b, past, 256), BF16), fq), cat(past_state['fox_k'], fk), cat(past_state['fox_v'], fv),
                cat(past_state['fox_logf'], logf_slab), cat(past_state['mla_ckv'], ckv_n),
                cat(past_state['mla_kpe'], kpe_slab), cat(past_state['sb_k'], sk), cat(past_state['sb_v'], sv))
        conv_buf = past_state['lru_conv'].astype(F32)
        h0 = past_state['lru_h'].reshape(b, 1, LRU_WIDTH).astype(F32)

    fox_qT, fox_k, fox_vT, mla_k, mla_vT, sb_k, sb_vT = _kv_prep(*keys, lw, cst, tk)
    if past_state is not None:
        fox_qT = fox_qT[:, :, past:past + t]

    att = functools.partial(_attention, cst=cst, past=past, s_valid=s_valid, tk=tk)
    tq_softmax = _pick(t, (2 * tk, tk, 128))
    o_a = att('fox', fox_qT, fox_k, fox_vT, tq=tq_softmax)
    o_c = att('mla', mla_qT, mla_k, mla_vT, tq=tq_softmax)
    o_d = att('sb', sb_qT, sb_k, sb_vT, tq=_pick(t, (tk, 128)))
    o_b, lru_conv, lru_h = _lru(lx, lg, conv_buf, h0, lw, past)

    x = _merge(x, o_a, o_b, o_c, o_d, lw)
    x = _mem_attn(x, mem_k, mem_v, lw)
    x = _ffn(x, lw)

    new = dict(fox_k=fk.reshape(b, t, N_HEADS, HEAD_DIM), fox_v=fv.reshape(b, t, N_HEADS, HEAD_DIM),
               fox_logf=logf_slab[:, :, :N_HEADS], lru_h=lru_h.reshape(b, LRU_WIDTH), lru_conv=lru_conv,
               mla_ckv=ckv_n, mla_kpe=kpe_slab[:, :, :MLA_ROPE],
               sb_k=sk.reshape(b, t, N_HEADS, HEAD_DIM), sb_v=sv.reshape(b, t, N_HEADS, HEAD_DIM))
    return x, new


def kernel(x_prompt, x_sample, cache_fox_k, cache_fox_v, cache_fox_logf, state_lru_h, state_lru_conv, cache_mla_ckv, cache_mla_kpe, cache_sb_k, cache_sb_v, cache_mem_k, cache_mem_v, mem_prompt, ln_mix_pre, ln_mix_post, w_in, fox_bf, lru_conv_w, lru_conv_b, lru_wr, lru_br, lru_wi, lru_bi, lru_lam, mla_q_norm, mla_w_uq, mla_kv_norm, mla_w_uk, mla_w_uv, w_branch, w_out, ln_mem_pre, ln_mem_post, mem_norm, mem_wq, mem_wk, mem_wv, mem_wo, ln_ffn_pre, ln_ffn_post, ffn_wg, ffn_wu, ffn_wd):
    params = dict(ln_mix_pre=ln_mix_pre, ln_mix_post=ln_mix_post, w_in=w_in, fox_bf=fox_bf, lru_conv_w=lru_conv_w,
                  lru_conv_b=lru_conv_b, lru_wr=lru_wr, lru_br=lru_br, lru_wi=lru_wi, lru_bi=lru_bi, lru_lam=lru_lam,
                  mla_q_norm=mla_q_norm, mla_w_uq=mla_w_uq, mla_kv_norm=mla_kv_norm, mla_w_uk=mla_w_uk,
                  mla_w_uv=mla_w_uv, w_branch=w_branch, w_out=w_out, ln_mem_pre=ln_mem_pre, ln_mem_post=ln_mem_post,
                  mem_norm=mem_norm, mem_wq=mem_wq, mem_wk=mem_wk, mem_wv=mem_wv, mem_wo=mem_wo,
                  ln_ffn_pre=ln_ffn_pre, ln_ffn_post=ln_ffn_post, ffn_wg=ffn_wg, ffn_wu=ffn_wu, ffn_wd=ffn_wd)
    depth = w_in.shape[0]
    cst = _constants()
    weights = [_layer_weights(l, params) for l in range(depth)]
    bp, mem_len = mem_prompt.shape[0], mem_prompt.shape[1]

    y_prompt, p_states = x_prompt, []
    for l in range(depth):
        mk, mv = _mem_kv(mem_prompt, weights[l])
        y_prompt, st = _trunk_layer(y_prompt, None, mk.astype(BF16), mv.astype(BF16), weights[l], cst)
        st['mem_k'] = mk.reshape(bp, mem_len, N_HEADS, MEM_HEAD_DIM)
        st['mem_v'] = mv.reshape(bp, mem_len, N_HEADS, MEM_HEAD_DIM)
        p_states.append(st)

    y_sample, s_states = x_sample, []
    bs = x_sample.shape[0]
    for l in range(depth):
        past = dict(fox_k=cache_fox_k[l], fox_v=cache_fox_v[l], fox_logf=cache_fox_logf[l], lru_h=state_lru_h[l],
                    lru_conv=state_lru_conv[l], mla_ckv=cache_mla_ckv[l], mla_kpe=cache_mla_kpe[l],
                    sb_k=cache_sb_k[l], sb_v=cache_sb_v[l])
        mk = cache_mem_k[l].reshape(bs, mem_len, -1).astype(BF16)
        mv = cache_mem_v[l].reshape(bs, mem_len, -1).astype(BF16)
        y_sample, st = _trunk_layer(y_sample, past, mk, mv, weights[l], cst)
        s_states.append(st)

    def stk(states, name):
        return jnp.stack([s[name] for s in states])

    p_names = ('fox_k', 'fox_v', 'fox_logf', 'lru_h', 'lru_conv', 'mla_ckv', 'mla_kpe', 'sb_k', 'sb_v', 'mem_k', 'mem_v')
    s_names = p_names[:9]
    return ((y_prompt, y_sample) + tuple(stk(p_states, n) for n in p_names)
            + tuple(stk(s_states, n) for n in s_names))
```

```python
import functools
import math

import numpy as np
import jax
import jax.numpy as jnp
from jax import lax
from jax.experimental import pallas as pl
from jax.experimental.pallas import tpu as pltpu

F32 = jnp.float32
BF16 = jnp.bfloat16

CHUNK = 64
HEAD_DIM = 64
N_HEADS = 4
LRU_WIDTH = 256
LRU_CONV = 4
LRU_C = 8.0
MLA_Q_RANK = 256
MLA_KV_RANK = 128
MLA_NOPE = 64
MLA_ROPE = 32
MLA_V = 64
ROPE_BASE = 10000.0
N_BRANCH = 4
BRANCH_WIDTH = 256
MEM_HEAD_DIM = 128
EPS = 1e-6
NEG_INF = -1e30

HP = 128
QW = N_HEADS * HP
VW = N_HEADS * HEAD_DIM
LANE = 128
VMEM_LIMIT_BYTES = 56 * 1024 * 1024
LOG2E = 1.4426950408889634
SB_DECAY_LIMIT = 127.0
FOX_SKIP_LIMIT = 150.0

_NT = (((1,), (1,)), ((), ()))


def _dot(a, b):
    return jnp.dot(a, b, preferred_element_type=F32)


def _dot_nt(a, b):
    return lax.dot_general(a, b, _NT, preferred_element_type=F32)


def _rms(x, g):
    ms = jnp.mean(x * x, axis=-1, keepdims=True)
    return x * lax.rsqrt(ms + EPS) * g


def _sigmoid(x):
    return 1.0 / (1.0 + jnp.exp(-x))


def _softplus(x):
    return jnp.maximum(x, 0.0) + jnp.log(1.0 + jnp.exp(-jnp.abs(x)))


def _pick(n, cands):
    for c in cands:
        if n % c == 0:
            return c
    return n


def _params(*sem):
    return pltpu.CompilerParams(dimension_semantics=sem, vmem_limit_bytes=VMEM_LIMIT_BYTES)


def _const_spec(shape):
    nd = len(shape)
    return pl.BlockSpec(shape, lambda *_: (0,) * nd)


def _place_qk():
    e = np.zeros((N_HEADS * HEAD_DIM, QW), np.float32)
    for h in range(N_HEADS):
        for j in range(HEAD_DIM):
            e[h * HEAD_DIM + j, h * HP + j] = 1.0
    return e


def _fox_feature_maps():
    fk = np.zeros((3, LANE, QW), np.float32)
    fq = np.zeros((3, LANE, QW), np.float32)
    ones_k = np.zeros((1, QW), np.float32)
    ones_q = np.zeros((1, QW), np.float32)
    for h in range(N_HEADS):
        for part in range(3):
            fq[part, h, h * HP + HEAD_DIM + part] = 1.0
            fk[part, h, h * HP + HEAD_DIM + 3 + part] = -1.0
            ones_k[0, h * HP + HEAD_DIM + part] = 1.0
            ones_q[0, h * HP + HEAD_DIM + 3 + part] = 1.0
    return fk, fq, ones_k, ones_q


def _place_kpe():
    e = np.zeros((LANE, QW), np.float32)
    for h in range(N_HEADS):
        for j in range(MLA_ROPE):
            e[j, h * HP + MLA_NOPE + j] = 1.0
    return e


_W_OFF = dict(fq=0, fk=256, fv=512, lx=768, lg=1024, cq=1280, ckv=1536, kpe=1664, kpe_sw=1792, ff=1920,
              sq=2048, sk=2304, sv=2560)
_W_MIX_COLS = 2816


def _in_proj_kernel(x_ref, g_ref, w_ref, qn_ref, kvn_ref, bf_ref, ck_ref, sk_tab_ref, cqT_ref, sqT_ref,
                    wuqT_ref, wuqswT_ref, eqT_ref,
                    fq_ref, fk_ref, fv_ref, lx_ref, lg_ref, ckv_ref, kpe_ref, logf_ref, mqT_ref, sbqT_ref,
                    sk_ref, sv_ref):
    hb = _rms(x_ref[0], g_ref[...]).astype(BF16)

    def proj(name, width):
        a = _W_OFF[name]
        return _dot(hb, w_ref[:, a:a + width])

    fq_ref[0] = (proj('fq', 256) * (HEAD_DIM ** -0.5 * LOG2E)).astype(BF16)
    fk_ref[0] = proj('fk', 256)
    fv_ref[0] = proj('fv', 256)
    lx_ref[0] = proj('lx', 256)
    lg_ref[0] = proj('lg', 256)

    cqn = _rms(proj('cq', 256), qn_ref[...]).astype(BF16)
    q_t = _dot_nt(wuqT_ref[...], cqn)
    qsw_t = _dot_nt(wuqswT_ref[...], cqn)
    mqT_ref[0] = (q_t * cqT_ref[...] + qsw_t * sqT_ref[...]).astype(BF16)

    ckv_ref[0] = _rms(proj('ckv', 128), kvn_ref[...])
    kpe_ref[0] = proj('kpe', 128) * ck_ref[...] + proj('kpe_sw', 128) * sk_tab_ref[...]

    ff = proj('ff', 128) + bf_ref[...]
    log_sig = jnp.minimum(ff, 0.0) - jnp.log(1.0 + jnp.exp(-jnp.abs(ff)))
    lane = lax.broadcasted_iota(jnp.int32, ff.shape, 1)
    logf_ref[0] = jnp.where(lane < N_HEADS, log_sig, 0.0)

    sq = (proj('sq', 256) * (HEAD_DIM ** -0.5 * LOG2E)).astype(BF16)
    sbqT_ref[0] = _dot_nt(eqT_ref[...], sq).astype(BF16)
    sk_ref[0] = proj('sk', 256)
    sv_ref[0] = proj('sv', 256)


def _in_proj(x, lw, tabs):
    b, t, d = x.shape
    tm = _pick(t, (512, 256, 128))
    grid = (b, t // tm)
    row = lambda w: pl.BlockSpec((1, tm, w), lambda i, j: (i, j, 0))
    colT = pl.BlockSpec((1, QW, tm), lambda i, j: (i, 0, j))
    in_specs = [
        row(d), _const_spec((1, d)), _const_spec((d, _W_MIX_COLS)),
        _const_spec((1, MLA_Q_RANK)), _const_spec((1, MLA_KV_RANK)), _const_spec((1, LANE)),
        pl.BlockSpec((tm, LANE), lambda i, j: (j, 0)), pl.BlockSpec((tm, LANE), lambda i, j: (j, 0)),
        pl.BlockSpec((QW, tm), lambda i, j: (0, j)), pl.BlockSpec((QW, tm), lambda i, j: (0, j)),
        _const_spec((QW, MLA_Q_RANK)), _const_spec((QW, MLA_Q_RANK)), _const_spec((QW, 256)),
    ]
    sds = jax.ShapeDtypeStruct
    out_shape = [
        sds((b, t, 256), BF16),
        sds((b, t, 256), F32), sds((b, t, 256), F32),
        sds((b, t, 256), F32), sds((b, t, 256), F32),
        sds((b, t, 128), F32), sds((b, t, LANE), F32), sds((b, t, LANE), F32),
        sds((b, QW, t), BF16), sds((b, QW, t), BF16),
        sds((b, t, 256), F32), sds((b, t, 256), F32),
    ]
    out_specs = [row(256), row(256), row(256), row(256), row(256), row(128), row(LANE), row(LANE),
                 colT, colT, row(256), row(256)]
    return pl.pallas_call(
        _in_proj_kernel, grid=grid, in_specs=in_specs, out_specs=out_specs, out_shape=out_shape,
        compiler_params=_params("parallel", "parallel"), name="in_proj",
    )(x, lw['ln_mix_pre'], lw['w_mix'], lw['mla_q_norm'], lw['mla_kv_norm'], lw['fox_bf'],
      tabs['ck'], tabs['sk'], tabs['cqT'], tabs['sqT'], lw['wuqT'], lw['wuqswT'], lw['eqT'])


def _cumsum_rows(x):
    n = x.shape[0]
    row = lax.broadcasted_iota(jnp.int32, x.shape, 0)
    d = 1
    while d < n:
        x = x + jnp.where(row >= d, pltpu.roll(x, d, 0), 0.0)
        d *= 2
    return x


def _kv_prep_kernel(fq_ref, fk_ref, fv_ref, logf_ref, ckv_ref, kpe_ref, sk_ref, sv_ref,
                    ek_ref, evT_ref, eqT_ref, fkmap_ref, fqmapT_ref, onesk_ref, onesq_ref,
                    wuk_ref, ekpe_ref, wuvT_ref, seg_ref,
                    fqT_out, fk_out, fvT_out, mk_out, mvT_out, sbk_out, sbvT_out, cend_out, qn_out, kn_out,
                    carry_ref, *, tk):
    @pl.when(pl.program_id(1) == 0)
    def _():
        carry_ref[...] = jnp.zeros_like(carry_ref)
        qn_out[...] = jnp.zeros_like(qn_out)
        kn_out[...] = jnp.zeros_like(kn_out)

    ts = fk_ref.shape[1]
    c = _cumsum_rows(logf_ref[0]) + carry_ref[...]
    carry_ref[...] = c[ts - 1:ts, :]
    c = c * LOG2E
    for j in range(ts // tk):
        cend_out[0, j] = c[(j + 1) * tk - 1:(j + 1) * tk, :]

    def norm_bound(x):
        xf = x.astype(F32)
        sums = _dot((xf * xf).astype(BF16), seg_ref[...]) * 1.01
        return jnp.max(sums, axis=0, keepdims=True)

    qn_out[0] = jnp.maximum(qn_out[0], norm_bound(fq_ref[0]))
    kn_out[0] = jnp.maximum(kn_out[0], norm_bound(fk_ref[0].astype(BF16)))
    c_hi = c.astype(BF16)
    rem = c - c_hi.astype(F32)
    c_mid = rem.astype(BF16)
    c_lo = (rem - c_mid.astype(F32)).astype(BF16)

    def store_vt(out, v_t):
        for j in range(ts // tk):
            out[0, j] = v_t[:, j * tk:(j + 1) * tk]

    fk_out[0] = (_dot(fk_ref[0].astype(BF16), ek_ref[...]) + _dot(c_hi, fkmap_ref[0]) + _dot(c_mid, fkmap_ref[1])
                 + _dot(c_lo, fkmap_ref[2]) + onesk_ref[...]).astype(BF16)
    fqT_out[0] = (_dot_nt(eqT_ref[...], fq_ref[0]) + _dot_nt(fqmapT_ref[0], c_hi) + _dot_nt(fqmapT_ref[1], c_mid)
                  + _dot_nt(fqmapT_ref[2], c_lo) + onesq_ref[...]).astype(BF16)
    store_vt(fvT_out, _dot_nt(evT_ref[...], fv_ref[0].astype(BF16)).astype(BF16))

    ckv = ckv_ref[0].astype(BF16)
    mk_out[0] = (_dot(ckv, wuk_ref[...]) + _dot(kpe_ref[0].astype(BF16), ekpe_ref[...])).astype(BF16)
    store_vt(mvT_out, _dot_nt(wuvT_ref[...], ckv).astype(BF16))

    sbk_out[0] = _dot(sk_ref[0].astype(BF16), ek_ref[...]).astype(BF16)
    store_vt(sbvT_out, _dot_nt(evT_ref[...], sv_ref[0].astype(BF16)).astype(BF16))


def _kv_prep(fq, fk, fv, logf, ckv, kpe, sk, sv, lw, cst, tk):
    b, s, _ = fk.shape
    ts = _pick(s, (1024, 512, 256))
    grid = (b, s // ts)
    row = lambda w: pl.BlockSpec((1, ts, w), lambda i, j: (i, j, 0))
    colT = pl.BlockSpec((1, QW, ts), lambda i, j: (i, 0, j))
    vT = pl.BlockSpec((1, ts // tk, VW, tk), lambda i, j: (i, j, 0, 0))
    in_specs = [row(256), row(256), row(256), row(LANE), row(128), row(LANE), row(256), row(256),
                _const_spec((256, QW)), _const_spec((VW, VW)), _const_spec((QW, 256)),
                _const_spec((3, LANE, QW)), _const_spec((3, QW, LANE)), _const_spec((1, QW)), _const_spec((QW, 1)),
                _const_spec((MLA_KV_RANK, QW)), _const_spec((LANE, QW)), _const_spec((VW, MLA_KV_RANK)),
                _const_spec((VW, LANE))]
    sds = jax.ShapeDtypeStruct
    slab = sds((b, s, QW), BF16)
    slab_t = sds((b, s // tk, VW, tk), BF16)
    stat = sds((b, 1, LANE), F32)
    stat_spec = pl.BlockSpec((1, 1, LANE), lambda i, j: (i, 0, 0))
    out_shape = [sds((b, QW, s), BF16), slab, slab_t, slab, slab_t, slab, slab_t,
                 sds((b, s // tk, 1, LANE), F32), stat, stat]
    out_specs = [colT, row(QW), vT, row(QW), vT, row(QW), vT,
                 pl.BlockSpec((1, ts // tk, 1, LANE), lambda i, j: (i, j, 0, 0)), stat_spec, stat_spec]
    return pl.pallas_call(
        functools.partial(_kv_prep_kernel, tk=tk), grid=grid, in_specs=in_specs, out_specs=out_specs,
        out_shape=out_shape, scratch_shapes=[pltpu.VMEM((1, LANE), F32)],
        compiler_params=_params("parallel", "arbitrary"), name="kv_prep",
    )(fq, fk, fv, logf, ckv, kpe, sk, sv, cst['ek'], cst['evT'], lw['eqT'], cst['fkmap'], cst['fqmapT'],
      cst['ones_k'], cst['ones_qT'], lw['wuk'], cst['ekpe'], lw['wuvT'], cst['seg'])


def _head_rows(h):
    return slice(h * HP, (h + 1) * HP)


def _value_rows(h):
    return slice(h * HEAD_DIM, (h + 1) * HEAD_DIM)


def _load_kv(k_ref, vT_ref, kb, tk, h):
    off = pl.multiple_of(kb * tk, tk)
    return k_ref[0, pl.ds(off, tk), _head_rows(h)], vT_ref[0, kb, _value_rows(h), :]


def _finish(accs, invs, eye_ref, o_ref):
    for pair in range(N_HEADS // 2):
        o_t = jnp.concatenate([accs[2 * pair] * invs[2 * pair], accs[2 * pair + 1] * invs[2 * pair + 1]], axis=0)
        o_ref[0, :, _head_rows(pair)] = _dot_nt(eye_ref[...], o_t.astype(BF16)).astype(BF16)


def _softmax_attn_kernel(qT_ref, k_ref, vT_ref, eye_ref, o_ref, *, mode, past, tq, tk, s_valid):
    qi = pl.program_id(1)
    ltk = int(math.log2(tk))
    q_lo = past + qi * tq
    q_hi = q_lo + (tq - 1)
    if mode == 'fox':
        n_full = (q_lo + 1) >> ltk
        n_blk = (q_hi >> ltk) + 1
    else:
        lim_lo = ((q_lo // CHUNK) + 1) * CHUNK
        lim_hi = jnp.minimum(((q_hi // CHUNK) + 1) * CHUNK, s_valid)
        n_blk = (lim_hi + (tk - 1)) >> ltk
        n_full = jnp.minimum(lim_lo >> ltk, n_blk)
    qpos = q_lo + lax.broadcasted_iota(jnp.int32, (1, tq), 1)

    def step(kb, carry, masked):
        off = pl.multiple_of(kb * tk, tk)
        scores = [_dot(k_ref[0, pl.ds(off, tk), _head_rows(h)], qT_ref[0, _head_rows(h), :]) for h in range(N_HEADS)]
        if masked:
            kpos = kb * tk + lax.broadcasted_iota(jnp.int32, (tk, 1), 0)
            if mode == 'fox':
                vis = kpos <= qpos
            else:
                vis = jnp.logical_and((kpos // CHUNK) <= (qpos // CHUNK), kpos < s_valid)
        out = []
        for h in range(N_HEADS):
            m, l, acc = carry[h]
            s = jnp.where(vis, scores[h], NEG_INF) if masked else scores[h]
            m_new = jnp.maximum(m, jnp.max(s, axis=0, keepdims=True))
            p = jnp.exp2(s - m_new)
            alpha = jnp.exp2(m - m_new)
            l = alpha * l + jnp.sum(p, axis=0, keepdims=True)
            acc = alpha * acc + _dot(vT_ref[0, kb, _value_rows(h), :], p.astype(BF16))
            out.append((m_new, l, acc))
        return tuple(out)

    init = (jnp.full((1, tq), NEG_INF, F32), jnp.zeros((1, tq), F32), jnp.zeros((HEAD_DIM, tq), F32))
    carry = lax.fori_loop(0, n_full, functools.partial(step, masked=False), (init,) * N_HEADS)
    carry = lax.fori_loop(n_full, n_blk, functools.partial(step, masked=True), carry)
    _finish([c[2] for c in carry], [1.0 / c[1] for c in carry], eye_ref, o_ref)


def _first_needed_pair(cend_ref, qn_ref, kn_ref, n_pairs):
    nblk = cend_ref.shape[1]
    cend = cend_ref[0]
    c_tile = cend_ref[0, pl.ds(jnp.maximum(2 * n_pairs - 1, 0), 1), :]
    qk_bound = 2.0 * jnp.sqrt(qn_ref[0] * kn_ref[0]) + 1.0
    lane = lax.broadcasted_iota(jnp.int32, (nblk, LANE), 1)
    blk = lax.broadcasted_iota(jnp.int32, (nblk, 1), 0)
    worst = jnp.max(jnp.where(lane < N_HEADS, qk_bound + c_tile - cend, -jnp.inf), axis=1, keepdims=True)
    is_pair_end = jnp.logical_and((blk & 1) == 1, blk < 2 * n_pairs)
    skip = jnp.logical_and(is_pair_end, worst <= -FOX_SKIP_LIMIT)
    return jnp.sum(skip.astype(jnp.int32))


def _softmax_attn_pipelined_kernel(qT_ref, k_ref, vT_ref, eye_ref, *rest, mode, past, tq, tk, s_valid):
    o_ref, sa_ref, sb_ref, acc_ref = rest[-4:]
    qi = pl.program_id(1)
    q_lo = past + qi * tq
    n_pairs = past // tq + qi
    first_pair = _first_needed_pair(*rest[:3], n_pairs) if mode == 'fox' else 0
    qpos = q_lo + lax.broadcasted_iota(jnp.int32, (1, tq), 1)

    def qk(kb, dst_ref):
        off = pl.multiple_of(kb * tk, tk)
        for h in range(N_HEADS):
            dst_ref[h] = _dot(k_ref[0, pl.ds(off, tk), _head_rows(h)], qT_ref[0, _head_rows(h), :])

    def softmax_pv(kb, src_ref, stats, masked):
        if masked:
            kpos = kb * tk + lax.broadcasted_iota(jnp.int32, (tk, 1), 0)
            if mode == 'fox':
                vis = kpos <= qpos
            else:
                vis = (kpos // CHUNK) <= (qpos // CHUNK)
                if s_valid < k_ref.shape[1]:
                    vis = jnp.logical_and(vis, kpos < s_valid)
        out = []
        for h in range(N_HEADS):
            m, l = stats[h]
            s = src_ref[h]
            if masked:
                s = jnp.where(vis, s, NEG_INF)
            m_new = jnp.maximum(m, jnp.max(s, axis=0, keepdims=True))
            p = jnp.exp2(s - m_new)
            alpha = jnp.exp2(m - m_new)
            l = alpha * l + jnp.sum(p, axis=0, keepdims=True)
            rows = _value_rows(h)
            acc_ref[rows, :] = alpha * acc_ref[rows, :] + _dot(vT_ref[0, kb, rows, :], p.astype(BF16))
            out.append((m_new, l))
        return tuple(out)

    acc_ref[...] = jnp.zeros_like(acc_ref)
    qk(2 * first_pair, sa_ref)

    def pair(j, stats):
        kb = 2 * j
        qk(kb + 1, sb_ref)
        stats = softmax_pv(kb, sa_ref, stats, False)
        qk(kb + 2, sa_ref)
        return softmax_pv(kb + 1, sb_ref, stats, False)

    init = (jnp.full((1, tq), NEG_INF, F32), jnp.zeros((1, tq), F32))
    stats = lax.fori_loop(first_pair, n_pairs, pair, (init,) * N_HEADS)
    kb = 2 * n_pairs
    qk(kb + 1, sb_ref)
    stats = softmax_pv(kb, sa_ref, stats, True)
    stats = softmax_pv(kb + 1, sb_ref, stats, True)
    _finish([acc_ref[_value_rows(h), :] for h in range(N_HEADS)], [1.0 / st[1] for st in stats], eye_ref, o_ref)


def _stick_attn_kernel(qT_ref, k_ref, vT_ref, eye_ref, later_ref, o_ref, *, past, tq, tk):
    assert tk % tq == 0 and past % tq == 0
    qi = pl.program_id(1)
    ltk = int(math.log2(tk))
    q_lo = past + qi * tq
    own = q_lo >> ltk
    qpos = q_lo + lax.broadcasted_iota(jnp.int32, (1, tq), 1)

    def block_terms(kb, masked):
        kvs = [_load_kv(k_ref, vT_ref, kb, tk, h) for h in range(N_HEADS)]
        zs = [_dot(kvs[h][0], qT_ref[0, _head_rows(h), :]) for h in range(N_HEADS)]
        if masked:
            kpos = kb * tk + lax.broadcasted_iota(jnp.int32, (tk, 1), 0)
            vis = kpos < qpos
        out = []
        for h in range(N_HEADS):
            z = zs[h]
            drop = jnp.maximum(z, 0.0) + jnp.log2(1.0 + jnp.exp2(-jnp.abs(z)))
            if masked:
                drop = jnp.where(vis, drop, 0.0)
            hi = drop.astype(BF16)
            lo = (drop - hi.astype(F32)).astype(BF16)
            later = _dot(later_ref[...], hi) + _dot(later_ref[...], lo)
            logw = z - drop - later
            if masked:
                logw = jnp.where(vis, logw, NEG_INF)
            out.append((logw, jnp.sum(drop, axis=0, keepdims=True), kvs[h][1]))
        return out

    def accumulate(terms, carry, live=None):
        out = []
        for h in range(N_HEADS):
            logw, total, v_t = terms[h]
            decay, acc = carry[h]
            a = jnp.exp2(logw - decay)
            if live is not None:
                a = jnp.where(live, a, 0.0)
                total = jnp.where(live, total, 0.0)
            out.append((decay + total, acc + _dot(v_t, a.astype(BF16))))
        return tuple(out)

    def min_decay(carry):
        d = carry[0][0]
        for h in range(1, N_HEADS):
            d = jnp.minimum(d, carry[h][0])
        return jnp.min(d)

    init = (jnp.zeros((1, tq), F32), jnp.zeros((HEAD_DIM, tq), F32))
    own_terms = block_terms(own, True)
    prev_terms = block_terms(jnp.maximum(own - 1, 0), False)
    carry = accumulate(own_terms, (init,) * N_HEADS)
    carry = accumulate(prev_terms, carry, live=own >= 1)

    def cond(state):
        kb, dmin, _ = state
        return jnp.logical_and(kb >= 0, dmin < SB_DECAY_LIMIT)

    def body(state):
        kb, _, carry = state
        carry = accumulate(block_terms(kb, False), carry)
        return kb - 1, min_decay(carry), carry

    _, _, carry = lax.while_loop(cond, body, (own - 2, min_decay(carry), carry))
    _finish([c[1] for c in carry], [1.0] * N_HEADS, eye_ref, o_ref)


def _attention(mode, q_t, k, v_t, cst, *, past, s_valid, tq, tk, skip_stats=None):
    b, _, t = q_t.shape
    s = k.shape[1]
    grid = (b, t // tq)
    single = pl.Buffered(1)
    in_specs = [pl.BlockSpec((1, QW, tq), lambda i, j: (i, 0, j)),
                pl.BlockSpec((1, s, QW), lambda i, j: (i, 0, 0), pipeline_mode=single),
                pl.BlockSpec((1, s // tk, VW, tk), lambda i, j: (i, 0, 0, 0), pipeline_mode=single),
                _const_spec((tq, tq))]
    args = [q_t, k, v_t, jnp.eye(tq, dtype=BF16)]
    scratch = []
    if mode == 'sb':
        kern = functools.partial(_stick_attn_kernel, past=past, tq=tq, tk=tk)
        in_specs.append(_const_spec((tk, tk)))
        args.append(cst['later'][tk])
    elif tq == 2 * tk and past % tq == 0:
        kern = functools.partial(_softmax_attn_pipelined_kernel, mode=mode, past=past, tq=tq, tk=tk, s_valid=s_valid)
        scratch = [pltpu.VMEM((N_HEADS, tk, tq), F32), pltpu.VMEM((N_HEADS, tk, tq), F32), pltpu.VMEM((VW, tq), F32)]
        if mode == 'fox':
            cend, qn, kn = skip_stats
            stat_spec = pl.BlockSpec((1, 1, LANE), lambda i, j: (i, 0, 0))
            in_specs += [pl.BlockSpec((1, s // tk, LANE), lambda i, j: (i, 0, 0)), stat_spec, stat_spec]
            args += [cend.reshape(b, s // tk, LANE), qn, kn]
    else:
        kern = functools.partial(_softmax_attn_kernel, mode=mode, past=past, tq=tq, tk=tk, s_valid=s_valid)
    width = (N_HEADS // 2) * HP
    return pl.pallas_call(
        kern, grid=grid, in_specs=in_specs,
        out_specs=pl.BlockSpec((1, tq, width), lambda i, j: (i, j, 0)),
        out_shape=jax.ShapeDtypeStruct((b, t, width), BF16), scratch_shapes=scratch,
        compiler_params=_params("parallel", "parallel"), name="attn_" + mode,
    )(*args)


def _lru_kernel(lx_ref, lg_ref, cb_ref, h0_ref, cw_ref, cbias_ref, wr_ref, br_ref, wi_ref, bi_ref, lam_ref,
                y_ref, nb_ref, hl_ref, xcat_ref, h_ref, *, past, tc):
    ci = pl.program_id(1)
    keep = LRU_CONV - 1

    @pl.when(ci == 0)
    def _():
        xcat_ref[0:8, :] = jnp.zeros((8, LRU_WIDTH), F32)
        xcat_ref[8 - keep:8, :] = cb_ref[0]
        h_ref[...] = h0_ref[0]

    x = lx_ref[0]
    xcat_ref[8:8 + tc, :] = x
    xc = cbias_ref[...] + x * cw_ref[keep:keep + 1, :]
    for tap in range(keep):
        shift = keep - tap
        xc = xc + xcat_ref[8 - shift:8 - shift + tc, :] * cw_ref[tap:tap + 1, :]
    nb_ref[0] = xcat_ref[8 + tc - keep:8 + tc, :]
    xcat_ref[0:8, :] = x[tc - 8:tc, :]

    xcb = xc.astype(BF16)
    r = _sigmoid(_dot(xcb, wr_ref[...]) + br_ref[...])
    gate_in = _sigmoid(_dot(xcb, wi_ref[...]) + bi_ref[...])
    log_a = (-LRU_C) * r * _softplus(-lam_ref[...])
    row = lax.broadcasted_iota(jnp.int32, (tc, LRU_WIDTH), 0)
    reset = (past + ci * tc + row) == 0
    a = jnp.where(reset, 0.0, jnp.exp(log_a))
    y2 = 2.0 * log_a
    series = -y2 * (1.0 + y2 * (0.5 + y2 * (1.0 / 6.0 + y2 * (1.0 / 24.0 + y2 * (1.0 / 120.0)))))
    one_minus = jnp.where(y2 > -0.05, series, 1.0 - jnp.exp(y2))
    mult = jnp.where(reset, 1.0, jnp.sqrt(one_minus))
    u = mult * gate_in * xc

    d = 1
    while d < tc:
        ok = row >= d
        u = u + jnp.where(ok, a * pltpu.roll(u, d, 0), 0.0)
        a = jnp.where(ok, a * pltpu.roll(a, d, 0), a)
        d *= 2
    hs = a * h_ref[...] + u
    h_last = hs[tc - 1:tc, :]
    h_ref[...] = h_last
    hl_ref[0] = h_last

    g = lg_ref[0]
    gelu = 0.5 * g * (1.0 + jnp.tanh(0.7978845608028654 * (g + 0.044715 * g * g * g)))
    y_ref[0] = (hs * gelu).astype(BF16)


def _lru(lx, lg, conv_buf, h0, lw, past):
    b, t, w = lx.shape
    tc = _pick(t, (256, 128, 64, 32))
    grid = (b, t // tc)
    row = pl.BlockSpec((1, tc, w), lambda i, j: (i, j, 0))
    keep = LRU_CONV - 1
    in_specs = [row, row, pl.BlockSpec((1, keep, w), lambda i, j: (i, 0, 0)), pl.BlockSpec((1, 1, w), lambda i, j: (i, 0, 0)),
                _const_spec((LRU_CONV, w)), _const_spec((1, w)), _const_spec((w, w)), _const_spec((1, w)),
                _const_spec((w, w)), _const_spec((1, w)), _const_spec((1, w))]
    sds = jax.ShapeDtypeStruct
    return pl.pallas_call(
        functools.partial(_lru_kernel, past=past, tc=tc), grid=grid, in_specs=in_specs,
        out_specs=[row, pl.BlockSpec((1, keep, w), lambda i, j: (i, 0, 0)), pl.BlockSpec((1, 1, w), lambda i, j: (i, 0, 0))],
        out_shape=[sds((b, t, w), BF16), sds((b, keep, w), F32), sds((b, 1, w), F32)],
        scratch_shapes=[pltpu.VMEM((tc + 8, w), F32), pltpu.VMEM((1, w), F32)],
        compiler_params=_params("parallel", "arbitrary"), name="rg_lru",
    )(lx, lg, conv_buf, h0, lw['lru_conv_w'], lw['lru_conv_b'], lw['lru_wr'], lw['lru_br'], lw['lru_wi'],
      lw['lru_bi'], lw['lru_lam'])


def _merge_kernel(x_ref, oa_ref, ob_ref, oc_ref, od_ref, g1_ref, g2_ref, wg_ref, wb_ref, wo_ref, out_ref):
    x = x_ref[0]
    d = x.shape[1]
    hb = _rms(x, g1_ref[...]).astype(BF16)
    branches = (oa_ref[0], ob_ref[0], oc_ref[0], od_ref[0])
    merged = None
    for n, o in enumerate(branches):
        gate = _sigmoid(_dot(hb, wg_ref[:, n * d:(n + 1) * d]))
        term = gate * _dot(o, wb_ref[n])
        merged = term if merged is None else merged + term
    y = _dot(merged.astype(BF16), wo_ref[...])
    out_ref[0] = x + _rms(y, g2_ref[...])


def _merge(x, o_a, o_b, o_c, o_d, lw):
    b, t, d = x.shape
    tm = _pick(t, (256, 128))
    grid = (b, t // tm)
    row = lambda w: pl.BlockSpec((1, tm, w), lambda i, j: (i, j, 0))
    bw = BRANCH_WIDTH
    in_specs = [row(d), row(bw), row(bw), row(bw), row(bw), _const_spec((1, d)), _const_spec((1, d)),
                _const_spec((d, N_BRANCH * d)), _const_spec((N_BRANCH, BRANCH_WIDTH, d)), _const_spec((d, d))]
    return pl.pallas_call(
        _merge_kernel, grid=grid, in_specs=in_specs, out_specs=row(d),
        out_shape=jax.ShapeDtypeStruct((b, t, d), F32),
        compiler_params=_params("parallel", "parallel"), name="merge",
    )(x, o_a, o_b, o_c, o_d, lw['ln_mix_pre'], lw['ln_mix_post'], lw['w_gate'], lw['w_branch'], lw['w_out'])


def _mem_kv_kernel(mem_ref, g_ref, wk_ref, wv_ref, k_ref, v_ref):
    mn = _rms(mem_ref[0], g_ref[...]).astype(BF16)
    k_ref[0] = _dot(mn, wk_ref[...])
    v_ref[0] = _dot(mn, wv_ref[...])


def _mem_kv(mem, lw):
    b, m, d = mem.shape
    w = lw['mem_wk'].shape[1]
    out = jax.ShapeDtypeStruct((b, m, w), F32)
    blk = pl.BlockSpec((1, m, w), lambda i: (i, 0, 0))
    return pl.pallas_call(
        _mem_kv_kernel, grid=(b,),
        in_specs=[pl.BlockSpec((1, m, d), lambda i: (i, 0, 0)), _const_spec((1, d)), _const_spec((d, w)), _const_spec((d, w))],
        out_specs=[blk, blk], out_shape=[out, out], compiler_params=_params("parallel"), name="mem_kv",
    )(mem, lw['mem_norm'], lw['mem_wk'], lw['mem_wv'])


def _mem_attn_kernel(x_ref, mk_ref, mv_ref, g1_ref, g2_ref, wq_ref, wo_ref, out_ref):
    x = x_ref[0]
    hb = _rms(x, g1_ref[...]).astype(BF16)
    q = (_dot(hb, wq_ref[...]) * (MEM_HEAD_DIM ** -0.5)).astype(BF16)
    heads = []
    for h in range(N_HEADS):
        sl = slice(h * MEM_HEAD_DIM, (h + 1) * MEM_HEAD_DIM)
        s = _dot_nt(q[:, sl], mk_ref[0, :, sl])
        p = jnp.exp(s - jnp.max(s, axis=-1, keepdims=True))
        inv = 1.0 / jnp.sum(p, axis=-1, keepdims=True)
        heads.append((_dot(p.astype(BF16), mv_ref[0, :, sl]) * inv).astype(BF16))
    y = _dot(jnp.concatenate(heads, axis=1), wo_ref[...])
    out_ref[0] = x + _rms(y, g2_ref[...])


def _mem_attn(x, mk, mv, lw):
    b, t, d = x.shape
    m, w = mk.shape[1], mk.shape[2]
    tm = _pick(t, (512, 256, 128))
    row = pl.BlockSpec((1, tm, d), lambda i, j: (i, j, 0))
    kv = pl.BlockSpec((1, m, w), lambda i, j: (i, 0, 0))
    return pl.pallas_call(
        _mem_attn_kernel, grid=(b, t // tm),
        in_specs=[row, kv, kv, _const_spec((1, d)), _const_spec((1, d)), _const_spec((d, w)), _const_spec((w, d))],
        out_specs=row, out_shape=jax.ShapeDtypeStruct((b, t, d), F32),
        compiler_params=_params("parallel", "parallel"), name="mem_attn",
    )(x, mk, mv, lw['ln_mem_pre'], lw['ln_mem_post'], lw['mem_wq'], lw['mem_wo'])


def _ffn_kernel(x_ref, g1_ref, g2_ref, wg_ref, wu_ref, wd_ref, out_ref):
    x = x_ref[0]
    hb = _rms(x, g1_ref[...]).astype(BF16)
    gate = _dot(hb, wg_ref[...])
    act = (gate * _sigmoid(gate) * _dot(hb, wu_ref[...])).astype(BF16)
    y = _dot(act, wd_ref[...])
    out_ref[0] = x + _rms(y, g2_ref[...])


def _ffn(x, lw):
    b, t, d = x.shape
    f = lw['ffn_wg'].shape[1]
    tm = _pick(t, (256, 128))
    row = pl.BlockSpec((1, tm, d), lambda i, j: (i, j, 0))
    single = pl.Buffered(1)
    return pl.pallas_call(
        _ffn_kernel, grid=(b, t // tm),
        in_specs=[row, _const_spec((1, d)), _const_spec((1, d)),
                  pl.BlockSpec((d, f), lambda i, j: (0, 0), pipeline_mode=single),
                  pl.BlockSpec((d, f), lambda i, j: (0, 0), pipeline_mode=single),
                  pl.BlockSpec((f, d), lambda i, j: (0, 0), pipeline_mode=single)],
        out_specs=row, out_shape=jax.ShapeDtypeStruct((b, t, d), F32),
        compiler_params=_params("parallel", "parallel"), name="ffn",
    )(x, lw['ln_ffn_pre'], lw['ln_ffn_post'], lw['ffn_wg'], lw['ffn_wu'], lw['ffn_wd'])


def _constants():
    fk, fq, ones_k, ones_q = _fox_feature_maps()
    later = {}
    for tk in (256,):
        idx = np.arange(tk)
        later[tk] = jnp.asarray((idx[None, :] > idx[:, None]).astype(np.float32), BF16)
    return dict(
        ek=jnp.asarray(_place_qk(), BF16), evT=jnp.eye(VW, dtype=BF16),
        fkmap=jnp.asarray(fk, BF16), fqmapT=jnp.asarray(np.transpose(fq, (0, 2, 1)), BF16),
        ones_k=jnp.asarray(ones_k, F32), ones_qT=jnp.asarray(ones_q.T, F32),
        ekpe=jnp.asarray(_place_kpe(), BF16), later=later,
        seg=jnp.asarray(np.repeat(np.eye(N_HEADS, LANE, dtype=np.float32), HEAD_DIM, axis=0), BF16))


def _pad_cols(w, n):
    return jnp.pad(w, ((0, 0), (0, n - w.shape[1])))


def _layer_weights(l, p):
    d = p['w_in'].shape[1]
    w_in = p['w_in'][l]
    sizes = (256, 256, 256, N_HEADS, LRU_WIDTH, LRU_WIDTH, MLA_Q_RANK, MLA_KV_RANK, MLA_ROPE, 256, 256, 256)
    offs = np.concatenate([[0], np.cumsum(sizes)])
    fq, fk, fv, ff, lx, lg, cq, ckv, kpe, sq, sk, sv = [w_in[:, offs[i]:offs[i + 1]] for i in range(len(sizes))]
    half = MLA_ROPE // 2
    kpe_sw = jnp.concatenate([kpe[:, half:], kpe[:, :half]], axis=1)
    w_mix = jnp.concatenate([fq, fk, fv, lx, lg, cq, ckv, _pad_cols(kpe, LANE), _pad_cols(kpe_sw, LANE),
                             _pad_cols(ff, LANE), sq, sk, sv], axis=1).astype(BF16)

    qk = MLA_NOPE + MLA_ROPE
    wuq = p['mla_w_uq'][l].reshape(MLA_Q_RANK, N_HEADS, qk)
    rope = wuq[:, :, MLA_NOPE:]
    rope_sw = jnp.concatenate([rope[:, :, half:], rope[:, :, :half]], axis=2)
    wuq_p = jnp.pad(wuq, ((0, 0), (0, 0), (0, HP - qk))).reshape(MLA_Q_RANK, QW)
    wuqsw_p = jnp.pad(rope_sw, ((0, 0), (0, 0), (MLA_NOPE, HP - qk))).reshape(MLA_Q_RANK, QW)
    wuk = p['mla_w_uk'][l].reshape(MLA_KV_RANK, N_HEADS, MLA_NOPE)
    wuk_p = jnp.pad(wuk, ((0, 0), (0, 0), (0, HP - MLA_NOPE))).reshape(MLA_KV_RANK, QW)

    def block_diag(w):
        return jax.scipy.linalg.block_diag(*[w[i] for i in range(w.shape[0])]).astype(BF16)

    row = lambda v: v[l].reshape(1, -1).astype(F32)
    return dict(
        w_mix=w_mix, w_gate=w_in[:, offs[-1]:].astype(BF16),
        ln_mix_pre=row(p['ln_mix_pre']), ln_mix_post=row(p['ln_mix_post']),
        fox_bf=_pad_cols(row(p['fox_bf']), LANE),
        mla_q_norm=row(p['mla_q_norm']), mla_kv_norm=row(p['mla_kv_norm']),
        wuqT=wuq_p.T.astype(BF16), wuqswT=wuqsw_p.T.astype(BF16), wuk=wuk_p.astype(BF16), wuvT=p['mla_w_uv'][l].T.astype(BF16),
        eqT=jnp.asarray(_place_qk().T, BF16),
        lru_conv_w=p['lru_conv_w'][l].astype(F32), lru_conv_b=row(p['lru_conv_b']),
        lru_wr=block_diag(p['lru_wr'][l]), lru_br=row(p['lru_br']),
        lru_wi=block_diag(p['lru_wi'][l]), lru_bi=row(p['lru_bi']), lru_lam=row(p['lru_lam']),
        w_branch=p['w_branch'][l].astype(BF16), w_out=p['w_out'][l].astype(BF16),
        ln_mem_pre=row(p['ln_mem_pre']), ln_mem_post=row(p['ln_mem_post']), mem_norm=row(p['mem_norm']),
        mem_wq=p['mem_wq'][l].astype(BF16), mem_wk=p['mem_wk'][l].astype(BF16),
        mem_wv=p['mem_wv'][l].astype(BF16), mem_wo=p['mem_wo'][l].astype(BF16),
        ln_ffn_pre=row(p['ln_ffn_pre']), ln_ffn_post=row(p['ln_ffn_post']),
        ffn_wg=p['ffn_wg'][l].astype(BF16), ffn_wu=p['ffn_wu'][l].astype(BF16), ffn_wd=p['ffn_wd'][l].astype(BF16))


def _rope_tables(past, t):
    half = MLA_ROPE // 2
    inv = jnp.power(ROPE_BASE, -jnp.arange(half, dtype=F32) / half)
    ang = (past + jnp.arange(t, dtype=jnp.int32)).astype(F32)[:, None] * inv
    cos, sin = jnp.cos(ang), jnp.sin(ang)
    c32 = jnp.concatenate([cos, cos], axis=1)
    s32 = jnp.concatenate([-sin, sin], axis=1)
    scale = (MLA_NOPE + MLA_ROPE) ** -0.5 * LOG2E
    slot_c = jnp.concatenate([jnp.ones((t, MLA_NOPE), F32), c32, jnp.zeros((t, HP - MLA_NOPE - MLA_ROPE), F32)], axis=1)
    slot_s = jnp.concatenate([jnp.zeros((t, MLA_NOPE), F32), s32, jnp.zeros((t, HP - MLA_NOPE - MLA_ROPE), F32)], axis=1)
    return dict(ck=_pad_cols(c32, LANE), sk=_pad_cols(s32, LANE),
                cqT=(jnp.tile(slot_c, (1, N_HEADS)) * scale).T, sqT=(jnp.tile(slot_s, (1, N_HEADS)) * scale).T)


def _trunk_layer(x, past_state, mem_k, mem_v, lw, cst):
    b, t, _ = x.shape
    past = 0 if past_state is None else past_state['fox_k'].shape[1]
    tk = 256
    fq, fk, fv, lx, lg, ckv_n, kpe_slab, logf_slab, mla_qT, sb_qT, sk, sv = _in_proj(x, lw, _rope_tables(past, t))

    if past_state is None:
        s_valid = t
        keys = (fq, fk, fv, logf_slab, ckv_n, kpe_slab, sk, sv)
        conv_buf = jnp.zeros((b, LRU_CONV - 1, LRU_WIDTH), F32)
        h0 = jnp.zeros((b, 1, LRU_WIDTH), F32)
    else:
        s_valid = past + t
        s_pad = -(-s_valid // tk) * tk

        def cat(old, new):
            old = old.reshape(b, past, -1).astype(new.dtype)
            old = jnp.pad(old, ((0, 0), (0, 0), (0, new.shape[2] - old.shape[2])))
            return jnp.pad(jnp.concatenate([old, new], axis=1), ((0, 0), (0, s_pad - s_valid), (0, 0)))

        keys = (cat(jnp.zeros((b, past, 256), BF16), fq), cat(past_state['fox_k'], fk), cat(past_state['fox_v'], fv),
                cat(past_state['fox_logf'], logf_slab), cat(past_state['mla_ckv'], ckv_n),
                cat(past_state['mla_kpe'], kpe_slab), cat(past_state['sb_k'], sk), cat(past_state['sb_v'], sv))
        conv_buf = past_state['lru_conv'].astype(F32)
        h0 = past_state['lru_h'].reshape(b, 1, LRU_WIDTH).astype(F32)

    fox_qT, fox_k, fox_vT, mla_k, mla_vT, sb_k, sb_vT, *fox_skip_stats = _kv_prep(*keys, lw, cst, tk)
    if past_state is not None:
        fox_qT = fox_qT[:, :, past:past + t]

    att = functools.partial(_attention, cst=cst, past=past, s_valid=s_valid, tk=tk)
    tq_softmax = _pick(t, (2 * tk, tk, 128))
    o_a = att('fox', fox_qT, fox_k, fox_vT, tq=tq_softmax, skip_stats=fox_skip_stats)
    o_c = att('mla', mla_qT, mla_k, mla_vT, tq=tq_softmax)
    o_d = att('sb', sb_qT, sb_k, sb_vT, tq=_pick(t, (tk, 128)))
    o_b, lru_conv, lru_h = _lru(lx, lg, conv_buf, h0, lw, past)

    x = _merge(x, o_a, o_b, o_c, o_d, lw)
    x = _mem_attn(x, mem_k, mem_v, lw)
    x = _ffn(x, lw)

    new = dict(fox_k=fk.reshape(b, t, N_HEADS, HEAD_DIM), fox_v=fv.reshape(b, t, N_HEADS, HEAD_DIM),
               fox_logf=logf_slab[:, :, :N_HEADS], lru_h=lru_h.reshape(b, LRU_WIDTH), lru_conv=lru_conv,
               mla_ckv=ckv_n, mla_kpe=kpe_slab[:, :, :MLA_ROPE],
               sb_k=sk.reshape(b, t, N_HEADS, HEAD_DIM), sb_v=sv.reshape(b, t, N_HEADS, HEAD_DIM))
    return x, new


def kernel(x_prompt, x_sample, cache_fox_k, cache_fox_v, cache_fox_logf, state_lru_h, state_lru_conv, cache_mla_ckv, cache_mla_kpe, cache_sb_k, cache_sb_v, cache_mem_k, cache_mem_v, mem_prompt, ln_mix_pre, ln_mix_post, w_in, fox_bf, lru_conv_w, lru_conv_b, lru_wr, lru_br, lru_wi, lru_bi, lru_lam, mla_q_norm, mla_w_uq, mla_kv_norm, mla_w_uk, mla_w_uv, w_branch, w_out, ln_mem_pre, ln_mem_post, mem_norm, mem_wq, mem_wk, mem_wv, mem_wo, ln_ffn_pre, ln_ffn_post, ffn_wg, ffn_wu, ffn_wd):
    params = dict(ln_mix_pre=ln_mix_pre, ln_mix_post=ln_mix_post, w_in=w_in, fox_bf=fox_bf, lru_conv_w=lru_conv_w,
                  lru_conv_b=lru_conv_b, lru_wr=lru_wr, lru_br=lru_br, lru_wi=lru_wi, lru_bi=lru_bi, lru_lam=lru_lam,
                  mla_q_norm=mla_q_norm, mla_w_uq=mla_w_uq, mla_kv_norm=mla_kv_norm, mla_w_uk=mla_w_uk,
                  mla_w_uv=mla_w_uv, w_branch=w_branch, w_out=w_out, ln_mem_pre=ln_mem_pre, ln_mem_post=ln_mem_post,
                  mem_norm=mem_norm, mem_wq=mem_wq, mem_wk=mem_wk, mem_wv=mem_wv, mem_wo=mem_wo,
                  ln_ffn_pre=ln_ffn_pre, ln_ffn_post=ln_ffn_post, ffn_wg=ffn_wg, ffn_wu=ffn_wu, ffn_wd=ffn_wd)
    depth = w_in.shape[0]
    cst = _constants()
    weights = [_layer_weights(l, params) for l in range(depth)]
    bp, mem_len = mem_prompt.shape[0], mem_prompt.shape[1]

    y_prompt, p_states = x_prompt, []
    for l in range(depth):
        mk, mv = _mem_kv(mem_prompt, weights[l])
        y_prompt, st = _trunk_layer(y_prompt, None, mk.astype(BF16), mv.astype(BF16), weights[l], cst)
        st['mem_k'] = mk.reshape(bp, mem_len, N_HEADS, MEM_HEAD_DIM)
        st['mem_v'] = mv.reshape(bp, mem_len, N_HEADS, MEM_HEAD_DIM)
        p_states.append(st)

    y_sample, s_states = x_sample, []
    bs = x_sample.shape[0]
    for l in range(depth):
        past = dict(fox_k=cache_fox_k[l], fox_v=cache_fox_v[l], fox_logf=cache_fox_logf[l], lru_h=state_lru_h[l],
                    lru_conv=state_lru_conv[l], mla_ckv=cache_mla_ckv[l], mla_kpe=cache_mla_kpe[l],
                    sb_k=cache_sb_k[l], sb_v=cache_sb_v[l])
        mk = cache_mem_k[l].reshape(bs, mem_len, -1).astype(BF16)
        mv = cache_mem_v[l].reshape(bs, mem_len, -1).astype(BF16)
        y_sample, st = _trunk_layer(y_sample, past, mk, mv, weights[l], cst)
        s_states.append(st)

    def stk(states, name):
        return jnp.stack([s[name] for s in states])

    p_names = ('fox_k', 'fox_v', 'fox_logf', 'lru_h', 'lru_conv', 'mla_ckv', 'mla_kpe', 'sb_k', 'sb_v', 'mem_k', 'mem_v')
    s_names = p_names[:9]
    return ((y_prompt, y_sample) + tuple(stk(p_states, n) for n in p_names)
            + tuple(stk(s_states, n) for n in s_names))
```

```python
import functools
import math

import numpy as np
import jax
import jax.numpy as jnp
from jax import lax
from jax.experimental import pallas as pl
from jax.experimental.pallas import tpu as pltpu

F32 = jnp.float32
BF16 = jnp.bfloat16

CHUNK = 64
HEAD_DIM = 64
N_HEADS = 4
LRU_WIDTH = 256
LRU_CONV = 4
LRU_C = 8.0
MLA_Q_RANK = 256
MLA_KV_RANK = 128
MLA_NOPE = 64
MLA_ROPE = 32
MLA_V = 64
ROPE_BASE = 10000.0
N_BRANCH = 4
BRANCH_WIDTH = 256
MEM_HEAD_DIM = 128
EPS = 1e-6
NEG_INF = -1e30

HP = 128
QW = N_HEADS * HP
VW = N_HEADS * HEAD_DIM
LANE = 128
VMEM_LIMIT_BYTES = 56 * 1024 * 1024
LOG2E = 1.4426950408889634
SB_DECAY_LIMIT = 127.0
FOX_SKIP_LIMIT = 150.0

_NT = (((1,), (1,)), ((), ()))


def _dot(a, b):
    return jnp.dot(a, b, preferred_element_type=F32)


def _dot_nt(a, b):
    return lax.dot_general(a, b, _NT, preferred_element_type=F32)


def _rms(x, g):
    ms = jnp.mean(x * x, axis=-1, keepdims=True)
    return x * lax.rsqrt(ms + EPS) * g


def _sigmoid(x):
    return 1.0 / (1.0 + jnp.exp(-x))


def _softplus(x):
    return jnp.maximum(x, 0.0) + jnp.log(1.0 + jnp.exp(-jnp.abs(x)))


def _pick(n, cands):
    for c in cands:
        if n % c == 0:
            return c
    return n


def _params(*sem):
    return pltpu.CompilerParams(dimension_semantics=sem, vmem_limit_bytes=VMEM_LIMIT_BYTES)


def _const_spec(shape):
    nd = len(shape)
    return pl.BlockSpec(shape, lambda *_: (0,) * nd)


def _place_qk():
    e = np.zeros((N_HEADS * HEAD_DIM, QW), np.float32)
    for h in range(N_HEADS):
        for j in range(HEAD_DIM):
            e[h * HEAD_DIM + j, h * HP + j] = 1.0
    return e


def _fox_feature_maps():
    fk = np.zeros((3, LANE, QW), np.float32)
    fq = np.zeros((3, LANE, QW), np.float32)
    ones_k = np.zeros((1, QW), np.float32)
    ones_q = np.zeros((1, QW), np.float32)
    for h in range(N_HEADS):
        for part in range(3):
            fq[part, h, h * HP + HEAD_DIM + part] = 1.0
            fk[part, h, h * HP + HEAD_DIM + 3 + part] = -1.0
            ones_k[0, h * HP + HEAD_DIM + part] = 1.0
            ones_q[0, h * HP + HEAD_DIM + 3 + part] = 1.0
    return fk, fq, ones_k, ones_q


def _place_kpe():
    e = np.zeros((LANE, QW), np.float32)
    for h in range(N_HEADS):
        for j in range(MLA_ROPE):
            e[j, h * HP + MLA_NOPE + j] = 1.0
    return e


_W_OFF = dict(fq=0, fk=256, fv=512, lx=768, lg=1024, cq=1280, ckv=1536, kpe=1664, kpe_sw=1792, ff=1920,
              sq=2048, sk=2304, sv=2560)
_W_MIX_COLS = 2816


_N_IN_PROJ_INPUTS = 13
_IN_PROJ_STATE_OUTPUTS = (1, 2, 5, 6, 7, 10, 11)


def _in_proj_kernel(*refs):
    (x_ref, g_ref, w_ref, qn_ref, kvn_ref, bf_ref, ck_ref, sk_tab_ref, cqT_ref, sqT_ref,
     wuqT_ref, wuqswT_ref, eqT_ref) = refs[:_N_IN_PROJ_INPUTS]
    (fq_ref, fk_ref, fv_ref, lx_ref, lg_ref, ckv_ref, kpe_ref, logf_ref, mqT_ref, sbqT_ref, sk_ref, sv_ref) = refs[-12:]
    fk_ref, fv_ref, ckv_ref, kpe_ref, logf_ref, sk_ref, sv_ref = (
        r.at[0] for r in (fk_ref, fv_ref, ckv_ref, kpe_ref, logf_ref, sk_ref, sv_ref))
    hb = _rms(x_ref[0], g_ref[...]).astype(BF16)

    def proj(name, width):
        a = _W_OFF[name]
        return _dot(hb, w_ref[:, a:a + width])

    fq_ref[0] = (proj('fq', 256) * (HEAD_DIM ** -0.5 * LOG2E)).astype(BF16)
    fk_ref[0] = proj('fk', 256)
    fv_ref[0] = proj('fv', 256)
    lx_ref[0] = proj('lx', 256)
    lg_ref[0] = proj('lg', 256)

    cqn = _rms(proj('cq', 256), qn_ref[...]).astype(BF16)
    q_t = _dot_nt(wuqT_ref[...], cqn)
    qsw_t = _dot_nt(wuqswT_ref[...], cqn)
    mqT_ref[0] = (q_t * cqT_ref[...] + qsw_t * sqT_ref[...]).astype(BF16)

    ckv_ref[0] = _rms(proj('ckv', 128), kvn_ref[...])
    kpe_ref[0] = proj('kpe', 128) * ck_ref[...] + proj('kpe_sw', 128) * sk_tab_ref[...]

    ff = proj('ff', 128) + bf_ref[...]
    log_sig = jnp.minimum(ff, 0.0) - jnp.log(1.0 + jnp.exp(-jnp.abs(ff)))
    lane = lax.broadcasted_iota(jnp.int32, ff.shape, 1)
    logf_ref[0] = jnp.where(lane < N_HEADS, log_sig, 0.0)

    sq = (proj('sq', 256) * (HEAD_DIM ** -0.5 * LOG2E)).astype(BF16)
    sbqT_ref[0] = _dot_nt(eqT_ref[...], sq).astype(BF16)
    sk_ref[0] = proj('sk', 256)
    sv_ref[0] = proj('sv', 256)


def _in_proj(x, lw, tabs, layer, depth, earlier):
    b, t, d = x.shape
    tm = _pick(t, (512, 256, 128))
    grid = (b, t // tm)
    row = lambda w: pl.BlockSpec((1, tm, w), lambda i, j: (i, j, 0))
    srow = lambda w: pl.BlockSpec((1, 1, tm, w), lambda i, j: (layer, i, j, 0))
    colT = pl.BlockSpec((1, QW, tm), lambda i, j: (i, 0, j))
    in_specs = [
        row(d), _const_spec((1, d)), _const_spec((d, _W_MIX_COLS)),
        _const_spec((1, MLA_Q_RANK)), _const_spec((1, MLA_KV_RANK)), _const_spec((1, LANE)),
        pl.BlockSpec((tm, LANE), lambda i, j: (j, 0)), pl.BlockSpec((tm, LANE), lambda i, j: (j, 0)),
        pl.BlockSpec((QW, tm), lambda i, j: (0, j)), pl.BlockSpec((QW, tm), lambda i, j: (0, j)),
        _const_spec((QW, MLA_Q_RANK)), _const_spec((QW, MLA_Q_RANK)), _const_spec((QW, 256)),
    ]
    sds = jax.ShapeDtypeStruct
    state = lambda w: sds((depth, b, t, w), F32)
    out_shape = [
        sds((b, t, 256), BF16),
        state(256), state(256),
        sds((b, t, 256), F32), sds((b, t, 256), F32),
        state(128), state(LANE), state(LANE),
        sds((b, QW, t), BF16), sds((b, QW, t), BF16),
        state(256), state(256),
    ]
    out_specs = [row(256), srow(256), srow(256), row(256), row(256), srow(128), srow(LANE), srow(LANE),
                 colT, colT, srow(256), srow(256)]
    assert len(in_specs) == _N_IN_PROJ_INPUTS
    aliases = {}
    if earlier is not None:
        in_specs += [pl.BlockSpec(memory_space=pl.ANY)] * len(earlier)
        aliases = {_N_IN_PROJ_INPUTS + n: out for n, out in enumerate(_IN_PROJ_STATE_OUTPUTS)}
    return pl.pallas_call(
        _in_proj_kernel, grid=grid, in_specs=in_specs, out_specs=out_specs, out_shape=out_shape,
        input_output_aliases=aliases, compiler_params=_params("parallel", "parallel"), name="in_proj",
    )(x, lw['ln_mix_pre'], lw['w_mix'], lw['mla_q_norm'], lw['mla_kv_norm'], lw['fox_bf'],
      tabs['ck'], tabs['sk'], tabs['cqT'], tabs['sqT'], lw['wuqT'], lw['wuqswT'], lw['eqT'], *(earlier or ()))


def _cumsum_rows(x):
    n = x.shape[0]
    row = lax.broadcasted_iota(jnp.int32, x.shape, 0)
    d = 1
    while d < n:
        x = x + jnp.where(row >= d, pltpu.roll(x, d, 0), 0.0)
        d *= 2
    return x


def _kv_prep_kernel(fq_ref, fk_ref, fv_ref, logf_ref, ckv_ref, kpe_ref, sk_ref, sv_ref,
                    ek_ref, evT_ref, eqT_ref, fkmap_ref, fqmapT_ref, onesk_ref, onesq_ref,
                    wuk_ref, ekpe_ref, wuvT_ref, seg_ref,
                    fqT_out, fk_out, fvT_out, mk_out, mvT_out, sbk_out, sbvT_out, cend_out, qn_out, kn_out,
                    carry_ref, *, tk):
    fk_ref, fv_ref, logf_ref, ckv_ref, kpe_ref, sk_ref, sv_ref = (
        r.at[0] for r in (fk_ref, fv_ref, logf_ref, ckv_ref, kpe_ref, sk_ref, sv_ref))

    @pl.when(pl.program_id(1) == 0)
    def _():
        carry_ref[...] = jnp.zeros_like(carry_ref)
        qn_out[...] = jnp.zeros_like(qn_out)
        kn_out[...] = jnp.zeros_like(kn_out)

    ts = fk_ref.shape[1]
    c = _cumsum_rows(logf_ref[0]) + carry_ref[...]
    carry_ref[...] = c[ts - 1:ts, :]
    c = c * LOG2E
    for j in range(ts // tk):
        cend_out[0, j] = c[(j + 1) * tk - 1:(j + 1) * tk, :]

    def norm_bound(x):
        xf = x.astype(F32)
        sums = _dot((xf * xf).astype(BF16), seg_ref[...]) * 1.01
        return jnp.max(sums, axis=0, keepdims=True)

    qn_out[0] = jnp.maximum(qn_out[0], norm_bound(fq_ref[0]))
    kn_out[0] = jnp.maximum(kn_out[0], norm_bound(fk_ref[0].astype(BF16)))
    c_hi = c.astype(BF16)
    rem = c - c_hi.astype(F32)
    c_mid = rem.astype(BF16)
    c_lo = (rem - c_mid.astype(F32)).astype(BF16)

    def store_vt(out, v_t):
        for j in range(ts // tk):
            out[0, j] = v_t[:, j * tk:(j + 1) * tk]

    fk_out[0] = (_dot(fk_ref[0].astype(BF16), ek_ref[...]) + _dot(c_hi, fkmap_ref[0]) + _dot(c_mid, fkmap_ref[1])
                 + _dot(c_lo, fkmap_ref[2]) + onesk_ref[...]).astype(BF16)
    fqT_out[0] = (_dot_nt(eqT_ref[...], fq_ref[0]) + _dot_nt(fqmapT_ref[0], c_hi) + _dot_nt(fqmapT_ref[1], c_mid)
                  + _dot_nt(fqmapT_ref[2], c_lo) + onesq_ref[...]).astype(BF16)
    store_vt(fvT_out, _dot_nt(evT_ref[...], fv_ref[0].astype(BF16)).astype(BF16))

    ckv = ckv_ref[0].astype(BF16)
    mk_out[0] = (_dot(ckv, wuk_ref[...]) + _dot(kpe_ref[0].astype(BF16), ekpe_ref[...])).astype(BF16)
    store_vt(mvT_out, _dot_nt(wuvT_ref[...], ckv).astype(BF16))

    sbk_out[0] = _dot(sk_ref[0].astype(BF16), ek_ref[...]).astype(BF16)
    store_vt(sbvT_out, _dot_nt(evT_ref[...], sv_ref[0].astype(BF16)).astype(BF16))


def _kv_prep(fq, fk, fv, logf, ckv, kpe, sk, sv, layer, lw, cst, tk):
    _, b, s, _ = fk.shape
    ts = _pick(s, (1024, 512, 256))
    grid = (b, s // ts)
    row = lambda w: pl.BlockSpec((1, ts, w), lambda i, j: (i, j, 0))
    srow = lambda w: pl.BlockSpec((1, 1, ts, w), lambda i, j: (layer, i, j, 0))
    colT = pl.BlockSpec((1, QW, ts), lambda i, j: (i, 0, j))
    vT = pl.BlockSpec((1, ts // tk, VW, tk), lambda i, j: (i, j, 0, 0))
    in_specs = [row(256), srow(256), srow(256), srow(LANE), srow(128), srow(LANE), srow(256), srow(256),
                _const_spec((256, QW)), _const_spec((VW, VW)), _const_spec((QW, 256)),
                _const_spec((3, LANE, QW)), _const_spec((3, QW, LANE)), _const_spec((1, QW)), _const_spec((QW, 1)),
                _const_spec((MLA_KV_RANK, QW)), _const_spec((LANE, QW)), _const_spec((VW, MLA_KV_RANK)),
                _const_spec((VW, LANE))]
    sds = jax.ShapeDtypeStruct
    slab = sds((b, s, QW), BF16)
    slab_t = sds((b, s // tk, VW, tk), BF16)
    stat = sds((b, 1, LANE), F32)
    stat_spec = pl.BlockSpec((1, 1, LANE), lambda i, j: (i, 0, 0))
    out_shape = [sds((b, QW, s), BF16), slab, slab_t, slab, slab_t, slab, slab_t,
                 sds((b, s // tk, 1, LANE), F32), stat, stat]
    out_specs = [colT, row(QW), vT, row(QW), vT, row(QW), vT,
                 pl.BlockSpec((1, ts // tk, 1, LANE), lambda i, j: (i, j, 0, 0)), stat_spec, stat_spec]
    return pl.pallas_call(
        functools.partial(_kv_prep_kernel, tk=tk), grid=grid, in_specs=in_specs, out_specs=out_specs,
        out_shape=out_shape, scratch_shapes=[pltpu.VMEM((1, LANE), F32)],
        compiler_params=_params("parallel", "arbitrary"), name="kv_prep",
    )(fq, fk, fv, logf, ckv, kpe, sk, sv, cst['ek'], cst['evT'], lw['eqT'], cst['fkmap'], cst['fqmapT'],
      cst['ones_k'], cst['ones_qT'], lw['wuk'], cst['ekpe'], lw['wuvT'], cst['seg'])


def _head_rows(h):
    return slice(h * HP, (h + 1) * HP)


def _value_rows(h):
    return slice(h * HEAD_DIM, (h + 1) * HEAD_DIM)


def _load_kv(k_ref, vT_ref, kb, tk, h):
    off = pl.multiple_of(kb * tk, tk)
    return k_ref[0, pl.ds(off, tk), _head_rows(h)], vT_ref[0, kb, _value_rows(h), :]


def _finish(accs, invs, eye_ref, o_ref):
    for pair in range(N_HEADS // 2):
        o_t = jnp.concatenate([accs[2 * pair] * invs[2 * pair], accs[2 * pair + 1] * invs[2 * pair + 1]], axis=0)
        o_ref[0, :, _head_rows(pair)] = _dot_nt(eye_ref[...], o_t.astype(BF16)).astype(BF16)


def _softmax_attn_kernel(qT_ref, k_ref, vT_ref, eye_ref, o_ref, *, mode, past, tq, tk, s_valid):
    qi = pl.program_id(1)
    ltk = int(math.log2(tk))
    q_lo = past + qi * tq
    q_hi = q_lo + (tq - 1)
    if mode == 'fox':
        n_full = (q_lo + 1) >> ltk
        n_blk = (q_hi >> ltk) + 1
    else:
        lim_lo = ((q_lo // CHUNK) + 1) * CHUNK
        lim_hi = jnp.minimum(((q_hi // CHUNK) + 1) * CHUNK, s_valid)
        n_blk = (lim_hi + (tk - 1)) >> ltk
        n_full = jnp.minimum(lim_lo >> ltk, n_blk)
    qpos = q_lo + lax.broadcasted_iota(jnp.int32, (1, tq), 1)

    def step(kb, carry, masked):
        off = pl.multiple_of(kb * tk, tk)
        scores = [_dot(k_ref[0, pl.ds(off, tk), _head_rows(h)], qT_ref[0, _head_rows(h), :]) for h in range(N_HEADS)]
        if masked:
            kpos = kb * tk + lax.broadcasted_iota(jnp.int32, (tk, 1), 0)
            if mode == 'fox':
                vis = kpos <= qpos
            else:
                vis = jnp.logical_and((kpos // CHUNK) <= (qpos // CHUNK), kpos < s_valid)
        out = []
        for h in range(N_HEADS):
            m, l, acc = carry[h]
            s = jnp.where(vis, scores[h], NEG_INF) if masked else scores[h]
            m_new = jnp.maximum(m, jnp.max(s, axis=0, keepdims=True))
            p = jnp.exp2(s - m_new)
            alpha = jnp.exp2(m - m_new)
            l = alpha * l + jnp.sum(p, axis=0, keepdims=True)
            acc = alpha * acc + _dot(vT_ref[0, kb, _value_rows(h), :], p.astype(BF16))
            out.append((m_new, l, acc))
        return tuple(out)

    init = (jnp.full((1, tq), NEG_INF, F32), jnp.zeros((1, tq), F32), jnp.zeros((HEAD_DIM, tq), F32))
    carry = lax.fori_loop(0, n_full, functools.partial(step, masked=False), (init,) * N_HEADS)
    carry = lax.fori_loop(n_full, n_blk, functools.partial(step, masked=True), carry)
    _finish([c[2] for c in carry], [1.0 / c[1] for c in carry], eye_ref, o_ref)


def _first_needed_pair(cend_ref, qn_ref, kn_ref, n_pairs):
    nblk = cend_ref.shape[1]
    cend = cend_ref[0]
    c_tile = cend_ref[0, pl.ds(jnp.maximum(2 * n_pairs - 1, 0), 1), :]
    qk_bound = 2.0 * jnp.sqrt(qn_ref[0] * kn_ref[0]) + 1.0
    lane = lax.broadcasted_iota(jnp.int32, (nblk, LANE), 1)
    blk = lax.broadcasted_iota(jnp.int32, (nblk, 1), 0)
    worst = jnp.max(jnp.where(lane < N_HEADS, qk_bound + c_tile - cend, -jnp.inf), axis=1, keepdims=True)
    is_pair_end = jnp.logical_and((blk & 1) == 1, blk < 2 * n_pairs)
    skip = jnp.logical_and(is_pair_end, worst <= -FOX_SKIP_LIMIT)
    return jnp.sum(skip.astype(jnp.int32))


def _softmax_attn_pipelined_kernel(qT_ref, k_ref, vT_ref, eye_ref, *rest, mode, past, tq, tk, s_valid):
    o_ref, sa_ref, sb_ref, acc_ref = rest[-4:]
    qi = pl.program_id(1)
    q_lo = past + qi * tq
    n_pairs = past // tq + qi
    first_pair = _first_needed_pair(*rest[:3], n_pairs) if mode == 'fox' else 0
    qpos = q_lo + lax.broadcasted_iota(jnp.int32, (1, tq), 1)

    def qk(kb, dst_ref):
        off = pl.multiple_of(kb * tk, tk)
        for h in range(N_HEADS):
            dst_ref[h] = _dot(k_ref[0, pl.ds(off, tk), _head_rows(h)], qT_ref[0, _head_rows(h), :])

    def softmax_pv(kb, src_ref, stats, masked):
        if masked:
            kpos = kb * tk + lax.broadcasted_iota(jnp.int32, (tk, 1), 0)
            if mode == 'fox':
                vis = kpos <= qpos
            else:
                vis = (kpos // CHUNK) <= (qpos // CHUNK)
                if s_valid < k_ref.shape[1]:
                    vis = jnp.logical_and(vis, kpos < s_valid)
        out = []
        for h in range(N_HEADS):
            m, l = stats[h]
            s = src_ref[h]
            if masked:
                s = jnp.where(vis, s, NEG_INF)
            m_new = jnp.maximum(m, jnp.max(s, axis=0, keepdims=True))
            p = jnp.exp2(s - m_new)
            alpha = jnp.exp2(m - m_new)
            l = alpha * l + jnp.sum(p, axis=0, keepdims=True)
            rows = _value_rows(h)
            acc_ref[rows, :] = alpha * acc_ref[rows, :] + _dot(vT_ref[0, kb, rows, :], p.astype(BF16))
            out.append((m_new, l))
        return tuple(out)

    acc_ref[...] = jnp.zeros_like(acc_ref)
    qk(2 * first_pair, sa_ref)

    def pair(j, stats):
        kb = 2 * j
        qk(kb + 1, sb_ref)
        stats = softmax_pv(kb, sa_ref, stats, False)
        qk(kb + 2, sa_ref)
        return softmax_pv(kb + 1, sb_ref, stats, False)

    init = (jnp.full((1, tq), NEG_INF, F32), jnp.zeros((1, tq), F32))
    stats = lax.fori_loop(first_pair, n_pairs, pair, (init,) * N_HEADS)
    kb = 2 * n_pairs
    qk(kb + 1, sb_ref)
    stats = softmax_pv(kb, sa_ref, stats, True)
    stats = softmax_pv(kb + 1, sb_ref, stats, True)
    _finish([acc_ref[_value_rows(h), :] for h in range(N_HEADS)], [1.0 / st[1] for st in stats], eye_ref, o_ref)


def _stick_attn_kernel(qT_ref, k_ref, vT_ref, eye_ref, later_ref, o_ref, *, past, tq, tk):
    assert tk % tq == 0 and past % tq == 0
    qi = pl.program_id(1)
    ltk = int(math.log2(tk))
    q_lo = past + qi * tq
    own = q_lo >> ltk
    qpos = q_lo + lax.broadcasted_iota(jnp.int32, (1, tq), 1)

    def block_terms(kb, masked):
        kvs = [_load_kv(k_ref, vT_ref, kb, tk, h) for h in range(N_HEADS)]
        zs = [_dot(kvs[h][0], qT_ref[0, _head_rows(h), :]) for h in range(N_HEADS)]
        if masked:
            kpos = kb * tk + lax.broadcasted_iota(jnp.int32, (tk, 1), 0)
            vis = kpos < qpos
        out = []
        for h in range(N_HEADS):
            z = zs[h]
            drop = jnp.maximum(z, 0.0) + jnp.log2(1.0 + jnp.exp2(-jnp.abs(z)))
            if masked:
                drop = jnp.where(vis, drop, 0.0)
            hi = drop.astype(BF16)
            lo = (drop - hi.astype(F32)).astype(BF16)
            later = _dot(later_ref[...], hi) + _dot(later_ref[...], lo)
            logw = z - drop - later
            if masked:
                logw = jnp.where(vis, logw, NEG_INF)
            out.append((logw, jnp.sum(drop, axis=0, keepdims=True), kvs[h][1]))
        return out

    def accumulate(terms, carry, live=None):
        out = []
        for h in range(N_HEADS):
            logw, total, v_t = terms[h]
            decay, acc = carry[h]
            a = jnp.exp2(logw - decay)
            if live is not None:
                a = jnp.where(live, a, 0.0)
                total = jnp.where(live, total, 0.0)
            out.append((decay + total, acc + _dot(v_t, a.astype(BF16))))
        return tuple(out)

    def min_decay(carry):
        d = carry[0][0]
        for h in range(1, N_HEADS):
            d = jnp.minimum(d, carry[h][0])
        return jnp.min(d)

    init = (jnp.zeros((1, tq), F32), jnp.zeros((HEAD_DIM, tq), F32))
    own_terms = block_terms(own, True)
    prev_terms = block_terms(jnp.maximum(own - 1, 0), False)
    carry = accumulate(own_terms, (init,) * N_HEADS)
    carry = accumulate(prev_terms, carry, live=own >= 1)

    def cond(state):
        kb, dmin, _ = state
        return jnp.logical_and(kb >= 0, dmin < SB_DECAY_LIMIT)

    def body(state):
        kb, _, carry = state
        carry = accumulate(block_terms(kb, False), carry)
        return kb - 1, min_decay(carry), carry

    _, _, carry = lax.while_loop(cond, body, (own - 2, min_decay(carry), carry))
    _finish([c[1] for c in carry], [1.0] * N_HEADS, eye_ref, o_ref)


def _attention(mode, q_t, k, v_t, cst, *, past, s_valid, tq, tk, skip_stats=None):
    b, _, t = q_t.shape
    s = k.shape[1]
    grid = (b, t // tq)
    single = pl.Buffered(1)
    in_specs = [pl.BlockSpec((1, QW, tq), lambda i, j: (i, 0, j)),
                pl.BlockSpec((1, s, QW), lambda i, j: (i, 0, 0), pipeline_mode=single),
                pl.BlockSpec((1, s // tk, VW, tk), lambda i, j: (i, 0, 0, 0), pipeline_mode=single),
                _const_spec((tq, tq))]
    args = [q_t, k, v_t, jnp.eye(tq, dtype=BF16)]
    scratch = []
    if mode == 'sb':
        kern = functools.partial(_stick_attn_kernel, past=past, tq=tq, tk=tk)
        in_specs.append(_const_spec((tk, tk)))
        args.append(cst['later'][tk])
    elif tq == 2 * tk and past % tq == 0:
        kern = functools.partial(_softmax_attn_pipelined_kernel, mode=mode, past=past, tq=tq, tk=tk, s_valid=s_valid)
        scratch = [pltpu.VMEM((N_HEADS, tk, tq), F32), pltpu.VMEM((N_HEADS, tk, tq), F32), pltpu.VMEM((VW, tq), F32)]
        if mode == 'fox':
            cend, qn, kn = skip_stats
            stat_spec = pl.BlockSpec((1, 1, LANE), lambda i, j: (i, 0, 0))
            in_specs += [pl.BlockSpec((1, s // tk, LANE), lambda i, j: (i, 0, 0)), stat_spec, stat_spec]
            args += [cend.reshape(b, s // tk, LANE), qn, kn]
    else:
        kern = functools.partial(_softmax_attn_kernel, mode=mode, past=past, tq=tq, tk=tk, s_valid=s_valid)
    width = (N_HEADS // 2) * HP
    return pl.pallas_call(
        kern, grid=grid, in_specs=in_specs,
        out_specs=pl.BlockSpec((1, tq, width), lambda i, j: (i, j, 0)),
        out_shape=jax.ShapeDtypeStruct((b, t, width), BF16), scratch_shapes=scratch,
        compiler_params=_params("parallel", "parallel"), name="attn_" + mode,
    )(*args)


def _lru_kernel(lx_ref, lg_ref, cb_ref, h0_ref, cw_ref, cbias_ref, wr_ref, br_ref, wi_ref, bi_ref, lam_ref,
                y_ref, nb_ref, hl_ref, xcat_ref, h_ref, *, past, tc):
    ci = pl.program_id(1)
    keep = LRU_CONV - 1

    @pl.when(ci == 0)
    def _():
        xcat_ref[0:8, :] = jnp.zeros((8, LRU_WIDTH), F32)
        xcat_ref[8 - keep:8, :] = cb_ref[0]
        h_ref[...] = h0_ref[0]

    x = lx_ref[0]
    xcat_ref[8:8 + tc, :] = x
    xc = cbias_ref[...] + x * cw_ref[keep:keep + 1, :]
    for tap in range(keep):
        shift = keep - tap
        xc = xc + xcat_ref[8 - shift:8 - shift + tc, :] * cw_ref[tap:tap + 1, :]
    nb_ref[0] = xcat_ref[8 + tc - keep:8 + tc, :]
    xcat_ref[0:8, :] = x[tc - 8:tc, :]

    xcb = xc.astype(BF16)
    r = _sigmoid(_dot(xcb, wr_ref[...]) + br_ref[...])
    gate_in = _sigmoid(_dot(xcb, wi_ref[...]) + bi_ref[...])
    log_a = (-LRU_C) * r * _softplus(-lam_ref[...])
    row = lax.broadcasted_iota(jnp.int32, (tc, LRU_WIDTH), 0)
    reset = (past + ci * tc + row) == 0
    a = jnp.where(reset, 0.0, jnp.exp(log_a))
    y2 = 2.0 * log_a
    series = -y2 * (1.0 + y2 * (0.5 + y2 * (1.0 / 6.0 + y2 * (1.0 / 24.0 + y2 * (1.0 / 120.0)))))
    one_minus = jnp.where(y2 > -0.05, series, 1.0 - jnp.exp(y2))
    mult = jnp.where(reset, 1.0, jnp.sqrt(one_minus))
    u = mult * gate_in * xc

    d = 1
    while d < tc:
        ok = row >= d
        u = u + jnp.where(ok, a * pltpu.roll(u, d, 0), 0.0)
        a = jnp.where(ok, a * pltpu.roll(a, d, 0), a)
        d *= 2
    hs = a * h_ref[...] + u
    h_last = hs[tc - 1:tc, :]
    h_ref[...] = h_last
    hl_ref[0] = h_last

    g = lg_ref[0]
    gelu = 0.5 * g * (1.0 + jnp.tanh(0.7978845608028654 * (g + 0.044715 * g * g * g)))
    y_ref[0] = (hs * gelu).astype(BF16)


def _lru(lx, lg, conv_buf, h0, lw, past):
    b, t, w = lx.shape
    tc = _pick(t, (256, 128, 64, 32))
    grid = (b, t // tc)
    row = pl.BlockSpec((1, tc, w), lambda i, j: (i, j, 0))
    keep = LRU_CONV - 1
    in_specs = [row, row, pl.BlockSpec((1, keep, w), lambda i, j: (i, 0, 0)), pl.BlockSpec((1, 1, w), lambda i, j: (i, 0, 0)),
                _const_spec((LRU_CONV, w)), _const_spec((1, w)), _const_spec((w, w)), _const_spec((1, w)),
                _const_spec((w, w)), _const_spec((1, w)), _const_spec((1, w))]
    sds = jax.ShapeDtypeStruct
    return pl.pallas_call(
        functools.partial(_lru_kernel, past=past, tc=tc), grid=grid, in_specs=in_specs,
        out_specs=[row, pl.BlockSpec((1, keep, w), lambda i, j: (i, 0, 0)), pl.BlockSpec((1, 1, w), lambda i, j: (i, 0, 0))],
        out_shape=[sds((b, t, w), BF16), sds((b, keep, w), F32), sds((b, 1, w), F32)],
        scratch_shapes=[pltpu.VMEM((tc + 8, w), F32), pltpu.VMEM((1, w), F32)],
        compiler_params=_params("parallel", "arbitrary"), name="rg_lru",
    )(lx, lg, conv_buf, h0, lw['lru_conv_w'], lw['lru_conv_b'], lw['lru_wr'], lw['lru_br'], lw['lru_wi'],
      lw['lru_bi'], lw['lru_lam'])


def _merge_kernel(x_ref, oa_ref, ob_ref, oc_ref, od_ref, g1_ref, g2_ref, wg_ref, wb_ref, wo_ref, out_ref):
    x = x_ref[0]
    d = x.shape[1]
    hb = _rms(x, g1_ref[...]).astype(BF16)
    branches = (oa_ref[0], ob_ref[0], oc_ref[0], od_ref[0])
    merged = None
    for n, o in enumerate(branches):
        gate = _sigmoid(_dot(hb, wg_ref[:, n * d:(n + 1) * d]))
        term = gate * _dot(o, wb_ref[n])
        merged = term if merged is None else merged + term
    y = _dot(merged.astype(BF16), wo_ref[...])
    out_ref[0] = x + _rms(y, g2_ref[...])


def _merge(x, o_a, o_b, o_c, o_d, lw):
    b, t, d = x.shape
    tm = _pick(t, (256, 128))
    grid = (b, t // tm)
    row = lambda w: pl.BlockSpec((1, tm, w), lambda i, j: (i, j, 0))
    bw = BRANCH_WIDTH
    in_specs = [row(d), row(bw), row(bw), row(bw), row(bw), _const_spec((1, d)), _const_spec((1, d)),
                _const_spec((d, N_BRANCH * d)), _const_spec((N_BRANCH, BRANCH_WIDTH, d)), _const_spec((d, d))]
    return pl.pallas_call(
        _merge_kernel, grid=grid, in_specs=in_specs, out_specs=row(d),
        out_shape=jax.ShapeDtypeStruct((b, t, d), F32),
        compiler_params=_params("parallel", "parallel"), name="merge",
    )(x, o_a, o_b, o_c, o_d, lw['ln_mix_pre'], lw['ln_mix_post'], lw['w_gate'], lw['w_branch'], lw['w_out'])


def _mem_kv_kernel(mem_ref, g_ref, wk_ref, wv_ref, k_ref, v_ref):
    mn = _rms(mem_ref[0], g_ref[...]).astype(BF16)
    k_ref[0] = _dot(mn, wk_ref[...])
    v_ref[0] = _dot(mn, wv_ref[...])


def _mem_kv(mem, lw):
    b, m, d = mem.shape
    w = lw['mem_wk'].shape[1]
    out = jax.ShapeDtypeStruct((b, m, w), F32)
    blk = pl.BlockSpec((1, m, w), lambda i: (i, 0, 0))
    return pl.pallas_call(
        _mem_kv_kernel, grid=(b,),
        in_specs=[pl.BlockSpec((1, m, d), lambda i: (i, 0, 0)), _const_spec((1, d)), _const_spec((d, w)), _const_spec((d, w))],
        out_specs=[blk, blk], out_shape=[out, out], compiler_params=_params("parallel"), name="mem_kv",
    )(mem, lw['mem_norm'], lw['mem_wk'], lw['mem_wv'])


def _mem_attn_kernel(x_ref, mk_ref, mv_ref, g1_ref, g2_ref, wq_ref, wo_ref, out_ref):
    x = x_ref[0]
    hb = _rms(x, g1_ref[...]).astype(BF16)
    q = (_dot(hb, wq_ref[...]) * (MEM_HEAD_DIM ** -0.5)).astype(BF16)
    heads = []
    for h in range(N_HEADS):
        sl = slice(h * MEM_HEAD_DIM, (h + 1) * MEM_HEAD_DIM)
        s = _dot_nt(q[:, sl], mk_ref[0, :, sl])
        p = jnp.exp(s - jnp.max(s, axis=-1, keepdims=True))
        inv = 1.0 / jnp.sum(p, axis=-1, keepdims=True)
        heads.append((_dot(p.astype(BF16), mv_ref[0, :, sl]) * inv).astype(BF16))
    y = _dot(jnp.concatenate(heads, axis=1), wo_ref[...])
    out_ref[0] = x + _rms(y, g2_ref[...])


def _mem_attn(x, mk, mv, lw):
    b, t, d = x.shape
    m, w = mk.shape[1], mk.shape[2]
    tm = _pick(t, (512, 256, 128))
    row = pl.BlockSpec((1, tm, d), lambda i, j: (i, j, 0))
    kv = pl.BlockSpec((1, m, w), lambda i, j: (i, 0, 0))
    return pl.pallas_call(
        _mem_attn_kernel, grid=(b, t // tm),
        in_specs=[row, kv, kv, _const_spec((1, d)), _const_spec((1, d)), _const_spec((d, w)), _const_spec((w, d))],
        out_specs=row, out_shape=jax.ShapeDtypeStruct((b, t, d), F32),
        compiler_params=_params("parallel", "parallel"), name="mem_attn",
    )(x, mk, mv, lw['ln_mem_pre'], lw['ln_mem_post'], lw['mem_wq'], lw['mem_wo'])


def _ffn_kernel(x_ref, g1_ref, g2_ref, wg_ref, wu_ref, wd_ref, out_ref):
    x = x_ref[0]
    hb = _rms(x, g1_ref[...]).astype(BF16)
    gate = _dot(hb, wg_ref[...])
    act = (gate * _sigmoid(gate) * _dot(hb, wu_ref[...])).astype(BF16)
    y = _dot(act, wd_ref[...])
    out_ref[0] = x + _rms(y, g2_ref[...])


def _ffn(x, lw):
    b, t, d = x.shape
    f = lw['ffn_wg'].shape[1]
    tm = _pick(t, (256, 128))
    row = pl.BlockSpec((1, tm, d), lambda i, j: (i, j, 0))
    single = pl.Buffered(1)
    return pl.pallas_call(
        _ffn_kernel, grid=(b, t // tm),
        in_specs=[row, _const_spec((1, d)), _const_spec((1, d)),
                  pl.BlockSpec((d, f), lambda i, j: (0, 0), pipeline_mode=single),
                  pl.BlockSpec((d, f), lambda i, j: (0, 0), pipeline_mode=single),
                  pl.BlockSpec((f, d), lambda i, j: (0, 0), pipeline_mode=single)],
        out_specs=row, out_shape=jax.ShapeDtypeStruct((b, t, d), F32),
        compiler_params=_params("parallel", "parallel"), name="ffn",
    )(x, lw['ln_ffn_pre'], lw['ln_ffn_post'], lw['ffn_wg'], lw['ffn_wu'], lw['ffn_wd'])


def _constants():
    fk, fq, ones_k, ones_q = _fox_feature_maps()
    later = {}
    for tk in (256,):
        idx = np.arange(tk)
        later[tk] = jnp.asarray((idx[None, :] > idx[:, None]).astype(np.float32), BF16)
    return dict(
        ek=jnp.asarray(_place_qk(), BF16), evT=jnp.eye(VW, dtype=BF16),
        fkmap=jnp.asarray(fk, BF16), fqmapT=jnp.asarray(np.transpose(fq, (0, 2, 1)), BF16),
        ones_k=jnp.asarray(ones_k, F32), ones_qT=jnp.asarray(ones_q.T, F32),
        ekpe=jnp.asarray(_place_kpe(), BF16), later=later,
        seg=jnp.asarray(np.repeat(np.eye(N_HEADS, LANE, dtype=np.float32), HEAD_DIM, axis=0), BF16))


def _pad_cols(w, n):
    return jnp.pad(w, ((0, 0), (0, n - w.shape[1])))


_MIX_SIZES = (256, 256, 256, N_HEADS, LRU_WIDTH, LRU_WIDTH, MLA_Q_RANK, MLA_KV_RANK, MLA_ROPE, 256, 256, 256)


def _w_in_plan(d_model):
    offs = np.concatenate([[0], np.cumsum(_MIX_SIZES)])
    fq, fk, fv, ff, lx, lg, cq, ckv, kpe, sq, sk, sv = [np.arange(offs[i], offs[i + 1]) for i in range(len(_MIX_SIZES))]
    half = MLA_ROPE // 2

    def padded(cols):
        return np.concatenate([cols, np.full(LANE - len(cols), -1)])

    mix = [fq, fk, fv, lx, lg, cq, ckv, padded(kpe), padded(np.concatenate([kpe[half:], kpe[:half]])), padded(ff),
           sq, sk, sv]
    pieces, col = [], 0
    for src in mix:
        pieces.append((0, col, src))
        col += len(src)
    assert col == _W_MIX_COLS
    gate0 = offs[-1]
    for i in range(N_BRANCH * d_model // 256):
        pieces.append((1, i * 256, gate0 + i * 256 + np.arange(256)))
    return pieces


def _transpose_kernel(xT_ref, eye_ref, x_ref):
    x_ref[...] = _dot_nt(eye_ref[...], xT_ref[...]).astype(x_ref.dtype)


def _transpose_bf16(x_t):
    r, c = x_t.shape
    rb = _pick(r, (256, 128))
    return pl.pallas_call(
        _transpose_kernel, grid=(r // rb,),
        in_specs=[pl.BlockSpec((rb, c), lambda i: (i, 0)), _const_spec((c, c))],
        out_specs=pl.BlockSpec((c, rb), lambda i: (0, i)), out_shape=jax.ShapeDtypeStruct((c, r), BF16),
        compiler_params=_params("parallel"), name="transpose_w",
    )(x_t, jnp.eye(c, dtype=BF16))


def _repack_w_in(w_in_t):
    d_model = w_in_t.shape[1]
    rows = [[], []]
    for dst, _, src in _w_in_plan(d_model):
        valid = src[src >= 0]
        runs = np.split(valid, np.nonzero(np.diff(valid) != 1)[0] + 1)
        piece = jnp.concatenate([w_in_t[int(r[0]):int(r[-1]) + 1] for r in runs], axis=0)
        rows[dst].append(jnp.pad(piece, ((0, len(src) - len(valid)), (0, 0))))
    return tuple(_transpose_bf16(jnp.concatenate(r, axis=0).astype(BF16)) for r in rows)


def _layer_weights(l, p):
    w_mix, w_gate = _repack_w_in(jnp.transpose(p['w_in'], (2, 0, 1))[:, l, :])
    half = MLA_ROPE // 2
    qk = MLA_NOPE + MLA_ROPE
    wuq = p['mla_w_uq'][l].reshape(MLA_Q_RANK, N_HEADS, qk)
    rope = wuq[:, :, MLA_NOPE:]
    rope_sw = jnp.concatenate([rope[:, :, half:], rope[:, :, :half]], axis=2)
    wuq_p = jnp.pad(wuq, ((0, 0), (0, 0), (0, HP - qk))).reshape(MLA_Q_RANK, QW)
    wuqsw_p = jnp.pad(rope_sw, ((0, 0), (0, 0), (MLA_NOPE, HP - qk))).reshape(MLA_Q_RANK, QW)
    wuk = p['mla_w_uk'][l].reshape(MLA_KV_RANK, N_HEADS, MLA_NOPE)
    wuk_p = jnp.pad(wuk, ((0, 0), (0, 0), (0, HP - MLA_NOPE))).reshape(MLA_KV_RANK, QW)

    def block_diag(w):
        return jax.scipy.linalg.block_diag(*[w[i] for i in range(w.shape[0])]).astype(BF16)

    row = lambda v: v[l].reshape(1, -1).astype(F32)
    return dict(
        w_mix=w_mix, w_gate=w_gate,
        ln_mix_pre=row(p['ln_mix_pre']), ln_mix_post=row(p['ln_mix_post']),
        fox_bf=_pad_cols(row(p['fox_bf']), LANE),
        mla_q_norm=row(p['mla_q_norm']), mla_kv_norm=row(p['mla_kv_norm']),
        wuqT=wuq_p.T.astype(BF16), wuqswT=wuqsw_p.T.astype(BF16), wuk=wuk_p.astype(BF16), wuvT=p['mla_w_uv'][l].T.astype(BF16),
        eqT=jnp.asarray(_place_qk().T, BF16),
        lru_conv_w=p['lru_conv_w'][l].astype(F32), lru_conv_b=row(p['lru_conv_b']),
        lru_wr=block_diag(p['lru_wr'][l]), lru_br=row(p['lru_br']),
        lru_wi=block_diag(p['lru_wi'][l]), lru_bi=row(p['lru_bi']), lru_lam=row(p['lru_lam']),
        w_branch=p['w_branch'][l].astype(BF16), w_out=p['w_out'][l].astype(BF16),
        ln_mem_pre=row(p['ln_mem_pre']), ln_mem_post=row(p['ln_mem_post']), mem_norm=row(p['mem_norm']),
        mem_wq=p['mem_wq'][l].astype(BF16), mem_wk=p['mem_wk'][l].astype(BF16),
        mem_wv=p['mem_wv'][l].astype(BF16), mem_wo=p['mem_wo'][l].astype(BF16),
        ln_ffn_pre=row(p['ln_ffn_pre']), ln_ffn_post=row(p['ln_ffn_post']),
        ffn_wg=p['ffn_wg'][l].astype(BF16), ffn_wu=p['ffn_wu'][l].astype(BF16), ffn_wd=p['ffn_wd'][l].astype(BF16))


def _rope_tables(past, t):
    half = MLA_ROPE // 2
    inv = jnp.power(ROPE_BASE, -jnp.arange(half, dtype=F32) / half)
    ang = (past + jnp.arange(t, dtype=jnp.int32)).astype(F32)[:, None] * inv
    cos, sin = jnp.cos(ang), jnp.sin(ang)
    c32 = jnp.concatenate([cos, cos], axis=1)
    s32 = jnp.concatenate([-sin, sin], axis=1)
    scale = (MLA_NOPE + MLA_ROPE) ** -0.5 * LOG2E
    slot_c = jnp.concatenate([jnp.ones((t, MLA_NOPE), F32), c32, jnp.zeros((t, HP - MLA_NOPE - MLA_ROPE), F32)], axis=1)
    slot_s = jnp.concatenate([jnp.zeros((t, MLA_NOPE), F32), s32, jnp.zeros((t, HP - MLA_NOPE - MLA_ROPE), F32)], axis=1)
    return dict(ck=_pad_cols(c32, LANE), sk=_pad_cols(s32, LANE),
                cqT=(jnp.tile(slot_c, (1, N_HEADS)) * scale).T, sqT=(jnp.tile(slot_s, (1, N_HEADS)) * scale).T)


def _trunk_layer(x, past_state, mem_k, mem_v, lw, cst, layer, depth, earlier):
    b, t, _ = x.shape
    past = 0 if past_state is None else past_state['fox_k'].shape[1]
    tk = 256
    fq, fk, fv, lx, lg, ckv_n, kpe_slab, logf_slab, mla_qT, sb_qT, sk, sv = _in_proj(
        x, lw, _rope_tables(past, t), layer, depth, earlier)
    stacked = (fk, fv, ckv_n, kpe_slab, logf_slab, sk, sv)

    if past_state is None:
        s_valid = t
        keys = (fq, fk, fv, logf_slab, ckv_n, kpe_slab, sk, sv)
        key_layer = layer
        conv_buf = jnp.zeros((b, LRU_CONV - 1, LRU_WIDTH), F32)
        h0 = jnp.zeros((b, 1, LRU_WIDTH), F32)
    else:
        s_valid = past + t
        s_pad = -(-s_valid // tk) * tk

        def cat(old, new):
            old = old.reshape(b, past, -1).astype(new.dtype)
            old = jnp.pad(old, ((0, 0), (0, 0), (0, new.shape[2] - old.shape[2])))
            return jnp.pad(jnp.concatenate([old, new], axis=1), ((0, 0), (0, s_pad - s_valid), (0, 0)))

        def cat_state(old, new):
            return cat(old, new[layer])[None]

        keys = (cat(jnp.zeros((b, past, 256), BF16), fq), cat_state(past_state['fox_k'], fk),
                cat_state(past_state['fox_v'], fv), cat_state(past_state['fox_logf'], logf_slab),
                cat_state(past_state['mla_ckv'], ckv_n), cat_state(past_state['mla_kpe'], kpe_slab),
                cat_state(past_state['sb_k'], sk), cat_state(past_state['sb_v'], sv))
        key_layer = 0
        conv_buf = past_state['lru_conv'].astype(F32)
        h0 = past_state['lru_h'].reshape(b, 1, LRU_WIDTH).astype(F32)

    fox_qT, fox_k, fox_vT, mla_k, mla_vT, sb_k, sb_vT, *fox_skip_stats = _kv_prep(*keys, key_layer, lw, cst, tk)
    if past_state is not None:
        fox_qT = fox_qT[:, :, past:past + t]

    att = functools.partial(_attention, cst=cst, past=past, s_valid=s_valid, tk=tk)
    tq_softmax = _pick(t, (2 * tk, tk, 128))
    o_a = att('fox', fox_qT, fox_k, fox_vT, tq=tq_softmax, skip_stats=fox_skip_stats)
    o_c = att('mla', mla_qT, mla_k, mla_vT, tq=tq_softmax)
    o_d = att('sb', sb_qT, sb_k, sb_vT, tq=_pick(t, (tk, 128)))
    o_b, lru_conv, lru_h = _lru(lx, lg, conv_buf, h0, lw, past)

    x = _merge(x, o_a, o_b, o_c, o_d, lw)
    x = _mem_attn(x, mem_k, mem_v, lw)
    x = _ffn(x, lw)

    return x, stacked, dict(lru_h=lru_h.reshape(b, LRU_WIDTH), lru_conv=lru_conv)


def _state_outputs(stacked, small):
    fk, fv, ckv_n, kpe_slab, logf_slab, sk, sv = stacked
    depth, b, t, _ = fk.shape
    heads = lambda a: a.reshape(depth, b, t, N_HEADS, HEAD_DIM)
    stk = lambda name: jnp.stack([s[name] for s in small])
    return (heads(fk), heads(fv), logf_slab[..., :N_HEADS], stk('lru_h'), stk('lru_conv'),
            ckv_n, kpe_slab[..., :MLA_ROPE], heads(sk), heads(sv))


def kernel(x_prompt, x_sample, cache_fox_k, cache_fox_v, cache_fox_logf, state_lru_h, state_lru_conv, cache_mla_ckv, cache_mla_kpe, cache_sb_k, cache_sb_v, cache_mem_k, cache_mem_v, mem_prompt, ln_mix_pre, ln_mix_post, w_in, fox_bf, lru_conv_w, lru_conv_b, lru_wr, lru_br, lru_wi, lru_bi, lru_lam, mla_q_norm, mla_w_uq, mla_kv_norm, mla_w_uk, mla_w_uv, w_branch, w_out, ln_mem_pre, ln_mem_post, mem_norm, mem_wq, mem_wk, mem_wv, mem_wo, ln_ffn_pre, ln_ffn_post, ffn_wg, ffn_wu, ffn_wd):
    params = dict(ln_mix_pre=ln_mix_pre, ln_mix_post=ln_mix_post, w_in=w_in, fox_bf=fox_bf, lru_conv_w=lru_conv_w,
                  lru_conv_b=lru_conv_b, lru_wr=lru_wr, lru_br=lru_br, lru_wi=lru_wi, lru_bi=lru_bi, lru_lam=lru_lam,
                  mla_q_norm=mla_q_norm, mla_w_uq=mla_w_uq, mla_kv_norm=mla_kv_norm, mla_w_uk=mla_w_uk,
                  mla_w_uv=mla_w_uv, w_branch=w_branch, w_out=w_out, ln_mem_pre=ln_mem_pre, ln_mem_post=ln_mem_post,
                  mem_norm=mem_norm, mem_wq=mem_wq, mem_wk=mem_wk, mem_wv=mem_wv, mem_wo=mem_wo,
                  ln_ffn_pre=ln_ffn_pre, ln_ffn_post=ln_ffn_post, ffn_wg=ffn_wg, ffn_wu=ffn_wu, ffn_wd=ffn_wd)
    depth = w_in.shape[0]
    cst = _constants()
    weights = [_layer_weights(l, params) for l in range(depth)]
    bp, mem_len = mem_prompt.shape[0], mem_prompt.shape[1]

    y_prompt, p_rows, p_small, p_mem = x_prompt, None, [], []
    for l in range(depth):
        mk, mv = _mem_kv(mem_prompt, weights[l])
        y_prompt, p_rows, small = _trunk_layer(y_prompt, None, mk.astype(BF16), mv.astype(BF16), weights[l], cst,
                                               l, depth, p_rows)
        p_small.append(small)
        p_mem.append((mk.reshape(bp, mem_len, N_HEADS, MEM_HEAD_DIM), mv.reshape(bp, mem_len, N_HEADS, MEM_HEAD_DIM)))

    y_sample, s_rows, s_small = x_sample, None, []
    bs = x_sample.shape[0]
    for l in range(depth):
        past = dict(fox_k=cache_fox_k[l], fox_v=cache_fox_v[l], fox_logf=cache_fox_logf[l], lru_h=state_lru_h[l],
                    lru_conv=state_lru_conv[l], mla_ckv=cache_mla_ckv[l], mla_kpe=cache_mla_kpe[l],
                    sb_k=cache_sb_k[l], sb_v=cache_sb_v[l])
        mk = cache_mem_k[l].reshape(bs, mem_len, -1).astype(BF16)
        mv = cache_mem_v[l].reshape(bs, mem_len, -1).astype(BF16)
        y_sample, s_rows, small = _trunk_layer(y_sample, past, mk, mv, weights[l], cst, l, depth, s_rows)
        s_small.append(small)

    mem_out = (jnp.stack([m[0] for m in p_mem]), jnp.stack([m[1] for m in p_mem]))
    return (y_prompt, y_sample) + _state_outputs(p_rows, p_small) + mem_out + _state_outputs(s_rows, s_small)
```

```python
import functools
import math

import numpy as np
import jax
import jax.numpy as jnp
from jax import lax
from jax.experimental import pallas as pl
from jax.experimental.pallas import tpu as pltpu

F32 = jnp.float32
BF16 = jnp.bfloat16

CHUNK = 64
HEAD_DIM = 64
N_HEADS = 4
LRU_WIDTH = 256
LRU_CONV = 4
LRU_C = 8.0
MLA_Q_RANK = 256
MLA_KV_RANK = 128
MLA_NOPE = 64
MLA_ROPE = 32
MLA_V = 64
ROPE_BASE = 10000.0
N_BRANCH = 4
BRANCH_WIDTH = 256
MEM_HEAD_DIM = 128
EPS = 1e-6
NEG_INF = -1e30

HP = 128
QW = N_HEADS * HP
VW = N_HEADS * HEAD_DIM
LANE = 128
VMEM_LIMIT_BYTES = 56 * 1024 * 1024
LOG2E = 1.4426950408889634
SB_DECAY_LIMIT = 127.0
FOX_SKIP_LIMIT = 150.0
FIXED_STABILISER_LIMIT = 60.0

_NT = (((1,), (1,)), ((), ()))


def _dot(a, b):
    return jnp.dot(a, b, preferred_element_type=F32)


def _dot_nt(a, b):
    return lax.dot_general(a, b, _NT, preferred_element_type=F32)


def _rms(x, g):
    ms = jnp.mean(x * x, axis=-1, keepdims=True)
    return x * lax.rsqrt(ms + EPS) * g


def _sigmoid(x):
    return 1.0 / (1.0 + jnp.exp(-x))


def _softplus(x):
    return jnp.maximum(x, 0.0) + jnp.log(1.0 + jnp.exp(-jnp.abs(x)))


def _pick(n, cands):
    for c in cands:
        if n % c == 0:
            return c
    return n


def _params(*sem):
    return pltpu.CompilerParams(dimension_semantics=sem, vmem_limit_bytes=VMEM_LIMIT_BYTES)


def _const_spec(shape):
    nd = len(shape)
    return pl.BlockSpec(shape, lambda *_: (0,) * nd)


def _place_qk():
    e = np.zeros((N_HEADS * HEAD_DIM, QW), np.float32)
    for h in range(N_HEADS):
        for j in range(HEAD_DIM):
            e[h * HEAD_DIM + j, h * HP + j] = 1.0
    return e


def _fox_feature_maps():
    fk = np.zeros((3, LANE, QW), np.float32)
    fq = np.zeros((3, LANE, QW), np.float32)
    ones_k = np.zeros((1, QW), np.float32)
    ones_q = np.zeros((1, QW), np.float32)
    for h in range(N_HEADS):
        for part in range(3):
            fq[part, h, h * HP + HEAD_DIM + part] = 1.0
            fk[part, h, h * HP + HEAD_DIM + 3 + part] = -1.0
            ones_k[0, h * HP + HEAD_DIM + part] = 1.0
            ones_q[0, h * HP + HEAD_DIM + 3 + part] = 1.0
    return fk, fq, ones_k, ones_q


def _place_kpe():
    e = np.zeros((LANE, QW), np.float32)
    for h in range(N_HEADS):
        for j in range(MLA_ROPE):
            e[j, h * HP + MLA_NOPE + j] = 1.0
    return e


_W_OFF = dict(fq=0, fk=256, fv=512, lx=768, lg=1024, cq=1280, ckv=1536, kpe=1664, kpe_sw=1792, ff=1920,
              sq=2048, sk=2304, sv=2560)
_W_MIX_COLS = 2816


_N_IN_PROJ_INPUTS = 13
_IN_PROJ_STATE_OUTPUTS = (1, 2, 5, 6, 7, 10, 11)


def _in_proj_kernel(*refs):
    (x_ref, g_ref, w_ref, qn_ref, kvn_ref, bf_ref, ck_ref, sk_tab_ref, cqT_ref, sqT_ref,
     wuqT_ref, wuqswT_ref, eqT_ref) = refs[:_N_IN_PROJ_INPUTS]
    (fq_ref, fk_ref, fv_ref, lx_ref, lg_ref, ckv_ref, kpe_ref, logf_ref, mqT_ref, sbqT_ref, sk_ref, sv_ref) = refs[-12:]
    fk_ref, fv_ref, ckv_ref, kpe_ref, logf_ref, sk_ref, sv_ref = (
        r.at[0] for r in (fk_ref, fv_ref, ckv_ref, kpe_ref, logf_ref, sk_ref, sv_ref))
    hb = _rms(x_ref[0], g_ref[...]).astype(BF16)

    def proj(name, width):
        a = _W_OFF[name]
        return _dot(hb, w_ref[:, a:a + width])

    fq_ref[0] = (proj('fq', 256) * (HEAD_DIM ** -0.5 * LOG2E)).astype(BF16)
    fk_ref[0] = proj('fk', 256)
    fv_ref[0] = proj('fv', 256)
    lx_ref[0] = proj('lx', 256)
    lg_ref[0] = proj('lg', 256)

    cqn = _rms(proj('cq', 256), qn_ref[...]).astype(BF16)
    q_t = _dot_nt(wuqT_ref[...], cqn)
    qsw_t = _dot_nt(wuqswT_ref[...], cqn)
    mqT_ref[0] = (q_t * cqT_ref[...] + qsw_t * sqT_ref[...]).astype(BF16)

    ckv_ref[0] = _rms(proj('ckv', 128), kvn_ref[...])
    kpe_ref[0] = proj('kpe', 128) * ck_ref[...] + proj('kpe_sw', 128) * sk_tab_ref[...]

    ff = proj('ff', 128) + bf_ref[...]
    log_sig = jnp.minimum(ff, 0.0) - jnp.log(1.0 + jnp.exp(-jnp.abs(ff)))
    lane = lax.broadcasted_iota(jnp.int32, ff.shape, 1)
    logf_ref[0] = jnp.where(lane < N_HEADS, log_sig, 0.0)

    sq = (proj('sq', 256) * (HEAD_DIM ** -0.5 * LOG2E)).astype(BF16)
    sbqT_ref[0] = _dot_nt(eqT_ref[...], sq).astype(BF16)
    sk_ref[0] = proj('sk', 256)
    sv_ref[0] = proj('sv', 256)


def _in_proj(x, lw, tabs, layer, depth, earlier):
    b, t, d = x.shape
    tm = _pick(t, (512, 256, 128))
    grid = (b, t // tm)
    row = lambda w: pl.BlockSpec((1, tm, w), lambda i, j: (i, j, 0))
    srow = lambda w: pl.BlockSpec((1, 1, tm, w), lambda i, j: (layer, i, j, 0))
    colT = pl.BlockSpec((1, QW, tm), lambda i, j: (i, 0, j))
    in_specs = [
        row(d), _const_spec((1, d)), _const_spec((d, _W_MIX_COLS)),
        _const_spec((1, MLA_Q_RANK)), _const_spec((1, MLA_KV_RANK)), _const_spec((1, LANE)),
        pl.BlockSpec((tm, LANE), lambda i, j: (j, 0)), pl.BlockSpec((tm, LANE), lambda i, j: (j, 0)),
        pl.BlockSpec((QW, tm), lambda i, j: (0, j)), pl.BlockSpec((QW, tm), lambda i, j: (0, j)),
        _const_spec((QW, MLA_Q_RANK)), _const_spec((QW, MLA_Q_RANK)), _const_spec((QW, 256)),
    ]
    sds = jax.ShapeDtypeStruct
    state = lambda w: sds((depth, b, t, w), F32)
    out_shape = [
        sds((b, t, 256), BF16),
        state(256), state(256),
        sds((b, t, 256), F32), sds((b, t, 256), F32),
        state(128), state(LANE), state(LANE),
        sds((b, QW, t), BF16), sds((b, QW, t), BF16),
        state(256), state(256),
    ]
    out_specs = [row(256), srow(256), srow(256), row(256), row(256), srow(128), srow(LANE), srow(LANE),
                 colT, colT, srow(256), srow(256)]
    assert len(in_specs) == _N_IN_PROJ_INPUTS
    aliases = {}
    if earlier is not None:
        in_specs += [pl.BlockSpec(memory_space=pl.ANY)] * len(earlier)
        aliases = {_N_IN_PROJ_INPUTS + n: out for n, out in enumerate(_IN_PROJ_STATE_OUTPUTS)}
    return pl.pallas_call(
        _in_proj_kernel, grid=grid, in_specs=in_specs, out_specs=out_specs, out_shape=out_shape,
        input_output_aliases=aliases, compiler_params=_params("parallel", "parallel"), name="in_proj",
    )(x, lw['ln_mix_pre'], lw['w_mix'], lw['mla_q_norm'], lw['mla_kv_norm'], lw['fox_bf'],
      tabs['ck'], tabs['sk'], tabs['cqT'], tabs['sqT'], lw['wuqT'], lw['wuqswT'], lw['eqT'], *(earlier or ()))


def _cumsum_rows(x):
    n = x.shape[0]
    row = lax.broadcasted_iota(jnp.int32, x.shape, 0)
    d = 1
    while d < n:
        x = x + jnp.where(row >= d, pltpu.roll(x, d, 0), 0.0)
        d *= 2
    return x


def _kv_prep_kernel(fq_ref, fk_ref, fv_ref, logf_ref, ckv_ref, kpe_ref, sk_ref, sv_ref,
                    ek_ref, evT_ref, eqT_ref, fkmap_ref, fqmapT_ref, onesk_ref, onesq_ref,
                    wuk_ref, ekpe_ref, wuvT_ref, seg_ref, segslot_ref,
                    fqT_out, fk_out, fvT_out, mk_out, mvT_out, sbk_out, sbvT_out, cend_out, qn_out, kn_out, mkn_out,
                    carry_ref, *, tk):
    fk_ref, fv_ref, logf_ref, ckv_ref, kpe_ref, sk_ref, sv_ref = (
        r.at[0] for r in (fk_ref, fv_ref, logf_ref, ckv_ref, kpe_ref, sk_ref, sv_ref))

    @pl.when(pl.program_id(1) == 0)
    def _():
        carry_ref[...] = jnp.zeros_like(carry_ref)
        qn_out[...] = jnp.zeros_like(qn_out)
        kn_out[...] = jnp.zeros_like(kn_out)
        mkn_out[...] = jnp.zeros_like(mkn_out)

    ts = fk_ref.shape[1]
    c = _cumsum_rows(logf_ref[0]) + carry_ref[...]
    carry_ref[...] = c[ts - 1:ts, :]
    c = c * LOG2E
    for j in range(ts // tk):
        cend_out[0, j] = c[(j + 1) * tk - 1:(j + 1) * tk, :]

    def norm_bound(x, seg):
        xf = x.astype(F32)
        sums = _dot((xf * xf).astype(BF16), seg) * 1.01
        return jnp.max(sums, axis=0, keepdims=True)

    qn_out[0] = jnp.maximum(qn_out[0], norm_bound(fq_ref[0], seg_ref[...]))
    kn_out[0] = jnp.maximum(kn_out[0], norm_bound(fk_ref[0].astype(BF16), seg_ref[...]))
    c_hi = c.astype(BF16)
    rem = c - c_hi.astype(F32)
    c_mid = rem.astype(BF16)
    c_lo = (rem - c_mid.astype(F32)).astype(BF16)

    def store_vt(out, v_t):
        for j in range(ts // tk):
            out[0, j] = v_t[:, j * tk:(j + 1) * tk]

    fk_out[0] = (_dot(fk_ref[0].astype(BF16), ek_ref[...]) + _dot(c_hi, fkmap_ref[0]) + _dot(c_mid, fkmap_ref[1])
                 + _dot(c_lo, fkmap_ref[2]) + onesk_ref[...]).astype(BF16)
    fqT_out[0] = (_dot_nt(eqT_ref[...], fq_ref[0]) + _dot_nt(fqmapT_ref[0], c_hi) + _dot_nt(fqmapT_ref[1], c_mid)
                  + _dot_nt(fqmapT_ref[2], c_lo) + onesq_ref[...]).astype(BF16)
    store_vt(fvT_out, _dot_nt(evT_ref[...], fv_ref[0].astype(BF16)).astype(BF16))

    ckv = ckv_ref[0].astype(BF16)
    mk = (_dot(ckv, wuk_ref[...]) + _dot(kpe_ref[0].astype(BF16), ekpe_ref[...])).astype(BF16)
    mk_out[0] = mk
    mkn_out[0] = jnp.maximum(mkn_out[0], norm_bound(mk, segslot_ref[...]))
    store_vt(mvT_out, _dot_nt(wuvT_ref[...], ckv).astype(BF16))

    sbk_out[0] = _dot(sk_ref[0].astype(BF16), ek_ref[...]).astype(BF16)
    store_vt(sbvT_out, _dot_nt(evT_ref[...], sv_ref[0].astype(BF16)).astype(BF16))


def _kv_prep(fq, fk, fv, logf, ckv, kpe, sk, sv, layer, lw, cst, tk):
    _, b, s, _ = fk.shape
    ts = _pick(s, (1024, 512, 256))
    grid = (b, s // ts)
    row = lambda w: pl.BlockSpec((1, ts, w), lambda i, j: (i, j, 0))
    srow = lambda w: pl.BlockSpec((1, 1, ts, w), lambda i, j: (layer, i, j, 0))
    colT = pl.BlockSpec((1, QW, ts), lambda i, j: (i, 0, j))
    vT = pl.BlockSpec((1, ts // tk, VW, tk), lambda i, j: (i, j, 0, 0))
    in_specs = [row(256), srow(256), srow(256), srow(LANE), srow(128), srow(LANE), srow(256), srow(256),
                _const_spec((256, QW)), _const_spec((VW, VW)), _const_spec((QW, 256)),
                _const_spec((3, LANE, QW)), _const_spec((3, QW, LANE)), _const_spec((1, QW)), _const_spec((QW, 1)),
                _const_spec((MLA_KV_RANK, QW)), _const_spec((LANE, QW)), _const_spec((VW, MLA_KV_RANK)),
                _const_spec((VW, LANE)), _const_spec((QW, LANE))]
    sds = jax.ShapeDtypeStruct
    slab = sds((b, s, QW), BF16)
    slab_t = sds((b, s // tk, VW, tk), BF16)
    stat = sds((b, 1, LANE), F32)
    stat_spec = pl.BlockSpec((1, 1, LANE), lambda i, j: (i, 0, 0))
    out_shape = [sds((b, QW, s), BF16), slab, slab_t, slab, slab_t, slab, slab_t,
                 sds((b, s // tk, 1, LANE), F32), stat, stat, stat]
    out_specs = [colT, row(QW), vT, row(QW), vT, row(QW), vT,
                 pl.BlockSpec((1, ts // tk, 1, LANE), lambda i, j: (i, j, 0, 0)), stat_spec, stat_spec, stat_spec]
    return pl.pallas_call(
        functools.partial(_kv_prep_kernel, tk=tk), grid=grid, in_specs=in_specs, out_specs=out_specs,
        out_shape=out_shape, scratch_shapes=[pltpu.VMEM((1, LANE), F32)],
        compiler_params=_params("parallel", "arbitrary"), name="kv_prep",
    )(fq, fk, fv, logf, ckv, kpe, sk, sv, cst['ek'], cst['evT'], lw['eqT'], cst['fkmap'], cst['fqmapT'],
      cst['ones_k'], cst['ones_qT'], lw['wuk'], cst['ekpe'], lw['wuvT'], cst['seg'], cst['seg_slot'])


def _head_rows(h):
    return slice(h * HP, (h + 1) * HP)


def _value_rows(h):
    return slice(h * HEAD_DIM, (h + 1) * HEAD_DIM)


def _load_kv(k_ref, vT_ref, kb, tk, h):
    off = pl.multiple_of(kb * tk, tk)
    return k_ref[0, pl.ds(off, tk), _head_rows(h)], vT_ref[0, kb, _value_rows(h), :]


def _finish(accs, invs, eye_ref, o_ref):
    for pair in range(N_HEADS // 2):
        o_t = jnp.concatenate([accs[2 * pair] * invs[2 * pair], accs[2 * pair + 1] * invs[2 * pair + 1]], axis=0)
        o_ref[0, :, _head_rows(pair)] = _dot_nt(eye_ref[...], o_t.astype(BF16)).astype(BF16)


def _softmax_attn_kernel(qT_ref, k_ref, vT_ref, eye_ref, o_ref, *, mode, past, tq, tk, s_valid):
    qi = pl.program_id(1)
    ltk = int(math.log2(tk))
    q_lo = past + qi * tq
    q_hi = q_lo + (tq - 1)
    if mode == 'fox':
        n_full = (q_lo + 1) >> ltk
        n_blk = (q_hi >> ltk) + 1
    else:
        lim_lo = ((q_lo // CHUNK) + 1) * CHUNK
        lim_hi = jnp.minimum(((q_hi // CHUNK) + 1) * CHUNK, s_valid)
        n_blk = (lim_hi + (tk - 1)) >> ltk
        n_full = jnp.minimum(lim_lo >> ltk, n_blk)
    qpos = q_lo + lax.broadcasted_iota(jnp.int32, (1, tq), 1)

    def step(kb, carry, masked):
        off = pl.multiple_of(kb * tk, tk)
        scores = [_dot(k_ref[0, pl.ds(off, tk), _head_rows(h)], qT_ref[0, _head_rows(h), :]) for h in range(N_HEADS)]
        if masked:
            kpos = kb * tk + lax.broadcasted_iota(jnp.int32, (tk, 1), 0)
            if mode == 'fox':
                vis = kpos <= qpos
            else:
                vis = jnp.logical_and((kpos // CHUNK) <= (qpos // CHUNK), kpos < s_valid)
        out = []
        for h in range(N_HEADS):
            m, l, acc = carry[h]
            s = jnp.where(vis, scores[h], NEG_INF) if masked else scores[h]
            m_new = jnp.maximum(m, jnp.max(s, axis=0, keepdims=True))
            p = jnp.exp2(s - m_new)
            alpha = jnp.exp2(m - m_new)
            l = alpha * l + jnp.sum(p, axis=0, keepdims=True)
            acc = alpha * acc + _dot(vT_ref[0, kb, _value_rows(h), :], p.astype(BF16))
            out.append((m_new, l, acc))
        return tuple(out)

    init = (jnp.full((1, tq), NEG_INF, F32), jnp.zeros((1, tq), F32), jnp.zeros((HEAD_DIM, tq), F32))
    carry = lax.fori_loop(0, n_full, functools.partial(step, masked=False), (init,) * N_HEADS)
    carry = lax.fori_loop(n_full, n_blk, functools.partial(step, masked=True), carry)
    _finish([c[2] for c in carry], [1.0 / c[1] for c in carry], eye_ref, o_ref)


def _first_needed_pair(cend_ref, qn_ref, kn_ref, n_pairs):
    nblk = cend_ref.shape[1]
    cend = cend_ref[0]
    c_tile = cend_ref[0, pl.ds(jnp.maximum(2 * n_pairs - 1, 0), 1), :]
    qk_bound = 2.0 * jnp.sqrt(qn_ref[0] * kn_ref[0]) + 1.0
    lane = lax.broadcasted_iota(jnp.int32, (nblk, LANE), 1)
    blk = lax.broadcasted_iota(jnp.int32, (nblk, 1), 0)
    worst = jnp.max(jnp.where(lane < N_HEADS, qk_bound + c_tile - cend, -jnp.inf), axis=1, keepdims=True)
    is_pair_end = jnp.logical_and((blk & 1) == 1, blk < 2 * n_pairs)
    skip = jnp.logical_and(is_pair_end, worst <= -FOX_SKIP_LIMIT)
    return jnp.sum(skip.astype(jnp.int32))


def _softmax_attn_pipelined_kernel(qT_ref, k_ref, vT_ref, eye_ref, *rest, mode, past, tq, tk, s_valid):
    o_ref, sa_ref, sb_ref, acc_ref = rest[-4:]
    kn_ref = rest[-5]
    qi = pl.program_id(1)
    q_lo = past + qi * tq
    n_pairs = past // tq + qi
    first_pair = _first_needed_pair(*rest[:3], n_pairs) if mode == 'fox' else 0
    qpos = q_lo + lax.broadcasted_iota(jnp.int32, (1, tq), 1)
    feature_rows = HEAD_DIM if mode == 'fox' else HP

    def qk(kb, dst_ref):
        off = pl.multiple_of(kb * tk, tk)
        for h in range(N_HEADS):
            dst_ref[h] = _dot(k_ref[0, pl.ds(off, tk), _head_rows(h)], qT_ref[0, _head_rows(h), :])

    def softmax_pv(kb, src_ref, stats, masked):
        if masked:
            kpos = kb * tk + lax.broadcasted_iota(jnp.int32, (tk, 1), 0)
            if mode == 'fox':
                vis = kpos <= qpos
            else:
                vis = (kpos // CHUNK) <= (qpos // CHUNK)
                if s_valid < k_ref.shape[1]:
                    vis = jnp.logical_and(vis, kpos < s_valid)
        out = []
        for h in range(N_HEADS):
            m, l = stats[h]
            s = src_ref[h]
            if masked:
                s = jnp.where(vis, s, NEG_INF)
            m_new = jnp.maximum(m, jnp.max(s, axis=0, keepdims=True))
            p = jnp.exp2(s - m_new)
            alpha = jnp.exp2(m - m_new)
            l = alpha * l + jnp.sum(p, axis=0, keepdims=True)
            rows = _value_rows(h)
            acc_ref[rows, :] = alpha * acc_ref[rows, :] + _dot(vT_ref[0, kb, rows, :], p.astype(BF16))
            out.append((m_new, l))
        return tuple(out)

    def fixed_pv(kb, src_ref, m_fix, sums):
        out = []
        for h in range(N_HEADS):
            p = jnp.exp2(src_ref[h] - m_fix[h])
            rows = _value_rows(h)
            acc_ref[rows, :] = acc_ref[rows, :] + _dot(vT_ref[0, kb, rows, :], p.astype(BF16))
            out.append(sums[h] + jnp.sum(p, axis=0, keepdims=True))
        return tuple(out)

    def pairs(step, carry):
        def body(j, carry):
            kb = 2 * j
            qk(kb + 1, sb_ref)
            carry = step(kb, sa_ref, carry)
            qk(kb + 2, sa_ref)
            return step(kb + 1, sb_ref, carry)
        return lax.fori_loop(first_pair, n_pairs, body, carry)

    acc_ref[...] = jnp.zeros_like(acc_ref)
    own = 2 * n_pairs
    qk(own, sa_ref)
    qk(own + 1, sb_ref)
    init = (jnp.full((1, tq), NEG_INF, F32), jnp.zeros((1, tq), F32))
    stats = softmax_pv(own, sa_ref, (init,) * N_HEADS, True)
    qk(2 * first_pair, sa_ref)
    stats = softmax_pv(own + 1, sb_ref, stats, True)

    gap = None
    for h in range(N_HEADS):
        q_h = qT_ref[0, h * HP:h * HP + feature_rows, :].astype(F32)
        qn2 = jnp.sum(q_h * q_h, axis=0, keepdims=True)
        g = jnp.sqrt(qn2 * kn_ref[0][:, h:h + 1]) * 1.01 + 0.5 - stats[h][0]
        gap = g if gap is None else jnp.maximum(gap, g)
    frozen_ok = jnp.max(gap) <= FIXED_STABILISER_LIMIT

    def run_frozen(stats):
        m_fix = [st[0] for st in stats]
        sums = pairs(lambda kb, src, sums: fixed_pv(kb, src, m_fix, sums), tuple(st[1] for st in stats))
        return tuple((m_fix[h], sums[h]) for h in range(N_HEADS))

    def run_online(stats):
        return pairs(lambda kb, src, st: softmax_pv(kb, src, st, False), stats)

    stats = lax.cond(frozen_ok, run_frozen, run_online, stats)
    _finish([acc_ref[_value_rows(h), :] for h in range(N_HEADS)], [1.0 / st[1] for st in stats], eye_ref, o_ref)


def _stick_attn_kernel(qT_ref, k_ref, vT_ref, eye_ref, later_ref, o_ref, *, past, tq, tk):
    assert tk % tq == 0 and past % tq == 0
    qi = pl.program_id(1)
    ltk = int(math.log2(tk))
    q_lo = past + qi * tq
    own = q_lo >> ltk
    qpos = q_lo + lax.broadcasted_iota(jnp.int32, (1, tq), 1)

    def block_terms(kb, masked):
        kvs = [_load_kv(k_ref, vT_ref, kb, tk, h) for h in range(N_HEADS)]
        zs = [_dot(kvs[h][0], qT_ref[0, _head_rows(h), :]) for h in range(N_HEADS)]
        if masked:
            kpos = kb * tk + lax.broadcasted_iota(jnp.int32, (tk, 1), 0)
            vis = kpos < qpos
        out = []
        for h in range(N_HEADS):
            z = zs[h]
            drop = jnp.maximum(z, 0.0) + jnp.log2(1.0 + jnp.exp2(-jnp.abs(z)))
            if masked:
                drop = jnp.where(vis, drop, 0.0)
            hi = drop.astype(BF16)
            lo = (drop - hi.astype(F32)).astype(BF16)
            later = _dot(later_ref[...], hi) + _dot(later_ref[...], lo)
            logw = z - drop - later
            if masked:
                logw = jnp.where(vis, logw, NEG_INF)
            out.append((logw, jnp.sum(drop, axis=0, keepdims=True), kvs[h][1]))
        return out

    def accumulate(terms, carry, live=None):
        out = []
        for h in range(N_HEADS):
            logw, total, v_t = terms[h]
            decay, acc = carry[h]
            a = jnp.exp2(logw - decay)
            if live is not None:
                a = jnp.where(live, a, 0.0)
                total = jnp.where(live, total, 0.0)
            out.append((decay + total, acc + _dot(v_t, a.astype(BF16))))
        return tuple(out)

    def min_decay(carry):
        d = carry[0][0]
        for h in range(1, N_HEADS):
            d = jnp.minimum(d, carry[h][0])
        return jnp.min(d)

    init = (jnp.zeros((1, tq), F32), jnp.zeros((HEAD_DIM, tq), F32))
    own_terms = block_terms(own, True)
    prev_terms = block_terms(jnp.maximum(own - 1, 0), False)
    carry = accumulate(own_terms, (init,) * N_HEADS)
    carry = accumulate(prev_terms, carry, live=own >= 1)

    def cond(state):
        kb, dmin, _ = state
        return jnp.logical_and(kb >= 0, dmin < SB_DECAY_LIMIT)

    def body(state):
        kb, _, carry = state
        carry = accumulate(block_terms(kb, False), carry)
        return kb - 1, min_decay(carry), carry

    _, _, carry = lax.while_loop(cond, body, (own - 2, min_decay(carry), carry))
    _finish([c[1] for c in carry], [1.0] * N_HEADS, eye_ref, o_ref)


def _attention(mode, q_t, k, v_t, cst, *, past, s_valid, tq, tk, skip_stats=None):
    b, _, t = q_t.shape
    s = k.shape[1]
    grid = (b, t // tq)
    single = pl.Buffered(1)
    in_specs = [pl.BlockSpec((1, QW, tq), lambda i, j: (i, 0, j)),
                pl.BlockSpec((1, s, QW), lambda i, j: (i, 0, 0), pipeline_mode=single),
                pl.BlockSpec((1, s // tk, VW, tk), lambda i, j: (i, 0, 0, 0), pipeline_mode=single),
                _const_spec((tq, tq))]
    args = [q_t, k, v_t, jnp.eye(tq, dtype=BF16)]
    scratch = []
    if mode == 'sb':
        kern = functools.partial(_stick_attn_kernel, past=past, tq=tq, tk=tk)
        in_specs.append(_const_spec((tk, tk)))
        args.append(cst['later'][tk])
    elif tq == 2 * tk and past % tq == 0:
        kern = functools.partial(_softmax_attn_pipelined_kernel, mode=mode, past=past, tq=tq, tk=tk, s_valid=s_valid)
        scratch = [pltpu.VMEM((N_HEADS, tk, tq), F32), pltpu.VMEM((N_HEADS, tk, tq), F32), pltpu.VMEM((VW, tq), F32)]
        stat_spec = pl.BlockSpec((1, 1, LANE), lambda i, j: (i, 0, 0))
        if mode == 'fox':
            cend, qn, kn = skip_stats
            in_specs += [pl.BlockSpec((1, s // tk, LANE), lambda i, j: (i, 0, 0)), stat_spec, stat_spec]
            args += [cend.reshape(b, s // tk, LANE), qn, kn]
        else:
            in_specs.append(stat_spec)
            args.append(skip_stats[0])
    else:
        kern = functools.partial(_softmax_attn_kernel, mode=mode, past=past, tq=tq, tk=tk, s_valid=s_valid)
    width = (N_HEADS // 2) * HP
    return pl.pallas_call(
        kern, grid=grid, in_specs=in_specs,
        out_specs=pl.BlockSpec((1, tq, width), lambda i, j: (i, j, 0)),
        out_shape=jax.ShapeDtypeStruct((b, t, width), BF16), scratch_shapes=scratch,
        compiler_params=_params("parallel", "parallel"), name="attn_" + mode,
    )(*args)


def _lru_kernel(lx_ref, lg_ref, cb_ref, h0_ref, cw_ref, cbias_ref, wr_ref, br_ref, wi_ref, bi_ref, lam_ref,
                y_ref, nb_ref, hl_ref, xcat_ref, h_ref, *, past, tc):
    ci = pl.program_id(1)
    keep = LRU_CONV - 1

    @pl.when(ci == 0)
    def _():
        xcat_ref[0:8, :] = jnp.zeros((8, LRU_WIDTH), F32)
        xcat_ref[8 - keep:8, :] = cb_ref[0]
        h_ref[...] = h0_ref[0]

    x = lx_ref[0]
    xcat_ref[8:8 + tc, :] = x
    xc = cbias_ref[...] + x * cw_ref[keep:keep + 1, :]
    for tap in range(keep):
        shift = keep - tap
        xc = xc + xcat_ref[8 - shift:8 - shift + tc, :] * cw_ref[tap:tap + 1, :]
    nb_ref[0] = xcat_ref[8 + tc - keep:8 + tc, :]
    xcat_ref[0:8, :] = x[tc - 8:tc, :]

    xcb = xc.astype(BF16)
    r = _sigmoid(_dot(xcb, wr_ref[...]) + br_ref[...])
    gate_in = _sigmoid(_dot(xcb, wi_ref[...]) + bi_ref[...])
    log_a = (-LRU_C) * r * _softplus(-lam_ref[...])
    row = lax.broadcasted_iota(jnp.int32, (tc, LRU_WIDTH), 0)
    reset = (past + ci * tc + row) == 0
    a = jnp.where(reset, 0.0, jnp.exp(log_a))
    y2 = 2.0 * log_a
    series = -y2 * (1.0 + y2 * (0.5 + y2 * (1.0 / 6.0 + y2 * (1.0 / 24.0 + y2 * (1.0 / 120.0)))))
    one_minus = jnp.where(y2 > -0.05, series, 1.0 - jnp.exp(y2))
    mult = jnp.where(reset, 1.0, jnp.sqrt(one_minus))
    u = mult * gate_in * xc

    d = 1
    while d < tc:
        ok = row >= d
        u = u + jnp.where(ok, a * pltpu.roll(u, d, 0), 0.0)
        a = jnp.where(ok, a * pltpu.roll(a, d, 0), a)
        d *= 2
    hs = a * h_ref[...] + u
    h_last = hs[tc - 1:tc, :]
    h_ref[...] = h_last
    hl_ref[0] = h_last

    g = lg_ref[0]
    gelu = 0.5 * g * (1.0 + jnp.tanh(0.7978845608028654 * (g + 0.044715 * g * g * g)))
    y_ref[0] = (hs * gelu).astype(BF16)


def _lru(lx, lg, conv_buf, h0, lw, past):
    b, t, w = lx.shape
    tc = _pick(t, (256, 128, 64, 32))
    grid = (b, t // tc)
    row = pl.BlockSpec((1, tc, w), lambda i, j: (i, j, 0))
    keep = LRU_CONV - 1
    in_specs = [row, row, pl.BlockSpec((1, keep, w), lambda i, j: (i, 0, 0)), pl.BlockSpec((1, 1, w), lambda i, j: (i, 0, 0)),
                _const_spec((LRU_CONV, w)), _const_spec((1, w)), _const_spec((w, w)), _const_spec((1, w)),
                _const_spec((w, w)), _const_spec((1, w)), _const_spec((1, w))]
    sds = jax.ShapeDtypeStruct
    return pl.pallas_call(
        functools.partial(_lru_kernel, past=past, tc=tc), grid=grid, in_specs=in_specs,
        out_specs=[row, pl.BlockSpec((1, keep, w), lambda i, j: (i, 0, 0)), pl.BlockSpec((1, 1, w), lambda i, j: (i, 0, 0))],
        out_shape=[sds((b, t, w), BF16), sds((b, keep, w), F32), sds((b, 1, w), F32)],
        scratch_shapes=[pltpu.VMEM((tc + 8, w), F32), pltpu.VMEM((1, w), F32)],
        compiler_params=_params("parallel", "arbitrary"), name="rg_lru",
    )(lx, lg, conv_buf, h0, lw['lru_conv_w'], lw['lru_conv_b'], lw['lru_wr'], lw['lru_br'], lw['lru_wi'],
      lw['lru_bi'], lw['lru_lam'])


def _merge_kernel(x_ref, oa_ref, ob_ref, oc_ref, od_ref, g1_ref, g2_ref, wg_ref, wb_ref, wo_ref, out_ref):
    x = x_ref[0]
    d = x.shape[1]
    hb = _rms(x, g1_ref[...]).astype(BF16)
    branches = (oa_ref[0], ob_ref[0], oc_ref[0], od_ref[0])
    merged = None
    for n, o in enumerate(branches):
        gate = _sigmoid(_dot(hb, wg_ref[:, n * d:(n + 1) * d]))
        term = gate * _dot(o, wb_ref[n])
        merged = term if merged is None else merged + term
    y = _dot(merged.astype(BF16), wo_ref[...])
    out_ref[0] = x + _rms(y, g2_ref[...])


def _merge(x, o_a, o_b, o_c, o_d, lw):
    b, t, d = x.shape
    tm = _pick(t, (256, 128))
    grid = (b, t // tm)
    row = lambda w: pl.BlockSpec((1, tm, w), lambda i, j: (i, j, 0))
    bw = BRANCH_WIDTH
    in_specs = [row(d), row(bw), row(bw), row(bw), row(bw), _const_spec((1, d)), _const_spec((1, d)),
                _const_spec((d, N_BRANCH * d)), _const_spec((N_BRANCH, BRANCH_WIDTH, d)), _const_spec((d, d))]
    return pl.pallas_call(
        _merge_kernel, grid=grid, in_specs=in_specs, out_specs=row(d),
        out_shape=jax.ShapeDtypeStruct((b, t, d), F32),
        compiler_params=_params("parallel", "parallel"), name="merge",
    )(x, o_a, o_b, o_c, o_d, lw['ln_mix_pre'], lw['ln_mix_post'], lw['w_gate'], lw['w_branch'], lw['w_out'])


def _mem_kv_kernel(mem_ref, g_ref, wk_ref, wv_ref, k_ref, v_ref):
    mn = _rms(mem_ref[0], g_ref[...]).astype(BF16)
    k_ref[0] = _dot(mn, wk_ref[...])
    v_ref[0] = _dot(mn, wv_ref[...])


def _mem_kv(mem, lw):
    b, m, d = mem.shape
    w = lw['mem_wk'].shape[1]
    out = jax.ShapeDtypeStruct((b, m, w), F32)
    blk = pl.BlockSpec((1, m, w), lambda i: (i, 0, 0))
    return pl.pallas_call(
        _mem_kv_kernel, grid=(b,),
        in_specs=[pl.BlockSpec((1, m, d), lambda i: (i, 0, 0)), _const_spec((1, d)), _const_spec((d, w)), _const_spec((d, w))],
        out_specs=[blk, blk], out_shape=[out, out], compiler_params=_params("parallel"), name="mem_kv",
    )(mem, lw['mem_norm'], lw['mem_wk'], lw['mem_wv'])


def _mem_attn_kernel(x_ref, mk_ref, mv_ref, g1_ref, g2_ref, wq_ref, wo_ref, out_ref):
    x = x_ref[0]
    hb = _rms(x, g1_ref[...]).astype(BF16)
    q = (_dot(hb, wq_ref[...]) * (MEM_HEAD_DIM ** -0.5)).astype(BF16)
    heads = []
    for h in range(N_HEADS):
        sl = slice(h * MEM_HEAD_DIM, (h + 1) * MEM_HEAD_DIM)
        s = _dot_nt(q[:, sl], mk_ref[0, :, sl])
        p = jnp.exp(s - jnp.max(s, axis=-1, keepdims=True))
        inv = 1.0 / jnp.sum(p, axis=-1, keepdims=True)
        heads.append((_dot(p.astype(BF16), mv_ref[0, :, sl]) * inv).astype(BF16))
    y = _dot(jnp.concatenate(heads, axis=1), wo_ref[...])
    out_ref[0] = x + _rms(y, g2_ref[...])


def _mem_attn(x, mk, mv, lw):
    b, t, d = x.shape
    m, w = mk.shape[1], mk.shape[2]
    tm = _pick(t, (512, 256, 128))
    row = pl.BlockSpec((1, tm, d), lambda i, j: (i, j, 0))
    kv = pl.BlockSpec((1, m, w), lambda i, j: (i, 0, 0))
    return pl.pallas_call(
        _mem_attn_kernel, grid=(b, t // tm),
        in_specs=[row, kv, kv, _const_spec((1, d)), _const_spec((1, d)), _const_spec((d, w)), _const_spec((w, d))],
        out_specs=row, out_shape=jax.ShapeDtypeStruct((b, t, d), F32),
        compiler_params=_params("parallel", "parallel"), name="mem_attn",
    )(x, mk, mv, lw['ln_mem_pre'], lw['ln_mem_post'], lw['mem_wq'], lw['mem_wo'])


def _ffn_kernel(x_ref, g1_ref, g2_ref, wg_ref, wu_ref, wd_ref, out_ref):
    x = x_ref[0]
    hb = _rms(x, g1_ref[...]).astype(BF16)
    gate = _dot(hb, wg_ref[...])
    act = (gate * _sigmoid(gate) * _dot(hb, wu_ref[...])).astype(BF16)
    y = _dot(act, wd_ref[...])
    out_ref[0] = x + _rms(y, g2_ref[...])


def _ffn(x, lw):
    b, t, d = x.shape
    f = lw['ffn_wg'].shape[1]
    tm = _pick(t, (256, 128))
    row = pl.BlockSpec((1, tm, d), lambda i, j: (i, j, 0))
    single = pl.Buffered(1)
    return pl.pallas_call(
        _ffn_kernel, grid=(b, t // tm),
        in_specs=[row, _const_spec((1, d)), _const_spec((1, d)),
                  pl.BlockSpec((d, f), lambda i, j: (0, 0), pipeline_mode=single),
                  pl.BlockSpec((d, f), lambda i, j: (0, 0), pipeline_mode=single),
                  pl.BlockSpec((f, d), lambda i, j: (0, 0), pipeline_mode=single)],
        out_specs=row, out_shape=jax.ShapeDtypeStruct((b, t, d), F32),
        compiler_params=_params("parallel", "parallel"), name="ffn",
    )(x, lw['ln_ffn_pre'], lw['ln_ffn_post'], lw['ffn_wg'], lw['ffn_wu'], lw['ffn_wd'])


def _constants():
    fk, fq, ones_k, ones_q = _fox_feature_maps()
    later = {}
    for tk in (256,):
        idx = np.arange(tk)
        later[tk] = jnp.asarray((idx[None, :] > idx[:, None]).astype(np.float32), BF16)
    return dict(
        ek=jnp.asarray(_place_qk(), BF16), evT=jnp.eye(VW, dtype=BF16),
        fkmap=jnp.asarray(fk, BF16), fqmapT=jnp.asarray(np.transpose(fq, (0, 2, 1)), BF16),
        ones_k=jnp.asarray(ones_k, F32), ones_qT=jnp.asarray(ones_q.T, F32),
        ekpe=jnp.asarray(_place_kpe(), BF16), later=later,
        seg=jnp.asarray(np.repeat(np.eye(N_HEADS, LANE, dtype=np.float32), HEAD_DIM, axis=0), BF16),
        seg_slot=jnp.asarray(np.repeat(np.eye(N_HEADS, LANE, dtype=np.float32), HP, axis=0), BF16))


def _pad_cols(w, n):
    return jnp.pad(w, ((0, 0), (0, n - w.shape[1])))


_MIX_SIZES = (256, 256, 256, N_HEADS, LRU_WIDTH, LRU_WIDTH, MLA_Q_RANK, MLA_KV_RANK, MLA_ROPE, 256, 256, 256)


def _w_in_plan(d_model):
    offs = np.concatenate([[0], np.cumsum(_MIX_SIZES)])
    fq, fk, fv, ff, lx, lg, cq, ckv, kpe, sq, sk, sv = [np.arange(offs[i], offs[i + 1]) for i in range(len(_MIX_SIZES))]
    half = MLA_ROPE // 2

    def padded(cols):
        return np.concatenate([cols, np.full(LANE - len(cols), -1)])

    mix = [fq, fk, fv, lx, lg, cq, ckv, padded(kpe), padded(np.concatenate([kpe[half:], kpe[:half]])), padded(ff),
           sq, sk, sv]
    pieces, col = [], 0
    for src in mix:
        pieces.append((0, col, src))
        col += len(src)
    assert col == _W_MIX_COLS
    gate0 = offs[-1]
    for i in range(N_BRANCH * d_model // 256):
        pieces.append((1, i * 256, gate0 + i * 256 + np.arange(256)))
    return pieces


def _transpose_kernel(xT_ref, eye_ref, x_ref):
    x_ref[...] = _dot_nt(eye_ref[...], xT_ref[...]).astype(x_ref.dtype)


def _transpose_bf16(x_t):
    r, c = x_t.shape
    rb = _pick(r, (256, 128))
    return pl.pallas_call(
        _transpose_kernel, grid=(r // rb,),
        in_specs=[pl.BlockSpec((rb, c), lambda i: (i, 0)), _const_spec((c, c))],
        out_specs=pl.BlockSpec((c, rb), lambda i: (0, i)), out_shape=jax.ShapeDtypeStruct((c, r), BF16),
        compiler_params=_params("parallel"), name="transpose_w",
    )(x_t, jnp.eye(c, dtype=BF16))


def _repack_w_in(w_in_t):
    d_model = w_in_t.shape[1]
    rows = [[], []]
    for dst, _, src in _w_in_plan(d_model):
        valid = src[src >= 0]
        runs = np.split(valid, np.nonzero(np.diff(valid) != 1)[0] + 1)
        piece = jnp.concatenate([w_in_t[int(r[0]):int(r[-1]) + 1] for r in runs], axis=0)
        rows[dst].append(jnp.pad(piece, ((0, len(src) - len(valid)), (0, 0))))
    return tuple(_transpose_bf16(jnp.concatenate(r, axis=0).astype(BF16)) for r in rows)


def _layer_weights(l, p):
    w_mix, w_gate = _repack_w_in(jnp.transpose(p['w_in'], (2, 0, 1))[:, l, :])
    half = MLA_ROPE // 2
    qk = MLA_NOPE + MLA_ROPE
    wuq = p['mla_w_uq'][l].reshape(MLA_Q_RANK, N_HEADS, qk)
    rope = wuq[:, :, MLA_NOPE:]
    rope_sw = jnp.concatenate([rope[:, :, half:], rope[:, :, :half]], axis=2)
    wuq_p = jnp.pad(wuq, ((0, 0), (0, 0), (0, HP - qk))).reshape(MLA_Q_RANK, QW)
    wuqsw_p = jnp.pad(rope_sw, ((0, 0), (0, 0), (MLA_NOPE, HP - qk))).reshape(MLA_Q_RANK, QW)
    wuk = p['mla_w_uk'][l].reshape(MLA_KV_RANK, N_HEADS, MLA_NOPE)
    wuk_p = jnp.pad(wuk, ((0, 0), (0, 0), (0, HP - MLA_NOPE))).reshape(MLA_KV_RANK, QW)

    def block_diag(w):
        return jax.scipy.linalg.block_diag(*[w[i] for i in range(w.shape[0])]).astype(BF16)

    row = lambda v: v[l].reshape(1, -1).astype(F32)
    return dict(
        w_mix=w_mix, w_gate=w_gate,
        ln_mix_pre=row(p['ln_mix_pre']), ln_mix_post=row(p['ln_mix_post']),
        fox_bf=_pad_cols(row(p['fox_bf']), LANE),
        mla_q_norm=row(p['mla_q_norm']), mla_kv_norm=row(p['mla_kv_norm']),
        wuqT=wuq_p.T.astype(BF16), wuqswT=wuqsw_p.T.astype(BF16), wuk=wuk_p.astype(BF16), wuvT=p['mla_w_uv'][l].T.astype(BF16),
        eqT=jnp.asarray(_place_qk().T, BF16),
        lru_conv_w=p['lru_conv_w'][l].astype(F32), lru_conv_b=row(p['lru_conv_b']),
        lru_wr=block_diag(p['lru_wr'][l]), lru_br=row(p['lru_br']),
        lru_wi=block_diag(p['lru_wi'][l]), lru_bi=row(p['lru_bi']), lru_lam=row(p['lru_lam']),
        w_branch=p['w_branch'][l].astype(BF16), w_out=p['w_out'][l].astype(BF16),
        ln_mem_pre=row(p['ln_mem_pre']), ln_mem_post=row(p['ln_mem_post']), mem_norm=row(p['mem_norm']),
        mem_wq=p['mem_wq'][l].astype(BF16), mem_wk=p['mem_wk'][l].astype(BF16),
        mem_wv=p['mem_wv'][l].astype(BF16), mem_wo=p['mem_wo'][l].astype(BF16),
        ln_ffn_pre=row(p['ln_ffn_pre']), ln_ffn_post=row(p['ln_ffn_post']),
        ffn_wg=p['ffn_wg'][l].astype(BF16), ffn_wu=p['ffn_wu'][l].astype(BF16), ffn_wd=p['ffn_wd'][l].astype(BF16))


def _rope_tables(past, t):
    half = MLA_ROPE // 2
    inv = jnp.power(ROPE_BASE, -jnp.arange(half, dtype=F32) / half)
    ang = (past + jnp.arange(t, dtype=jnp.int32)).astype(F32)[:, None] * inv
    cos, sin = jnp.cos(ang), jnp.sin(ang)
    c32 = jnp.concatenate([cos, cos], axis=1)
    s32 = jnp.concatenate([-sin, sin], axis=1)
    scale = (MLA_NOPE + MLA_ROPE) ** -0.5 * LOG2E
    slot_c = jnp.concatenate([jnp.ones((t, MLA_NOPE), F32), c32, jnp.zeros((t, HP - MLA_NOPE - MLA_ROPE), F32)], axis=1)
    slot_s = jnp.concatenate([jnp.zeros((t, MLA_NOPE), F32), s32, jnp.zeros((t, HP - MLA_NOPE - MLA_ROPE), F32)], axis=1)
    return dict(ck=_pad_cols(c32, LANE), sk=_pad_cols(s32, LANE),
                cqT=(jnp.tile(slot_c, (1, N_HEADS)) * scale).T, sqT=(jnp.tile(slot_s, (1, N_HEADS)) * scale).T)


def _trunk_layer(x, past_state, mem_k, mem_v, lw, cst, layer, depth, earlier):
    b, t, _ = x.shape
    past = 0 if past_state is None else past_state['fox_k'].shape[1]
    tk = 256
    fq, fk, fv, lx, lg, ckv_n, kpe_slab, logf_slab, mla_qT, sb_qT, sk, sv = _in_proj(
        x, lw, _rope_tables(past, t), layer, depth, earlier)
    stacked = (fk, fv, ckv_n, kpe_slab, logf_slab, sk, sv)

    if past_state is None:
        s_valid = t
        keys = (fq, fk, fv, logf_slab, ckv_n, kpe_slab, sk, sv)
        key_layer = layer
        conv_buf = jnp.zeros((b, LRU_CONV - 1, LRU_WIDTH), F32)
        h0 = jnp.zeros((b, 1, LRU_WIDTH), F32)
    else:
        s_valid = past + t
        s_pad = -(-s_valid // tk) * tk

        def cat(old, new):
            old = old.reshape(b, past, -1).astype(new.dtype)
            old = jnp.pad(old, ((0, 0), (0, 0), (0, new.shape[2] - old.shape[2])))
            return jnp.pad(jnp.concatenate([old, new], axis=1), ((0, 0), (0, s_pad - s_valid), (0, 0)))

        def cat_state(old, new):
            return cat(old, new[layer])[None]

        keys = (cat(jnp.zeros((b, past, 256), BF16), fq), cat_state(past_state['fox_k'], fk),
                cat_state(past_state['fox_v'], fv), cat_state(past_state['fox_logf'], logf_slab),
                cat_state(past_state['mla_ckv'], ckv_n), cat_state(past_state['mla_kpe'], kpe_slab),
                cat_state(past_state['sb_k'], sk), cat_state(past_state['sb_v'], sv))
        key_layer = 0
        conv_buf = past_state['lru_conv'].astype(F32)
        h0 = past_state['lru_h'].reshape(b, 1, LRU_WIDTH).astype(F32)

    (fox_qT, fox_k, fox_vT, mla_k, mla_vT, sb_k, sb_vT,
     *fox_skip_stats, mla_key_norm) = _kv_prep(*keys, key_layer, lw, cst, tk)
    if past_state is not None:
        fox_qT = fox_qT[:, :, past:past + t]

    att = functools.partial(_attention, cst=cst, past=past, s_valid=s_valid, tk=tk)
    tq_softmax = _pick(t, (2 * tk, tk, 128))
    o_a = att('fox', fox_qT, fox_k, fox_vT, tq=tq_softmax, skip_stats=fox_skip_stats)
    o_c = att('mla', mla_qT, mla_k, mla_vT, tq=tq_softmax, skip_stats=[mla_key_norm])
    o_d = att('sb', sb_qT, sb_k, sb_vT, tq=_pick(t, (tk, 128)))
    o_b, lru_conv, lru_h = _lru(lx, lg, conv_buf, h0, lw, past)

    x = _merge(x, o_a, o_b, o_c, o_d, lw)
    x = _mem_attn(x, mem_k, mem_v, lw)
    x = _ffn(x, lw)

    return x, stacked, dict(lru_h=lru_h.reshape(b, LRU_WIDTH), lru_conv=lru_conv)


def _state_outputs(stacked, small):
    fk, fv, ckv_n, kpe_slab, logf_slab, sk, sv = stacked
    depth, b, t, _ = fk.shape
    heads = lambda a: a.reshape(depth, b, t, N_HEADS, HEAD_DIM)
    stk = lambda name: jnp.stack([s[name] for s in small])
    return (heads(fk), heads(fv), logf_slab[..., :N_HEADS], stk('lru_h'), stk('lru_conv'),
            ckv_n, kpe_slab[..., :MLA_ROPE], heads(sk), heads(sv))


def kernel(x_prompt, x_sample, cache_fox_k, cache_fox_v, cache_fox_logf, state_lru_h, state_lru_conv, cache_mla_ckv, cache_mla_kpe, cache_sb_k, cache_sb_v, cache_mem_k, cache_mem_v, mem_prompt, ln_mix_pre, ln_mix_post, w_in, fox_bf, lru_conv_w, lru_conv_b, lru_wr, lru_br, lru_wi, lru_bi, lru_lam, mla_q_norm, mla_w_uq, mla_kv_norm, mla_w_uk, mla_w_uv, w_branch, w_out, ln_mem_pre, ln_mem_post, mem_norm, mem_wq, mem_wk, mem_wv, mem_wo, ln_ffn_pre, ln_ffn_post, ffn_wg, ffn_wu, ffn_wd):
    params = dict(ln_mix_pre=ln_mix_pre, ln_mix_post=ln_mix_post, w_in=w_in, fox_bf=fox_bf, lru_conv_w=lru_conv_w,
                  lru_conv_b=lru_conv_b, lru_wr=lru_wr, lru_br=lru_br, lru_wi=lru_wi, lru_bi=lru_bi, lru_lam=lru_lam,
                  mla_q_norm=mla_q_norm, mla_w_uq=mla_w_uq, mla_kv_norm=mla_kv_norm, mla_w_uk=mla_w_uk,
                  mla_w_uv=mla_w_uv, w_branch=w_branch, w_out=w_out, ln_mem_pre=ln_mem_pre, ln_mem_post=ln_mem_post,
                  mem_norm=mem_norm, mem_wq=mem_wq, mem_wk=mem_wk, mem_wv=mem_wv, mem_wo=mem_wo,
                  ln_ffn_pre=ln_ffn_pre, ln_ffn_post=ln_ffn_post, ffn_wg=ffn_wg, ffn_wu=ffn_wu, ffn_wd=ffn_wd)
    depth = w_in.shape[0]
    cst = _constants()
    weights = [_layer_weights(l, params) for l in range(depth)]
    bp, mem_len = mem_prompt.shape[0], mem_prompt.shape[1]

    y_prompt, p_rows, p_small, p_mem = x_prompt, None, [], []
    for l in range(depth):
        mk, mv = _mem_kv(mem_prompt, weights[l])
        y_prompt, p_rows, small = _trunk_layer(y_prompt, None, mk.astype(BF16), mv.astype(BF16), weights[l], cst,
                                               l, depth, p_rows)
        p_small.append(small)
        p_mem.append((mk.reshape(bp, mem_len, N_HEADS, MEM_HEAD_DIM), mv.reshape(bp, mem_len, N_HEADS, MEM_HEAD_DIM)))

    y_sample, s_rows, s_small = x_sample, None, []
    bs = x_sample.shape[0]
    for l in range(depth):
        past = dict(fox_k=cache_fox_k[l], fox_v=cache_fox_v[l], fox_logf=cache_fox_logf[l], lru_h=state_lru_h[l],
                    lru_conv=state_lru_conv[l], mla_ckv=cache_mla_ckv[l], mla_kpe=cache_mla_kpe[l],
                    sb_k=cache_sb_k[l], sb_v=cache_sb_v[l])
        mk = cache_mem_k[l].reshape(bs, mem_len, -1).astype(BF16)
        mv = cache_mem_v[l].reshape(bs, mem_len, -1).astype(BF16)
        y_sample, s_rows, small = _trunk_layer(y_sample, past, mk, mv, weights[l], cst, l, depth, s_rows)
        s_small.append(small)

    mem_out = (jnp.stack([m[0] for m in p_mem]), jnp.stack([m[1] for m in p_mem]))
    return (y_prompt, y_sample) + _state_outputs(p_rows, p_small) + mem_out + _state_outputs(s_rows, s_small)
```

```python
import functools
import math

import numpy as np
import jax
import jax.numpy as jnp
from jax import lax
from jax.experimental import pallas as pl
from jax.experimental.pallas import tpu as pltpu

F32 = jnp.float32
BF16 = jnp.bfloat16

CHUNK = 64
HEAD_DIM = 64
N_HEADS = 4
LRU_WIDTH = 256
LRU_CONV = 4
LRU_C = 8.0
MLA_Q_RANK = 256
MLA_KV_RANK = 128
MLA_NOPE = 64
MLA_ROPE = 32
MLA_V = 64
ROPE_BASE = 10000.0
N_BRANCH = 4
BRANCH_WIDTH = 256
MEM_HEAD_DIM = 128
EPS = 1e-6
NEG_INF = -1e30

HP = 128
QW = N_HEADS * HP
VW = N_HEADS * HEAD_DIM
LANE = 128
VMEM_LIMIT_BYTES = 56 * 1024 * 1024
LOG2E = 1.4426950408889634
SB_DECAY_LIMIT = 127.0
FOX_SKIP_LIMIT = 150.0
FIXED_STABILISER_LIMIT = 60.0

_NT = (((1,), (1,)), ((), ()))


def _dot(a, b):
    return jnp.dot(a, b, preferred_element_type=F32)


def _dot_nt(a, b):
    return lax.dot_general(a, b, _NT, preferred_element_type=F32)


def _rms(x, g):
    ms = jnp.mean(x * x, axis=-1, keepdims=True)
    return x * lax.rsqrt(ms + EPS) * g


def _sigmoid(x):
    return 1.0 / (1.0 + jnp.exp(-x))


def _softplus(x):
    return jnp.maximum(x, 0.0) + jnp.log(1.0 + jnp.exp(-jnp.abs(x)))


def _pick(n, cands):
    for c in cands:
        if n % c == 0:
            return c
    return n


def _params(*sem):
    return pltpu.CompilerParams(dimension_semantics=sem, vmem_limit_bytes=VMEM_LIMIT_BYTES)


def _const_spec(shape):
    nd = len(shape)
    return pl.BlockSpec(shape, lambda *_: (0,) * nd)


def _place_qk():
    e = np.zeros((N_HEADS * HEAD_DIM, QW), np.float32)
    for h in range(N_HEADS):
        for j in range(HEAD_DIM):
            e[h * HEAD_DIM + j, h * HP + j] = 1.0
    return e


def _fox_feature_maps():
    fk = np.zeros((LANE, QW), np.float32)
    fq = np.zeros((LANE, QW), np.float32)
    ones_k = np.zeros((1, QW), np.float32)
    ones_q = np.zeros((1, QW), np.float32)
    for h in range(N_HEADS):
        for part in range(3):
            fq[part * N_HEADS + h, h * HP + HEAD_DIM + part] = 1.0
            fk[part * N_HEADS + h, h * HP + HEAD_DIM + 3 + part] = -1.0
            ones_k[0, h * HP + HEAD_DIM + part] = 1.0
            ones_q[0, h * HP + HEAD_DIM + 3 + part] = 1.0
    return fk, fq, ones_k, ones_q


def _place_kpe():
    e = np.zeros((LANE, QW), np.float32)
    for h in range(N_HEADS):
        for j in range(MLA_ROPE):
            e[j, h * HP + MLA_NOPE + j] = 1.0
    return e


_W_OFF = dict(fq=0, fk=256, fv=512, lx=768, lg=1024, cq=1280, ckv=1536, kpe=1664, kpe_sw=1792, ff=1920,
              sq=2048, sk=2304, sv=2560)
_W_MIX_COLS = 2816


_N_IN_PROJ_INPUTS = 13
_IN_PROJ_STATE_OUTPUTS = (1, 2, 5, 6, 7, 10, 11)


def _in_proj_kernel(*refs):
    (x_ref, g_ref, w_ref, qn_ref, kvn_ref, bf_ref, ck_ref, sk_tab_ref, cqT_ref, sqT_ref,
     wuqT_ref, wuqswT_ref, eqT_ref) = refs[:_N_IN_PROJ_INPUTS]
    (fq_ref, fk_ref, fv_ref, lx_ref, lg_ref, ckv_ref, kpe_ref, logf_ref, mqT_ref, sbqT_ref, sk_ref, sv_ref) = refs[-12:]
    fk_ref, fv_ref, ckv_ref, kpe_ref, logf_ref, sk_ref, sv_ref = (
        r.at[0] for r in (fk_ref, fv_ref, ckv_ref, kpe_ref, logf_ref, sk_ref, sv_ref))
    chains = _row_chains(x_ref.shape[1])
    late = []
    for rows in chains:
        hb = _rms(x_ref[0, rows, :], g_ref[...]).astype(BF16)

        def proj(name, width):
            a = _W_OFF[name]
            return _dot(hb, w_ref[:, a:a + width])

        fq_ref[0, rows, :] = (proj('fq', 256) * (HEAD_DIM ** -0.5 * LOG2E)).astype(BF16)
        fk_ref[0, rows, :] = proj('fk', 256)
        fv_ref[0, rows, :] = proj('fv', 256)
        lx_ref[0, rows, :] = proj('lx', 256)
        lg_ref[0, rows, :] = proj('lg', 256)
        cqn = _rms(proj('cq', 256), qn_ref[...]).astype(BF16)
        ckv_ref[0, rows, :] = _rms(proj('ckv', 128), kvn_ref[...])
        kpe_ref[0, rows, :] = proj('kpe', 128) * ck_ref[rows, :] + proj('kpe_sw', 128) * sk_tab_ref[rows, :]

        ff = proj('ff', 128) + bf_ref[...]
        log_sig = jnp.minimum(ff, 0.0) - jnp.log(1.0 + jnp.exp(-jnp.abs(ff)))
        lane = lax.broadcasted_iota(jnp.int32, ff.shape, 1)
        logf_ref[0, rows, :] = jnp.where(lane < N_HEADS, log_sig, 0.0)

        sq = (proj('sq', 256) * (HEAD_DIM ** -0.5 * LOG2E)).astype(BF16)
        sk_ref[0, rows, :] = proj('sk', 256)
        sv_ref[0, rows, :] = proj('sv', 256)
        late.append((rows, cqn, sq))

    for rows, cqn, sq in late:
        q_t = _dot_nt(wuqT_ref[...], cqn)
        qsw_t = _dot_nt(wuqswT_ref[...], cqn)
        mqT_ref[0, :, rows] = (q_t * cqT_ref[:, rows] + qsw_t * sqT_ref[:, rows]).astype(BF16)
        sbqT_ref[0, :, rows] = _dot_nt(eqT_ref[...], sq).astype(BF16)


def _in_proj(x, lw, tabs, layer, depth, earlier):
    b, t, d = x.shape
    tm = _pick(t, (512, 256, 128))
    grid = (b, t // tm)
    row = lambda w: pl.BlockSpec((1, tm, w), lambda i, j: (i, j, 0))
    srow = lambda w: pl.BlockSpec((1, 1, tm, w), lambda i, j: (layer, i, j, 0))
    colT = pl.BlockSpec((1, QW, tm), lambda i, j: (i, 0, j))
    in_specs = [
        row(d), _const_spec((1, d)), _const_spec((d, _W_MIX_COLS)),
        _const_spec((1, MLA_Q_RANK)), _const_spec((1, MLA_KV_RANK)), _const_spec((1, LANE)),
        pl.BlockSpec((tm, LANE), lambda i, j: (j, 0)), pl.BlockSpec((tm, LANE), lambda i, j: (j, 0)),
        pl.BlockSpec((QW, tm), lambda i, j: (0, j)), pl.BlockSpec((QW, tm), lambda i, j: (0, j)),
        _const_spec((QW, MLA_Q_RANK)), _const_spec((QW, MLA_Q_RANK)), _const_spec((QW, 256)),
    ]
    sds = jax.ShapeDtypeStruct
    state = lambda w: sds((depth, b, t, w), F32)
    out_shape = [
        sds((b, t, 256), BF16),
        state(256), state(256),
        sds((b, t, 256), F32), sds((b, t, 256), F32),
        state(128), state(LANE), state(LANE),
        sds((b, QW, t), BF16), sds((b, QW, t), BF16),
        state(256), state(256),
    ]
    out_specs = [row(256), srow(256), srow(256), row(256), row(256), srow(128), srow(LANE), srow(LANE),
                 colT, colT, srow(256), srow(256)]
    assert len(in_specs) == _N_IN_PROJ_INPUTS
    aliases = {}
    if earlier is not None:
        in_specs += [pl.BlockSpec(memory_space=pl.ANY)] * len(earlier)
        aliases = {_N_IN_PROJ_INPUTS + n: out for n, out in enumerate(_IN_PROJ_STATE_OUTPUTS)}
    return pl.pallas_call(
        _in_proj_kernel, grid=grid, in_specs=in_specs, out_specs=out_specs, out_shape=out_shape,
        input_output_aliases=aliases, compiler_params=_params("parallel", "parallel"), name="in_proj",
    )(x, lw['ln_mix_pre'], lw['w_mix'], lw['mla_q_norm'], lw['mla_kv_norm'], lw['fox_bf'],
      tabs['ck'], tabs['sk'], tabs['cqT'], tabs['sqT'], lw['wuqT'], lw['wuqswT'], lw['eqT'], *(earlier or ()))


def _cumsum_rows(x):
    n = x.shape[0]
    row = lax.broadcasted_iota(jnp.int32, x.shape, 0)
    d = 1
    while d < n:
        x = x + jnp.where(row >= d, pltpu.roll(x, d, 0), 0.0)
        d *= 2
    return x


def _kv_prep_kernel(fq_ref, fk_ref, fv_ref, logf_ref, ckv_ref, kpe_ref, sk_ref, sv_ref,
                    ek_ref, evT_ref, eqT_ref, fkmap_ref, fqmapT_ref, onesk_ref, onesq_ref,
                    wuk_ref, ekpe_ref, wuvT_ref, seg_ref, segslot_ref,
                    fqT_out, fk_out, fvT_out, mk_out, mvT_out, sbk_out, sbvT_out, cend_out, qn_out, kn_out, mkn_out,
                    carry_ref, *, tk):
    fk_ref, fv_ref, logf_ref, ckv_ref, kpe_ref, sk_ref, sv_ref = (
        r.at[0] for r in (fk_ref, fv_ref, logf_ref, ckv_ref, kpe_ref, sk_ref, sv_ref))

    @pl.when(pl.program_id(1) == 0)
    def _():
        carry_ref[...] = jnp.zeros_like(carry_ref)
        qn_out[...] = jnp.zeros_like(qn_out)
        kn_out[...] = jnp.zeros_like(kn_out)
        mkn_out[...] = jnp.zeros_like(mkn_out)

    ts = fk_ref.shape[1]
    c = _cumsum_rows(logf_ref[0]) + carry_ref[...]
    carry_ref[...] = c[ts - 1:ts, :]
    c = c * LOG2E
    for j in range(ts // tk):
        cend_out[0, j] = c[(j + 1) * tk - 1:(j + 1) * tk, :]

    def norm_bound(x, seg):
        xf = x.astype(F32)
        sums = _dot((xf * xf).astype(BF16), seg) * 1.01
        return jnp.max(sums, axis=0, keepdims=True)

    qn_out[0] = jnp.maximum(qn_out[0], norm_bound(fq_ref[0], seg_ref[...]))
    kn_out[0] = jnp.maximum(kn_out[0], norm_bound(fk_ref[0].astype(BF16), seg_ref[...]))
    c_hi = c.astype(BF16).astype(F32)
    rem = c - c_hi
    c_mid = rem.astype(BF16).astype(F32)
    c_lo = (rem - c_mid).astype(BF16).astype(F32)
    feat = (c_hi + pltpu.roll(c_mid, N_HEADS, 1) + pltpu.roll(c_lo, 2 * N_HEADS, 1)).astype(BF16)

    def store_vt(out, v_t):
        for j in range(ts // tk):
            out[0, j] = v_t[:, j * tk:(j + 1) * tk]

    fk_out[0] = (_dot(fk_ref[0].astype(BF16), ek_ref[...]) + _dot(feat, fkmap_ref[...]) + onesk_ref[...]).astype(BF16)
    fqT_out[0] = (_dot_nt(eqT_ref[...], fq_ref[0]) + _dot_nt(fqmapT_ref[...], feat) + onesq_ref[...]).astype(BF16)
    store_vt(fvT_out, _dot_nt(evT_ref[...], fv_ref[0].astype(BF16)).astype(BF16))

    ckv = ckv_ref[0].astype(BF16)
    mk = (_dot(ckv, wuk_ref[...]) + _dot(kpe_ref[0].astype(BF16), ekpe_ref[...])).astype(BF16)
    mk_out[0] = mk
    mkn_out[0] = jnp.maximum(mkn_out[0], norm_bound(mk, segslot_ref[...]))
    store_vt(mvT_out, _dot_nt(wuvT_ref[...], ckv).astype(BF16))

    sbk_out[0] = _dot(sk_ref[0].astype(BF16), ek_ref[...]).astype(BF16)
    store_vt(sbvT_out, _dot_nt(evT_ref[...], sv_ref[0].astype(BF16)).astype(BF16))


def _kv_prep(fq, fk, fv, logf, ckv, kpe, sk, sv, layer, lw, cst, tk):
    _, b, s, _ = fk.shape
    ts = _pick(s, (1024, 512, 256))
    grid = (b, s // ts)
    row = lambda w: pl.BlockSpec((1, ts, w), lambda i, j: (i, j, 0))
    srow = lambda w: pl.BlockSpec((1, 1, ts, w), lambda i, j: (layer, i, j, 0))
    colT = pl.BlockSpec((1, QW, ts), lambda i, j: (i, 0, j))
    vT = pl.BlockSpec((1, ts // tk, VW, tk), lambda i, j: (i, j, 0, 0))
    in_specs = [row(256), srow(256), srow(256), srow(LANE), srow(128), srow(LANE), srow(256), srow(256),
                _const_spec((256, QW)), _const_spec((VW, VW)), _const_spec((QW, 256)),
                _const_spec((LANE, QW)), _const_spec((QW, LANE)), _const_spec((1, QW)), _const_spec((QW, 1)),
                _const_spec((MLA_KV_RANK, QW)), _const_spec((LANE, QW)), _const_spec((VW, MLA_KV_RANK)),
                _const_spec((VW, LANE)), _const_spec((QW, LANE))]
    sds = jax.ShapeDtypeStruct
    slab = sds((b, s, QW), BF16)
    slab_t = sds((b, s // tk, VW, tk), BF16)
    stat = sds((b, 1, LANE), F32)
    stat_spec = pl.BlockSpec((1, 1, LANE), lambda i, j: (i, 0, 0))
    out_shape = [sds((b, QW, s), BF16), slab, slab_t, slab, slab_t, slab, slab_t,
                 sds((b, s // tk, 1, LANE), F32), stat, stat, stat]
    out_specs = [colT, row(QW), vT, row(QW), vT, row(QW), vT,
                 pl.BlockSpec((1, ts // tk, 1, LANE), lambda i, j: (i, j, 0, 0)), stat_spec, stat_spec, stat_spec]
    return pl.pallas_call(
        functools.partial(_kv_prep_kernel, tk=tk), grid=grid, in_specs=in_specs, out_specs=out_specs,
        out_shape=out_shape, scratch_shapes=[pltpu.VMEM((1, LANE), F32)],
        compiler_params=_params("parallel", "arbitrary"), name="kv_prep",
    )(fq, fk, fv, logf, ckv, kpe, sk, sv, cst['ek'], cst['evT'], lw['eqT'], cst['fkmap'], cst['fqmapT'],
      cst['ones_k'], cst['ones_qT'], lw['wuk'], cst['ekpe'], lw['wuvT'], cst['seg'], cst['seg_slot'])


def _head_rows(h):
    return slice(h * HP, (h + 1) * HP)


def _value_rows(h):
    return slice(h * HEAD_DIM, (h + 1) * HEAD_DIM)


def _load_kv(k_ref, vT_ref, kb, tk, h):
    off = pl.multiple_of(kb * tk, tk)
    return k_ref[0, pl.ds(off, tk), _head_rows(h)], vT_ref[0, kb, _value_rows(h), :]


def _finish(accs, invs, eye_ref, o_ref):
    for pair in range(N_HEADS // 2):
        o_t = jnp.concatenate([accs[2 * pair] * invs[2 * pair], accs[2 * pair + 1] * invs[2 * pair + 1]], axis=0)
        o_ref[0, :, _head_rows(pair)] = _dot_nt(eye_ref[...], o_t.astype(BF16)).astype(BF16)


def _softmax_attn_kernel(qT_ref, k_ref, vT_ref, eye_ref, o_ref, *, mode, past, tq, tk, s_valid):
    qi = pl.program_id(1)
    ltk = int(math.log2(tk))
    q_lo = past + qi * tq
    q_hi = q_lo + (tq - 1)
    if mode == 'fox':
        n_full = (q_lo + 1) >> ltk
        n_blk = (q_hi >> ltk) + 1
    else:
        lim_lo = ((q_lo // CHUNK) + 1) * CHUNK
        lim_hi = jnp.minimum(((q_hi // CHUNK) + 1) * CHUNK, s_valid)
        n_blk = (lim_hi + (tk - 1)) >> ltk
        n_full = jnp.minimum(lim_lo >> ltk, n_blk)
    qpos = q_lo + lax.broadcasted_iota(jnp.int32, (1, tq), 1)

    def step(kb, carry, masked):
        off = pl.multiple_of(kb * tk, tk)
        scores = [_dot(k_ref[0, pl.ds(off, tk), _head_rows(h)], qT_ref[0, _head_rows(h), :]) for h in range(N_HEADS)]
        if masked:
            kpos = kb * tk + lax.broadcasted_iota(jnp.int32, (tk, 1), 0)
            if mode == 'fox':
                vis = kpos <= qpos
            else:
                vis = jnp.logical_and((kpos // CHUNK) <= (qpos // CHUNK), kpos < s_valid)
        out = []
        for h in range(N_HEADS):
            m, l, acc = carry[h]
            s = jnp.where(vis, scores[h], NEG_INF) if masked else scores[h]
            m_new = jnp.maximum(m, jnp.max(s, axis=0, keepdims=True))
            p = jnp.exp2(s - m_new)
            alpha = jnp.exp2(m - m_new)
            l = alpha * l + jnp.sum(p, axis=0, keepdims=True)
            acc = alpha * acc + _dot(vT_ref[0, kb, _value_rows(h), :], p.astype(BF16))
            out.append((m_new, l, acc))
        return tuple(out)

    init = (jnp.full((1, tq), NEG_INF, F32), jnp.zeros((1, tq), F32), jnp.zeros((HEAD_DIM, tq), F32))
    carry = lax.fori_loop(0, n_full, functools.partial(step, masked=False), (init,) * N_HEADS)
    carry = lax.fori_loop(n_full, n_blk, functools.partial(step, masked=True), carry)
    _finish([c[2] for c in carry], [1.0 / c[1] for c in carry], eye_ref, o_ref)


def _first_needed_pair(cend_ref, qn_ref, kn_ref, n_pairs):
    nblk = cend_ref.shape[1]
    cend = cend_ref[0]
    c_tile = cend_ref[0, pl.ds(jnp.maximum(2 * n_pairs - 1, 0), 1), :]
    qk_bound = 2.0 * jnp.sqrt(qn_ref[0] * kn_ref[0]) + 1.0
    lane = lax.broadcasted_iota(jnp.int32, (nblk, LANE), 1)
    blk = lax.broadcasted_iota(jnp.int32, (nblk, 1), 0)
    worst = jnp.max(jnp.where(lane < N_HEADS, qk_bound + c_tile - cend, -jnp.inf), axis=1, keepdims=True)
    is_pair_end = jnp.logical_and((blk & 1) == 1, blk < 2 * n_pairs)
    skip = jnp.logical_and(is_pair_end, worst <= -FOX_SKIP_LIMIT)
    return jnp.sum(skip.astype(jnp.int32))


def _softmax_attn_pipelined_kernel(qT_ref, k_ref, vT_ref, eye_ref, *rest, mode, past, tq, tk, s_valid):
    o_ref, sa_ref, sb_ref, acc_ref = rest[-4:]
    kn_ref = rest[-5]
    qi = pl.program_id(1)
    q_lo = past + qi * tq
    n_pairs = past // tq + qi
    first_pair = _first_needed_pair(*rest[:3], n_pairs) if mode == 'fox' else 0
    qpos = q_lo + lax.broadcasted_iota(jnp.int32, (1, tq), 1)
    feature_rows = HEAD_DIM if mode == 'fox' else HP

    def qk(kb, dst_ref):
        off = pl.multiple_of(kb * tk, tk)
        for h in range(N_HEADS):
            dst_ref[h] = _dot(k_ref[0, pl.ds(off, tk), _head_rows(h)], qT_ref[0, _head_rows(h), :])

    def softmax_pv(kb, src_ref, stats, masked):
        if masked:
            kpos = kb * tk + lax.broadcasted_iota(jnp.int32, (tk, 1), 0)
            if mode == 'fox':
                vis = kpos <= qpos
            else:
                vis = (kpos // CHUNK) <= (qpos // CHUNK)
                if s_valid < k_ref.shape[1]:
                    vis = jnp.logical_and(vis, kpos < s_valid)
        out = []
        for h in range(N_HEADS):
            m, l = stats[h]
            s = src_ref[h]
            if masked:
                s = jnp.where(vis, s, NEG_INF)
            m_new = jnp.maximum(m, jnp.max(s, axis=0, keepdims=True))
            p = jnp.exp2(s - m_new)
            alpha = jnp.exp2(m - m_new)
            l = alpha * l + jnp.sum(p, axis=0, keepdims=True)
            rows = _value_rows(h)
            acc_ref[rows, :] = alpha * acc_ref[rows, :] + _dot(vT_ref[0, kb, rows, :], p.astype(BF16))
            out.append((m_new, l))
        return tuple(out)

    def fixed_pv(kb, src_ref, m_fix, sums):
        out = []
        for h in range(N_HEADS):
            p = jnp.exp2(src_ref[h] - m_fix[h])
            rows = _value_rows(h)
            acc_ref[rows, :] = acc_ref[rows, :] + _dot(vT_ref[0, kb, rows, :], p.astype(BF16))
            out.append(sums[h] + jnp.sum(p, axis=0, keepdims=True))
        return tuple(out)

    def pairs(step, carry):
        def body(j, carry):
            kb = 2 * j
            qk(kb + 1, sb_ref)
            carry = step(kb, sa_ref, carry)
            qk(kb + 2, sa_ref)
            return step(kb + 1, sb_ref, carry)
        return lax.fori_loop(first_pair, n_pairs, body, carry)

    acc_ref[...] = jnp.zeros_like(acc_ref)
    own = 2 * n_pairs
    qk(own, sa_ref)
    qk(own + 1, sb_ref)
    init = (jnp.full((1, tq), NEG_INF, F32), jnp.zeros((1, tq), F32))
    stats = softmax_pv(own, sa_ref, (init,) * N_HEADS, True)
    qk(2 * first_pair, sa_ref)
    stats = softmax_pv(own + 1, sb_ref, stats, True)

    gap = None
    for h in range(N_HEADS):
        q_h = qT_ref[0, h * HP:h * HP + feature_rows, :].astype(F32)
        qn2 = jnp.sum(q_h * q_h, axis=0, keepdims=True)
        g = jnp.sqrt(qn2 * kn_ref[0][:, h:h + 1]) * 1.01 + 0.5 - stats[h][0]
        gap = g if gap is None else jnp.maximum(gap, g)
    frozen_ok = jnp.max(gap) <= FIXED_STABILISER_LIMIT

    def run_frozen(stats):
        m_fix = [st[0] for st in stats]
        sums = pairs(lambda kb, src, sums: fixed_pv(kb, src, m_fix, sums), tuple(st[1] for st in stats))
        return tuple((m_fix[h], sums[h]) for h in range(N_HEADS))

    def run_online(stats):
        return pairs(lambda kb, src, st: softmax_pv(kb, src, st, False), stats)

    stats = lax.cond(frozen_ok, run_frozen, run_online, stats)
    _finish([acc_ref[_value_rows(h), :] for h in range(N_HEADS)], [1.0 / st[1] for st in stats], eye_ref, o_ref)


def _stick_attn_kernel(qT_ref, k_ref, vT_ref, eye_ref, later_ref, o_ref, *, past, tq, tk):
    assert tk % tq == 0 and past % tq == 0
    qi = pl.program_id(1)
    ltk = int(math.log2(tk))
    q_lo = past + qi * tq
    own = q_lo >> ltk
    qpos = q_lo + lax.broadcasted_iota(jnp.int32, (1, tq), 1)

    def block_terms(kb, masked):
        kvs = [_load_kv(k_ref, vT_ref, kb, tk, h) for h in range(N_HEADS)]
        zs = [_dot(kvs[h][0], qT_ref[0, _head_rows(h), :]) for h in range(N_HEADS)]
        if masked:
            kpos = kb * tk + lax.broadcasted_iota(jnp.int32, (tk, 1), 0)
            vis = kpos < qpos
        out = []
        for h in range(N_HEADS):
            z = zs[h]
            drop = jnp.maximum(z, 0.0) + jnp.log2(1.0 + jnp.exp2(-jnp.abs(z)))
            if masked:
                drop = jnp.where(vis, drop, 0.0)
            hi = drop.astype(BF16)
            lo = (drop - hi.astype(F32)).astype(BF16)
            later = _dot(later_ref[...], hi) + _dot(later_ref[...], lo)
            logw = z - drop - later
            if masked:
                logw = jnp.where(vis, logw, NEG_INF)
            out.append((logw, jnp.sum(drop, axis=0, keepdims=True), kvs[h][1]))
        return out

    def accumulate(terms, carry, live=None):
        out = []
        for h in range(N_HEADS):
            logw, total, v_t = terms[h]
            decay, acc = carry[h]
            a = jnp.exp2(logw - decay)
            if live is not None:
                a = jnp.where(live, a, 0.0)
                total = jnp.where(live, total, 0.0)
            out.append((decay + total, acc + _dot(v_t, a.astype(BF16))))
        return tuple(out)

    def min_decay(carry):
        d = carry[0][0]
        for h in range(1, N_HEADS):
            d = jnp.minimum(d, carry[h][0])
        return jnp.min(d)

    init = (jnp.zeros((1, tq), F32), jnp.zeros((HEAD_DIM, tq), F32))
    own_terms = block_terms(own, True)
    prev_terms = block_terms(jnp.maximum(own - 1, 0), False)
    carry = accumulate(own_terms, (init,) * N_HEADS)
    carry = accumulate(prev_terms, carry, live=own >= 1)

    def cond(state):
        kb, dmin, _ = state
        return jnp.logical_and(kb >= 0, dmin < SB_DECAY_LIMIT)

    def body(state):
        kb, _, carry = state
        carry = accumulate(block_terms(kb, False), carry)
        return kb - 1, min_decay(carry), carry

    _, _, carry = lax.while_loop(cond, body, (own - 2, min_decay(carry), carry))
    _finish([c[1] for c in carry], [1.0] * N_HEADS, eye_ref, o_ref)


def _attention(mode, q_t, k, v_t, cst, *, past, s_valid, tq, tk, skip_stats=None):
    b, _, t = q_t.shape
    s = k.shape[1]
    grid = (b, t // tq)
    single = pl.Buffered(1)
    in_specs = [pl.BlockSpec((1, QW, tq), lambda i, j: (i, 0, j)),
                pl.BlockSpec((1, s, QW), lambda i, j: (i, 0, 0), pipeline_mode=single),
                pl.BlockSpec((1, s // tk, VW, tk), lambda i, j: (i, 0, 0, 0), pipeline_mode=single),
                _const_spec((tq, tq))]
    args = [q_t, k, v_t, jnp.eye(tq, dtype=BF16)]
    scratch = []
    if mode == 'sb':
        kern = functools.partial(_stick_attn_kernel, past=past, tq=tq, tk=tk)
        in_specs.append(_const_spec((tk, tk)))
        args.append(cst['later'][tk])
    elif tq == 2 * tk and past % tq == 0:
        kern = functools.partial(_softmax_attn_pipelined_kernel, mode=mode, past=past, tq=tq, tk=tk, s_valid=s_valid)
        scratch = [pltpu.VMEM((N_HEADS, tk, tq), F32), pltpu.VMEM((N_HEADS, tk, tq), F32), pltpu.VMEM((VW, tq), F32)]
        stat_spec = pl.BlockSpec((1, 1, LANE), lambda i, j: (i, 0, 0))
        if mode == 'fox':
            cend, qn, kn = skip_stats
            in_specs += [pl.BlockSpec((1, s // tk, LANE), lambda i, j: (i, 0, 0)), stat_spec, stat_spec]
            args += [cend.reshape(b, s // tk, LANE), qn, kn]
        else:
            in_specs.append(stat_spec)
            args.append(skip_stats[0])
    else:
        kern = functools.partial(_softmax_attn_kernel, mode=mode, past=past, tq=tq, tk=tk, s_valid=s_valid)
    width = (N_HEADS // 2) * HP
    return pl.pallas_call(
        kern, grid=grid, in_specs=in_specs,
        out_specs=pl.BlockSpec((1, tq, width), lambda i, j: (i, j, 0)),
        out_shape=jax.ShapeDtypeStruct((b, t, width), BF16), scratch_shapes=scratch,
        compiler_params=_params("parallel", "parallel"), name="attn_" + mode,
    )(*args)


def _lru_kernel(lx_ref, lg_ref, cb_ref, h0_ref, cw_ref, cbias_ref, wr_ref, br_ref, wi_ref, bi_ref, lam_ref,
                y_ref, nb_ref, hl_ref, xcat_ref, h_ref, *, past, tc):
    ci = pl.program_id(1)
    keep = LRU_CONV - 1

    @pl.when(ci == 0)
    def _():
        xcat_ref[0:8, :] = jnp.zeros((8, LRU_WIDTH), F32)
        xcat_ref[8 - keep:8, :] = cb_ref[0]
        h_ref[...] = h0_ref[0]

    x = lx_ref[0]
    xcat_ref[8:8 + tc, :] = x
    xc = cbias_ref[...] + x * cw_ref[keep:keep + 1, :]
    for tap in range(keep):
        shift = keep - tap
        xc = xc + xcat_ref[8 - shift:8 - shift + tc, :] * cw_ref[tap:tap + 1, :]
    nb_ref[0] = xcat_ref[8 + tc - keep:8 + tc, :]
    xcat_ref[0:8, :] = x[tc - 8:tc, :]

    xcb = xc.astype(BF16)
    r = _sigmoid(_dot(xcb, wr_ref[...]) + br_ref[...])
    gate_in = _sigmoid(_dot(xcb, wi_ref[...]) + bi_ref[...])
    log_a = (-LRU_C) * r * _softplus(-lam_ref[...])
    row = lax.broadcasted_iota(jnp.int32, (tc, LRU_WIDTH), 0)
    reset = (past + ci * tc + row) == 0
    a = jnp.where(reset, 0.0, jnp.exp(log_a))
    y2 = 2.0 * log_a
    series = -y2 * (1.0 + y2 * (0.5 + y2 * (1.0 / 6.0 + y2 * (1.0 / 24.0 + y2 * (1.0 / 120.0)))))
    one_minus = jnp.where(y2 > -0.05, series, 1.0 - jnp.exp(y2))
    mult = jnp.where(reset, 1.0, jnp.sqrt(one_minus))
    u = mult * gate_in * xc

    d = 1
    while d < tc:
        ok = row >= d
        u = u + jnp.where(ok, a * pltpu.roll(u, d, 0), 0.0)
        a = jnp.where(ok, a * pltpu.roll(a, d, 0), a)
        d *= 2
    hs = a * h_ref[...] + u
    h_last = hs[tc - 1:tc, :]
    h_ref[...] = h_last
    hl_ref[0] = h_last

    g = lg_ref[0]
    gelu = 0.5 * g * (1.0 + jnp.tanh(0.7978845608028654 * (g + 0.044715 * g * g * g)))
    y_ref[0] = (hs * gelu).astype(BF16)


def _lru(lx, lg, conv_buf, h0, lw, past):
    b, t, w = lx.shape
    tc = _pick(t, (256, 128, 64, 32))
    grid = (b, t // tc)
    row = pl.BlockSpec((1, tc, w), lambda i, j: (i, j, 0))
    keep = LRU_CONV - 1
    in_specs = [row, row, pl.BlockSpec((1, keep, w), lambda i, j: (i, 0, 0)), pl.BlockSpec((1, 1, w), lambda i, j: (i, 0, 0)),
                _const_spec((LRU_CONV, w)), _const_spec((1, w)), _const_spec((w, w)), _const_spec((1, w)),
                _const_spec((w, w)), _const_spec((1, w)), _const_spec((1, w))]
    sds = jax.ShapeDtypeStruct
    return pl.pallas_call(
        functools.partial(_lru_kernel, past=past, tc=tc), grid=grid, in_specs=in_specs,
        out_specs=[row, pl.BlockSpec((1, keep, w), lambda i, j: (i, 0, 0)), pl.BlockSpec((1, 1, w), lambda i, j: (i, 0, 0))],
        out_shape=[sds((b, t, w), BF16), sds((b, keep, w), F32), sds((b, 1, w), F32)],
        scratch_shapes=[pltpu.VMEM((tc + 8, w), F32), pltpu.VMEM((1, w), F32)],
        compiler_params=_params("parallel", "arbitrary"), name="rg_lru",
    )(lx, lg, conv_buf, h0, lw['lru_conv_w'], lw['lru_conv_b'], lw['lru_wr'], lw['lru_br'], lw['lru_wi'],
      lw['lru_bi'], lw['lru_lam'])


def _row_chains(tm):
    n = 2 if tm % 512 == 0 else 1
    return [slice(i * (tm // n), (i + 1) * (tm // n)) for i in range(n)]


def _merge_kernel(x_ref, oa_ref, ob_ref, oc_ref, od_ref, g1_ref, g2_ref, wg_ref, wb_ref, wo_ref, out_ref):
    d = x_ref.shape[2]
    for rows in _row_chains(x_ref.shape[1]):
        x = x_ref[0, rows, :]
        hb = _rms(x, g1_ref[...]).astype(BF16)
        merged = None
        for n, o_ref in enumerate((oa_ref, ob_ref, oc_ref, od_ref)):
            gate = _sigmoid(_dot(hb, wg_ref[:, n * d:(n + 1) * d]))
            term = gate * _dot(o_ref[0, rows, :], wb_ref[n])
            merged = term if merged is None else merged + term
        y = _dot(merged.astype(BF16), wo_ref[...])
        out_ref[0, rows, :] = x + _rms(y, g2_ref[...])


def _merge(x, o_a, o_b, o_c, o_d, lw):
    b, t, d = x.shape
    tm = _pick(t, (512, 256, 128))
    grid = (b, t // tm)
    row = lambda w: pl.BlockSpec((1, tm, w), lambda i, j: (i, j, 0))
    bw = BRANCH_WIDTH
    single = pl.Buffered(1)
    in_specs = [row(d), row(bw), row(bw), row(bw), row(bw), _const_spec((1, d)), _const_spec((1, d)),
                pl.BlockSpec((d, N_BRANCH * d), lambda i, j: (0, 0), pipeline_mode=single),
                pl.BlockSpec((N_BRANCH, BRANCH_WIDTH, d), lambda i, j: (0, 0, 0), pipeline_mode=single),
                pl.BlockSpec((d, d), lambda i, j: (0, 0), pipeline_mode=single)]
    return pl.pallas_call(
        _merge_kernel, grid=grid, in_specs=in_specs, out_specs=row(d),
        out_shape=jax.ShapeDtypeStruct((b, t, d), F32),
        compiler_params=_params("parallel", "parallel"), name="merge",
    )(x, o_a, o_b, o_c, o_d, lw['ln_mix_pre'], lw['ln_mix_post'], lw['w_gate'], lw['w_branch'], lw['w_out'])


def _mem_kv_kernel(mem_ref, g_ref, wk_ref, wv_ref, k_ref, v_ref):
    mn = _rms(mem_ref[0], g_ref[...]).astype(BF16)
    k_ref[0] = _dot(mn, wk_ref[...])
    v_ref[0] = _dot(mn, wv_ref[...])


def _mem_kv(mem, lw):
    b, m, d = mem.shape
    w = lw['mem_wk'].shape[1]
    out = jax.ShapeDtypeStruct((b, m, w), F32)
    blk = pl.BlockSpec((1, m, w), lambda i: (i, 0, 0))
    return pl.pallas_call(
        _mem_kv_kernel, grid=(b,),
        in_specs=[pl.BlockSpec((1, m, d), lambda i: (i, 0, 0)), _const_spec((1, d)), _const_spec((d, w)), _const_spec((d, w))],
        out_specs=[blk, blk], out_shape=[out, out], compiler_params=_params("parallel"), name="mem_kv",
    )(mem, lw['mem_norm'], lw['mem_wk'], lw['mem_wv'])


def _mem_attn_kernel(x_ref, mk_ref, mv_ref, g1_ref, g2_ref, wq_ref, wo_ref, out_ref):
    for rows in _row_chains(x_ref.shape[1]):
        x = x_ref[0, rows, :]
        hb = _rms(x, g1_ref[...]).astype(BF16)
        q = (_dot(hb, wq_ref[...]) * (MEM_HEAD_DIM ** -0.5)).astype(BF16)
        heads = []
        for h in range(N_HEADS):
            sl = slice(h * MEM_HEAD_DIM, (h + 1) * MEM_HEAD_DIM)
            s = _dot_nt(q[:, sl], mk_ref[0, :, sl])
            p = jnp.exp(s - jnp.max(s, axis=-1, keepdims=True))
            inv = 1.0 / jnp.sum(p, axis=-1, keepdims=True)
            heads.append((_dot(p.astype(BF16), mv_ref[0, :, sl]) * inv).astype(BF16))
        y = _dot(jnp.concatenate(heads, axis=1), wo_ref[...])
        out_ref[0, rows, :] = x + _rms(y, g2_ref[...])


def _mem_attn(x, mk, mv, lw):
    b, t, d = x.shape
    m, w = mk.shape[1], mk.shape[2]
    tm = _pick(t, (512, 256, 128))
    row = pl.BlockSpec((1, tm, d), lambda i, j: (i, j, 0))
    kv = pl.BlockSpec((1, m, w), lambda i, j: (i, 0, 0))
    return pl.pallas_call(
        _mem_attn_kernel, grid=(b, t // tm),
        in_specs=[row, kv, kv, _const_spec((1, d)), _const_spec((1, d)), _const_spec((d, w)), _const_spec((w, d))],
        out_specs=row, out_shape=jax.ShapeDtypeStruct((b, t, d), F32),
        compiler_params=_params("parallel", "parallel"), name="mem_attn",
    )(x, mk, mv, lw['ln_mem_pre'], lw['ln_mem_post'], lw['mem_wq'], lw['mem_wo'])


def _ffn_kernel(x_ref, g1_ref, g2_ref, wg_ref, wu_ref, wd_ref, out_ref):
    for rows in _row_chains(x_ref.shape[1]):
        x = x_ref[0, rows, :]
        hb = _rms(x, g1_ref[...]).astype(BF16)
        gate = _dot(hb, wg_ref[...])
        act = (gate * _sigmoid(gate) * _dot(hb, wu_ref[...])).astype(BF16)
        y = _dot(act, wd_ref[...])
        out_ref[0, rows, :] = x + _rms(y, g2_ref[...])


def _ffn(x, lw):
    b, t, d = x.shape
    f = lw['ffn_wg'].shape[1]
    tm = _pick(t, (512, 256, 128))
    row = pl.BlockSpec((1, tm, d), lambda i, j: (i, j, 0))
    single = pl.Buffered(1)
    return pl.pallas_call(
        _ffn_kernel, grid=(b, t // tm),
        in_specs=[row, _const_spec((1, d)), _const_spec((1, d)),
                  pl.BlockSpec((d, f), lambda i, j: (0, 0), pipeline_mode=single),
                  pl.BlockSpec((d, f), lambda i, j: (0, 0), pipeline_mode=single),
                  pl.BlockSpec((f, d), lambda i, j: (0, 0), pipeline_mode=single)],
        out_specs=row, out_shape=jax.ShapeDtypeStruct((b, t, d), F32),
        compiler_params=_params("parallel", "parallel"), name="ffn",
    )(x, lw['ln_ffn_pre'], lw['ln_ffn_post'], lw['ffn_wg'], lw['ffn_wu'], lw['ffn_wd'])


def _constants():
    fk, fq, ones_k, ones_q = _fox_feature_maps()
    later = {}
    for tk in (256,):
        idx = np.arange(tk)
        later[tk] = jnp.asarray((idx[None, :] > idx[:, None]).astype(np.float32), BF16)
    return dict(
        ek=jnp.asarray(_place_qk(), BF16), evT=jnp.eye(VW, dtype=BF16),
        fkmap=jnp.asarray(fk, BF16), fqmapT=jnp.asarray(fq.T, BF16),
        ones_k=jnp.asarray(ones_k, F32), ones_qT=jnp.asarray(ones_q.T, F32),
        ekpe=jnp.asarray(_place_kpe(), BF16), later=later,
        seg=jnp.asarray(np.repeat(np.eye(N_HEADS, LANE, dtype=np.float32), HEAD_DIM, axis=0), BF16),
        seg_slot=jnp.asarray(np.repeat(np.eye(N_HEADS, LANE, dtype=np.float32), HP, axis=0), BF16))


def _pad_cols(w, n):
    return jnp.pad(w, ((0, 0), (0, n - w.shape[1])))


_MIX_SIZES = (256, 256, 256, N_HEADS, LRU_WIDTH, LRU_WIDTH, MLA_Q_RANK, MLA_KV_RANK, MLA_ROPE, 256, 256, 256)


def _w_in_plan(d_model):
    offs = np.concatenate([[0], np.cumsum(_MIX_SIZES)])
    fq, fk, fv, ff, lx, lg, cq, ckv, kpe, sq, sk, sv = [np.arange(offs[i], offs[i + 1]) for i in range(len(_MIX_SIZES))]
    half = MLA_ROPE // 2

    def padded(cols):
        return np.concatenate([cols, np.full(LANE - len(cols), -1)])

    mix = [fq, fk, fv, lx, lg, cq, ckv, padded(kpe), padded(np.concatenate([kpe[half:], kpe[:half]])), padded(ff),
           sq, sk, sv]
    pieces, col = [], 0
    for src in mix:
        pieces.append((0, col, src))
        col += len(src)
    assert col == _W_MIX_COLS
    gate0 = offs[-1]
    for i in range(N_BRANCH * d_model // 256):
        pieces.append((1, i * 256, gate0 + i * 256 + np.arange(256)))
    return pieces


def _transpose_kernel(xT_ref, eye_ref, x_ref):
    x_ref[...] = _dot_nt(eye_ref[...], xT_ref[...]).astype(x_ref.dtype)


def _transpose_bf16(x_t):
    r, c = x_t.shape
    rb = _pick(r, (256, 128))
    return pl.pallas_call(
        _transpose_kernel, grid=(r // rb,),
        in_specs=[pl.BlockSpec((rb, c), lambda i: (i, 0)), _const_spec((c, c))],
        out_specs=pl.BlockSpec((c, rb), lambda i: (0, i)), out_shape=jax.ShapeDtypeStruct((c, r), BF16),
        compiler_params=_params("parallel"), name="transpose_w",
    )(x_t, jnp.eye(c, dtype=BF16))


def _repack_w_in(w_in_t):
    d_model = w_in_t.shape[1]
    rows = [[], []]
    for dst, _, src in _w_in_plan(d_model):
        valid = src[src >= 0]
        runs = np.split(valid, np.nonzero(np.diff(valid) != 1)[0] + 1)
        piece = jnp.concatenate([w_in_t[int(r[0]):int(r[-1]) + 1] for r in runs], axis=0)
        rows[dst].append(jnp.pad(piece, ((0, len(src) - len(valid)), (0, 0))))
    return tuple(_transpose_bf16(jnp.concatenate(r, axis=0).astype(BF16)) for r in rows)


def _layer_weights(l, p):
    w_mix, w_gate = _repack_w_in(jnp.transpose(p['w_in'], (2, 0, 1))[:, l, :])
    half = MLA_ROPE // 2
    qk = MLA_NOPE + MLA_ROPE
    wuq = p['mla_w_uq'][l].reshape(MLA_Q_RANK, N_HEADS, qk)
    rope = wuq[:, :, MLA_NOPE:]
    rope_sw = jnp.concatenate([rope[:, :, half:], rope[:, :, :half]], axis=2)
    wuq_p = jnp.pad(wuq, ((0, 0), (0, 0), (0, HP - qk))).reshape(MLA_Q_RANK, QW)
    wuqsw_p = jnp.pad(rope_sw, ((0, 0), (0, 0), (MLA_NOPE, HP - qk))).reshape(MLA_Q_RANK, QW)
    wuk = p['mla_w_uk'][l].reshape(MLA_KV_RANK, N_HEADS, MLA_NOPE)
    wuk_p = jnp.pad(wuk, ((0, 0), (0, 0), (0, HP - MLA_NOPE))).reshape(MLA_KV_RANK, QW)

    def block_diag(w):
        return jax.scipy.linalg.block_diag(*[w[i] for i in range(w.shape[0])]).astype(BF16)

    row = lambda v: v[l].reshape(1, -1).astype(F32)
    return dict(
        w_mix=w_mix, w_gate=w_gate,
        ln_mix_pre=row(p['ln_mix_pre']), ln_mix_post=row(p['ln_mix_post']),
        fox_bf=_pad_cols(row(p['fox_bf']), LANE),
        mla_q_norm=row(p['mla_q_norm']), mla_kv_norm=row(p['mla_kv_norm']),
        wuqT=wuq_p.T.astype(BF16), wuqswT=wuqsw_p.T.astype(BF16), wuk=wuk_p.astype(BF16), wuvT=p['mla_w_uv'][l].T.astype(BF16),
        eqT=jnp.asarray(_place_qk().T, BF16),
        lru_conv_w=p['lru_conv_w'][l].astype(F32), lru_conv_b=row(p['lru_conv_b']),
        lru_wr=block_diag(p['lru_wr'][l]), lru_br=row(p['lru_br']),
        lru_wi=block_diag(p['lru_wi'][l]), lru_bi=row(p['lru_bi']), lru_lam=row(p['lru_lam']),
        w_branch=p['w_branch'][l].astype(BF16), w_out=p['w_out'][l].astype(BF16),
        ln_mem_pre=row(p['ln_mem_pre']), ln_mem_post=row(p['ln_mem_post']), mem_norm=row(p['mem_norm']),
        mem_wq=p['mem_wq'][l].astype(BF16), mem_wk=p['mem_wk'][l].astype(BF16),
        mem_wv=p['mem_wv'][l].astype(BF16), mem_wo=p['mem_wo'][l].astype(BF16),
        ln_ffn_pre=row(p['ln_ffn_pre']), ln_ffn_post=row(p['ln_ffn_post']),
        ffn_wg=p['ffn_wg'][l].astype(BF16), ffn_wu=p['ffn_wu'][l].astype(BF16), ffn_wd=p['ffn_wd'][l].astype(BF16))


def _rope_tables(past, t):
    half = MLA_ROPE // 2
    inv = jnp.power(ROPE_BASE, -jnp.arange(half, dtype=F32) / half)
    ang = (past + jnp.arange(t, dtype=jnp.int32)).astype(F32)[:, None] * inv
    cos, sin = jnp.cos(ang), jnp.sin(ang)
    c32 = jnp.concatenate([cos, cos], axis=1)
    s32 = jnp.concatenate([-sin, sin], axis=1)
    scale = (MLA_NOPE + MLA_ROPE) ** -0.5 * LOG2E
    slot_c = jnp.concatenate([jnp.ones((t, MLA_NOPE), F32), c32, jnp.zeros((t, HP - MLA_NOPE - MLA_ROPE), F32)], axis=1)
    slot_s = jnp.concatenate([jnp.zeros((t, MLA_NOPE), F32), s32, jnp.zeros((t, HP - MLA_NOPE - MLA_ROPE), F32)], axis=1)
    return dict(ck=_pad_cols(c32, LANE), sk=_pad_cols(s32, LANE),
                cqT=(jnp.tile(slot_c, (1, N_HEADS)) * scale).T, sqT=(jnp.tile(slot_s, (1, N_HEADS)) * scale).T)


def _trunk_layer(x, past_state, mem_k, mem_v, lw, cst, layer, depth, earlier):
    b, t, _ = x.shape
    past = 0 if past_state is None else past_state['fox_k'].shape[1]
    tk = 256
    fq, fk, fv, lx, lg, ckv_n, kpe_slab, logf_slab, mla_qT, sb_qT, sk, sv = _in_proj(
        x, lw, _rope_tables(past, t), layer, depth, earlier)
    stacked = (fk, fv, ckv_n, kpe_slab, logf_slab, sk, sv)

    if past_state is None:
        s_valid = t
        keys = (fq, fk, fv, logf_slab, ckv_n, kpe_slab, sk, sv)
        key_layer = layer
        conv_buf = jnp.zeros((b, LRU_CONV - 1, LRU_WIDTH), F32)
        h0 = jnp.zeros((b, 1, LRU_WIDTH), F32)
    else:
        s_valid = past + t
        s_pad = -(-s_valid // tk) * tk

        def cat(old, new):
            old = old.reshape(b, past, -1).astype(new.dtype)
            old = jnp.pad(old, ((0, 0), (0, 0), (0, new.shape[2] - old.shape[2])))
            return jnp.pad(jnp.concatenate([old, new], axis=1), ((0, 0), (0, s_pad - s_valid), (0, 0)))

        def cat_state(old, new):
            return cat(old, new[layer])[None]

        keys = (cat(jnp.zeros((b, past, 256), BF16), fq), cat_state(past_state['fox_k'], fk),
                cat_state(past_state['fox_v'], fv), cat_state(past_state['fox_logf'], logf_slab),
                cat_state(past_state['mla_ckv'], ckv_n), cat_state(past_state['mla_kpe'], kpe_slab),
                cat_state(past_state['sb_k'], sk), cat_state(past_state['sb_v'], sv))
        key_layer = 0
        conv_buf = past_state['lru_conv'].astype(F32)
        h0 = past_state['lru_h'].reshape(b, 1, LRU_WIDTH).astype(F32)

    (fox_qT, fox_k, fox_vT, mla_k, mla_vT, sb_k, sb_vT,
     *fox_skip_stats, mla_key_norm) = _kv_prep(*keys, key_layer, lw, cst, tk)
    if past_state is not None:
        fox_qT = fox_qT[:, :, past:past + t]

    att = functools.partial(_attention, cst=cst, past=past, s_valid=s_valid, tk=tk)
    tq_softmax = _pick(t, (2 * tk, tk, 128))
    o_a = att('fox', fox_qT, fox_k, fox_vT, tq=tq_softmax, skip_stats=fox_skip_stats)
    o_c = att('mla', mla_qT, mla_k, mla_vT, tq=tq_softmax, skip_stats=[mla_key_norm])
    o_d = att('sb', sb_qT, sb_k, sb_vT, tq=_pick(t, (tk, 128)))
    o_b, lru_conv, lru_h = _lru(lx, lg, conv_buf, h0, lw, past)

    x = _merge(x, o_a, o_b, o_c, o_d, lw)
    x = _mem_attn(x, mem_k, mem_v, lw)
    x = _ffn(x, lw)

    return x, stacked, dict(lru_h=lru_h.reshape(b, LRU_WIDTH), lru_conv=lru_conv)


def _state_outputs(stacked, small):
    fk, fv, ckv_n, kpe_slab, logf_slab, sk, sv = stacked
    depth, b, t, _ = fk.shape
    heads = lambda a: a.reshape(depth, b, t, N_HEADS, HEAD_DIM)
    stk = lambda name: jnp.stack([s[name] for s in small])
    return (heads(fk), heads(fv), logf_slab[..., :N_HEADS], stk('lru_h'), stk('lru_conv'),
            ckv_n, kpe_slab[..., :MLA_ROPE], heads(sk), heads(sv))


def kernel(x_prompt, x_sample, cache_fox_k, cache_fox_v, cache_fox_logf, state_lru_h, state_lru_conv, cache_mla_ckv, cache_mla_kpe, cache_sb_k, cache_sb_v, cache_mem_k, cache_mem_v, mem_prompt, ln_mix_pre, ln_mix_post, w_in, fox_bf, lru_conv_w, lru_conv_b, lru_wr, lru_br, lru_wi, lru_bi, lru_lam, mla_q_norm, mla_w_uq, mla_kv_norm, mla_w_uk, mla_w_uv, w_branch, w_out, ln_mem_pre, ln_mem_post, mem_norm, mem_wq, mem_wk, mem_wv, mem_wo, ln_ffn_pre, ln_ffn_post, ffn_wg, ffn_wu, ffn_wd):
    params = dict(ln_mix_pre=ln_mix_pre, ln_mix_post=ln_mix_post, w_in=w_in, fox_bf=fox_bf, lru_conv_w=lru_conv_w,
                  lru_conv_b=lru_conv_b, lru_wr=lru_wr, lru_br=lru_br, lru_wi=lru_wi, lru_bi=lru_bi, lru_lam=lru_lam,
                  mla_q_norm=mla_q_norm, mla_w_uq=mla_w_uq, mla_kv_norm=mla_kv_norm, mla_w_uk=mla_w_uk,
                  mla_w_uv=mla_w_uv, w_branch=w_branch, w_out=w_out, ln_mem_pre=ln_mem_pre, ln_mem_post=ln_mem_post,
                  mem_norm=mem_norm, mem_wq=mem_wq, mem_wk=mem_wk, mem_wv=mem_wv, mem_wo=mem_wo,
                  ln_ffn_pre=ln_ffn_pre, ln_ffn_post=ln_ffn_post, ffn_wg=ffn_wg, ffn_wu=ffn_wu, ffn_wd=ffn_wd)
    depth = w_in.shape[0]
    cst = _constants()
    weights = [_layer_weights(l, params) for l in range(depth)]
    bp, mem_len = mem_prompt.shape[0], mem_prompt.shape[1]

    y_prompt, p_rows, p_small, p_mem = x_prompt, None, [], []
    for l in range(depth):
        mk, mv = _mem_kv(mem_prompt, weights[l])
        y_prompt, p_rows, small = _trunk_layer(y_prompt, None, mk.astype(BF16), mv.astype(BF16), weights[l], cst,
                                               l, depth, p_rows)
        p_small.append(small)
        p_mem.append((mk.reshape(bp, mem_len, N_HEADS, MEM_HEAD_DIM), mv.reshape(bp, mem_len, N_HEADS, MEM_HEAD_DIM)))

    y_sample, s_rows, s_small = x_sample, None, []
    bs = x_sample.shape[0]
    for l in range(depth):
        past = dict(fox_k=cache_fox_k[l], fox_v=cache_fox_v[l], fox_logf=cache_fox_logf[l], lru_h=state_lru_h[l],
                    lru_conv=state_lru_conv[l], mla_ckv=cache_mla_ckv[l], mla_kpe=cache_mla_kpe[l],
                    sb_k=cache_sb_k[l], sb_v=cache_sb_v[l])
        mk = cache_mem_k[l].reshape(bs, mem_len, -1).astype(BF16)
        mv = cache_mem_v[l].reshape(bs, mem_len, -1).astype(BF16)
        y_sample, s_rows, small = _trunk_layer(y_sample, past, mk, mv, weights[l], cst, l, depth, s_rows)
        s_small.append(small)

    mem_out = (jnp.stack([m[0] for m in p_mem]), jnp.stack([m[1] for m in p_mem]))
    return (y_prompt, y_sample) + _state_outputs(p_rows, p_small) + mem_out + _state_outputs(s_rows, s_small)
```

```python
import functools
import math

import numpy as np
import jax
import jax.numpy as jnp
from jax import lax
from jax.experimental import pallas as pl
from jax.experimental.pallas import tpu as pltpu

F32 = jnp.float32
BF16 = jnp.bfloat16

CHUNK = 64
HEAD_DIM = 64
N_HEADS = 4
LRU_WIDTH = 256
LRU_CONV = 4
LRU_C = 8.0
MLA_Q_RANK = 256
MLA_KV_RANK = 128
MLA_NOPE = 64
MLA_ROPE = 32
MLA_V = 64
ROPE_BASE = 10000.0
N_BRANCH = 4
BRANCH_WIDTH = 256
MEM_HEAD_DIM = 128
EPS = 1e-6
NEG_INF = -1e30

HP = 128
QW = N_HEADS * HP
VW = N_HEADS * HEAD_DIM
LANE = 128
VMEM_LIMIT_BYTES = 56 * 1024 * 1024
LOG2E = 1.4426950408889634
SB_DECAY_LIMIT = 127.0
FOX_SKIP_LIMIT = 150.0
FIXED_STABILISER_LIMIT = 60.0
SOFTPLUS_LINEAR_FROM = 60.0

_NT = (((1,), (1,)), ((), ()))


def _dot(a, b):
    return jnp.dot(a, b, preferred_element_type=F32)


def _dot_nt(a, b):
    return lax.dot_general(a, b, _NT, preferred_element_type=F32)


def _rms(x, g):
    ms = jnp.mean(x * x, axis=-1, keepdims=True)
    return x * lax.rsqrt(ms + EPS) * g


def _sigmoid(x):
    return 1.0 / (1.0 + jnp.exp(-x))


def _softplus(x):
    return jnp.maximum(x, 0.0) + jnp.log(1.0 + jnp.exp(-jnp.abs(x)))


def _pick(n, cands):
    for c in cands:
        if n % c == 0:
            return c
    return n


def _params(*sem):
    return pltpu.CompilerParams(dimension_semantics=sem, vmem_limit_bytes=VMEM_LIMIT_BYTES)


def _const_spec(shape):
    nd = len(shape)
    return pl.BlockSpec(shape, lambda *_: (0,) * nd)


def _place_qk():
    e = np.zeros((N_HEADS * HEAD_DIM, QW), np.float32)
    for h in range(N_HEADS):
        for j in range(HEAD_DIM):
            e[h * HEAD_DIM + j, h * HP + j] = 1.0
    return e


def _fox_feature_maps():
    fk = np.zeros((LANE, QW), np.float32)
    fq = np.zeros((LANE, QW), np.float32)
    ones_k = np.zeros((1, QW), np.float32)
    ones_q = np.zeros((1, QW), np.float32)
    for h in range(N_HEADS):
        for part in range(3):
            fq[part * N_HEADS + h, h * HP + HEAD_DIM + part] = 1.0
            fk[part * N_HEADS + h, h * HP + HEAD_DIM + 3 + part] = -1.0
            ones_k[0, h * HP + HEAD_DIM + part] = 1.0
            ones_q[0, h * HP + HEAD_DIM + 3 + part] = 1.0
    return fk, fq, ones_k, ones_q


def _place_kpe():
    e = np.zeros((LANE, QW), np.float32)
    for h in range(N_HEADS):
        for j in range(MLA_ROPE):
            e[j, h * HP + MLA_NOPE + j] = 1.0
    return e


_W_OFF = dict(fq=0, fk=256, fv=512, lx=768, lg=1024, cq=1280, ckv=1536, kpe=1664, kpe_sw=1792, ff=1920,
              sq=2048, sk=2304, sv=2560)
_W_MIX_COLS = 2816


_N_IN_PROJ_INPUTS = 13
_IN_PROJ_STATE_OUTPUTS = (1, 2, 5, 6, 7, 10, 11)


def _in_proj_kernel(*refs):
    (x_ref, g_ref, w_ref, qn_ref, kvn_ref, bf_ref, ck_ref, sk_tab_ref, cqT_ref, sqT_ref,
     wuqT_ref, wuqswT_ref, eqT_ref) = refs[:_N_IN_PROJ_INPUTS]
    (fq_ref, fk_ref, fv_ref, lx_ref, lg_ref, ckv_ref, kpe_ref, logf_ref, mqT_ref, sbqT_ref, sk_ref, sv_ref) = refs[-12:]
    fk_ref, fv_ref, ckv_ref, kpe_ref, logf_ref, sk_ref, sv_ref = (
        r.at[0] for r in (fk_ref, fv_ref, ckv_ref, kpe_ref, logf_ref, sk_ref, sv_ref))
    chains = _row_chains(x_ref.shape[1])
    late = []
    for rows in chains:
        hb = _rms(x_ref[0, rows, :], g_ref[...]).astype(BF16)

        def proj(name, width):
            a = _W_OFF[name]
            return _dot(hb, w_ref[:, a:a + width])

        fq_ref[0, rows, :] = (proj('fq', 256) * (HEAD_DIM ** -0.5 * LOG2E)).astype(BF16)
        fk_ref[0, rows, :] = proj('fk', 256)
        fv_ref[0, rows, :] = proj('fv', 256)
        lx_ref[0, rows, :] = proj('lx', 256)
        lg_ref[0, rows, :] = proj('lg', 256)
        cqn = _rms(proj('cq', 256), qn_ref[...]).astype(BF16)
        ckv_ref[0, rows, :] = _rms(proj('ckv', 128), kvn_ref[...])
        kpe_ref[0, rows, :] = proj('kpe', 128) * ck_ref[rows, :] + proj('kpe_sw', 128) * sk_tab_ref[rows, :]

        ff = proj('ff', 128) + bf_ref[...]
        log_sig = jnp.minimum(ff, 0.0) - jnp.log(1.0 + jnp.exp(-jnp.abs(ff)))
        lane = lax.broadcasted_iota(jnp.int32, ff.shape, 1)
        logf_ref[0, rows, :] = jnp.where(lane < N_HEADS, log_sig, 0.0)

        sq = (proj('sq', 256) * (HEAD_DIM ** -0.5 * LOG2E)).astype(BF16)
        sk_ref[0, rows, :] = proj('sk', 256)
        sv_ref[0, rows, :] = proj('sv', 256)
        late.append((rows, cqn, sq))

    for rows, cqn, sq in late:
        q_t = _dot_nt(wuqT_ref[...], cqn)
        qsw_t = _dot_nt(wuqswT_ref[...], cqn)
        mqT_ref[0, :, rows] = (q_t * cqT_ref[:, rows] + qsw_t * sqT_ref[:, rows]).astype(BF16)
        sbqT_ref[0, :, rows] = _dot_nt(eqT_ref[...], sq).astype(BF16)


def _in_proj(x, lw, tabs, layer, depth, earlier):
    b, t, d = x.shape
    tm = _pick(t, (512, 256, 128))
    grid = (b, t // tm)
    row = lambda w: pl.BlockSpec((1, tm, w), lambda i, j: (i, j, 0))
    srow = lambda w: pl.BlockSpec((1, 1, tm, w), lambda i, j: (layer, i, j, 0))
    colT = pl.BlockSpec((1, QW, tm), lambda i, j: (i, 0, j))
    in_specs = [
        row(d), _const_spec((1, d)), _const_spec((d, _W_MIX_COLS)),
        _const_spec((1, MLA_Q_RANK)), _const_spec((1, MLA_KV_RANK)), _const_spec((1, LANE)),
        pl.BlockSpec((tm, LANE), lambda i, j: (j, 0)), pl.BlockSpec((tm, LANE), lambda i, j: (j, 0)),
        pl.BlockSpec((QW, tm), lambda i, j: (0, j)), pl.BlockSpec((QW, tm), lambda i, j: (0, j)),
        _const_spec((QW, MLA_Q_RANK)), _const_spec((QW, MLA_Q_RANK)), _const_spec((QW, 256)),
    ]
    sds = jax.ShapeDtypeStruct
    state = lambda w: sds((depth, b, t, w), F32)
    out_shape = [
        sds((b, t, 256), BF16),
        state(256), state(256),
        sds((b, t, 256), F32), sds((b, t, 256), F32),
        state(128), state(LANE), state(LANE),
        sds((b, QW, t), BF16), sds((b, QW, t), BF16),
        state(256), state(256),
    ]
    out_specs = [row(256), srow(256), srow(256), row(256), row(256), srow(128), srow(LANE), srow(LANE),
                 colT, colT, srow(256), srow(256)]
    assert len(in_specs) == _N_IN_PROJ_INPUTS
    aliases = {}
    if earlier is not None:
        in_specs += [pl.BlockSpec(memory_space=pl.ANY)] * len(earlier)
        aliases = {_N_IN_PROJ_INPUTS + n: out for n, out in enumerate(_IN_PROJ_STATE_OUTPUTS)}
    return pl.pallas_call(
        _in_proj_kernel, grid=grid, in_specs=in_specs, out_specs=out_specs, out_shape=out_shape,
        input_output_aliases=aliases, compiler_params=_params("parallel", "parallel"), name="in_proj",
    )(x, lw['ln_mix_pre'], lw['w_mix'], lw['mla_q_norm'], lw['mla_kv_norm'], lw['fox_bf'],
      tabs['ck'], tabs['sk'], tabs['cqT'], tabs['sqT'], lw['wuqT'], lw['wuqswT'], lw['eqT'], *(earlier or ()))


def _cumsum_rows(x):
    n = x.shape[0]
    row = lax.broadcasted_iota(jnp.int32, x.shape, 0)
    d = 1
    while d < n:
        x = x + jnp.where(row >= d, pltpu.roll(x, d, 0), 0.0)
        d *= 2
    return x


def _kv_prep_kernel(fq_ref, fk_ref, fv_ref, logf_ref, ckv_ref, kpe_ref, sk_ref, sv_ref,
                    ek_ref, evT_ref, eqT_ref, fkmap_ref, fqmapT_ref, onesk_ref, onesq_ref,
                    wuk_ref, ekpe_ref, wuvT_ref, seg_ref, segslot_ref,
                    fqT_out, fk_out, fvT_out, mk_out, mvT_out, sbk_out, sbvT_out, cend_out, qn_out, kn_out, mkn_out,
                    carry_ref, *, tk):
    fk_ref, fv_ref, logf_ref, ckv_ref, kpe_ref, sk_ref, sv_ref = (
        r.at[0] for r in (fk_ref, fv_ref, logf_ref, ckv_ref, kpe_ref, sk_ref, sv_ref))

    @pl.when(pl.program_id(1) == 0)
    def _():
        carry_ref[...] = jnp.zeros_like(carry_ref)
        qn_out[...] = jnp.zeros_like(qn_out)
        kn_out[...] = jnp.zeros_like(kn_out)
        mkn_out[...] = jnp.zeros_like(mkn_out)

    ts = fk_ref.shape[1]
    c = _cumsum_rows(logf_ref[0]) + carry_ref[...]
    carry_ref[...] = c[ts - 1:ts, :]
    c = c * LOG2E
    for j in range(ts // tk):
        cend_out[0, j] = c[(j + 1) * tk - 1:(j + 1) * tk, :]

    def norm_bound(x, seg):
        xf = x.astype(F32)
        sums = _dot((xf * xf).astype(BF16), seg) * 1.01
        return jnp.max(sums, axis=0, keepdims=True)

    qn_out[0] = jnp.maximum(qn_out[0], norm_bound(fq_ref[0], seg_ref[...]))
    kn_out[0] = jnp.maximum(kn_out[0], norm_bound(fk_ref[0].astype(BF16), seg_ref[...]))
    c_hi = c.astype(BF16).astype(F32)
    rem = c - c_hi
    c_mid = rem.astype(BF16).astype(F32)
    c_lo = (rem - c_mid).astype(BF16).astype(F32)
    feat = (c_hi + pltpu.roll(c_mid, N_HEADS, 1) + pltpu.roll(c_lo, 2 * N_HEADS, 1)).astype(BF16)

    def store_vt(out, v_t):
        for j in range(ts // tk):
            out[0, j] = v_t[:, j * tk:(j + 1) * tk]

    fk_out[0] = (_dot(fk_ref[0].astype(BF16), ek_ref[...]) + _dot(feat, fkmap_ref[...]) + onesk_ref[...]).astype(BF16)
    fqT_out[0] = (_dot_nt(eqT_ref[...], fq_ref[0]) + _dot_nt(fqmapT_ref[...], feat) + onesq_ref[...]).astype(BF16)
    store_vt(fvT_out, _dot_nt(evT_ref[...], fv_ref[0].astype(BF16)).astype(BF16))

    ckv = ckv_ref[0].astype(BF16)
    mk = (_dot(ckv, wuk_ref[...]) + _dot(kpe_ref[0].astype(BF16), ekpe_ref[...])).astype(BF16)
    mk_out[0] = mk
    mkn_out[0] = jnp.maximum(mkn_out[0], norm_bound(mk, segslot_ref[...]))
    store_vt(mvT_out, _dot_nt(wuvT_ref[...], ckv).astype(BF16))

    sbk_out[0] = _dot(sk_ref[0].astype(BF16), ek_ref[...]).astype(BF16)
    store_vt(sbvT_out, _dot_nt(evT_ref[...], sv_ref[0].astype(BF16)).astype(BF16))


def _kv_prep(fq, fk, fv, logf, ckv, kpe, sk, sv, layer, lw, cst, tk):
    _, b, s, _ = fk.shape
    ts = _pick(s, (1024, 512, 256))
    grid = (b, s // ts)
    row = lambda w: pl.BlockSpec((1, ts, w), lambda i, j: (i, j, 0))
    srow = lambda w: pl.BlockSpec((1, 1, ts, w), lambda i, j: (layer, i, j, 0))
    colT = pl.BlockSpec((1, QW, ts), lambda i, j: (i, 0, j))
    vT = pl.BlockSpec((1, ts // tk, VW, tk), lambda i, j: (i, j, 0, 0))
    in_specs = [row(256), srow(256), srow(256), srow(LANE), srow(128), srow(LANE), srow(256), srow(256),
                _const_spec((256, QW)), _const_spec((VW, VW)), _const_spec((QW, 256)),
                _const_spec((LANE, QW)), _const_spec((QW, LANE)), _const_spec((1, QW)), _const_spec((QW, 1)),
                _const_spec((MLA_KV_RANK, QW)), _const_spec((LANE, QW)), _const_spec((VW, MLA_KV_RANK)),
                _const_spec((VW, LANE)), _const_spec((QW, LANE))]
    sds = jax.ShapeDtypeStruct
    slab = sds((b, s, QW), BF16)
    slab_t = sds((b, s // tk, VW, tk), BF16)
    stat = sds((b, 1, LANE), F32)
    stat_spec = pl.BlockSpec((1, 1, LANE), lambda i, j: (i, 0, 0))
    out_shape = [sds((b, QW, s), BF16), slab, slab_t, slab, slab_t, slab, slab_t,
                 sds((b, s // tk, 1, LANE), F32), stat, stat, stat]
    out_specs = [colT, row(QW), vT, row(QW), vT, row(QW), vT,
                 pl.BlockSpec((1, ts // tk, 1, LANE), lambda i, j: (i, j, 0, 0)), stat_spec, stat_spec, stat_spec]
    return pl.pallas_call(
        functools.partial(_kv_prep_kernel, tk=tk), grid=grid, in_specs=in_specs, out_specs=out_specs,
        out_shape=out_shape, scratch_shapes=[pltpu.VMEM((1, LANE), F32)],
        compiler_params=_params("parallel", "arbitrary"), name="kv_prep",
    )(fq, fk, fv, logf, ckv, kpe, sk, sv, cst['ek'], cst['evT'], lw['eqT'], cst['fkmap'], cst['fqmapT'],
      cst['ones_k'], cst['ones_qT'], lw['wuk'], cst['ekpe'], lw['wuvT'], cst['seg'], cst['seg_slot'])


def _head_rows(h):
    return slice(h * HP, (h + 1) * HP)


def _value_rows(h):
    return slice(h * HEAD_DIM, (h + 1) * HEAD_DIM)


def _load_kv(k_ref, vT_ref, kb, tk, h):
    off = pl.multiple_of(kb * tk, tk)
    return k_ref[0, pl.ds(off, tk), _head_rows(h)], vT_ref[0, kb, _value_rows(h), :]


def _finish(accs, invs, eye_ref, o_ref):
    for pair in range(N_HEADS // 2):
        o_t = jnp.concatenate([accs[2 * pair] * invs[2 * pair], accs[2 * pair + 1] * invs[2 * pair + 1]], axis=0)
        o_ref[0, :, _head_rows(pair)] = _dot_nt(eye_ref[...], o_t.astype(BF16)).astype(BF16)


def _softmax_attn_kernel(qT_ref, k_ref, vT_ref, eye_ref, o_ref, *, mode, past, tq, tk, s_valid):
    qi = pl.program_id(1)
    ltk = int(math.log2(tk))
    q_lo = past + qi * tq
    q_hi = q_lo + (tq - 1)
    if mode == 'fox':
        n_full = (q_lo + 1) >> ltk
        n_blk = (q_hi >> ltk) + 1
    else:
        lim_lo = ((q_lo // CHUNK) + 1) * CHUNK
        lim_hi = jnp.minimum(((q_hi // CHUNK) + 1) * CHUNK, s_valid)
        n_blk = (lim_hi + (tk - 1)) >> ltk
        n_full = jnp.minimum(lim_lo >> ltk, n_blk)
    qpos = q_lo + lax.broadcasted_iota(jnp.int32, (1, tq), 1)

    def step(kb, carry, masked):
        off = pl.multiple_of(kb * tk, tk)
        scores = [_dot(k_ref[0, pl.ds(off, tk), _head_rows(h)], qT_ref[0, _head_rows(h), :]) for h in range(N_HEADS)]
        if masked:
            kpos = kb * tk + lax.broadcasted_iota(jnp.int32, (tk, 1), 0)
            if mode == 'fox':
                vis = kpos <= qpos
            else:
                vis = jnp.logical_and((kpos // CHUNK) <= (qpos // CHUNK), kpos < s_valid)
        out = []
        for h in range(N_HEADS):
            m, l, acc = carry[h]
            s = jnp.where(vis, scores[h], NEG_INF) if masked else scores[h]
            m_new = jnp.maximum(m, jnp.max(s, axis=0, keepdims=True))
            p = jnp.exp2(s - m_new)
            alpha = jnp.exp2(m - m_new)
            l = alpha * l + jnp.sum(p, axis=0, keepdims=True)
            acc = alpha * acc + _dot(vT_ref[0, kb, _value_rows(h), :], p.astype(BF16))
            out.append((m_new, l, acc))
        return tuple(out)

    init = (jnp.full((1, tq), NEG_INF, F32), jnp.zeros((1, tq), F32), jnp.zeros((HEAD_DIM, tq), F32))
    carry = lax.fori_loop(0, n_full, functools.partial(step, masked=False), (init,) * N_HEADS)
    carry = lax.fori_loop(n_full, n_blk, functools.partial(step, masked=True), carry)
    _finish([c[2] for c in carry], [1.0 / c[1] for c in carry], eye_ref, o_ref)


def _first_needed_pair(cend_ref, qn_ref, kn_ref, n_pairs):
    nblk = cend_ref.shape[1]
    cend = cend_ref[0]
    c_tile = cend_ref[0, pl.ds(jnp.maximum(2 * n_pairs - 1, 0), 1), :]
    qk_bound = 2.0 * jnp.sqrt(qn_ref[0] * kn_ref[0]) + 1.0
    lane = lax.broadcasted_iota(jnp.int32, (nblk, LANE), 1)
    blk = lax.broadcasted_iota(jnp.int32, (nblk, 1), 0)
    worst = jnp.max(jnp.where(lane < N_HEADS, qk_bound + c_tile - cend, -jnp.inf), axis=1, keepdims=True)
    is_pair_end = jnp.logical_and((blk & 1) == 1, blk < 2 * n_pairs)
    skip = jnp.logical_and(is_pair_end, worst <= -FOX_SKIP_LIMIT)
    return jnp.sum(skip.astype(jnp.int32))


def _softmax_attn_pipelined_kernel(qT_ref, k_ref, vT_ref, eye_ref, *rest, mode, past, tq, tk, s_valid):
    o_ref, sa_ref, sb_ref, acc_ref = rest[-4:]
    kn_ref = rest[-5]
    qi = pl.program_id(1)
    q_lo = past + qi * tq
    n_pairs = past // tq + qi
    first_pair = _first_needed_pair(*rest[:3], n_pairs) if mode == 'fox' else 0
    qpos = q_lo + lax.broadcasted_iota(jnp.int32, (1, tq), 1)
    feature_rows = HEAD_DIM if mode == 'fox' else HP

    def qk(kb, dst_ref):
        off = pl.multiple_of(kb * tk, tk)
        for h in range(N_HEADS):
            dst_ref[h] = _dot(k_ref[0, pl.ds(off, tk), _head_rows(h)], qT_ref[0, _head_rows(h), :])

    def softmax_pv(kb, src_ref, stats, masked):
        if masked:
            kpos = kb * tk + lax.broadcasted_iota(jnp.int32, (tk, 1), 0)
            if mode == 'fox':
                vis = kpos <= qpos
            else:
                vis = (kpos // CHUNK) <= (qpos // CHUNK)
                if s_valid < k_ref.shape[1]:
                    vis = jnp.logical_and(vis, kpos < s_valid)
        out = []
        for h in range(N_HEADS):
            m, l = stats[h]
            s = src_ref[h]
            if masked:
                s = jnp.where(vis, s, NEG_INF)
            m_new = jnp.maximum(m, jnp.max(s, axis=0, keepdims=True))
            p = jnp.exp2(s - m_new)
            alpha = jnp.exp2(m - m_new)
            l = alpha * l + jnp.sum(p, axis=0, keepdims=True)
            rows = _value_rows(h)
            acc_ref[rows, :] = alpha * acc_ref[rows, :] + _dot(vT_ref[0, kb, rows, :], p.astype(BF16))
            out.append((m_new, l))
        return tuple(out)

    def fixed_pv(kb, src_ref, m_fix, sums):
        out = []
        for h in range(N_HEADS):
            p = jnp.exp2(src_ref[h] - m_fix[h])
            rows = _value_rows(h)
            acc_ref[rows, :] = acc_ref[rows, :] + _dot(vT_ref[0, kb, rows, :], p.astype(BF16))
            out.append(sums[h] + jnp.sum(p, axis=0, keepdims=True))
        return tuple(out)

    def pairs(step, carry):
        def body(j, carry):
            kb = 2 * j
            qk(kb + 1, sb_ref)
            carry = step(kb, sa_ref, carry)
            qk(kb + 2, sa_ref)
            return step(kb + 1, sb_ref, carry)
        return lax.fori_loop(first_pair, n_pairs, body, carry)

    acc_ref[...] = jnp.zeros_like(acc_ref)
    own = 2 * n_pairs
    qk(own, sa_ref)
    qk(own + 1, sb_ref)
    init = (jnp.full((1, tq), NEG_INF, F32), jnp.zeros((1, tq), F32))
    stats = softmax_pv(own, sa_ref, (init,) * N_HEADS, True)
    qk(2 * first_pair, sa_ref)
    stats = softmax_pv(own + 1, sb_ref, stats, True)

    gap = None
    for h in range(N_HEADS):
        q_h = qT_ref[0, h * HP:h * HP + feature_rows, :].astype(F32)
        qn2 = jnp.sum(q_h * q_h, axis=0, keepdims=True)
        g = jnp.sqrt(qn2 * kn_ref[0][:, h:h + 1]) * 1.01 + 0.5 - stats[h][0]
        gap = g if gap is None else jnp.maximum(gap, g)
    frozen_ok = jnp.max(gap) <= FIXED_STABILISER_LIMIT

    def run_frozen(stats):
        m_fix = [st[0] for st in stats]
        sums = pairs(lambda kb, src, sums: fixed_pv(kb, src, m_fix, sums), tuple(st[1] for st in stats))
        return tuple((m_fix[h], sums[h]) for h in range(N_HEADS))

    def run_online(stats):
        return pairs(lambda kb, src, st: softmax_pv(kb, src, st, False), stats)

    stats = lax.cond(frozen_ok, run_frozen, run_online, stats)
    _finish([acc_ref[_value_rows(h), :] for h in range(N_HEADS)], [1.0 / st[1] for st in stats], eye_ref, o_ref)


def _stick_attn_kernel(qT_ref, k_ref, vT_ref, eye_ref, later_ref, o_ref, *, past, tq, tk):
    assert tk % tq == 0 and past % tq == 0
    qi = pl.program_id(1)
    ltk = int(math.log2(tk))
    q_lo = past + qi * tq
    own = q_lo >> ltk
    qpos = q_lo + lax.broadcasted_iota(jnp.int32, (1, tq), 1)

    def block_terms(kb, masked):
        kvs = [_load_kv(k_ref, vT_ref, kb, tk, h) for h in range(N_HEADS)]
        zs = [_dot(kvs[h][0], qT_ref[0, _head_rows(h), :]) for h in range(N_HEADS)]
        if masked:
            kpos = kb * tk + lax.broadcasted_iota(jnp.int32, (tk, 1), 0)
            vis = kpos < qpos
        out = []
        for h in range(N_HEADS):
            z = zs[h]
            drop = jnp.where(z > SOFTPLUS_LINEAR_FROM, z, jnp.log2(1.0 + jnp.exp2(z)))
            if masked:
                drop = jnp.where(vis, drop, 0.0)
            later = _dot(later_ref[...], drop.astype(BF16))
            logw = z - drop - later
            if masked:
                logw = jnp.where(vis, logw, NEG_INF)
            out.append((logw, jnp.sum(drop, axis=0, keepdims=True), kvs[h][1]))
        return out

    def accumulate(terms, carry, live=None):
        out = []
        for h in range(N_HEADS):
            logw, total, v_t = terms[h]
            decay, acc = carry[h]
            a = jnp.exp2(logw - decay)
            if live is not None:
                a = jnp.where(live, a, 0.0)
                total = jnp.where(live, total, 0.0)
            out.append((decay + total, acc + _dot(v_t, a.astype(BF16))))
        return tuple(out)

    def min_decay(carry):
        d = carry[0][0]
        for h in range(1, N_HEADS):
            d = jnp.minimum(d, carry[h][0])
        return jnp.min(d)

    init = (jnp.zeros((1, tq), F32), jnp.zeros((HEAD_DIM, tq), F32))
    own_terms = block_terms(own, True)
    prev_terms = block_terms(jnp.maximum(own - 1, 0), False)
    carry = accumulate(own_terms, (init,) * N_HEADS)
    carry = accumulate(prev_terms, carry, live=own >= 1)

    def cond(state):
        kb, dmin, _ = state
        return jnp.logical_and(kb >= 0, dmin < SB_DECAY_LIMIT)

    def body(state):
        kb, _, carry = state
        carry = accumulate(block_terms(kb, False), carry)
        return kb - 1, min_decay(carry), carry

    _, _, carry = lax.while_loop(cond, body, (own - 2, min_decay(carry), carry))
    _finish([c[1] for c in carry], [1.0] * N_HEADS, eye_ref, o_ref)


def _attention(mode, q_t, k, v_t, cst, *, past, s_valid, tq, tk, skip_stats=None):
    b, _, t = q_t.shape
    s = k.shape[1]
    grid = (b, t // tq)
    single = pl.Buffered(1)
    in_specs = [pl.BlockSpec((1, QW, tq), lambda i, j: (i, 0, j)),
                pl.BlockSpec((1, s, QW), lambda i, j: (i, 0, 0), pipeline_mode=single),
                pl.BlockSpec((1, s // tk, VW, tk), lambda i, j: (i, 0, 0, 0), pipeline_mode=single),
                _const_spec((tq, tq))]
    args = [q_t, k, v_t, jnp.eye(tq, dtype=BF16)]
    scratch = []
    if mode == 'sb':
        kern = functools.partial(_stick_attn_kernel, past=past, tq=tq, tk=tk)
        in_specs.append(_const_spec((tk, tk)))
        args.append(cst['later'][tk])
    elif tq == 2 * tk and past % tq == 0:
        kern = functools.partial(_softmax_attn_pipelined_kernel, mode=mode, past=past, tq=tq, tk=tk, s_valid=s_valid)
        scratch = [pltpu.VMEM((N_HEADS, tk, tq), F32), pltpu.VMEM((N_HEADS, tk, tq), F32), pltpu.VMEM((VW, tq), F32)]
        stat_spec = pl.BlockSpec((1, 1, LANE), lambda i, j: (i, 0, 0))
        if mode == 'fox':
            cend, qn, kn = skip_stats
            in_specs += [pl.BlockSpec((1, s // tk, LANE), lambda i, j: (i, 0, 0)), stat_spec, stat_spec]
            args += [cend.reshape(b, s // tk, LANE), qn, kn]
        else:
            in_specs.append(stat_spec)
            args.append(skip_stats[0])
    else:
        kern = functools.partial(_softmax_attn_kernel, mode=mode, past=past, tq=tq, tk=tk, s_valid=s_valid)
    width = (N_HEADS // 2) * HP
    return pl.pallas_call(
        kern, grid=grid, in_specs=in_specs,
        out_specs=pl.BlockSpec((1, tq, width), lambda i, j: (i, j, 0)),
        out_shape=jax.ShapeDtypeStruct((b, t, width), BF16), scratch_shapes=scratch,
        compiler_params=_params("parallel", "parallel"), name="attn_" + mode,
    )(*args)


def _lru_kernel(lx_ref, lg_ref, cb_ref, h0_ref, cw_ref, cbias_ref, wr_ref, br_ref, wi_ref, bi_ref, lam_ref,
                y_ref, nb_ref, hl_ref, xcat_ref, h_ref, *, past, tc):
    ci = pl.program_id(1)
    keep = LRU_CONV - 1

    @pl.when(ci == 0)
    def _():
        xcat_ref[0:8, :] = jnp.zeros((8, LRU_WIDTH), F32)
        xcat_ref[8 - keep:8, :] = cb_ref[0]
        h_ref[...] = h0_ref[0]

    x = lx_ref[0]
    xcat_ref[8:8 + tc, :] = x
    xc = cbias_ref[...] + x * cw_ref[keep:keep + 1, :]
    for tap in range(keep):
        shift = keep - tap
        xc = xc + xcat_ref[8 - shift:8 - shift + tc, :] * cw_ref[tap:tap + 1, :]
    nb_ref[0] = xcat_ref[8 + tc - keep:8 + tc, :]
    xcat_ref[0:8, :] = x[tc - 8:tc, :]

    xcb = xc.astype(BF16)
    r = _sigmoid(_dot(xcb, wr_ref[...]) + br_ref[...])
    gate_in = _sigmoid(_dot(xcb, wi_ref[...]) + bi_ref[...])
    log_a = (-LRU_C) * r * _softplus(-lam_ref[...])
    row = lax.broadcasted_iota(jnp.int32, (tc, LRU_WIDTH), 0)
    reset = (past + ci * tc + row) == 0
    a = jnp.where(reset, 0.0, jnp.exp(log_a))
    y2 = 2.0 * log_a
    series = -y2 * (1.0 + y2 * (0.5 + y2 * (1.0 / 6.0 + y2 * (1.0 / 24.0 + y2 * (1.0 / 120.0)))))
    one_minus = jnp.where(y2 > -0.05, series, 1.0 - jnp.exp(y2))
    mult = jnp.where(reset, 1.0, jnp.sqrt(one_minus))
    u = mult * gate_in * xc

    d = 1
    while d < tc:
        if d < 8:
            ok = row >= d
            u = u + jnp.where(ok, a * pltpu.roll(u, d, 0), 0.0)
            a = jnp.where(ok, a * pltpu.roll(a, d, 0), a)
        else:
            u = jnp.concatenate([u[:d], u[d:] + a[d:] * u[:-d]], axis=0)
            a = jnp.concatenate([a[:d], a[d:] * a[:-d]], axis=0)
        d *= 2
    hs = a * h_ref[...] + u
    h_last = hs[tc - 1:tc, :]
    h_ref[...] = h_last
    hl_ref[0] = h_last

    g = lg_ref[0]
    gelu = 0.5 * g * (1.0 + jnp.tanh(0.7978845608028654 * (g + 0.044715 * g * g * g)))
    y_ref[0] = (hs * gelu).astype(BF16)


def _lru(lx, lg, conv_buf, h0, lw, past):
    b, t, w = lx.shape
    tc = _pick(t, (256, 128, 64, 32))
    grid = (b, t // tc)
    row = pl.BlockSpec((1, tc, w), lambda i, j: (i, j, 0))
    keep = LRU_CONV - 1
    in_specs = [row, row, pl.BlockSpec((1, keep, w), lambda i, j: (i, 0, 0)), pl.BlockSpec((1, 1, w), lambda i, j: (i, 0, 0)),
                _const_spec((LRU_CONV, w)), _const_spec((1, w)), _const_spec((w, w)), _const_spec((1, w)),
                _const_spec((w, w)), _const_spec((1, w)), _const_spec((1, w))]
    sds = jax.ShapeDtypeStruct
    return pl.pallas_call(
        functools.partial(_lru_kernel, past=past, tc=tc), grid=grid, in_specs=in_specs,
        out_specs=[row, pl.BlockSpec((1, keep, w), lambda i, j: (i, 0, 0)), pl.BlockSpec((1, 1, w), lambda i, j: (i, 0, 0))],
        out_shape=[sds((b, t, w), BF16), sds((b, keep, w), F32), sds((b, 1, w), F32)],
        scratch_shapes=[pltpu.VMEM((tc + 8, w), F32), pltpu.VMEM((1, w), F32)],
        compiler_params=_params("parallel", "arbitrary"), name="rg_lru",
    )(lx, lg, conv_buf, h0, lw['lru_conv_w'], lw['lru_conv_b'], lw['lru_wr'], lw['lru_br'], lw['lru_wi'],
      lw['lru_bi'], lw['lru_lam'])


def _row_chains(tm):
    n = 2 if tm % 512 == 0 else 1
    return [slice(i * (tm // n), (i + 1) * (tm // n)) for i in range(n)]


def _merge_kernel(x_ref, oa_ref, ob_ref, oc_ref, od_ref, g1_ref, g2_ref, wg_ref, wb_ref, wo_ref, out_ref):
    d = x_ref.shape[2]
    for rows in _row_chains(x_ref.shape[1]):
        x = x_ref[0, rows, :]
        hb = _rms(x, g1_ref[...]).astype(BF16)
        merged = None
        for n, o_ref in enumerate((oa_ref, ob_ref, oc_ref, od_ref)):
            gate = _sigmoid(_dot(hb, wg_ref[:, n * d:(n + 1) * d]))
            term = gate * _dot(o_ref[0, rows, :], wb_ref[n])
            merged = term if merged is None else merged + term
        y = _dot(merged.astype(BF16), wo_ref[...])
        out_ref[0, rows, :] = x + _rms(y, g2_ref[...])


def _merge(x, o_a, o_b, o_c, o_d, lw):
    b, t, d = x.shape
    tm = _pick(t, (512, 256, 128))
    grid = (b, t // tm)
    row = lambda w: pl.BlockSpec((1, tm, w), lambda i, j: (i, j, 0))
    bw = BRANCH_WIDTH
    single = pl.Buffered(1)
    in_specs = [row(d), row(bw), row(bw), row(bw), row(bw), _const_spec((1, d)), _const_spec((1, d)),
                pl.BlockSpec((d, N_BRANCH * d), lambda i, j: (0, 0), pipeline_mode=single),
                pl.BlockSpec((N_BRANCH, BRANCH_WIDTH, d), lambda i, j: (0, 0, 0), pipeline_mode=single),
                pl.BlockSpec((d, d), lambda i, j: (0, 0), pipeline_mode=single)]
    return pl.pallas_call(
        _merge_kernel, grid=grid, in_specs=in_specs, out_specs=row(d),
        out_shape=jax.ShapeDtypeStruct((b, t, d), F32),
        compiler_params=_params("parallel", "parallel"), name="merge",
    )(x, o_a, o_b, o_c, o_d, lw['ln_mix_pre'], lw['ln_mix_post'], lw['w_gate'], lw['w_branch'], lw['w_out'])


def _mem_kv_kernel(mem_ref, g_ref, wk_ref, wv_ref, k_ref, v_ref):
    mn = _rms(mem_ref[0], g_ref[...]).astype(BF16)
    k_ref[0] = _dot(mn, wk_ref[...])
    v_ref[0] = _dot(mn, wv_ref[...])


def _mem_kv(mem, lw):
    b, m, d = mem.shape
    w = lw['mem_wk'].shape[1]
    out = jax.ShapeDtypeStruct((b, m, w), F32)
    blk = pl.BlockSpec((1, m, w), lambda i: (i, 0, 0))
    return pl.pallas_call(
        _mem_kv_kernel, grid=(b,),
        in_specs=[pl.BlockSpec((1, m, d), lambda i: (i, 0, 0)), _const_spec((1, d)), _const_spec((d, w)), _const_spec((d, w))],
        out_specs=[blk, blk], out_shape=[out, out], compiler_params=_params("parallel"), name="mem_kv",
    )(mem, lw['mem_norm'], lw['mem_wk'], lw['mem_wv'])


def _mem_attn_kernel(x_ref, mk_ref, mv_ref, g1_ref, g2_ref, wq_ref, wo_ref, out_ref):
    x = x_ref[0]
    hb = _rms(x, g1_ref[...]).astype(BF16)
    q = (_dot(hb, wq_ref[...]) * (MEM_HEAD_DIM ** -0.5)).astype(BF16)
    heads = []
    for h in range(N_HEADS):
        sl = slice(h * MEM_HEAD_DIM, (h + 1) * MEM_HEAD_DIM)
        s = _dot_nt(q[:, sl], mk_ref[0, :, sl])
        p = jnp.exp(s - jnp.max(s, axis=-1, keepdims=True))
        inv = 1.0 / jnp.sum(p, axis=-1, keepdims=True)
        heads.append((_dot(p.astype(BF16), mv_ref[0, :, sl]) * inv).astype(BF16))
    y = _dot(jnp.concatenate(heads, axis=1), wo_ref[...])
    out_ref[0] = x + _rms(y, g2_ref[...])


def _mem_attn(x, mk, mv, lw):
    b, t, d = x.shape
    m, w = mk.shape[1], mk.shape[2]
    tm = _pick(t, (512, 256, 128))
    row = pl.BlockSpec((1, tm, d), lambda i, j: (i, j, 0))
    kv = pl.BlockSpec((1, m, w), lambda i, j: (i, 0, 0))
    return pl.pallas_call(
        _mem_attn_kernel, grid=(b, t // tm),
        in_specs=[row, kv, kv, _const_spec((1, d)), _const_spec((1, d)), _const_spec((d, w)), _const_spec((w, d))],
        out_specs=row, out_shape=jax.ShapeDtypeStruct((b, t, d), F32),
        compiler_params=_params("parallel", "parallel"), name="mem_attn",
    )(x, mk, mv, lw['ln_mem_pre'], lw['ln_mem_post'], lw['mem_wq'], lw['mem_wo'])


def _ffn_kernel(x_ref, g1_ref, g2_ref, wg_ref, wu_ref, wd_ref, out_ref):
    for rows in _row_chains(x_ref.shape[1]):
        x = x_ref[0, rows, :]
        hb = _rms(x, g1_ref[...]).astype(BF16)
        gate = _dot(hb, wg_ref[...])
        act = (gate * _sigmoid(gate) * _dot(hb, wu_ref[...])).astype(BF16)
        y = _dot(act, wd_ref[...])
        out_ref[0, rows, :] = x + _rms(y, g2_ref[...])


def _ffn(x, lw):
    b, t, d = x.shape
    f = lw['ffn_wg'].shape[1]
    tm = _pick(t, (512, 256, 128))
    row = pl.BlockSpec((1, tm, d), lambda i, j: (i, j, 0))
    single = pl.Buffered(1)
    return pl.pallas_call(
        _ffn_kernel, grid=(b, t // tm),
        in_specs=[row, _const_spec((1, d)), _const_spec((1, d)),
                  pl.BlockSpec((d, f), lambda i, j: (0, 0), pipeline_mode=single),
                  pl.BlockSpec((d, f), lambda i, j: (0, 0), pipeline_mode=single),
                  pl.BlockSpec((f, d), lambda i, j: (0, 0), pipeline_mode=single)],
        out_specs=row, out_shape=jax.ShapeDtypeStruct((b, t, d), F32),
        compiler_params=_params("parallel", "parallel"), name="ffn",
    )(x, lw['ln_ffn_pre'], lw['ln_ffn_post'], lw['ffn_wg'], lw['ffn_wu'], lw['ffn_wd'])


def _constants():
    fk, fq, ones_k, ones_q = _fox_feature_maps()
    later = {}
    for tk in (256,):
        idx = np.arange(tk)
        later[tk] = jnp.asarray((idx[None, :] > idx[:, None]).astype(np.float32), BF16)
    return dict(
        ek=jnp.asarray(_place_qk(), BF16), evT=jnp.eye(VW, dtype=BF16),
        fkmap=jnp.asarray(fk, BF16), fqmapT=jnp.asarray(fq.T, BF16),
        ones_k=jnp.asarray(ones_k, F32), ones_qT=jnp.asarray(ones_q.T, F32),
        ekpe=jnp.asarray(_place_kpe(), BF16), later=later,
        seg=jnp.asarray(np.repeat(np.eye(N_HEADS, LANE, dtype=np.float32), HEAD_DIM, axis=0), BF16),
        seg_slot=jnp.asarray(np.repeat(np.eye(N_HEADS, LANE, dtype=np.float32), HP, axis=0), BF16))


def _pad_cols(w, n):
    return jnp.pad(w, ((0, 0), (0, n - w.shape[1])))


_MIX_SIZES = (256, 256, 256, N_HEADS, LRU_WIDTH, LRU_WIDTH, MLA_Q_RANK, MLA_KV_RANK, MLA_ROPE, 256, 256, 256)


def _w_in_plan(d_model):
    offs = np.concatenate([[0], np.cumsum(_MIX_SIZES)])
    fq, fk, fv, ff, lx, lg, cq, ckv, kpe, sq, sk, sv = [np.arange(offs[i], offs[i + 1]) for i in range(len(_MIX_SIZES))]
    half = MLA_ROPE // 2

    def padded(cols):
        return np.concatenate([cols, np.full(LANE - len(cols), -1)])

    mix = [fq, fk, fv, lx, lg, cq, ckv, padded(kpe), padded(np.concatenate([kpe[half:], kpe[:half]])), padded(ff),
           sq, sk, sv]
    pieces, col = [], 0
    for src in mix:
        pieces.append((0, col, src))
        col += len(src)
    assert col == _W_MIX_COLS
    gate0 = offs[-1]
    for i in range(N_BRANCH * d_model // 256):
        pieces.append((1, i * 256, gate0 + i * 256 + np.arange(256)))
    return pieces


def _transpose_kernel(xT_ref, eye_ref, x_ref):
    x_ref[...] = _dot_nt(eye_ref[...], xT_ref[...]).astype(x_ref.dtype)


def _transpose_bf16(x_t):
    r, c = x_t.shape
    rb = _pick(r, (256, 128))
    return pl.pallas_call(
        _transpose_kernel, grid=(r // rb,),
        in_specs=[pl.BlockSpec((rb, c), lambda i: (i, 0)), _const_spec((c, c))],
        out_specs=pl.BlockSpec((c, rb), lambda i: (0, i)), out_shape=jax.ShapeDtypeStruct((c, r), BF16),
        compiler_params=_params("parallel"), name="transpose_w",
    )(x_t, jnp.eye(c, dtype=BF16))


def _repack_w_in(w_in_t):
    d_model = w_in_t.shape[1]
    rows = [[], []]
    for dst, _, src in _w_in_plan(d_model):
        valid = src[src >= 0]
        runs = np.split(valid, np.nonzero(np.diff(valid) != 1)[0] + 1)
        piece = jnp.concatenate([w_in_t[int(r[0]):int(r[-1]) + 1] for r in runs], axis=0)
        rows[dst].append(jnp.pad(piece, ((0, len(src) - len(valid)), (0, 0))))
    return tuple(_transpose_bf16(jnp.concatenate(r, axis=0).astype(BF16)) for r in rows)


def _layer_weights(l, p):
    w_mix, w_gate = _repack_w_in(jnp.transpose(p['w_in'], (2, 0, 1))[:, l, :])
    half = MLA_ROPE // 2
    qk = MLA_NOPE + MLA_ROPE
    wuq = p['mla_w_uq'][l].reshape(MLA_Q_RANK, N_HEADS, qk)
    rope = wuq[:, :, MLA_NOPE:]
    rope_sw = jnp.concatenate([rope[:, :, half:], rope[:, :, :half]], axis=2)
    wuq_p = jnp.pad(wuq, ((0, 0), (0, 0), (0, HP - qk))).reshape(MLA_Q_RANK, QW)
    wuqsw_p = jnp.pad(rope_sw, ((0, 0), (0, 0), (MLA_NOPE, HP - qk))).reshape(MLA_Q_RANK, QW)
    wuk = p['mla_w_uk'][l].reshape(MLA_KV_RANK, N_HEADS, MLA_NOPE)
    wuk_p = jnp.pad(wuk, ((0, 0), (0, 0), (0, HP - MLA_NOPE))).reshape(MLA_KV_RANK, QW)

    def block_diag(w):
        return jax.scipy.linalg.block_diag(*[w[i] for i in range(w.shape[0])]).astype(BF16)

    row = lambda v: v[l].reshape(1, -1).astype(F32)
    return dict(
        w_mix=w_mix, w_gate=w_gate,
        ln_mix_pre=row(p['ln_mix_pre']), ln_mix_post=row(p['ln_mix_post']),
        fox_bf=_pad_cols(row(p['fox_bf']), LANE),
        mla_q_norm=row(p['mla_q_norm']), mla_kv_norm=row(p['mla_kv_norm']),
        wuqT=wuq_p.T.astype(BF16), wuqswT=wuqsw_p.T.astype(BF16), wuk=wuk_p.astype(BF16), wuvT=p['mla_w_uv'][l].T.astype(BF16),
        eqT=jnp.asarray(_place_qk().T, BF16),
        lru_conv_w=p['lru_conv_w'][l].astype(F32), lru_conv_b=row(p['lru_conv_b']),
        lru_wr=block_diag(p['lru_wr'][l]), lru_br=row(p['lru_br']),
        lru_wi=block_diag(p['lru_wi'][l]), lru_bi=row(p['lru_bi']), lru_lam=row(p['lru_lam']),
        w_branch=p['w_branch'][l].astype(BF16), w_out=p['w_out'][l].astype(BF16),
        ln_mem_pre=row(p['ln_mem_pre']), ln_mem_post=row(p['ln_mem_post']), mem_norm=row(p['mem_norm']),
        mem_wq=p['mem_wq'][l].astype(BF16), mem_wk=p['mem_wk'][l].astype(BF16),
        mem_wv=p['mem_wv'][l].astype(BF16), mem_wo=p['mem_wo'][l].astype(BF16),
        ln_ffn_pre=row(p['ln_ffn_pre']), ln_ffn_post=row(p['ln_ffn_post']),
        ffn_wg=p['ffn_wg'][l].astype(BF16), ffn_wu=p['ffn_wu'][l].astype(BF16), ffn_wd=p['ffn_wd'][l].astype(BF16))


def _rope_tables(past, t):
    half = MLA_ROPE // 2
    inv = jnp.power(ROPE_BASE, -jnp.arange(half, dtype=F32) / half)
    ang = (past + jnp.arange(t, dtype=jnp.int32)).astype(F32)[:, None] * inv
    cos, sin = jnp.cos(ang), jnp.sin(ang)
    c32 = jnp.concatenate([cos, cos], axis=1)
    s32 = jnp.concatenate([-sin, sin], axis=1)
    scale = (MLA_NOPE + MLA_ROPE) ** -0.5 * LOG2E
    slot_c = jnp.concatenate([jnp.ones((t, MLA_NOPE), F32), c32, jnp.zeros((t, HP - MLA_NOPE - MLA_ROPE), F32)], axis=1)
    slot_s = jnp.concatenate([jnp.zeros((t, MLA_NOPE), F32), s32, jnp.zeros((t, HP - MLA_NOPE - MLA_ROPE), F32)], axis=1)
    return dict(ck=_pad_cols(c32, LANE), sk=_pad_cols(s32, LANE),
                cqT=(jnp.tile(slot_c, (1, N_HEADS)) * scale).T, sqT=(jnp.tile(slot_s, (1, N_HEADS)) * scale).T)


def _trunk_layer(x, past_state, mem_k, mem_v, lw, cst, layer, depth, earlier):
    b, t, _ = x.shape
    past = 0 if past_state is None else past_state['fox_k'].shape[1]
    tk = 256
    fq, fk, fv, lx, lg, ckv_n, kpe_slab, logf_slab, mla_qT, sb_qT, sk, sv = _in_proj(
        x, lw, _rope_tables(past, t), layer, depth, earlier)
    stacked = (fk, fv, ckv_n, kpe_slab, logf_slab, sk, sv)

    if past_state is None:
        s_valid = t
        keys = (fq, fk, fv, logf_slab, ckv_n, kpe_slab, sk, sv)
        key_layer = layer
        conv_buf = jnp.zeros((b, LRU_CONV - 1, LRU_WIDTH), F32)
        h0 = jnp.zeros((b, 1, LRU_WIDTH), F32)
    else:
        s_valid = past + t
        s_pad = -(-s_valid // tk) * tk

        def cat(old, new):
            old = old.reshape(b, past, -1).astype(new.dtype)
            old = jnp.pad(old, ((0, 0), (0, 0), (0, new.shape[2] - old.shape[2])))
            return jnp.pad(jnp.concatenate([old, new], axis=1), ((0, 0), (0, s_pad - s_valid), (0, 0)))

        def cat_state(old, new):
            return cat(old, new[layer])[None]

        keys = (cat(jnp.zeros((b, past, 256), BF16), fq), cat_state(past_state['fox_k'], fk),
                cat_state(past_state['fox_v'], fv), cat_state(past_state['fox_logf'], logf_slab),
                cat_state(past_state['mla_ckv'], ckv_n), cat_state(past_state['mla_kpe'], kpe_slab),
                cat_state(past_state['sb_k'], sk), cat_state(past_state['sb_v'], sv))
        key_layer = 0
        conv_buf = past_state['lru_conv'].astype(F32)
        h0 = past_state['lru_h'].reshape(b, 1, LRU_WIDTH).astype(F32)

    (fox_qT, fox_k, fox_vT, mla_k, mla_vT, sb_k, sb_vT,
     *fox_skip_stats, mla_key_norm) = _kv_prep(*keys, key_layer, lw, cst, tk)
    if past_state is not None:
        fox_qT = fox_qT[:, :, past:past + t]

    att = functools.partial(_attention, cst=cst, past=past, s_valid=s_valid, tk=tk)
    tq_softmax = _pick(t, (2 * tk, tk, 128))
    o_a = att('fox', fox_qT, fox_k, fox_vT, tq=tq_softmax, skip_stats=fox_skip_stats)
    o_c = att('mla', mla_qT, mla_k, mla_vT, tq=tq_softmax, skip_stats=[mla_key_norm])
    o_d = att('sb', sb_qT, sb_k, sb_vT, tq=_pick(t, (tk, 128)))
    o_b, lru_conv, lru_h = _lru(lx, lg, conv_buf, h0, lw, past)

    x = _merge(x, o_a, o_b, o_c, o_d, lw)
    x = _mem_attn(x, mem_k, mem_v, lw)
    x = _ffn(x, lw)

    return x, stacked, dict(lru_h=lru_h.reshape(b, LRU_WIDTH), lru_conv=lru_conv)


def _state_outputs(stacked, small):
    fk, fv, ckv_n, kpe_slab, logf_slab, sk, sv = stacked
    depth, b, t, _ = fk.shape
    heads = lambda a: a.reshape(depth, b, t, N_HEADS, HEAD_DIM)
    stk = lambda name: jnp.stack([s[name] for s in small])
    return (heads(fk), heads(fv), logf_slab[..., :N_HEADS], stk('lru_h'), stk('lru_conv'),
            ckv_n, kpe_slab[..., :MLA_ROPE], heads(sk), heads(sv))


def kernel(x_prompt, x_sample, cache_fox_k, cache_fox_v, cache_fox_logf, state_lru_h, state_lru_conv, cache_mla_ckv, cache_mla_kpe, cache_sb_k, cache_sb_v, cache_mem_k, cache_mem_v, mem_prompt, ln_mix_pre, ln_mix_post, w_in, fox_bf, lru_conv_w, lru_conv_b, lru_wr, lru_br, lru_wi, lru_bi, lru_lam, mla_q_norm, mla_w_uq, mla_kv_norm, mla_w_uk, mla_w_uv, w_branch, w_out, ln_mem_pre, ln_mem_post, mem_norm, mem_wq, mem_wk, mem_wv, mem_wo, ln_ffn_pre, ln_ffn_post, ffn_wg, ffn_wu, ffn_wd):
    params = dict(ln_mix_pre=ln_mix_pre, ln_mix_post=ln_mix_post, w_in=w_in, fox_bf=fox_bf, lru_conv_w=lru_conv_w,
                  lru_conv_b=lru_conv_b, lru_wr=lru_wr, lru_br=lru_br, lru_wi=lru_wi, lru_bi=lru_bi, lru_lam=lru_lam,
                  mla_q_norm=mla_q_norm, mla_w_uq=mla_w_uq, mla_kv_norm=mla_kv_norm, mla_w_uk=mla_w_uk,
                  mla_w_uv=mla_w_uv, w_branch=w_branch, w_out=w_out, ln_mem_pre=ln_mem_pre, ln_mem_post=ln_mem_post,
                  mem_norm=mem_norm, mem_wq=mem_wq, mem_wk=mem_wk, mem_wv=mem_wv, mem_wo=mem_wo,
                  ln_ffn_pre=ln_ffn_pre, ln_ffn_post=ln_ffn_post, ffn_wg=ffn_wg, ffn_wu=ffn_wu, ffn_wd=ffn_wd)
    depth = w_in.shape[0]
    cst = _constants()
    weights = [_layer_weights(l, params) for l in range(depth)]
    bp, mem_len = mem_prompt.shape[0], mem_prompt.shape[1]

    y_prompt, p_rows, p_small, p_mem = x_prompt, None, [], []
    for l in range(depth):
        mk, mv = _mem_kv(mem_prompt, weights[l])
        y_prompt, p_rows, small = _trunk_layer(y_prompt, None, mk.astype(BF16), mv.astype(BF16), weights[l], cst,
                                               l, depth, p_rows)
        p_small.append(small)
        p_mem.append((mk.reshape(bp, mem_len, N_HEADS, MEM_HEAD_DIM), mv.reshape(bp, mem_len, N_HEADS, MEM_HEAD_DIM)))

    y_sample, s_rows, s_small = x_sample, None, []
    bs = x_sample.shape[0]
    for l in range(depth):
        past = dict(fox_k=cache_fox_k[l], fox_v=cache_fox_v[l], fox_logf=cache_fox_logf[l], lru_h=state_lru_h[l],
                    lru_conv=state_lru_conv[l], mla_ckv=cache_mla_ckv[l], mla_kpe=cache_mla_kpe[l],
                    sb_k=cache_sb_k[l], sb_v=cache_sb_v[l])
        mk = cache_mem_k[l].reshape(bs, mem_len, -1).astype(BF16)
        mv = cache_mem_v[l].reshape(bs, mem_len, -1).astype(BF16)
        y_sample, s_rows, small = _trunk_layer(y_sample, past, mk, mv, weights[l], cst, l, depth, s_rows)
        s_small.append(small)

    mem_out = (jnp.stack([m[0] for m in p_mem]), jnp.stack([m[1] for m in p_mem]))
    return (y_prompt, y_sample) + _state_outputs(p_rows, p_small) + mem_out + _state_outputs(s_rows, s_small)
```

```python
import functools
import math

import numpy as np
import jax
import jax.numpy as jnp
from jax import lax
from jax.experimental import pallas as pl
from jax.experimental.pallas import tpu as pltpu

F32 = jnp.float32
BF16 = jnp.bfloat16

CHUNK = 64
HEAD_DIM = 64
N_HEADS = 4
LRU_WIDTH = 256
LRU_CONV = 4
LRU_C = 8.0
MLA_Q_RANK = 256
MLA_KV_RANK = 128
MLA_NOPE = 64
MLA_ROPE = 32
MLA_V = 64
ROPE_BASE = 10000.0
N_BRANCH = 4
BRANCH_WIDTH = 256
MEM_HEAD_DIM = 128
EPS = 1e-6
NEG_INF = -1e30

HP = 128
QW = N_HEADS * HP
VW = N_HEADS * HEAD_DIM
LANE = 128
VMEM_LIMIT_BYTES = 56 * 1024 * 1024
LOG2E = 1.4426950408889634
SB_DECAY_LIMIT = 127.0
FOX_SKIP_LIMIT = 150.0
FIXED_STABILISER_LIMIT = 60.0
SOFTPLUS_LINEAR_FROM = 60.0

_NT = (((1,), (1,)), ((), ()))


def _dot(a, b):
    return jnp.dot(a, b, preferred_element_type=F32)


def _dot_nt(a, b):
    return lax.dot_general(a, b, _NT, preferred_element_type=F32)


def _rms(x, g):
    ms = jnp.mean(x * x, axis=-1, keepdims=True)
    return x * lax.rsqrt(ms + EPS) * g


def _sigmoid(x):
    return 1.0 / (1.0 + jnp.exp(-x))


def _softplus(x):
    return jnp.maximum(x, 0.0) + jnp.log(1.0 + jnp.exp(-jnp.abs(x)))


def _pick(n, cands):
    for c in cands:
        if n % c == 0:
            return c
    return n


def _params(*sem):
    return pltpu.CompilerParams(dimension_semantics=sem, vmem_limit_bytes=VMEM_LIMIT_BYTES)


def _const_spec(shape):
    nd = len(shape)
    return pl.BlockSpec(shape, lambda *_: (0,) * nd)


def _place_qk():
    e = np.zeros((N_HEADS * HEAD_DIM, QW), np.float32)
    for h in range(N_HEADS):
        for j in range(HEAD_DIM):
            e[h * HEAD_DIM + j, h * HP + j] = 1.0
    return e


def _fox_feature_maps():
    fk = np.zeros((LANE, QW), np.float32)
    fq = np.zeros((LANE, QW), np.float32)
    ones_k = np.zeros((1, QW), np.float32)
    ones_q = np.zeros((1, QW), np.float32)
    for h in range(N_HEADS):
        for part in range(3):
            fq[part * N_HEADS + h, h * HP + HEAD_DIM + part] = 1.0
            fk[part * N_HEADS + h, h * HP + HEAD_DIM + 3 + part] = -1.0
            ones_k[0, h * HP + HEAD_DIM + part] = 1.0
            ones_q[0, h * HP + HEAD_DIM + 3 + part] = 1.0
    return fk, fq, ones_k, ones_q


def _place_kpe():
    e = np.zeros((LANE, QW), np.float32)
    for h in range(N_HEADS):
        for j in range(MLA_ROPE):
            e[j, h * HP + MLA_NOPE + j] = 1.0
    return e


_W_OFF = dict(fq=0, fk=256, fv=512, lx=768, lg=1024, cq=1280, ckv=1536, kpe=1664, kpe_sw=1792, ff=1920,
              sq=2048, sk=2304, sv=2560)
_W_MIX_COLS = 2816


_N_IN_PROJ_INPUTS = 13
_IN_PROJ_STATE_OUTPUTS = (1, 2, 5, 6, 7, 10, 11)


def _in_proj_kernel(*refs):
    (x_ref, g_ref, w_ref, qn_ref, kvn_ref, bf_ref, ck_ref, sk_tab_ref, cqT_ref, sqT_ref,
     wuqT_ref, wuqswT_ref, eqT_ref) = refs[:_N_IN_PROJ_INPUTS]
    (fq_ref, fk_ref, fv_ref, lx_ref, lg_ref, ckv_ref, kpe_ref, logf_ref, mqT_ref, sbqT_ref, sk_ref, sv_ref) = refs[-12:]
    fk_ref, fv_ref, ckv_ref, kpe_ref, logf_ref, sk_ref, sv_ref = (
        r.at[0] for r in (fk_ref, fv_ref, ckv_ref, kpe_ref, logf_ref, sk_ref, sv_ref))
    chains = _row_chains(x_ref.shape[1])
    late = []
    for rows in chains:
        hb = _rms(x_ref[0, rows, :], g_ref[...]).astype(BF16)

        def proj(name, width):
            a = _W_OFF[name]
            return _dot(hb, w_ref[:, a:a + width])

        fq_ref[0, rows, :] = (proj('fq', 256) * (HEAD_DIM ** -0.5 * LOG2E)).astype(BF16)
        fk_ref[0, rows, :] = proj('fk', 256)
        fv_ref[0, rows, :] = proj('fv', 256)
        lx_ref[0, rows, :] = proj('lx', 256)
        lg_ref[0, rows, :] = proj('lg', 256)
        cqn = _rms(proj('cq', 256), qn_ref[...]).astype(BF16)
        ckv_ref[0, rows, :] = _rms(proj('ckv', 128), kvn_ref[...])
        kpe_ref[0, rows, :] = proj('kpe', 128) * ck_ref[rows, :] + proj('kpe_sw', 128) * sk_tab_ref[rows, :]

        ff = proj('ff', 128) + bf_ref[...]
        log_sig = jnp.minimum(ff, 0.0) - jnp.log(1.0 + jnp.exp(-jnp.abs(ff)))
        lane = lax.broadcasted_iota(jnp.int32, ff.shape, 1)
        logf_ref[0, rows, :] = jnp.where(lane < N_HEADS, log_sig, 0.0)

        sq = (proj('sq', 256) * (HEAD_DIM ** -0.5 * LOG2E)).astype(BF16)
        sk_ref[0, rows, :] = proj('sk', 256)
        sv_ref[0, rows, :] = proj('sv', 256)
        late.append((rows, cqn, sq))

    for rows, cqn, sq in late:
        q_t = _dot_nt(wuqT_ref[...], cqn)
        qsw_t = _dot_nt(wuqswT_ref[...], cqn)
        mqT_ref[0, :, rows] = (q_t * cqT_ref[:, rows] + qsw_t * sqT_ref[:, rows]).astype(BF16)
        sbqT_ref[0, :, rows] = _dot_nt(eqT_ref[...], sq).astype(BF16)


def _in_proj(x, lw, tabs, layer, depth, earlier):
    b, t, d = x.shape
    tm = _pick(t, (512, 256, 128))
    grid = (b, t // tm)
    row = lambda w: pl.BlockSpec((1, tm, w), lambda i, j: (i, j, 0))
    srow = lambda w: pl.BlockSpec((1, 1, tm, w), lambda i, j: (layer, i, j, 0))
    colT = pl.BlockSpec((1, QW, tm), lambda i, j: (i, 0, j))
    in_specs = [
        row(d), _const_spec((1, d)), _const_spec((d, _W_MIX_COLS)),
        _const_spec((1, MLA_Q_RANK)), _const_spec((1, MLA_KV_RANK)), _const_spec((1, LANE)),
        pl.BlockSpec((tm, LANE), lambda i, j: (j, 0)), pl.BlockSpec((tm, LANE), lambda i, j: (j, 0)),
        pl.BlockSpec((QW, tm), lambda i, j: (0, j)), pl.BlockSpec((QW, tm), lambda i, j: (0, j)),
        _const_spec((QW, MLA_Q_RANK)), _const_spec((QW, MLA_Q_RANK)), _const_spec((QW, 256)),
    ]
    sds = jax.ShapeDtypeStruct
    state = lambda w: sds((depth, b, t, w), F32)
    out_shape = [
        sds((b, t, 256), BF16),
        state(256), state(256),
        sds((b, t, 256), F32), sds((b, t, 256), F32),
        state(128), state(LANE), state(LANE),
        sds((b, QW, t), BF16), sds((b, QW, t), BF16),
        state(256), state(256),
    ]
    out_specs = [row(256), srow(256), srow(256), row(256), row(256), srow(128), srow(LANE), srow(LANE),
                 colT, colT, srow(256), srow(256)]
    assert len(in_specs) == _N_IN_PROJ_INPUTS
    aliases = {}
    if earlier is not None:
        in_specs += [pl.BlockSpec(memory_space=pl.ANY)] * len(earlier)
        aliases = {_N_IN_PROJ_INPUTS + n: out for n, out in enumerate(_IN_PROJ_STATE_OUTPUTS)}
    return pl.pallas_call(
        _in_proj_kernel, grid=grid, in_specs=in_specs, out_specs=out_specs, out_shape=out_shape,
        input_output_aliases=aliases, compiler_params=_params("parallel", "parallel"), name="in_proj",
    )(x, lw['ln_mix_pre'], lw['w_mix'], lw['mla_q_norm'], lw['mla_kv_norm'], lw['fox_bf'],
      tabs['ck'], tabs['sk'], tabs['cqT'], tabs['sqT'], lw['wuqT'], lw['wuqswT'], lw['eqT'], *(earlier or ()))


def _cumsum_rows(x):
    n = x.shape[0]
    row = lax.broadcasted_iota(jnp.int32, x.shape, 0)
    d = 1
    while d < n:
        x = x + jnp.where(row >= d, pltpu.roll(x, d, 0), 0.0)
        d *= 2
    return x


def _kv_prep_kernel(fq_ref, fk_ref, fv_ref, logf_ref, ckv_ref, kpe_ref, sk_ref, sv_ref,
                    ek_ref, evT_ref, eqT_ref, fkmap_ref, fqmapT_ref, onesk_ref, onesq_ref,
                    wuk_ref, ekpe_ref, wuvT_ref, seg_ref, segslot_ref,
                    fqT_out, fk_out, fvT_out, mk_out, mvT_out, sbk_out, sbvT_out, cend_out, qn_out, kn_out, mkn_out,
                    carry_ref, *, tk):
    fk_ref, fv_ref, logf_ref, ckv_ref, kpe_ref, sk_ref, sv_ref = (
        r.at[0] for r in (fk_ref, fv_ref, logf_ref, ckv_ref, kpe_ref, sk_ref, sv_ref))

    @pl.when(pl.program_id(1) == 0)
    def _():
        carry_ref[...] = jnp.zeros_like(carry_ref)
        qn_out[...] = jnp.zeros_like(qn_out)
        kn_out[...] = jnp.zeros_like(kn_out)
        mkn_out[...] = jnp.zeros_like(mkn_out)

    ts = fk_ref.shape[1]
    c = _cumsum_rows(logf_ref[0]) + carry_ref[...]
    carry_ref[...] = c[ts - 1:ts, :]
    c = c * LOG2E
    for j in range(ts // tk):
        cend_out[0, j] = c[(j + 1) * tk - 1:(j + 1) * tk, :]

    def norm_bound(x, seg):
        xf = x.astype(F32)
        sums = _dot((xf * xf).astype(BF16), seg) * 1.01
        return jnp.max(sums, axis=0, keepdims=True)

    qn_out[0] = jnp.maximum(qn_out[0], norm_bound(fq_ref[0], seg_ref[...]))
    kn_out[0] = jnp.maximum(kn_out[0], norm_bound(fk_ref[0].astype(BF16), seg_ref[...]))
    c_hi = c.astype(BF16).astype(F32)
    rem = c - c_hi
    c_mid = rem.astype(BF16).astype(F32)
    c_lo = (rem - c_mid).astype(BF16).astype(F32)
    feat = (c_hi + pltpu.roll(c_mid, N_HEADS, 1) + pltpu.roll(c_lo, 2 * N_HEADS, 1)).astype(BF16)

    def store_vt(out, v_t):
        for j in range(ts // tk):
            out[0, j] = v_t[:, j * tk:(j + 1) * tk]

    fk_out[0] = (_dot(fk_ref[0].astype(BF16), ek_ref[...]) + _dot(feat, fkmap_ref[...]) + onesk_ref[...]).astype(BF16)
    fqT_out[0] = (_dot_nt(eqT_ref[...], fq_ref[0]) + _dot_nt(fqmapT_ref[...], feat) + onesq_ref[...]).astype(BF16)
    store_vt(fvT_out, _dot_nt(evT_ref[...], fv_ref[0].astype(BF16)).astype(BF16))

    ckv = ckv_ref[0].astype(BF16)
    mk = (_dot(ckv, wuk_ref[...]) + _dot(kpe_ref[0].astype(BF16), ekpe_ref[...])).astype(BF16)
    mk_out[0] = mk
    mkn_out[0] = jnp.maximum(mkn_out[0], norm_bound(mk, segslot_ref[...]))
    store_vt(mvT_out, _dot_nt(wuvT_ref[...], ckv).astype(BF16))

    sbk_out[0] = _dot(sk_ref[0].astype(BF16), ek_ref[...]).astype(BF16)
    store_vt(sbvT_out, _dot_nt(evT_ref[...], sv_ref[0].astype(BF16)).astype(BF16))


def _kv_prep(fq, fk, fv, logf, ckv, kpe, sk, sv, layer, lw, cst, tk):
    _, b, s, _ = fk.shape
    ts = _pick(s, (1024, 768, 512, 256))
    grid = (b, s // ts)
    row = lambda w: pl.BlockSpec((1, ts, w), lambda i, j: (i, j, 0))
    srow = lambda w: pl.BlockSpec((1, 1, ts, w), lambda i, j: (layer, i, j, 0))
    colT = pl.BlockSpec((1, QW, ts), lambda i, j: (i, 0, j))
    vT = pl.BlockSpec((1, ts // tk, VW, tk), lambda i, j: (i, j, 0, 0))
    in_specs = [row(256), srow(256), srow(256), srow(LANE), srow(128), srow(LANE), srow(256), srow(256),
                _const_spec((256, QW)), _const_spec((VW, VW)), _const_spec((QW, 256)),
                _const_spec((LANE, QW)), _const_spec((QW, LANE)), _const_spec((1, QW)), _const_spec((QW, 1)),
                _const_spec((MLA_KV_RANK, QW)), _const_spec((LANE, QW)), _const_spec((VW, MLA_KV_RANK)),
                _const_spec((VW, LANE)), _const_spec((QW, LANE))]
    sds = jax.ShapeDtypeStruct
    slab = sds((b, s, QW), BF16)
    slab_t = sds((b, s // tk, VW, tk), BF16)
    stat = sds((b, 1, LANE), F32)
    stat_spec = pl.BlockSpec((1, 1, LANE), lambda i, j: (i, 0, 0))
    out_shape = [sds((b, QW, s), BF16), slab, slab_t, slab, slab_t, slab, slab_t,
                 sds((b, s // tk, 1, LANE), F32), stat, stat, stat]
    out_specs = [colT, row(QW), vT, row(QW), vT, row(QW), vT,
                 pl.BlockSpec((1, ts // tk, 1, LANE), lambda i, j: (i, j, 0, 0)), stat_spec, stat_spec, stat_spec]
    return pl.pallas_call(
        functools.partial(_kv_prep_kernel, tk=tk), grid=grid, in_specs=in_specs, out_specs=out_specs,
        out_shape=out_shape, scratch_shapes=[pltpu.VMEM((1, LANE), F32)],
        compiler_params=_params("parallel", "arbitrary"), name="kv_prep",
    )(fq, fk, fv, logf, ckv, kpe, sk, sv, cst['ek'], cst['evT'], lw['eqT'], cst['fkmap'], cst['fqmapT'],
      cst['ones_k'], cst['ones_qT'], lw['wuk'], cst['ekpe'], lw['wuvT'], cst['seg'], cst['seg_slot'])


def _head_rows(h):
    return slice(h * HP, (h + 1) * HP)


def _value_rows(h):
    return slice(h * HEAD_DIM, (h + 1) * HEAD_DIM)


def _load_kv(k_ref, vT_ref, kb, tk, h):
    off = pl.multiple_of(kb * tk, tk)
    return k_ref[0, pl.ds(off, tk), _head_rows(h)], vT_ref[0, kb, _value_rows(h), :]


def _finish(accs, invs, eye_ref, o_ref):
    for pair in range(N_HEADS // 2):
        o_t = jnp.concatenate([accs[2 * pair] * invs[2 * pair], accs[2 * pair + 1] * invs[2 * pair + 1]], axis=0)
        o_ref[0, :, _head_rows(pair)] = _dot_nt(eye_ref[...], o_t.astype(BF16)).astype(BF16)


def _softmax_attn_kernel(qT_ref, k_ref, vT_ref, eye_ref, o_ref, *, mode, past, tq, tk, s_valid):
    qi = pl.program_id(1)
    ltk = int(math.log2(tk))
    q_lo = past + qi * tq
    q_hi = q_lo + (tq - 1)
    if mode == 'fox':
        n_full = (q_lo + 1) >> ltk
        n_blk = (q_hi >> ltk) + 1
    else:
        lim_lo = ((q_lo // CHUNK) + 1) * CHUNK
        lim_hi = jnp.minimum(((q_hi // CHUNK) + 1) * CHUNK, s_valid)
        n_blk = (lim_hi + (tk - 1)) >> ltk
        n_full = jnp.minimum(lim_lo >> ltk, n_blk)
    qpos = q_lo + lax.broadcasted_iota(jnp.int32, (1, tq), 1)

    def step(kb, carry, masked):
        off = pl.multiple_of(kb * tk, tk)
        scores = [_dot(k_ref[0, pl.ds(off, tk), _head_rows(h)], qT_ref[0, _head_rows(h), :]) for h in range(N_HEADS)]
        if masked:
            kpos = kb * tk + lax.broadcasted_iota(jnp.int32, (tk, 1), 0)
            if mode == 'fox':
                vis = kpos <= qpos
            else:
                vis = jnp.logical_and((kpos // CHUNK) <= (qpos // CHUNK), kpos < s_valid)
        out = []
        for h in range(N_HEADS):
            m, l, acc = carry[h]
            s = jnp.where(vis, scores[h], NEG_INF) if masked else scores[h]
            m_new = jnp.maximum(m, jnp.max(s, axis=0, keepdims=True))
            p = jnp.exp2(s - m_new)
            alpha = jnp.exp2(m - m_new)
            l = alpha * l + jnp.sum(p, axis=0, keepdims=True)
            acc = alpha * acc + _dot(vT_ref[0, kb, _value_rows(h), :], p.astype(BF16))
            out.append((m_new, l, acc))
        return tuple(out)

    init = (jnp.full((1, tq), NEG_INF, F32), jnp.zeros((1, tq), F32), jnp.zeros((HEAD_DIM, tq), F32))
    carry = lax.fori_loop(0, n_full, functools.partial(step, masked=False), (init,) * N_HEADS)
    carry = lax.fori_loop(n_full, n_blk, functools.partial(step, masked=True), carry)
    _finish([c[2] for c in carry], [1.0 / c[1] for c in carry], eye_ref, o_ref)


def _first_needed_pair(cend_ref, qn_ref, kn_ref, n_pairs):
    nblk = cend_ref.shape[1]
    cend = cend_ref[0]
    c_tile = cend_ref[0, pl.ds(jnp.maximum(2 * n_pairs - 1, 0), 1), :]
    qk_bound = 2.0 * jnp.sqrt(qn_ref[0] * kn_ref[0]) + 1.0
    lane = lax.broadcasted_iota(jnp.int32, (nblk, LANE), 1)
    blk = lax.broadcasted_iota(jnp.int32, (nblk, 1), 0)
    worst = jnp.max(jnp.where(lane < N_HEADS, qk_bound + c_tile - cend, -jnp.inf), axis=1, keepdims=True)
    is_pair_end = jnp.logical_and((blk & 1) == 1, blk < 2 * n_pairs)
    skip = jnp.logical_and(is_pair_end, worst <= -FOX_SKIP_LIMIT)
    return jnp.sum(skip.astype(jnp.int32))


def _softmax_attn_pipelined_kernel(qT_ref, k_ref, vT_ref, eye_ref, *rest, mode, past, tq, tk, s_valid):
    o_ref, sa_ref, sb_ref, acc_ref = rest[-4:]
    kn_ref = rest[-5]
    qi = pl.program_id(1)
    q_lo = past + qi * tq
    n_pairs = past // tq + qi
    first_pair = _first_needed_pair(*rest[:3], n_pairs) if mode == 'fox' else 0
    qpos = q_lo + lax.broadcasted_iota(jnp.int32, (1, tq), 1)
    feature_rows = HEAD_DIM if mode == 'fox' else HP

    def qk(kb, dst_ref):
        off = pl.multiple_of(kb * tk, tk)
        for h in range(N_HEADS):
            dst_ref[h] = _dot(k_ref[0, pl.ds(off, tk), _head_rows(h)], qT_ref[0, _head_rows(h), :])

    def softmax_pv(kb, src_ref, stats, masked, lanes=slice(None)):
        if masked:
            kpos = kb * tk + lax.broadcasted_iota(jnp.int32, (tk, 1), 0)
            if mode == 'fox':
                vis = kpos <= qpos[:, lanes]
            else:
                vis = (kpos // CHUNK) <= (qpos[:, lanes] // CHUNK)
                if s_valid < k_ref.shape[1]:
                    vis = jnp.logical_and(vis, kpos < s_valid)
        out = []
        for h in range(N_HEADS):
            m, l = stats[h]
            s = src_ref[h, :, lanes]
            if masked:
                s = jnp.where(vis, s, NEG_INF)
            m_new = jnp.maximum(m, jnp.max(s, axis=0, keepdims=True))
            p = jnp.exp2(s - m_new)
            alpha = jnp.exp2(m - m_new)
            l = alpha * l + jnp.sum(p, axis=0, keepdims=True)
            rows = _value_rows(h)
            acc_ref[rows, lanes] = alpha * acc_ref[rows, lanes] + _dot(vT_ref[0, kb, rows, :], p.astype(BF16))
            out.append((m_new, l))
        return tuple(out)

    def fixed_pv(kb, src_ref, m_fix, sums):
        out = []
        for h in range(N_HEADS):
            p = jnp.exp2(src_ref[h] - m_fix[h])
            rows = _value_rows(h)
            acc_ref[rows, :] = acc_ref[rows, :] + _dot(vT_ref[0, kb, rows, :], p.astype(BF16))
            out.append(sums[h] + jnp.sum(p, axis=0, keepdims=True))
        return tuple(out)

    def pairs(step, carry):
        def body(j, carry):
            kb = 2 * j
            qk(kb + 1, sb_ref)
            carry = step(kb, sa_ref, carry)
            qk(kb + 2, sa_ref)
            return step(kb + 1, sb_ref, carry)
        return lax.fori_loop(first_pair, n_pairs, body, carry)

    acc_ref[...] = jnp.zeros_like(acc_ref)
    own = 2 * n_pairs
    early, late = slice(0, tk), slice(tk, tq)
    qk(own, sa_ref)
    qk(own + 1, sb_ref)
    init = ((jnp.full((1, tk), NEG_INF, F32), jnp.zeros((1, tk), F32)),) * N_HEADS
    st_early = softmax_pv(own, sa_ref, init, True, early)
    st_late = softmax_pv(own, sa_ref, init, False, late)
    qk(2 * first_pair, sa_ref)
    st_late = softmax_pv(own + 1, sb_ref, st_late, True, late)
    stats = tuple(tuple(jnp.concatenate([a, b], axis=1) for a, b in zip(st_early[h], st_late[h]))
                  for h in range(N_HEADS))

    gap = None
    for h in range(N_HEADS):
        q_h = qT_ref[0, h * HP:h * HP + feature_rows, :].astype(F32)
        qn2 = jnp.sum(q_h * q_h, axis=0, keepdims=True)
        g = jnp.sqrt(qn2 * kn_ref[0][:, h:h + 1]) * 1.01 + 0.5 - stats[h][0]
        gap = g if gap is None else jnp.maximum(gap, g)
    frozen_ok = jnp.max(gap) <= FIXED_STABILISER_LIMIT

    def run_frozen(stats):
        m_fix = [st[0] for st in stats]
        sums = pairs(lambda kb, src, sums: fixed_pv(kb, src, m_fix, sums), tuple(st[1] for st in stats))
        return tuple((m_fix[h], sums[h]) for h in range(N_HEADS))

    def run_online(stats):
        return pairs(lambda kb, src, st: softmax_pv(kb, src, st, False), stats)

    stats = lax.cond(frozen_ok, run_frozen, run_online, stats)
    _finish([acc_ref[_value_rows(h), :] for h in range(N_HEADS)], [1.0 / st[1] for st in stats], eye_ref, o_ref)


def _stick_attn_kernel(qT_ref, k_ref, vT_ref, eye_ref, later_ref, o_ref, *, past, tq, tk):
    assert tk % tq == 0 and past % tq == 0
    qi = pl.program_id(1)
    ltk = int(math.log2(tk))
    q_lo = past + qi * tq
    own = q_lo >> ltk
    qpos = q_lo + lax.broadcasted_iota(jnp.int32, (1, tq), 1)

    def block_terms(kb, masked):
        kvs = [_load_kv(k_ref, vT_ref, kb, tk, h) for h in range(N_HEADS)]
        zs = [_dot(kvs[h][0], qT_ref[0, _head_rows(h), :]) for h in range(N_HEADS)]
        if masked:
            kpos = kb * tk + lax.broadcasted_iota(jnp.int32, (tk, 1), 0)
            vis = kpos < qpos
        out = []
        for h in range(N_HEADS):
            z = zs[h]
            drop = jnp.where(z > SOFTPLUS_LINEAR_FROM, z, jnp.log2(1.0 + jnp.exp2(z)))
            if masked:
                drop = jnp.where(vis, drop, 0.0)
            later = _dot(later_ref[...], drop.astype(BF16))
            logw = z - drop - later
            if masked:
                logw = jnp.where(vis, logw, NEG_INF)
            out.append((logw, jnp.sum(drop, axis=0, keepdims=True), kvs[h][1]))
        return out

    def accumulate(terms, carry, live=None):
        out = []
        for h in range(N_HEADS):
            logw, total, v_t = terms[h]
            decay, acc = carry[h]
            a = jnp.exp2(logw - decay)
            if live is not None:
                a = jnp.where(live, a, 0.0)
                total = jnp.where(live, total, 0.0)
            out.append((decay + total, acc + _dot(v_t, a.astype(BF16))))
        return tuple(out)

    def min_decay(carry):
        d = carry[0][0]
        for h in range(1, N_HEADS):
            d = jnp.minimum(d, carry[h][0])
        return jnp.min(d)

    init = (jnp.zeros((1, tq), F32), jnp.zeros((HEAD_DIM, tq), F32))
    own_terms = block_terms(own, True)
    prev_terms = block_terms(jnp.maximum(own - 1, 0), False)
    carry = accumulate(own_terms, (init,) * N_HEADS)
    carry = accumulate(prev_terms, carry, live=own >= 1)

    def cond(state):
        kb, dmin, _ = state
        return jnp.logical_and(kb >= 0, dmin < SB_DECAY_LIMIT)

    def body(state):
        kb, _, carry = state
        carry = accumulate(block_terms(kb, False), carry)
        return kb - 1, min_decay(carry), carry

    _, _, carry = lax.while_loop(cond, body, (own - 2, min_decay(carry), carry))
    _finish([c[1] for c in carry], [1.0] * N_HEADS, eye_ref, o_ref)


def _attention(mode, q_t, k, v_t, cst, *, past, s_valid, tq, tk, skip_stats=None):
    b, _, t = q_t.shape
    s = k.shape[1]
    grid = (b, t // tq)
    kv_mode = pl.Buffered(1) if 2 * s * (QW + VW) * 2 > VMEM_LIMIT_BYTES // 2 else None
    in_specs = [pl.BlockSpec((1, QW, tq), lambda i, j: (i, 0, j)),
                pl.BlockSpec((1, s, QW), lambda i, j: (i, 0, 0), pipeline_mode=kv_mode),
                pl.BlockSpec((1, s // tk, VW, tk), lambda i, j: (i, 0, 0, 0), pipeline_mode=kv_mode),
                _const_spec((tq, tq))]
    args = [q_t, k, v_t, jnp.eye(tq, dtype=BF16)]
    scratch = []
    if mode == 'sb':
        kern = functools.partial(_stick_attn_kernel, past=past, tq=tq, tk=tk)
        in_specs.append(_const_spec((tk, tk)))
        args.append(cst['later'][tk])
    elif tq == 2 * tk and past % tq == 0:
        kern = functools.partial(_softmax_attn_pipelined_kernel, mode=mode, past=past, tq=tq, tk=tk, s_valid=s_valid)
        scratch = [pltpu.VMEM((N_HEADS, tk, tq), F32), pltpu.VMEM((N_HEADS, tk, tq), F32), pltpu.VMEM((VW, tq), F32)]
        stat_spec = pl.BlockSpec((1, 1, LANE), lambda i, j: (i, 0, 0))
        if mode == 'fox':
            cend, qn, kn = skip_stats
            in_specs += [pl.BlockSpec((1, s // tk, LANE), lambda i, j: (i, 0, 0)), stat_spec, stat_spec]
            args += [cend.reshape(b, s // tk, LANE), qn, kn]
        else:
            in_specs.append(stat_spec)
            args.append(skip_stats[0])
    else:
        kern = functools.partial(_softmax_attn_kernel, mode=mode, past=past, tq=tq, tk=tk, s_valid=s_valid)
    width = (N_HEADS // 2) * HP
    return pl.pallas_call(
        kern, grid=grid, in_specs=in_specs,
        out_specs=pl.BlockSpec((1, tq, width), lambda i, j: (i, j, 0)),
        out_shape=jax.ShapeDtypeStruct((b, t, width), BF16), scratch_shapes=scratch,
        compiler_params=_params("parallel", "parallel"), name="attn_" + mode,
    )(*args)


def _lru_kernel(lx_ref, lg_ref, cb_ref, h0_ref, cw_ref, cbias_ref, wr_ref, br_ref, wi_ref, bi_ref, lam_ref,
                y_ref, nb_ref, hl_ref, xcat_ref, h_ref, *, past, tc):
    ci = pl.program_id(1)
    keep = LRU_CONV - 1

    @pl.when(ci == 0)
    def _():
        xcat_ref[0:8, :] = jnp.zeros((8, LRU_WIDTH), F32)
        xcat_ref[8 - keep:8, :] = cb_ref[0]
        h_ref[...] = h0_ref[0]

    x = lx_ref[0]
    xcat_ref[8:8 + tc, :] = x
    xc = cbias_ref[...] + x * cw_ref[keep:keep + 1, :]
    for tap in range(keep):
        shift = keep - tap
        xc = xc + xcat_ref[8 - shift:8 - shift + tc, :] * cw_ref[tap:tap + 1, :]
    nb_ref[0] = xcat_ref[8 + tc - keep:8 + tc, :]
    xcat_ref[0:8, :] = x[tc - 8:tc, :]

    xcb = xc.astype(BF16)
    r = _sigmoid(_dot(xcb, wr_ref[...]) + br_ref[...])
    gate_in = _sigmoid(_dot(xcb, wi_ref[...]) + bi_ref[...])
    log_a = (-LRU_C) * r * _softplus(-lam_ref[...])
    row = lax.broadcasted_iota(jnp.int32, (tc, LRU_WIDTH), 0)
    reset = (past + ci * tc + row) == 0
    a = jnp.where(reset, 0.0, jnp.exp(log_a))
    y2 = 2.0 * log_a
    series = -y2 * (1.0 + y2 * (0.5 + y2 * (1.0 / 6.0 + y2 * (1.0 / 24.0 + y2 * (1.0 / 120.0)))))
    one_minus = jnp.where(y2 > -0.05, series, 1.0 - jnp.exp(y2))
    mult = jnp.where(reset, 1.0, jnp.sqrt(one_minus))
    u = mult * gate_in * xc

    d = 1
    while d < tc:
        if d < 8:
            ok = row >= d
            u = u + jnp.where(ok, a * pltpu.roll(u, d, 0), 0.0)
            a = jnp.where(ok, a * pltpu.roll(a, d, 0), a)
        else:
            u = jnp.concatenate([u[:d], u[d:] + a[d:] * u[:-d]], axis=0)
            a = jnp.concatenate([a[:d], a[d:] * a[:-d]], axis=0)
        d *= 2
    hs = a * h_ref[...] + u
    h_last = hs[tc - 1:tc, :]
    h_ref[...] = h_last
    hl_ref[0] = h_last

    g = lg_ref[0]
    gelu = 0.5 * g * (1.0 + jnp.tanh(0.7978845608028654 * (g + 0.044715 * g * g * g)))
    y_ref[0] = (hs * gelu).astype(BF16)


def _lru(lx, lg, conv_buf, h0, lw, past):
    b, t, w = lx.shape
    tc = _pick(t, (256, 128, 64, 32))
    grid = (b, t // tc)
    row = pl.BlockSpec((1, tc, w), lambda i, j: (i, j, 0))
    keep = LRU_CONV - 1
    in_specs = [row, row, pl.BlockSpec((1, keep, w), lambda i, j: (i, 0, 0)), pl.BlockSpec((1, 1, w), lambda i, j: (i, 0, 0)),
                _const_spec((LRU_CONV, w)), _const_spec((1, w)), _const_spec((w, w)), _const_spec((1, w)),
                _const_spec((w, w)), _const_spec((1, w)), _const_spec((1, w))]
    sds = jax.ShapeDtypeStruct
    return pl.pallas_call(
        functools.partial(_lru_kernel, past=past, tc=tc), grid=grid, in_specs=in_specs,
        out_specs=[row, pl.BlockSpec((1, keep, w), lambda i, j: (i, 0, 0)), pl.BlockSpec((1, 1, w), lambda i, j: (i, 0, 0))],
        out_shape=[sds((b, t, w), BF16), sds((b, keep, w), F32), sds((b, 1, w), F32)],
        scratch_shapes=[pltpu.VMEM((tc + 8, w), F32), pltpu.VMEM((1, w), F32)],
        compiler_params=_params("parallel", "arbitrary"), name="rg_lru",
    )(lx, lg, conv_buf, h0, lw['lru_conv_w'], lw['lru_conv_b'], lw['lru_wr'], lw['lru_br'], lw['lru_wi'],
      lw['lru_bi'], lw['lru_lam'])


def _row_chains(tm):
    n = 2 if tm % 512 == 0 else 1
    return [slice(i * (tm // n), (i + 1) * (tm // n)) for i in range(n)]


def _merge_kernel(x_ref, oa_ref, ob_ref, oc_ref, od_ref, g1_ref, g2_ref, wg_ref, wb_ref, wo_ref, out_ref):
    d = x_ref.shape[2]
    for rows in _row_chains(x_ref.shape[1]):
        x = x_ref[0, rows, :]
        hb = _rms(x, g1_ref[...]).astype(BF16)
        merged = None
        for n, o_ref in enumerate((oa_ref, ob_ref, oc_ref, od_ref)):
            gate = _sigmoid(_dot(hb, wg_ref[:, n * d:(n + 1) * d]))
            term = gate * _dot(o_ref[0, rows, :], wb_ref[n])
            merged = term if merged is None else merged + term
        y = _dot(merged.astype(BF16), wo_ref[...])
        out_ref[0, rows, :] = x + _rms(y, g2_ref[...])


def _merge(x, o_a, o_b, o_c, o_d, lw):
    b, t, d = x.shape
    tm = _pick(t, (512, 256, 128))
    grid = (b, t // tm)
    row = lambda w: pl.BlockSpec((1, tm, w), lambda i, j: (i, j, 0))
    bw = BRANCH_WIDTH
    single = pl.Buffered(1)
    in_specs = [row(d), row(bw), row(bw), row(bw), row(bw), _const_spec((1, d)), _const_spec((1, d)),
                pl.BlockSpec((d, N_BRANCH * d), lambda i, j: (0, 0), pipeline_mode=single),
                pl.BlockSpec((N_BRANCH, BRANCH_WIDTH, d), lambda i, j: (0, 0, 0), pipeline_mode=single),
                pl.BlockSpec((d, d), lambda i, j: (0, 0), pipeline_mode=single)]
    return pl.pallas_call(
        _merge_kernel, grid=grid, in_specs=in_specs, out_specs=row(d),
        out_shape=jax.ShapeDtypeStruct((b, t, d), F32),
        compiler_params=_params("parallel", "parallel"), name="merge",
    )(x, o_a, o_b, o_c, o_d, lw['ln_mix_pre'], lw['ln_mix_post'], lw['w_gate'], lw['w_branch'], lw['w_out'])


def _mem_kv_kernel(mem_ref, g_ref, wk_ref, wv_ref, k_ref, v_ref):
    mn = _rms(mem_ref[0], g_ref[...]).astype(BF16)
    k_ref[0] = _dot(mn, wk_ref[...])
    v_ref[0] = _dot(mn, wv_ref[...])


def _mem_kv(mem, lw):
    b, m, d = mem.shape
    w = lw['mem_wk'].shape[1]
    out = jax.ShapeDtypeStruct((b, m, w), F32)
    blk = pl.BlockSpec((1, m, w), lambda i: (i, 0, 0))
    return pl.pallas_call(
        _mem_kv_kernel, grid=(b,),
        in_specs=[pl.BlockSpec((1, m, d), lambda i: (i, 0, 0)), _const_spec((1, d)), _const_spec((d, w)), _const_spec((d, w))],
        out_specs=[blk, blk], out_shape=[out, out], compiler_params=_params("parallel"), name="mem_kv",
    )(mem, lw['mem_norm'], lw['mem_wk'], lw['mem_wv'])


def _mem_attn_kernel(x_ref, mk_ref, mv_ref, g1_ref, g2_ref, wq_ref, wo_ref, out_ref):
    x = x_ref[0]
    hb = _rms(x, g1_ref[...]).astype(BF16)
    q = (_dot(hb, wq_ref[...]) * (MEM_HEAD_DIM ** -0.5)).astype(BF16)
    heads = []
    for h in range(N_HEADS):
        sl = slice(h * MEM_HEAD_DIM, (h + 1) * MEM_HEAD_DIM)
        s = _dot_nt(q[:, sl], mk_ref[0, :, sl])
        p = jnp.exp(s - jnp.max(s, axis=-1, keepdims=True))
        inv = 1.0 / jnp.sum(p, axis=-1, keepdims=True)
        heads.append((_dot(p.astype(BF16), mv_ref[0, :, sl]) * inv).astype(BF16))
    y = _dot(jnp.concatenate(heads, axis=1), wo_ref[...])
    out_ref[0] = x + _rms(y, g2_ref[...])


def _mem_attn(x, mk, mv, lw):
    b, t, d = x.shape
    m, w = mk.shape[1], mk.shape[2]
    tm = _pick(t, (512, 256, 128))
    row = pl.BlockSpec((1, tm, d), lambda i, j: (i, j, 0))
    kv = pl.BlockSpec((1, m, w), lambda i, j: (i, 0, 0))
    return pl.pallas_call(
        _mem_attn_kernel, grid=(b, t // tm),
        in_specs=[row, kv, kv, _const_spec((1, d)), _const_spec((1, d)), _const_spec((d, w)), _const_spec((w, d))],
        out_specs=row, out_shape=jax.ShapeDtypeStruct((b, t, d), F32),
        compiler_params=_params("parallel", "parallel"), name="mem_attn",
    )(x, mk, mv, lw['ln_mem_pre'], lw['ln_mem_post'], lw['mem_wq'], lw['mem_wo'])


def _ffn_kernel(x_ref, g1_ref, g2_ref, wg_ref, wu_ref, wd_ref, out_ref):
    for rows in _row_chains(x_ref.shape[1]):
        x = x_ref[0, rows, :]
        hb = _rms(x, g1_ref[...]).astype(BF16)
        gate = _dot(hb, wg_ref[...])
        act = (gate * _sigmoid(gate) * _dot(hb, wu_ref[...])).astype(BF16)
        y = _dot(act, wd_ref[...])
        out_ref[0, rows, :] = x + _rms(y, g2_ref[...])


def _ffn(x, lw):
    b, t, d = x.shape
    f = lw['ffn_wg'].shape[1]
    tm = _pick(t, (512, 256, 128))
    row = pl.BlockSpec((1, tm, d), lambda i, j: (i, j, 0))
    single = pl.Buffered(1)
    return pl.pallas_call(
        _ffn_kernel, grid=(b, t // tm),
        in_specs=[row, _const_spec((1, d)), _const_spec((1, d)),
                  pl.BlockSpec((d, f), lambda i, j: (0, 0), pipeline_mode=single),
                  pl.BlockSpec((d, f), lambda i, j: (0, 0), pipeline_mode=single),
                  pl.BlockSpec((f, d), lambda i, j: (0, 0), pipeline_mode=single)],
        out_specs=row, out_shape=jax.ShapeDtypeStruct((b, t, d), F32),
        compiler_params=_params("parallel", "parallel"), name="ffn",
    )(x, lw['ln_ffn_pre'], lw['ln_ffn_post'], lw['ffn_wg'], lw['ffn_wu'], lw['ffn_wd'])


def _constants():
    fk, fq, ones_k, ones_q = _fox_feature_maps()
    later = {}
    for tk in (256,):
        idx = np.arange(tk)
        later[tk] = jnp.asarray((idx[None, :] > idx[:, None]).astype(np.float32), BF16)
    return dict(
        ek=jnp.asarray(_place_qk(), BF16), evT=jnp.eye(VW, dtype=BF16),
        fkmap=jnp.asarray(fk, BF16), fqmapT=jnp.asarray(fq.T, BF16),
        ones_k=jnp.asarray(ones_k, F32), ones_qT=jnp.asarray(ones_q.T, F32),
        ekpe=jnp.asarray(_place_kpe(), BF16), later=later,
        seg=jnp.asarray(np.repeat(np.eye(N_HEADS, LANE, dtype=np.float32), HEAD_DIM, axis=0), BF16),
        seg_slot=jnp.asarray(np.repeat(np.eye(N_HEADS, LANE, dtype=np.float32), HP, axis=0), BF16))


def _pad_cols(w, n):
    return jnp.pad(w, ((0, 0), (0, n - w.shape[1])))


_MIX_SIZES = (256, 256, 256, N_HEADS, LRU_WIDTH, LRU_WIDTH, MLA_Q_RANK, MLA_KV_RANK, MLA_ROPE, 256, 256, 256)


def _w_in_plan(d_model):
    offs = np.concatenate([[0], np.cumsum(_MIX_SIZES)])
    fq, fk, fv, ff, lx, lg, cq, ckv, kpe, sq, sk, sv = [np.arange(offs[i], offs[i + 1]) for i in range(len(_MIX_SIZES))]
    half = MLA_ROPE // 2

    def padded(cols):
        return np.concatenate([cols, np.full(LANE - len(cols), -1)])

    mix = [fq, fk, fv, lx, lg, cq, ckv, padded(kpe), padded(np.concatenate([kpe[half:], kpe[:half]])), padded(ff),
           sq, sk, sv]
    pieces, col = [], 0
    for src in mix:
        pieces.append((0, col, src))
        col += len(src)
    assert col == _W_MIX_COLS
    gate0 = offs[-1]
    for i in range(N_BRANCH * d_model // 256):
        pieces.append((1, i * 256, gate0 + i * 256 + np.arange(256)))
    return pieces


def _transpose_kernel(xT_ref, eye_ref, x_ref):
    x_ref[...] = _dot_nt(eye_ref[...], xT_ref[...]).astype(x_ref.dtype)


def _transpose_bf16(x_t):
    r, c = x_t.shape
    rb = _pick(r, (256, 128))
    return pl.pallas_call(
        _transpose_kernel, grid=(r // rb,),
        in_specs=[pl.BlockSpec((rb, c), lambda i: (i, 0)), _const_spec((c, c))],
        out_specs=pl.BlockSpec((c, rb), lambda i: (0, i)), out_shape=jax.ShapeDtypeStruct((c, r), BF16),
        compiler_params=_params("parallel"), name="transpose_w",
    )(x_t, jnp.eye(c, dtype=BF16))


def _repack_w_in(w_in_t):
    d_model = w_in_t.shape[1]
    rows = [[], []]
    for dst, _, src in _w_in_plan(d_model):
        valid = src[src >= 0]
        runs = np.split(valid, np.nonzero(np.diff(valid) != 1)[0] + 1)
        piece = jnp.concatenate([w_in_t[int(r[0]):int(r[-1]) + 1] for r in runs], axis=0)
        rows[dst].append(jnp.pad(piece, ((0, len(src) - len(valid)), (0, 0))))
    return tuple(_transpose_bf16(jnp.concatenate(r, axis=0).astype(BF16)) for r in rows)


def _layer_weights(l, p):
    w_mix, w_gate = _repack_w_in(jnp.transpose(p['w_in'], (2, 0, 1))[:, l, :])
    half = MLA_ROPE // 2
    qk = MLA_NOPE + MLA_ROPE
    wuq = p['mla_w_uq'][l].reshape(MLA_Q_RANK, N_HEADS, qk)
    rope = wuq[:, :, MLA_NOPE:]
    rope_sw = jnp.concatenate([rope[:, :, half:], rope[:, :, :half]], axis=2)
    wuq_p = jnp.pad(wuq, ((0, 0), (0, 0), (0, HP - qk))).reshape(MLA_Q_RANK, QW)
    wuqsw_p = jnp.pad(rope_sw, ((0, 0), (0, 0), (MLA_NOPE, HP - qk))).reshape(MLA_Q_RANK, QW)
    wuk = p['mla_w_uk'][l].reshape(MLA_KV_RANK, N_HEADS, MLA_NOPE)
    wuk_p = jnp.pad(wuk, ((0, 0), (0, 0), (0, HP - MLA_NOPE))).reshape(MLA_KV_RANK, QW)

    def block_diag(w):
        return jax.scipy.linalg.block_diag(*[w[i] for i in range(w.shape[0])]).astype(BF16)

    row = lambda v: v[l].reshape(1, -1).astype(F32)
    return dict(
        w_mix=w_mix, w_gate=w_gate,
        ln_mix_pre=row(p['ln_mix_pre']), ln_mix_post=row(p['ln_mix_post']),
        fox_bf=_pad_cols(row(p['fox_bf']), LANE),
        mla_q_norm=row(p['mla_q_norm']), mla_kv_norm=row(p['mla_kv_norm']),
        wuqT=wuq_p.T.astype(BF16), wuqswT=wuqsw_p.T.astype(BF16), wuk=wuk_p.astype(BF16), wuvT=p['mla_w_uv'][l].T.astype(BF16),
        eqT=jnp.asarray(_place_qk().T, BF16),
        lru_conv_w=p['lru_conv_w'][l].astype(F32), lru_conv_b=row(p['lru_conv_b']),
        lru_wr=block_diag(p['lru_wr'][l]), lru_br=row(p['lru_br']),
        lru_wi=block_diag(p['lru_wi'][l]), lru_bi=row(p['lru_bi']), lru_lam=row(p['lru_lam']),
        w_branch=p['w_branch'][l].astype(BF16), w_out=p['w_out'][l].astype(BF16),
        ln_mem_pre=row(p['ln_mem_pre']), ln_mem_post=row(p['ln_mem_post']), mem_norm=row(p['mem_norm']),
        mem_wq=p['mem_wq'][l].astype(BF16), mem_wk=p['mem_wk'][l].astype(BF16),
        mem_wv=p['mem_wv'][l].astype(BF16), mem_wo=p['mem_wo'][l].astype(BF16),
        ln_ffn_pre=row(p['ln_ffn_pre']), ln_ffn_post=row(p['ln_ffn_post']),
        ffn_wg=p['ffn_wg'][l].astype(BF16), ffn_wu=p['ffn_wu'][l].astype(BF16), ffn_wd=p['ffn_wd'][l].astype(BF16))


def _rope_tables(past, t):
    half = MLA_ROPE // 2
    inv = jnp.power(ROPE_BASE, -jnp.arange(half, dtype=F32) / half)
    ang = (past + jnp.arange(t, dtype=jnp.int32)).astype(F32)[:, None] * inv
    cos, sin = jnp.cos(ang), jnp.sin(ang)
    c32 = jnp.concatenate([cos, cos], axis=1)
    s32 = jnp.concatenate([-sin, sin], axis=1)
    scale = (MLA_NOPE + MLA_ROPE) ** -0.5 * LOG2E
    slot_c = jnp.concatenate([jnp.ones((t, MLA_NOPE), F32), c32, jnp.zeros((t, HP - MLA_NOPE - MLA_ROPE), F32)], axis=1)
    slot_s = jnp.concatenate([jnp.zeros((t, MLA_NOPE), F32), s32, jnp.zeros((t, HP - MLA_NOPE - MLA_ROPE), F32)], axis=1)
    return dict(ck=_pad_cols(c32, LANE), sk=_pad_cols(s32, LANE),
                cqT=(jnp.tile(slot_c, (1, N_HEADS)) * scale).T, sqT=(jnp.tile(slot_s, (1, N_HEADS)) * scale).T)


def _trunk_layer(x, past_state, mem_k, mem_v, lw, cst, layer, depth, earlier):
    b, t, _ = x.shape
    past = 0 if past_state is None else past_state['fox_k'].shape[1]
    tk = 256
    fq, fk, fv, lx, lg, ckv_n, kpe_slab, logf_slab, mla_qT, sb_qT, sk, sv = _in_proj(
        x, lw, _rope_tables(past, t), layer, depth, earlier)
    stacked = (fk, fv, ckv_n, kpe_slab, logf_slab, sk, sv)

    if past_state is None:
        s_valid = t
        keys = (fq, fk, fv, logf_slab, ckv_n, kpe_slab, sk, sv)
        key_layer = layer
        conv_buf = jnp.zeros((b, LRU_CONV - 1, LRU_WIDTH), F32)
        h0 = jnp.zeros((b, 1, LRU_WIDTH), F32)
    else:
        s_valid = past + t
        s_pad = -(-s_valid // tk) * tk

        def cat(old, new):
            old = old.reshape(b, past, -1).astype(new.dtype)
            old = jnp.pad(old, ((0, 0), (0, 0), (0, new.shape[2] - old.shape[2])))
            return jnp.pad(jnp.concatenate([old, new], axis=1), ((0, 0), (0, s_pad - s_valid), (0, 0)))

        def cat_state(old, new):
            return cat(old, new[layer])[None]

        keys = (cat(jnp.zeros((b, past, 256), BF16), fq), cat_state(past_state['fox_k'], fk),
                cat_state(past_state['fox_v'], fv), cat_state(past_state['fox_logf'], logf_slab),
                cat_state(past_state['mla_ckv'], ckv_n), cat_state(past_state['mla_kpe'], kpe_slab),
                cat_state(past_state['sb_k'], sk), cat_state(past_state['sb_v'], sv))
        key_layer = 0
        conv_buf = past_state['lru_conv'].astype(F32)
        h0 = past_state['lru_h'].reshape(b, 1, LRU_WIDTH).astype(F32)

    (fox_qT, fox_k, fox_vT, mla_k, mla_vT, sb_k, sb_vT,
     *fox_skip_stats, mla_key_norm) = _kv_prep(*keys, key_layer, lw, cst, tk)
    if past_state is not None:
        fox_qT = fox_qT[:, :, past:past + t]

    att = functools.partial(_attention, cst=cst, past=past, s_valid=s_valid, tk=tk)
    tq_softmax = _pick(t, (2 * tk, tk, 128))
    o_a = att('fox', fox_qT, fox_k, fox_vT, tq=tq_softmax, skip_stats=fox_skip_stats)
    o_c = att('mla', mla_qT, mla_k, mla_vT, tq=tq_softmax, skip_stats=[mla_key_norm])
    o_d = att('sb', sb_qT, sb_k, sb_vT, tq=_pick(t, (tk, 128)))
    o_b, lru_conv, lru_h = _lru(lx, lg, conv_buf, h0, lw, past)

    x = _merge(x, o_a, o_b, o_c, o_d, lw)
    x = _mem_attn(x, mem_k, mem_v, lw)
    x = _ffn(x, lw)

    return x, stacked, dict(lru_h=lru_h.reshape(b, LRU_WIDTH), lru_conv=lru_conv)


def _state_outputs(stacked, small):
    fk, fv, ckv_n, kpe_slab, logf_slab, sk, sv = stacked
    depth, b, t, _ = fk.shape
    heads = lambda a: a.reshape(depth, b, t, N_HEADS, HEAD_DIM)
    stk = lambda name: jnp.stack([s[name] for s in small])
    return (heads(fk), heads(fv), logf_slab[..., :N_HEADS], stk('lru_h'), stk('lru_conv'),
            ckv_n, kpe_slab[..., :MLA_ROPE], heads(sk), heads(sv))


def kernel(x_prompt, x_sample, cache_fox_k, cache_fox_v, cache_fox_logf, state_lru_h, state_lru_conv, cache_mla_ckv, cache_mla_kpe, cache_sb_k, cache_sb_v, cache_mem_k, cache_mem_v, mem_prompt, ln_mix_pre, ln_mix_post, w_in, fox_bf, lru_conv_w, lru_conv_b, lru_wr, lru_br, lru_wi, lru_bi, lru_lam, mla_q_norm, mla_w_uq, mla_kv_norm, mla_w_uk, mla_w_uv, w_branch, w_out, ln_mem_pre, ln_mem_post, mem_norm, mem_wq, mem_wk, mem_wv, mem_wo, ln_ffn_pre, ln_ffn_post, ffn_wg, ffn_wu, ffn_wd):
    params = dict(ln_mix_pre=ln_mix_pre, ln_mix_post=ln_mix_post, w_in=w_in, fox_bf=fox_bf, lru_conv_w=lru_conv_w,
                  lru_conv_b=lru_conv_b, lru_wr=lru_wr, lru_br=lru_br, lru_wi=lru_wi, lru_bi=lru_bi, lru_lam=lru_lam,
                  mla_q_norm=mla_q_norm, mla_w_uq=mla_w_uq, mla_kv_norm=mla_kv_norm, mla_w_uk=mla_w_uk,
                  mla_w_uv=mla_w_uv, w_branch=w_branch, w_out=w_out, ln_mem_pre=ln_mem_pre, ln_mem_post=ln_mem_post,
                  mem_norm=mem_norm, mem_wq=mem_wq, mem_wk=mem_wk, mem_wv=mem_wv, mem_wo=mem_wo,
                  ln_ffn_pre=ln_ffn_pre, ln_ffn_post=ln_ffn_post, ffn_wg=ffn_wg, ffn_wu=ffn_wu, ffn_wd=ffn_wd)
    depth = w_in.shape[0]
    cst = _constants()
    weights = [_layer_weights(l, params) for l in range(depth)]
    bp, mem_len = mem_prompt.shape[0], mem_prompt.shape[1]

    y_prompt, p_rows, p_small, p_mem = x_prompt, None, [], []
    for l in range(depth):
        mk, mv = _mem_kv(mem_prompt, weights[l])
        y_prompt, p_rows, small = _trunk_layer(y_prompt, None, mk.astype(BF16), mv.astype(BF16), weights[l], cst,
                                               l, depth, p_rows)
        p_small.append(small)
        p_mem.append((mk.reshape(bp, mem_len, N_HEADS, MEM_HEAD_DIM), mv.reshape(bp, mem_len, N_HEADS, MEM_HEAD_DIM)))

    y_sample, s_rows, s_small = x_sample, None, []
    bs = x_sample.shape[0]
    for l in range(depth):
        past = dict(fox_k=cache_fox_k[l], fox_v=cache_fox_v[l], fox_logf=cache_fox_logf[l], lru_h=state_lru_h[l],
                    lru_conv=state_lru_conv[l], mla_ckv=cache_mla_ckv[l], mla_kpe=cache_mla_kpe[l],
                    sb_k=cache_sb_k[l], sb_v=cache_sb_v[l])
        mk = cache_mem_k[l].reshape(bs, mem_len, -1).astype(BF16)
        mv = cache_mem_v[l].reshape(bs, mem_len, -1).astype(BF16)
        y_sample, s_rows, small = _trunk_layer(y_sample, past, mk, mv, weights[l], cst, l, depth, s_rows)
        s_small.append(small)

    mem_out = (jnp.stack([m[0] for m in p_mem]), jnp.stack([m[1] for m in p_mem]))
    return (y_prompt, y_sample) + _state_outputs(p_rows, p_small) + mem_out + _state_outputs(s_rows, s_small)
```

```python
import functools
import math

import numpy as np
import jax
import jax.numpy as jnp
from jax import lax
from jax.experimental import pallas as pl
from jax.experimental.pallas import tpu as pltpu

F32 = jnp.float32
BF16 = jnp.bfloat16

CHUNK = 64
HEAD_DIM = 64
N_HEADS = 4
LRU_WIDTH = 256
LRU_CONV = 4
LRU_C = 8.0
MLA_Q_RANK = 256
MLA_KV_RANK = 128
MLA_NOPE = 64
MLA_ROPE = 32
MLA_V = 64
ROPE_BASE = 10000.0
N_BRANCH = 4
BRANCH_WIDTH = 256
MEM_HEAD_DIM = 128
EPS = 1e-6
NEG_INF = -1e30

HP = 128
QW = N_HEADS * HP
VW = N_HEADS * HEAD_DIM
LANE = 128
VMEM_LIMIT_BYTES = 56 * 1024 * 1024
LOG2E = 1.4426950408889634
SB_DECAY_LIMIT = 127.0
FOX_SKIP_LIMIT = 150.0
FIXED_STABILISER_LIMIT = 60.0
SOFTPLUS_LINEAR_FROM = 60.0

_NT = (((1,), (1,)), ((), ()))


def _dot(a, b):
    return jnp.dot(a, b, preferred_element_type=F32)


def _dot_nt(a, b):
    return lax.dot_general(a, b, _NT, preferred_element_type=F32)


def _rms(x, g):
    ms = jnp.mean(x * x, axis=-1, keepdims=True)
    return x * lax.rsqrt(ms + EPS) * g


def _sigmoid(x):
    return 1.0 / (1.0 + jnp.exp(-x))


def _softplus(x):
    return jnp.maximum(x, 0.0) + jnp.log(1.0 + jnp.exp(-jnp.abs(x)))


def _pick(n, cands):
    for c in cands:
        if n % c == 0:
            return c
    return n


def _params(*sem):
    return pltpu.CompilerParams(dimension_semantics=sem, vmem_limit_bytes=VMEM_LIMIT_BYTES)


def _const_spec(shape):
    nd = len(shape)
    return pl.BlockSpec(shape, lambda *_: (0,) * nd)


def _place_qk():
    e = np.zeros((N_HEADS * HEAD_DIM, QW), np.float32)
    for h in range(N_HEADS):
        for j in range(HEAD_DIM):
            e[h * HEAD_DIM + j, h * HP + j] = 1.0
    return e


def _fox_feature_maps():
    fk = np.zeros((LANE, QW), np.float32)
    fq = np.zeros((LANE, QW), np.float32)
    ones_k = np.zeros((1, QW), np.float32)
    ones_q = np.zeros((1, QW), np.float32)
    for h in range(N_HEADS):
        for part in range(3):
            fq[part * N_HEADS + h, h * HP + HEAD_DIM + part] = 1.0
            fk[part * N_HEADS + h, h * HP + HEAD_DIM + 3 + part] = -1.0
            ones_k[0, h * HP + HEAD_DIM + part] = 1.0
            ones_q[0, h * HP + HEAD_DIM + 3 + part] = 1.0
    return fk, fq, ones_k, ones_q


def _place_kpe():
    e = np.zeros((LANE, QW), np.float32)
    for h in range(N_HEADS):
        for j in range(MLA_ROPE):
            e[j, h * HP + MLA_NOPE + j] = 1.0
    return e


_W_OFF = dict(fq=0, fk=256, fv=512, lx=768, lg=1024, cq=1280, ckv=1536, kpe=1664, kpe_sw=1792, ff=1920,
              sq=2048, sk=2304, sv=2560)
_W_MIX_COLS = 2816


_N_IN_PROJ_INPUTS = 13
_IN_PROJ_STATE_OUTPUTS = (1, 2, 5, 6, 7, 10, 11)


def _in_proj_kernel(*refs):
    (x_ref, g_ref, w_ref, qn_ref, kvn_ref, bf_ref, ck_ref, sk_tab_ref, cqT_ref, sqT_ref,
     wuqT_ref, wuqswT_ref, eqT_ref) = refs[:_N_IN_PROJ_INPUTS]
    (fq_ref, fk_ref, fv_ref, lx_ref, lg_ref, ckv_ref, kpe_ref, logf_ref, mqT_ref, sbqT_ref, sk_ref, sv_ref) = refs[-12:]
    fk_ref, fv_ref, ckv_ref, kpe_ref, logf_ref, sk_ref, sv_ref = (
        r.at[0] for r in (fk_ref, fv_ref, ckv_ref, kpe_ref, logf_ref, sk_ref, sv_ref))
    chains = _row_chains(x_ref.shape[1])
    late = []
    for rows in chains:
        hb = _rms(x_ref[0, rows, :], g_ref[...]).astype(BF16)

        def proj(name, width):
            a = _W_OFF[name]
            return _dot(hb, w_ref[:, a:a + width])

        fq_ref[0, rows, :] = (proj('fq', 256) * (HEAD_DIM ** -0.5 * LOG2E)).astype(BF16)
        fk_ref[0, rows, :] = proj('fk', 256)
        fv_ref[0, rows, :] = proj('fv', 256)
        lx_ref[0, rows, :] = proj('lx', 256)
        lg_ref[0, rows, :] = proj('lg', 256)
        cqn = _rms(proj('cq', 256), qn_ref[...]).astype(BF16)
        ckv_ref[0, rows, :] = _rms(proj('ckv', 128), kvn_ref[...])
        kpe_ref[0, rows, :] = proj('kpe', 128) * ck_ref[rows, :] + proj('kpe_sw', 128) * sk_tab_ref[rows, :]

        ff = proj('ff', 128) + bf_ref[...]
        log_sig = jnp.minimum(ff, 0.0) - jnp.log(1.0 + jnp.exp(-jnp.abs(ff)))
        lane = lax.broadcasted_iota(jnp.int32, ff.shape, 1)
        logf_ref[0, rows, :] = jnp.where(lane < N_HEADS, log_sig, 0.0)

        sq = (proj('sq', 256) * (HEAD_DIM ** -0.5 * LOG2E)).astype(BF16)
        sk_ref[0, rows, :] = proj('sk', 256)
        sv_ref[0, rows, :] = proj('sv', 256)
        late.append((rows, cqn, sq))

    for rows, cqn, sq in late:
        q_t = _dot_nt(wuqT_ref[...], cqn)
        qsw_t = _dot_nt(wuqswT_ref[...], cqn)
        mqT_ref[0, :, rows] = (q_t * cqT_ref[:, rows] + qsw_t * sqT_ref[:, rows]).astype(BF16)
        sbqT_ref[0, :, rows] = _dot_nt(eqT_ref[...], sq).astype(BF16)


def _in_proj(x, lw, tabs, layer, depth, earlier):
    b, t, d = x.shape
    tm = _pick(t, (512, 256, 128))
    grid = (b, t // tm)
    row = lambda w: pl.BlockSpec((1, tm, w), lambda i, j: (i, j, 0))
    srow = lambda w: pl.BlockSpec((1, 1, tm, w), lambda i, j: (layer, i, j, 0))
    colT = pl.BlockSpec((1, QW, tm), lambda i, j: (i, 0, j))
    in_specs = [
        row(d), _const_spec((1, d)), _const_spec((d, _W_MIX_COLS)),
        _const_spec((1, MLA_Q_RANK)), _const_spec((1, MLA_KV_RANK)), _const_spec((1, LANE)),
        pl.BlockSpec((tm, LANE), lambda i, j: (j, 0)), pl.BlockSpec((tm, LANE), lambda i, j: (j, 0)),
        pl.BlockSpec((QW, tm), lambda i, j: (0, j)), pl.BlockSpec((QW, tm), lambda i, j: (0, j)),
        _const_spec((QW, MLA_Q_RANK)), _const_spec((QW, MLA_Q_RANK)), _const_spec((QW, 256)),
    ]
    sds = jax.ShapeDtypeStruct
    state = lambda w: sds((depth, b, t, w), F32)
    out_shape = [
        sds((b, t, 256), BF16),
        state(256), state(256),
        sds((b, t, 256), F32), sds((b, t, 256), F32),
        state(128), state(LANE), state(LANE),
        sds((b, QW, t), BF16), sds((b, QW, t), BF16),
        state(256), state(256),
    ]
    out_specs = [row(256), srow(256), srow(256), row(256), row(256), srow(128), srow(LANE), srow(LANE),
                 colT, colT, srow(256), srow(256)]
    assert len(in_specs) == _N_IN_PROJ_INPUTS
    aliases = {}
    if earlier is not None:
        in_specs += [pl.BlockSpec(memory_space=pl.ANY)] * len(earlier)
        aliases = {_N_IN_PROJ_INPUTS + n: out for n, out in enumerate(_IN_PROJ_STATE_OUTPUTS)}
    return pl.pallas_call(
        _in_proj_kernel, grid=grid, in_specs=in_specs, out_specs=out_specs, out_shape=out_shape,
        input_output_aliases=aliases, compiler_params=_params("parallel", "parallel"), name="in_proj",
    )(x, lw['ln_mix_pre'], lw['w_mix'], lw['mla_q_norm'], lw['mla_kv_norm'], lw['fox_bf'],
      tabs['ck'], tabs['sk'], tabs['cqT'], tabs['sqT'], lw['wuqT'], lw['wuqswT'], lw['eqT'], *(earlier or ()))


def _cumsum_rows(x):
    n = x.shape[0]
    row = lax.broadcasted_iota(jnp.int32, x.shape, 0)
    d = 1
    while d < n:
        x = x + jnp.where(row >= d, pltpu.roll(x, d, 0), 0.0)
        d *= 2
    return x


def _kv_prep_kernel(fq_ref, fk_ref, fv_ref, logf_ref, ckv_ref, kpe_ref, sk_ref, sv_ref,
                    ek_ref, evT_ref, eqT_ref, fkmap_ref, fqmapT_ref, onesk_ref, onesq_ref,
                    wuk_ref, ekpe_ref, wuvT_ref, seg_ref, segslot_ref,
                    fqT_out, fk_out, fvT_out, mk_out, mvT_out, sbk_out, sbvT_out, cend_out, kn_out, mkn_out,
                    carry_ref, *, tk):
    fk_ref, fv_ref, logf_ref, ckv_ref, kpe_ref, sk_ref, sv_ref = (
        r.at[0] for r in (fk_ref, fv_ref, logf_ref, ckv_ref, kpe_ref, sk_ref, sv_ref))

    @pl.when(pl.program_id(1) == 0)
    def _():
        carry_ref[...] = jnp.zeros_like(carry_ref)
        kn_out[...] = jnp.zeros_like(kn_out)
        mkn_out[...] = jnp.zeros_like(mkn_out)

    ts = fk_ref.shape[1]
    c = _cumsum_rows(logf_ref[0]) + carry_ref[...]
    carry_ref[...] = c[ts - 1:ts, :]
    c = c * LOG2E
    for j in range(ts // tk):
        cend_out[0, j] = c[(j + 1) * tk - 1:(j + 1) * tk, :]

    def norm_bound(x, seg):
        xf = x.astype(F32)
        sums = _dot((xf * xf).astype(BF16), seg) * 1.01
        return jnp.max(sums, axis=0, keepdims=True)

    kn_out[0] = jnp.maximum(kn_out[0], norm_bound(fk_ref[0].astype(BF16), seg_ref[...]))
    c_hi = c.astype(BF16).astype(F32)
    rem = c - c_hi
    c_mid = rem.astype(BF16).astype(F32)
    c_lo = (rem - c_mid).astype(BF16).astype(F32)
    feat = (c_hi + pltpu.roll(c_mid, N_HEADS, 1) + pltpu.roll(c_lo, 2 * N_HEADS, 1)).astype(BF16)

    def store_vt(out, v_t):
        for j in range(ts // tk):
            out[0, j] = v_t[:, j * tk:(j + 1) * tk]

    fk_out[0] = (_dot(fk_ref[0].astype(BF16), ek_ref[...]) + _dot(feat, fkmap_ref[...]) + onesk_ref[...]).astype(BF16)
    fqT_out[0] = (_dot_nt(eqT_ref[...], fq_ref[0]) + _dot_nt(fqmapT_ref[...], feat) + onesq_ref[...]).astype(BF16)
    store_vt(fvT_out, _dot_nt(evT_ref[...], fv_ref[0].astype(BF16)).astype(BF16))

    ckv = ckv_ref[0].astype(BF16)
    mk = (_dot(ckv, wuk_ref[...]) + _dot(kpe_ref[0].astype(BF16), ekpe_ref[...])).astype(BF16)
    mk_out[0] = mk
    mkn_out[0] = jnp.maximum(mkn_out[0], norm_bound(mk, segslot_ref[...]))
    store_vt(mvT_out, _dot_nt(wuvT_ref[...], ckv).astype(BF16))

    sbk_out[0] = _dot(sk_ref[0].astype(BF16), ek_ref[...]).astype(BF16)
    store_vt(sbvT_out, _dot_nt(evT_ref[...], sv_ref[0].astype(BF16)).astype(BF16))


def _kv_prep(fq, fk, fv, logf, ckv, kpe, sk, sv, layer, lw, cst, tk):
    _, b, s, _ = fk.shape
    ts = _pick(s, (1024, 768, 512, 256))
    grid = (b, s // ts)
    row = lambda w: pl.BlockSpec((1, ts, w), lambda i, j: (i, j, 0))
    srow = lambda w: pl.BlockSpec((1, 1, ts, w), lambda i, j: (layer, i, j, 0))
    colT = pl.BlockSpec((1, QW, ts), lambda i, j: (i, 0, j))
    vT = pl.BlockSpec((1, ts // tk, VW, tk), lambda i, j: (i, j, 0, 0))
    in_specs = [row(256), srow(256), srow(256), srow(LANE), srow(128), srow(LANE), srow(256), srow(256),
                _const_spec((256, QW)), _const_spec((VW, VW)), _const_spec((QW, 256)),
                _const_spec((LANE, QW)), _const_spec((QW, LANE)), _const_spec((1, QW)), _const_spec((QW, 1)),
                _const_spec((MLA_KV_RANK, QW)), _const_spec((LANE, QW)), _const_spec((VW, MLA_KV_RANK)),
                _const_spec((VW, LANE)), _const_spec((QW, LANE))]
    sds = jax.ShapeDtypeStruct
    slab = sds((b, s, QW), BF16)
    slab_t = sds((b, s // tk, VW, tk), BF16)
    stat = sds((b, 1, LANE), F32)
    stat_spec = pl.BlockSpec((1, 1, LANE), lambda i, j: (i, 0, 0))
    out_shape = [sds((b, QW, s), BF16), slab, slab_t, slab, slab_t, slab, slab_t,
                 sds((b, s // tk, 1, LANE), F32), stat, stat]
    out_specs = [colT, row(QW), vT, row(QW), vT, row(QW), vT,
                 pl.BlockSpec((1, ts // tk, 1, LANE), lambda i, j: (i, j, 0, 0)), stat_spec, stat_spec]
    return pl.pallas_call(
        functools.partial(_kv_prep_kernel, tk=tk), grid=grid, in_specs=in_specs, out_specs=out_specs,
        out_shape=out_shape, scratch_shapes=[pltpu.VMEM((1, LANE), F32)],
        compiler_params=_params("parallel", "arbitrary"), name="kv_prep",
    )(fq, fk, fv, logf, ckv, kpe, sk, sv, cst['ek'], cst['evT'], lw['eqT'], cst['fkmap'], cst['fqmapT'],
      cst['ones_k'], cst['ones_qT'], lw['wuk'], cst['ekpe'], lw['wuvT'], cst['seg'], cst['seg_slot'])


def _head_rows(h):
    return slice(h * HP, (h + 1) * HP)


def _value_rows(h):
    return slice(h * HEAD_DIM, (h + 1) * HEAD_DIM)


def _load_kv(k_ref, vT_ref, kb, tk, h):
    off = pl.multiple_of(kb * tk, tk)
    return k_ref[0, pl.ds(off, tk), _head_rows(h)], vT_ref[0, kb, _value_rows(h), :]


def _finish(accs, invs, eye_ref, o_ref):
    for pair in range(N_HEADS // 2):
        o_t = jnp.concatenate([accs[2 * pair] * invs[2 * pair], accs[2 * pair + 1] * invs[2 * pair + 1]], axis=0)
        o_ref[0, :, _head_rows(pair)] = _dot_nt(eye_ref[...], o_t.astype(BF16)).astype(BF16)


def _softmax_attn_kernel(qT_ref, k_ref, vT_ref, eye_ref, o_ref, *, mode, past, tq, tk, s_valid):
    qi = pl.program_id(1)
    ltk = int(math.log2(tk))
    q_lo = past + qi * tq
    q_hi = q_lo + (tq - 1)
    if mode == 'fox':
        n_full = (q_lo + 1) >> ltk
        n_blk = (q_hi >> ltk) + 1
    else:
        lim_lo = ((q_lo // CHUNK) + 1) * CHUNK
        lim_hi = jnp.minimum(((q_hi // CHUNK) + 1) * CHUNK, s_valid)
        n_blk = (lim_hi + (tk - 1)) >> ltk
        n_full = jnp.minimum(lim_lo >> ltk, n_blk)
    qpos = q_lo + lax.broadcasted_iota(jnp.int32, (1, tq), 1)

    def step(kb, carry, masked):
        off = pl.multiple_of(kb * tk, tk)
        scores = [_dot(k_ref[0, pl.ds(off, tk), _head_rows(h)], qT_ref[0, _head_rows(h), :]) for h in range(N_HEADS)]
        if masked:
            kpos = kb * tk + lax.broadcasted_iota(jnp.int32, (tk, 1), 0)
            if mode == 'fox':
                vis = kpos <= qpos
            else:
                vis = jnp.logical_and((kpos // CHUNK) <= (qpos // CHUNK), kpos < s_valid)
        out = []
        for h in range(N_HEADS):
            m, l, acc = carry[h]
            s = jnp.where(vis, scores[h], NEG_INF) if masked else scores[h]
            m_new = jnp.maximum(m, jnp.max(s, axis=0, keepdims=True))
            p = jnp.exp2(s - m_new)
            alpha = jnp.exp2(m - m_new)
            l = alpha * l + jnp.sum(p, axis=0, keepdims=True)
            acc = alpha * acc + _dot(vT_ref[0, kb, _value_rows(h), :], p.astype(BF16))
            out.append((m_new, l, acc))
        return tuple(out)

    init = (jnp.full((1, tq), NEG_INF, F32), jnp.zeros((1, tq), F32), jnp.zeros((HEAD_DIM, tq), F32))
    carry = lax.fori_loop(0, n_full, functools.partial(step, masked=False), (init,) * N_HEADS)
    carry = lax.fori_loop(n_full, n_blk, functools.partial(step, masked=True), carry)
    _finish([c[2] for c in carry], [1.0 / c[1] for c in carry], eye_ref, o_ref)


def _first_needed_pair(cend_ref, kn_ref, q_norm2, n_pairs):
    nblk = cend_ref.shape[1]
    cend = cend_ref[0]
    c_tile = cend_ref[0, pl.ds(jnp.maximum(2 * n_pairs - 1, 0), 1), :]
    head_lane = lax.broadcasted_iota(jnp.int32, (1, LANE), 1)
    q_max2 = jnp.zeros((1, LANE), F32)
    for h in range(N_HEADS):
        q_max2 = jnp.where(head_lane == h, jnp.max(q_norm2[h], axis=1, keepdims=True) * 1.01, q_max2)
    qk_bound = 2.0 * jnp.sqrt(q_max2 * kn_ref[0]) + 1.0
    lane = lax.broadcasted_iota(jnp.int32, (nblk, LANE), 1)
    blk = lax.broadcasted_iota(jnp.int32, (nblk, 1), 0)
    worst = jnp.max(jnp.where(lane < N_HEADS, qk_bound + c_tile - cend, -jnp.inf), axis=1, keepdims=True)
    is_pair_end = jnp.logical_and((blk & 1) == 1, blk < 2 * n_pairs)
    skip = jnp.logical_and(is_pair_end, worst <= -FOX_SKIP_LIMIT)
    return jnp.sum(skip.astype(jnp.int32))


def _softmax_attn_pipelined_kernel(qT_ref, k_ref, vT_ref, eye_ref, *rest, mode, past, tq, tk, s_valid):
    o_ref, sa_ref, sb_ref, acc_ref = rest[-4:]
    kn_ref = rest[-5]
    qi = pl.program_id(1)
    q_lo = past + qi * tq
    n_pairs = past // tq + qi
    feature_rows = HEAD_DIM if mode == 'fox' else HP
    q_norm2 = []
    for h in range(N_HEADS):
        q_h = qT_ref[0, h * HP:h * HP + feature_rows, :].astype(F32)
        q_norm2.append(jnp.sum(q_h * q_h, axis=0, keepdims=True))
    first_pair = _first_needed_pair(rest[0], kn_ref, q_norm2, n_pairs) if mode == 'fox' else 0
    qpos = q_lo + lax.broadcasted_iota(jnp.int32, (1, tq), 1)

    def qk(kb, dst_ref):
        off = pl.multiple_of(kb * tk, tk)
        for h in range(N_HEADS):
            dst_ref[h] = _dot(k_ref[0, pl.ds(off, tk), _head_rows(h)], qT_ref[0, _head_rows(h), :])

    def softmax_pv(kb, src_ref, stats, masked, lanes=slice(None)):
        if masked:
            kpos = kb * tk + lax.broadcasted_iota(jnp.int32, (tk, 1), 0)
            if mode == 'fox':
                vis = kpos <= qpos[:, lanes]
            else:
                vis = (kpos // CHUNK) <= (qpos[:, lanes] // CHUNK)
                if s_valid < k_ref.shape[1]:
                    vis = jnp.logical_and(vis, kpos < s_valid)
        out = []
        for h in range(N_HEADS):
            m, l = stats[h]
            s = src_ref[h, :, lanes]
            if masked:
                s = jnp.where(vis, s, NEG_INF)
            m_new = jnp.maximum(m, jnp.max(s, axis=0, keepdims=True))
            p = jnp.exp2(s - m_new)
            alpha = jnp.exp2(m - m_new)
            l = alpha * l + jnp.sum(p, axis=0, keepdims=True)
            rows = _value_rows(h)
            acc_ref[rows, lanes] = alpha * acc_ref[rows, lanes] + _dot(vT_ref[0, kb, rows, :], p.astype(BF16))
            out.append((m_new, l))
        return tuple(out)

    def fixed_pv(kb, src_ref, m_fix, sums):
        out = []
        for h in range(N_HEADS):
            p = jnp.exp2(src_ref[h] - m_fix[h])
            rows = _value_rows(h)
            acc_ref[rows, :] = acc_ref[rows, :] + _dot(vT_ref[0, kb, rows, :], p.astype(BF16))
            out.append(sums[h] + jnp.sum(p, axis=0, keepdims=True))
        return tuple(out)

    def pairs(step, carry):
        def body(j, carry):
            kb = 2 * j
            qk(kb + 1, sb_ref)
            carry = step(kb, sa_ref, carry)
            qk(kb + 2, sa_ref)
            return step(kb + 1, sb_ref, carry)
        return lax.fori_loop(first_pair, n_pairs, body, carry)

    acc_ref[...] = jnp.zeros_like(acc_ref)
    own = 2 * n_pairs
    early, late = slice(0, tk), slice(tk, tq)
    qk(own, sa_ref)
    qk(own + 1, sb_ref)
    init = ((jnp.full((1, tk), NEG_INF, F32), jnp.zeros((1, tk), F32)),) * N_HEADS
    st_early = softmax_pv(own, sa_ref, init, True, early)
    st_late = softmax_pv(own, sa_ref, init, False, late)
    qk(2 * first_pair, sa_ref)
    st_late = softmax_pv(own + 1, sb_ref, st_late, True, late)
    stats = tuple(tuple(jnp.concatenate([a, b], axis=1) for a, b in zip(st_early[h], st_late[h]))
                  for h in range(N_HEADS))

    gap = None
    for h in range(N_HEADS):
        g = jnp.sqrt(q_norm2[h] * kn_ref[0][:, h:h + 1]) * 1.01 + 0.5 - stats[h][0]
        gap = g if gap is None else jnp.maximum(gap, g)
    frozen_ok = jnp.max(gap) <= FIXED_STABILISER_LIMIT

    def run_frozen(stats):
        m_fix = [st[0] for st in stats]
        sums = pairs(lambda kb, src, sums: fixed_pv(kb, src, m_fix, sums), tuple(st[1] for st in stats))
        return tuple((m_fix[h], sums[h]) for h in range(N_HEADS))

    def run_online(stats):
        return pairs(lambda kb, src, st: softmax_pv(kb, src, st, False), stats)

    stats = lax.cond(frozen_ok, run_frozen, run_online, stats)
    _finish([acc_ref[_value_rows(h), :] for h in range(N_HEADS)], [1.0 / st[1] for st in stats], eye_ref, o_ref)


def _stick_attn_kernel(qT_ref, k_ref, vT_ref, eye_ref, later_ref, o_ref, *, past, tq, tk):
    assert tk % tq == 0 and past % tq == 0
    qi = pl.program_id(1)
    ltk = int(math.log2(tk))
    q_lo = past + qi * tq
    own = q_lo >> ltk
    qpos = q_lo + lax.broadcasted_iota(jnp.int32, (1, tq), 1)

    def block_terms(kb, masked):
        kvs = [_load_kv(k_ref, vT_ref, kb, tk, h) for h in range(N_HEADS)]
        zs = [_dot(kvs[h][0], qT_ref[0, _head_rows(h), :]) for h in range(N_HEADS)]
        if masked:
            kpos = kb * tk + lax.broadcasted_iota(jnp.int32, (tk, 1), 0)
            vis = kpos < qpos
        out = []
        for h in range(N_HEADS):
            z = zs[h]
            drop = jnp.where(z > SOFTPLUS_LINEAR_FROM, z, jnp.log2(1.0 + jnp.exp2(z)))
            if masked:
                drop = jnp.where(vis, drop, 0.0)
            later = _dot(later_ref[...], drop.astype(BF16))
            logw = z - drop - later
            if masked:
                logw = jnp.where(vis, logw, NEG_INF)
            out.append((logw, jnp.sum(drop, axis=0, keepdims=True), kvs[h][1]))
        return out

    def accumulate(terms, carry, live=None):
        out = []
        for h in range(N_HEADS):
            logw, total, v_t = terms[h]
            decay, acc = carry[h]
            a = jnp.exp2(logw - decay)
            if live is not None:
                a = jnp.where(live, a, 0.0)
                total = jnp.where(live, total, 0.0)
            out.append((decay + total, acc + _dot(v_t, a.astype(BF16))))
        return tuple(out)

    def min_decay(carry):
        d = carry[0][0]
        for h in range(1, N_HEADS):
            d = jnp.minimum(d, carry[h][0])
        return jnp.min(d)

    init = (jnp.zeros((1, tq), F32), jnp.zeros((HEAD_DIM, tq), F32))
    own_terms = block_terms(own, True)
    prev_terms = block_terms(jnp.maximum(own - 1, 0), False)
    carry = accumulate(own_terms, (init,) * N_HEADS)
    carry = accumulate(prev_terms, carry, live=own >= 1)

    def cond(state):
        kb, dmin, _ = state
        return jnp.logical_and(kb >= 0, dmin < SB_DECAY_LIMIT)

    def body(state):
        kb, _, carry = state
        carry = accumulate(block_terms(kb, False), carry)
        return kb - 1, min_decay(carry), carry

    _, _, carry = lax.while_loop(cond, body, (own - 2, min_decay(carry), carry))
    _finish([c[1] for c in carry], [1.0] * N_HEADS, eye_ref, o_ref)


def _attention(mode, q_t, k, v_t, cst, *, past, s_valid, tq, tk, skip_stats=None):
    b, _, t = q_t.shape
    s = k.shape[1]
    grid = (b, t // tq)
    kv_mode = pl.Buffered(1) if 2 * s * (QW + VW) * 2 > VMEM_LIMIT_BYTES // 2 else None
    in_specs = [pl.BlockSpec((1, QW, tq), lambda i, j: (i, 0, j)),
                pl.BlockSpec((1, s, QW), lambda i, j: (i, 0, 0), pipeline_mode=kv_mode),
                pl.BlockSpec((1, s // tk, VW, tk), lambda i, j: (i, 0, 0, 0), pipeline_mode=kv_mode),
                _const_spec((tq, tq))]
    args = [q_t, k, v_t, jnp.eye(tq, dtype=BF16)]
    scratch = []
    if mode == 'sb':
        kern = functools.partial(_stick_attn_kernel, past=past, tq=tq, tk=tk)
        in_specs.append(_const_spec((tk, tk)))
        args.append(cst['later'][tk])
    elif tq == 2 * tk and past % tq == 0:
        kern = functools.partial(_softmax_attn_pipelined_kernel, mode=mode, past=past, tq=tq, tk=tk, s_valid=s_valid)
        scratch = [pltpu.VMEM((N_HEADS, tk, tq), F32), pltpu.VMEM((N_HEADS, tk, tq), F32), pltpu.VMEM((VW, tq), F32)]
        stat_spec = pl.BlockSpec((1, 1, LANE), lambda i, j: (i, 0, 0))
        if mode == 'fox':
            cend, kn = skip_stats
            in_specs += [pl.BlockSpec((1, s // tk, LANE), lambda i, j: (i, 0, 0)), stat_spec]
            args += [cend.reshape(b, s // tk, LANE), kn]
        else:
            in_specs.append(stat_spec)
            args.append(skip_stats[0])
    else:
        kern = functools.partial(_softmax_attn_kernel, mode=mode, past=past, tq=tq, tk=tk, s_valid=s_valid)
    width = (N_HEADS // 2) * HP
    return pl.pallas_call(
        kern, grid=grid, in_specs=in_specs,
        out_specs=pl.BlockSpec((1, tq, width), lambda i, j: (i, j, 0)),
        out_shape=jax.ShapeDtypeStruct((b, t, width), BF16), scratch_shapes=scratch,
        compiler_params=_params("parallel", "parallel"), name="attn_" + mode,
    )(*args)


def _lru_kernel(lx_ref, lg_ref, cb_ref, h0_ref, cw_ref, cbias_ref, wr_ref, br_ref, wi_ref, bi_ref, lam_ref,
                y_ref, nb_ref, hl_ref, xcat_ref, h_ref, *, past, tc):
    ci = pl.program_id(1)
    keep = LRU_CONV - 1

    @pl.when(ci == 0)
    def _():
        xcat_ref[0:8, :] = jnp.zeros((8, LRU_WIDTH), F32)
        xcat_ref[8 - keep:8, :] = cb_ref[0]
        h_ref[...] = h0_ref[0]

    x = lx_ref[0]
    xcat_ref[8:8 + tc, :] = x
    xc = cbias_ref[...] + x * cw_ref[keep:keep + 1, :]
    for tap in range(keep):
        shift = keep - tap
        xc = xc + xcat_ref[8 - shift:8 - shift + tc, :] * cw_ref[tap:tap + 1, :]
    nb_ref[0] = xcat_ref[8 + tc - keep:8 + tc, :]
    xcat_ref[0:8, :] = x[tc - 8:tc, :]

    xcb = xc.astype(BF16)
    r = _sigmoid(_dot(xcb, wr_ref[...]) + br_ref[...])
    gate_in = _sigmoid(_dot(xcb, wi_ref[...]) + bi_ref[...])
    log_a = (-LRU_C) * r * _softplus(-lam_ref[...])
    row = lax.broadcasted_iota(jnp.int32, (tc, LRU_WIDTH), 0)
    reset = (past + ci * tc + row) == 0
    a = jnp.where(reset, 0.0, jnp.exp(log_a))
    y2 = 2.0 * log_a
    series = -y2 * (1.0 + y2 * (0.5 + y2 * (1.0 / 6.0 + y2 * (1.0 / 24.0 + y2 * (1.0 / 120.0)))))
    one_minus = jnp.where(y2 > -0.05, series, 1.0 - jnp.exp(y2))
    mult = jnp.where(reset, 1.0, jnp.sqrt(one_minus))
    u = mult * gate_in * xc

    d = 1
    while d < tc:
        if d < 8:
            ok = row >= d
            u = u + jnp.where(ok, a * pltpu.roll(u, d, 0), 0.0)
            a = jnp.where(ok, a * pltpu.roll(a, d, 0), a)
        else:
            u = jnp.concatenate([u[:d], u[d:] + a[d:] * u[:-d]], axis=0)
            a = jnp.concatenate([a[:d], a[d:] * a[:-d]], axis=0)
        d *= 2
    hs = a * h_ref[...] + u
    h_last = hs[tc - 1:tc, :]
    h_ref[...] = h_last
    hl_ref[0] = h_last

    g = lg_ref[0]
    gelu = 0.5 * g * (1.0 + jnp.tanh(0.7978845608028654 * (g + 0.044715 * g * g * g)))
    y_ref[0] = (hs * gelu).astype(BF16)


def _lru(lx, lg, conv_buf, h0, lw, past):
    b, t, w = lx.shape
    tc = _pick(t, (256, 128, 64, 32))
    grid = (b, t // tc)
    row = pl.BlockSpec((1, tc, w), lambda i, j: (i, j, 0))
    keep = LRU_CONV - 1
    in_specs = [row, row, pl.BlockSpec((1, keep, w), lambda i, j: (i, 0, 0)), pl.BlockSpec((1, 1, w), lambda i, j: (i, 0, 0)),
                _const_spec((LRU_CONV, w)), _const_spec((1, w)), _const_spec((w, w)), _const_spec((1, w)),
                _const_spec((w, w)), _const_spec((1, w)), _const_spec((1, w))]
    sds = jax.ShapeDtypeStruct
    return pl.pallas_call(
        functools.partial(_lru_kernel, past=past, tc=tc), grid=grid, in_specs=in_specs,
        out_specs=[row, pl.BlockSpec((1, keep, w), lambda i, j: (i, 0, 0)), pl.BlockSpec((1, 1, w), lambda i, j: (i, 0, 0))],
        out_shape=[sds((b, t, w), BF16), sds((b, keep, w), F32), sds((b, 1, w), F32)],
        scratch_shapes=[pltpu.VMEM((tc + 8, w), F32), pltpu.VMEM((1, w), F32)],
        compiler_params=_params("parallel", "arbitrary"), name="rg_lru",
    )(lx, lg, conv_buf, h0, lw['lru_conv_w'], lw['lru_conv_b'], lw['lru_wr'], lw['lru_br'], lw['lru_wi'],
      lw['lru_bi'], lw['lru_lam'])


def _row_chains(tm):
    n = 2 if tm % 512 == 0 else 1
    return [slice(i * (tm // n), (i + 1) * (tm // n)) for i in range(n)]


def _merge_kernel(x_ref, oa_ref, ob_ref, oc_ref, od_ref, g1_ref, g2_ref, wg_ref, wb_ref, wo_ref, out_ref):
    d = x_ref.shape[2]
    for rows in _row_chains(x_ref.shape[1]):
        x = x_ref[0, rows, :]
        hb = _rms(x, g1_ref[...]).astype(BF16)
        merged = None
        for n, o_ref in enumerate((oa_ref, ob_ref, oc_ref, od_ref)):
            gate = _sigmoid(_dot(hb, wg_ref[:, n * d:(n + 1) * d]))
            term = gate * _dot(o_ref[0, rows, :], wb_ref[n])
            merged = term if merged is None else merged + term
        y = _dot(merged.astype(BF16), wo_ref[...])
        out_ref[0, rows, :] = x + _rms(y, g2_ref[...])


def _merge(x, o_a, o_b, o_c, o_d, lw):
    b, t, d = x.shape
    tm = _pick(t, (512, 256, 128))
    grid = (b, t // tm)
    row = lambda w: pl.BlockSpec((1, tm, w), lambda i, j: (i, j, 0))
    bw = BRANCH_WIDTH
    single = pl.Buffered(1)
    in_specs = [row(d), row(bw), row(bw), row(bw), row(bw), _const_spec((1, d)), _const_spec((1, d)),
                pl.BlockSpec((d, N_BRANCH * d), lambda i, j: (0, 0), pipeline_mode=single),
                pl.BlockSpec((N_BRANCH, BRANCH_WIDTH, d), lambda i, j: (0, 0, 0), pipeline_mode=single),
                pl.BlockSpec((d, d), lambda i, j: (0, 0), pipeline_mode=single)]
    return pl.pallas_call(
        _merge_kernel, grid=grid, in_specs=in_specs, out_specs=row(d),
        out_shape=jax.ShapeDtypeStruct((b, t, d), F32),
        compiler_params=_params("parallel", "parallel"), name="merge",
    )(x, o_a, o_b, o_c, o_d, lw['ln_mix_pre'], lw['ln_mix_post'], lw['w_gate'], lw['w_branch'], lw['w_out'])


def _mem_kv_kernel(mem_ref, g_ref, wk_ref, wv_ref, k_ref, v_ref):
    mn = _rms(mem_ref[0], g_ref[...]).astype(BF16)
    k_ref[0] = _dot(mn, wk_ref[...])
    v_ref[0] = _dot(mn, wv_ref[...])


def _mem_kv(mem, lw):
    b, m, d = mem.shape
    w = lw['mem_wk'].shape[1]
    out = jax.ShapeDtypeStruct((b, m, w), F32)
    blk = pl.BlockSpec((1, m, w), lambda i: (i, 0, 0))
    return pl.pallas_call(
        _mem_kv_kernel, grid=(b,),
        in_specs=[pl.BlockSpec((1, m, d), lambda i: (i, 0, 0)), _const_spec((1, d)), _const_spec((d, w)), _const_spec((d, w))],
        out_specs=[blk, blk], out_shape=[out, out], compiler_params=_params("parallel"), name="mem_kv",
    )(mem, lw['mem_norm'], lw['mem_wk'], lw['mem_wv'])


def _mem_attn_kernel(x_ref, mk_ref, mv_ref, g1_ref, g2_ref, wq_ref, wo_ref, out_ref):
    x = x_ref[0]
    hb = _rms(x, g1_ref[...]).astype(BF16)
    q = (_dot(hb, wq_ref[...]) * (MEM_HEAD_DIM ** -0.5)).astype(BF16)
    heads = []
    for h in range(N_HEADS):
        sl = slice(h * MEM_HEAD_DIM, (h + 1) * MEM_HEAD_DIM)
        s = _dot_nt(q[:, sl], mk_ref[0, :, sl])
        p = jnp.exp(s - jnp.max(s, axis=-1, keepdims=True))
        inv = 1.0 / jnp.sum(p, axis=-1, keepdims=True)
        heads.append((_dot(p.astype(BF16), mv_ref[0, :, sl]) * inv).astype(BF16))
    y = _dot(jnp.concatenate(heads, axis=1), wo_ref[...])
    out_ref[0] = x + _rms(y, g2_ref[...])


def _mem_attn(x, mk, mv, lw):
    b, t, d = x.shape
    m, w = mk.shape[1], mk.shape[2]
    tm = _pick(t, (512, 256, 128))
    row = pl.BlockSpec((1, tm, d), lambda i, j: (i, j, 0))
    kv = pl.BlockSpec((1, m, w), lambda i, j: (i, 0, 0))
    return pl.pallas_call(
        _mem_attn_kernel, grid=(b, t // tm),
        in_specs=[row, kv, kv, _const_spec((1, d)), _const_spec((1, d)), _const_spec((d, w)), _const_spec((w, d))],
        out_specs=row, out_shape=jax.ShapeDtypeStruct((b, t, d), F32),
        compiler_params=_params("parallel", "parallel"), name="mem_attn",
    )(x, mk, mv, lw['ln_mem_pre'], lw['ln_mem_post'], lw['mem_wq'], lw['mem_wo'])


def _ffn_kernel(x_ref, g1_ref, g2_ref, wg_ref, wu_ref, wd_ref, out_ref):
    for rows in _row_chains(x_ref.shape[1]):
        x = x_ref[0, rows, :]
        hb = _rms(x, g1_ref[...]).astype(BF16)
        gate = _dot(hb, wg_ref[...])
        act = (gate * _sigmoid(gate) * _dot(hb, wu_ref[...])).astype(BF16)
        y = _dot(act, wd_ref[...])
        out_ref[0, rows, :] = x + _rms(y, g2_ref[...])


def _ffn(x, lw):
    b, t, d = x.shape
    f = lw['ffn_wg'].shape[1]
    tm = _pick(t, (512, 256, 128))
    row = pl.BlockSpec((1, tm, d), lambda i, j: (i, j, 0))
    single = pl.Buffered(1)
    return pl.pallas_call(
        _ffn_kernel, grid=(b, t // tm),
        in_specs=[row, _const_spec((1, d)), _const_spec((1, d)),
                  pl.BlockSpec((d, f), lambda i, j: (0, 0), pipeline_mode=single),
                  pl.BlockSpec((d, f), lambda i, j: (0, 0), pipeline_mode=single),
                  pl.BlockSpec((f, d), lambda i, j: (0, 0), pipeline_mode=single)],
        out_specs=row, out_shape=jax.ShapeDtypeStruct((b, t, d), F32),
        compiler_params=_params("parallel", "parallel"), name="ffn",
    )(x, lw['ln_ffn_pre'], lw['ln_ffn_post'], lw['ffn_wg'], lw['ffn_wu'], lw['ffn_wd'])


def _constants():
    fk, fq, ones_k, ones_q = _fox_feature_maps()
    later = {}
    for tk in (256,):
        idx = np.arange(tk)
        later[tk] = jnp.asarray((idx[None, :] > idx[:, None]).astype(np.float32), BF16)
    return dict(
        ek=jnp.asarray(_place_qk(), BF16), evT=jnp.eye(VW, dtype=BF16),
        fkmap=jnp.asarray(fk, BF16), fqmapT=jnp.asarray(fq.T, BF16),
        ones_k=jnp.asarray(ones_k, F32), ones_qT=jnp.asarray(ones_q.T, F32),
        ekpe=jnp.asarray(_place_kpe(), BF16), later=later,
        seg=jnp.asarray(np.repeat(np.eye(N_HEADS, LANE, dtype=np.float32), HEAD_DIM, axis=0), BF16),
        seg_slot=jnp.asarray(np.repeat(np.eye(N_HEADS, LANE, dtype=np.float32), HP, axis=0), BF16))


def _pad_cols(w, n):
    return jnp.pad(w, ((0, 0), (0, n - w.shape[1])))


_MIX_SIZES = (256, 256, 256, N_HEADS, LRU_WIDTH, LRU_WIDTH, MLA_Q_RANK, MLA_KV_RANK, MLA_ROPE, 256, 256, 256)


def _w_in_plan(d_model):
    offs = np.concatenate([[0], np.cumsum(_MIX_SIZES)])
    fq, fk, fv, ff, lx, lg, cq, ckv, kpe, sq, sk, sv = [np.arange(offs[i], offs[i + 1]) for i in range(len(_MIX_SIZES))]
    half = MLA_ROPE // 2

    def padded(cols):
        return np.concatenate([cols, np.full(LANE - len(cols), -1)])

    mix = [fq, fk, fv, lx, lg, cq, ckv, padded(kpe), padded(np.concatenate([kpe[half:], kpe[:half]])), padded(ff),
           sq, sk, sv]
    pieces, col = [], 0
    for src in mix:
        pieces.append((0, col, src))
        col += len(src)
    assert col == _W_MIX_COLS
    gate0 = offs[-1]
    for i in range(N_BRANCH * d_model // 256):
        pieces.append((1, i * 256, gate0 + i * 256 + np.arange(256)))
    return pieces


def _transpose_kernel(xT_ref, eye_ref, x_ref):
    x_ref[...] = _dot_nt(eye_ref[...], xT_ref[...]).astype(x_ref.dtype)


def _transpose_bf16(x_t):
    r, c = x_t.shape
    rb = _pick(r, (256, 128))
    return pl.pallas_call(
        _transpose_kernel, grid=(r // rb,),
        in_specs=[pl.BlockSpec((rb, c), lambda i: (i, 0)), _const_spec((c, c))],
        out_specs=pl.BlockSpec((c, rb), lambda i: (0, i)), out_shape=jax.ShapeDtypeStruct((c, r), BF16),
        compiler_params=_params("parallel"), name="transpose_w",
    )(x_t, jnp.eye(c, dtype=BF16))


def _repack_w_in(w_in_t):
    d_model = w_in_t.shape[1]
    rows = [[], []]
    for dst, _, src in _w_in_plan(d_model):
        valid = src[src >= 0]
        runs = np.split(valid, np.nonzero(np.diff(valid) != 1)[0] + 1)
        piece = jnp.concatenate([w_in_t[int(r[0]):int(r[-1]) + 1] for r in runs], axis=0)
        rows[dst].append(jnp.pad(piece, ((0, len(src) - len(valid)), (0, 0))))
    return tuple(_transpose_bf16(jnp.concatenate(r, axis=0).astype(BF16)) for r in rows)


def _layer_weights(l, p):
    w_mix, w_gate = _repack_w_in(jnp.transpose(p['w_in'], (2, 0, 1))[:, l, :])
    half = MLA_ROPE // 2
    qk = MLA_NOPE + MLA_ROPE
    wuq = p['mla_w_uq'][l].reshape(MLA_Q_RANK, N_HEADS, qk)
    rope = wuq[:, :, MLA_NOPE:]
    rope_sw = jnp.concatenate([rope[:, :, half:], rope[:, :, :half]], axis=2)
    wuq_p = jnp.pad(wuq, ((0, 0), (0, 0), (0, HP - qk))).reshape(MLA_Q_RANK, QW)
    wuqsw_p = jnp.pad(rope_sw, ((0, 0), (0, 0), (MLA_NOPE, HP - qk))).reshape(MLA_Q_RANK, QW)
    wuk = p['mla_w_uk'][l].reshape(MLA_KV_RANK, N_HEADS, MLA_NOPE)
    wuk_p = jnp.pad(wuk, ((0, 0), (0, 0), (0, HP - MLA_NOPE))).reshape(MLA_KV_RANK, QW)

    def block_diag(w):
        return jax.scipy.linalg.block_diag(*[w[i] for i in range(w.shape[0])]).astype(BF16)

    row = lambda v: v[l].reshape(1, -1).astype(F32)
    return dict(
        w_mix=w_mix, w_gate=w_gate,
        ln_mix_pre=row(p['ln_mix_pre']), ln_mix_post=row(p['ln_mix_post']),
        fox_bf=_pad_cols(row(p['fox_bf']), LANE),
        mla_q_norm=row(p['mla_q_norm']), mla_kv_norm=row(p['mla_kv_norm']),
        wuqT=wuq_p.T.astype(BF16), wuqswT=wuqsw_p.T.astype(BF16), wuk=wuk_p.astype(BF16), wuvT=p['mla_w_uv'][l].T.astype(BF16),
        eqT=jnp.asarray(_place_qk().T, BF16),
        lru_conv_w=p['lru_conv_w'][l].astype(F32), lru_conv_b=row(p['lru_conv_b']),
        lru_wr=block_diag(p['lru_wr'][l]), lru_br=row(p['lru_br']),
        lru_wi=block_diag(p['lru_wi'][l]), lru_bi=row(p['lru_bi']), lru_lam=row(p['lru_lam']),
        w_branch=p['w_branch'][l].astype(BF16), w_out=p['w_out'][l].astype(BF16),
        ln_mem_pre=row(p['ln_mem_pre']), ln_mem_post=row(p['ln_mem_post']), mem_norm=row(p['mem_norm']),
        mem_wq=p['mem_wq'][l].astype(BF16), mem_wk=p['mem_wk'][l].astype(BF16),
        mem_wv=p['mem_wv'][l].astype(BF16), mem_wo=p['mem_wo'][l].astype(BF16),
        ln_ffn_pre=row(p['ln_ffn_pre']), ln_ffn_post=row(p['ln_ffn_post']),
        ffn_wg=p['ffn_wg'][l].astype(BF16), ffn_wu=p['ffn_wu'][l].astype(BF16), ffn_wd=p['ffn_wd'][l].astype(BF16))


def _rope_tables(past, t):
    half = MLA_ROPE // 2
    inv = jnp.power(ROPE_BASE, -jnp.arange(half, dtype=F32) / half)
    ang = (past + jnp.arange(t, dtype=jnp.int32)).astype(F32)[:, None] * inv
    cos, sin = jnp.cos(ang), jnp.sin(ang)
    c32 = jnp.concatenate([cos, cos], axis=1)
    s32 = jnp.concatenate([-sin, sin], axis=1)
    scale = (MLA_NOPE + MLA_ROPE) ** -0.5 * LOG2E
    slot_c = jnp.concatenate([jnp.ones((t, MLA_NOPE), F32), c32, jnp.zeros((t, HP - MLA_NOPE - MLA_ROPE), F32)], axis=1)
    slot_s = jnp.concatenate([jnp.zeros((t, MLA_NOPE), F32), s32, jnp.zeros((t, HP - MLA_NOPE - MLA_ROPE), F32)], axis=1)
    return dict(ck=_pad_cols(c32, LANE), sk=_pad_cols(s32, LANE),
                cqT=(jnp.tile(slot_c, (1, N_HEADS)) * scale).T, sqT=(jnp.tile(slot_s, (1, N_HEADS)) * scale).T)


def _trunk_layer(x, past_state, mem_k, mem_v, lw, cst, layer, depth, earlier):
    b, t, _ = x.shape
    past = 0 if past_state is None else past_state['fox_k'].shape[1]
    tk = 256
    fq, fk, fv, lx, lg, ckv_n, kpe_slab, logf_slab, mla_qT, sb_qT, sk, sv = _in_proj(
        x, lw, _rope_tables(past, t), layer, depth, earlier)
    stacked = (fk, fv, ckv_n, kpe_slab, logf_slab, sk, sv)

    if past_state is None:
        s_valid = t
        keys = (fq, fk, fv, logf_slab, ckv_n, kpe_slab, sk, sv)
        key_layer = layer
        conv_buf = jnp.zeros((b, LRU_CONV - 1, LRU_WIDTH), F32)
        h0 = jnp.zeros((b, 1, LRU_WIDTH), F32)
    else:
        s_valid = past + t
        s_pad = -(-s_valid // tk) * tk

        def cat(old, new):
            old = old.reshape(b, past, -1).astype(new.dtype)
            old = jnp.pad(old, ((0, 0), (0, 0), (0, new.shape[2] - old.shape[2])))
            return jnp.pad(jnp.concatenate([old, new], axis=1), ((0, 0), (0, s_pad - s_valid), (0, 0)))

        def cat_state(old, new):
            return cat(old, new[layer])[None]

        keys = (cat(jnp.zeros((b, past, 256), BF16), fq), cat_state(past_state['fox_k'], fk),
                cat_state(past_state['fox_v'], fv), cat_state(past_state['fox_logf'], logf_slab),
                cat_state(past_state['mla_ckv'], ckv_n), cat_state(past_state['mla_kpe'], kpe_slab),
                cat_state(past_state['sb_k'], sk), cat_state(past_state['sb_v'], sv))
        key_layer = 0
        conv_buf = past_state['lru_conv'].astype(F32)
        h0 = past_state['lru_h'].reshape(b, 1, LRU_WIDTH).astype(F32)

    (fox_qT, fox_k, fox_vT, mla_k, mla_vT, sb_k, sb_vT,
     *fox_skip_stats, mla_key_norm) = _kv_prep(*keys, key_layer, lw, cst, tk)
    if past_state is not None:
        fox_qT = fox_qT[:, :, past:past + t]

    att = functools.partial(_attention, cst=cst, past=past, s_valid=s_valid, tk=tk)
    tq_softmax = _pick(t, (2 * tk, tk, 128))
    o_a = att('fox', fox_qT, fox_k, fox_vT, tq=tq_softmax, skip_stats=fox_skip_stats)
    o_c = att('mla', mla_qT, mla_k, mla_vT, tq=tq_softmax, skip_stats=[mla_key_norm])
    o_d = att('sb', sb_qT, sb_k, sb_vT, tq=_pick(t, (tk, 128)))
    o_b, lru_conv, lru_h = _lru(lx, lg, conv_buf, h0, lw, past)

    x = _merge(x, o_a, o_b, o_c, o_d, lw)
    x = _mem_attn(x, mem_k, mem_v, lw)
    x = _ffn(x, lw)

    return x, stacked, dict(lru_h=lru_h.reshape(b, LRU_WIDTH), lru_conv=lru_conv)


def _state_outputs(stacked, small):
    fk, fv, ckv_n, kpe_slab, logf_slab, sk, sv = stacked
    depth, b, t, _ = fk.shape
    heads = lambda a: a.reshape(depth, b, t, N_HEADS, HEAD_DIM)
    stk = lambda name: jnp.stack([s[name] for s in small])
    return (heads(fk), heads(fv), logf_slab[..., :N_HEADS], stk('lru_h'), stk('lru_conv'),
            ckv_n, kpe_slab[..., :MLA_ROPE], heads(sk), heads(sv))


def kernel(x_prompt, x_sample, cache_fox_k, cache_fox_v, cache_fox_logf, state_lru_h, state_lru_conv, cache_mla_ckv, cache_mla_kpe, cache_sb_k, cache_sb_v, cache_mem_k, cache_mem_v, mem_prompt, ln_mix_pre, ln_mix_post, w_in, fox_bf, lru_conv_w, lru_conv_b, lru_wr, lru_br, lru_wi, lru_bi, lru_lam, mla_q_norm, mla_w_uq, mla_kv_norm, mla_w_uk, mla_w_uv, w_branch, w_out, ln_mem_pre, ln_mem_post, mem_norm, mem_wq, mem_wk, mem_wv, mem_wo, ln_ffn_pre, ln_ffn_post, ffn_wg, ffn_wu, ffn_wd):
    params = dict(ln_mix_pre=ln_mix_pre, ln_mix_post=ln_mix_post, w_in=w_in, fox_bf=fox_bf, lru_conv_w=lru_conv_w,
                  lru_conv_b=lru_conv_b, lru_wr=lru_wr, lru_br=lru_br, lru_wi=lru_wi, lru_bi=lru_bi, lru_lam=lru_lam,
                  mla_q_norm=mla_q_norm, mla_w_uq=mla_w_uq, mla_kv_norm=mla_kv_norm, mla_w_uk=mla_w_uk,
                  mla_w_uv=mla_w_uv, w_branch=w_branch, w_out=w_out, ln_mem_pre=ln_mem_pre, ln_mem_post=ln_mem_post,
                  mem_norm=mem_norm, mem_wq=mem_wq, mem_wk=mem_wk, mem_wv=mem_wv, mem_wo=mem_wo,
                  ln_ffn_pre=ln_ffn_pre, ln_ffn_post=ln_ffn_post, ffn_wg=ffn_wg, ffn_wu=ffn_wu, ffn_wd=ffn_wd)
    depth = w_in.shape[0]
    cst = _constants()
    weights = [_layer_weights(l, params) for l in range(depth)]
    bp, mem_len = mem_prompt.shape[0], mem_prompt.shape[1]

    y_prompt, p_rows, p_small, p_mem = x_prompt, None, [], []
    for l in range(depth):
        mk, mv = _mem_kv(mem_prompt, weights[l])
        y_prompt, p_rows, small = _trunk_layer(y_prompt, None, mk.astype(BF16), mv.astype(BF16), weights[l], cst,
                                               l, depth, p_rows)
        p_small.append(small)
        p_mem.append((mk.reshape(bp, mem_len, N_HEADS, MEM_HEAD_DIM), mv.reshape(bp, mem_len, N_HEADS, MEM_HEAD_DIM)))

    y_sample, s_rows, s_small = x_sample, None, []
    bs = x_sample.shape[0]
    for l in range(depth):
        past = dict(fox_k=cache_fox_k[l], fox_v=cache_fox_v[l], fox_logf=cache_fox_logf[l], lru_h=state_lru_h[l],
                    lru_conv=state_lru_conv[l], mla_ckv=cache_mla_ckv[l], mla_kpe=cache_mla_kpe[l],
                    sb_k=cache_sb_k[l], sb_v=cache_sb_v[l])
        mk = cache_mem_k[l].reshape(bs, mem_len, -1).astype(BF16)
        mv = cache_mem_v[l].reshape(bs, mem_len, -1).astype(BF16)
        y_sample, s_rows, small = _trunk_layer(y_sample, past, mk, mv, weights[l], cst, l, depth, s_rows)
        s_small.append(small)

    mem_out = (jnp.stack([m[0] for m in p_mem]), jnp.stack([m[1] for m in p_mem]))
    return (y_prompt, y_sample) + _state_outputs(p_rows, p_small) + mem_out + _state_outputs(s_rows, s_small)
```

```python
import functools
import math

import numpy as np
import jax
import jax.numpy as jnp
from jax import lax
from jax.experimental import pallas as pl
from jax.experimental.pallas import tpu as pltpu

F32 = jnp.float32
BF16 = jnp.bfloat16

CHUNK = 64
HEAD_DIM = 64
N_HEADS = 4
LRU_WIDTH = 256
LRU_CONV = 4
LRU_C = 8.0
MLA_Q_RANK = 256
MLA_KV_RANK = 128
MLA_NOPE = 64
MLA_ROPE = 32
MLA_V = 64
ROPE_BASE = 10000.0
N_BRANCH = 4
BRANCH_WIDTH = 256
MEM_HEAD_DIM = 128
EPS = 1e-6
NEG_INF = -1e30

HP = 128
QW = N_HEADS * HP
VW = N_HEADS * HEAD_DIM
LANE = 128
VMEM_LIMIT_BYTES = 56 * 1024 * 1024
LOG2E = 1.4426950408889634
SB_DECAY_LIMIT = 127.0
FOX_SKIP_LIMIT = 150.0
FIXED_STABILISER_LIMIT = 60.0
SOFTPLUS_LINEAR_FROM = 60.0

_NT = (((1,), (1,)), ((), ()))


def _dot(a, b):
    return jnp.dot(a, b, preferred_element_type=F32)


def _dot_nt(a, b):
    return lax.dot_general(a, b, _NT, preferred_element_type=F32)


def _rms(x, g):
    ms = jnp.mean(x * x, axis=-1, keepdims=True)
    return x * lax.rsqrt(ms + EPS) * g


def _sigmoid(x):
    return 1.0 / (1.0 + jnp.exp(-x))


def _softplus(x):
    return jnp.maximum(x, 0.0) + jnp.log(1.0 + jnp.exp(-jnp.abs(x)))


def _pick(n, cands):
    for c in cands:
        if n % c == 0:
            return c
    return n


def _params(*sem):
    return pltpu.CompilerParams(dimension_semantics=sem, vmem_limit_bytes=VMEM_LIMIT_BYTES)


def _const_spec(shape):
    nd = len(shape)
    return pl.BlockSpec(shape, lambda *_: (0,) * nd)


def _place_qk():
    e = np.zeros((N_HEADS * HEAD_DIM, QW), np.float32)
    for h in range(N_HEADS):
        for j in range(HEAD_DIM):
            e[h * HEAD_DIM + j, h * HP + j] = 1.0
    return e


def _fox_feature_maps():
    fk = np.zeros((LANE, QW), np.float32)
    fq = np.zeros((LANE, QW), np.float32)
    ones_k = np.zeros((1, QW), np.float32)
    ones_q = np.zeros((1, QW), np.float32)
    for h in range(N_HEADS):
        for part in range(3):
            fq[part * N_HEADS + h, h * HP + HEAD_DIM + part] = 1.0
            fk[part * N_HEADS + h, h * HP + HEAD_DIM + 3 + part] = -1.0
            ones_k[0, h * HP + HEAD_DIM + part] = 1.0
            ones_q[0, h * HP + HEAD_DIM + 3 + part] = 1.0
    return fk, fq, ones_k, ones_q


def _place_kpe():
    e = np.zeros((LANE, QW), np.float32)
    for h in range(N_HEADS):
        for j in range(MLA_ROPE):
            e[j, h * HP + MLA_NOPE + j] = 1.0
    return e


_W_OFF = dict(fq=0, fk=256, fv=512, lx=768, lg=1024, cq=1280, ckv=1536, kpe=1664, kpe_sw=1792, ff=1920,
              sq=2048, sk=2304, sv=2560)
_W_MIX_COLS = 2816


_N_IN_PROJ_INPUTS = 13
_IN_PROJ_STATE_OUTPUTS = (1, 2, 5, 6, 7, 10, 11)


def _in_proj_kernel(*refs):
    (x_ref, g_ref, w_ref, qn_ref, kvn_ref, bf_ref, ck_ref, sk_tab_ref, cqT_ref, sqT_ref,
     wuqT_ref, wuqswT_ref, eqT_ref) = refs[:_N_IN_PROJ_INPUTS]
    (fq_ref, fk_ref, fv_ref, lx_ref, lg_ref, ckv_ref, kpe_ref, logf_ref, mqT_ref, sbqT_ref, sk_ref, sv_ref) = refs[-12:]
    fk_ref, fv_ref, ckv_ref, kpe_ref, logf_ref, sk_ref, sv_ref = (
        r.at[0] for r in (fk_ref, fv_ref, ckv_ref, kpe_ref, logf_ref, sk_ref, sv_ref))
    chains = _row_chains(x_ref.shape[1])
    late = []
    for rows in chains:
        hb = _rms(x_ref[0, rows, :], g_ref[...]).astype(BF16)

        def proj(name, width):
            a = _W_OFF[name]
            return _dot(hb, w_ref[:, a:a + width])

        fq_ref[0, rows, :] = (proj('fq', 256) * (HEAD_DIM ** -0.5 * LOG2E)).astype(BF16)
        fk_ref[0, rows, :] = proj('fk', 256)
        fv_ref[0, rows, :] = proj('fv', 256)
        lx_ref[0, rows, :] = proj('lx', 256)
        lg_ref[0, rows, :] = proj('lg', 256)
        cqn = _rms(proj('cq', 256), qn_ref[...]).astype(BF16)
        ckv_ref[0, rows, :] = _rms(proj('ckv', 128), kvn_ref[...])
        kpe_ref[0, rows, :] = proj('kpe', 128) * ck_ref[rows, :] + proj('kpe_sw', 128) * sk_tab_ref[rows, :]

        ff = proj('ff', 128) + bf_ref[...]
        log_sig = jnp.minimum(ff, 0.0) - jnp.log(1.0 + jnp.exp(-jnp.abs(ff)))
        lane = lax.broadcasted_iota(jnp.int32, ff.shape, 1)
        logf_ref[0, rows, :] = jnp.where(lane < N_HEADS, log_sig, 0.0)

        sq = (proj('sq', 256) * (HEAD_DIM ** -0.5 * LOG2E)).astype(BF16)
        sk_ref[0, rows, :] = proj('sk', 256)
        sv_ref[0, rows, :] = proj('sv', 256)
        late.append((rows, cqn, sq))

    for rows, cqn, sq in late:
        q_t = _dot_nt(wuqT_ref[...], cqn)
        qsw_t = _dot_nt(wuqswT_ref[...], cqn)
        mqT_ref[0, :, rows] = (q_t * cqT_ref[:, rows] + qsw_t * sqT_ref[:, rows]).astype(BF16)
        sbqT_ref[0, :, rows] = _dot_nt(eqT_ref[...], sq).astype(BF16)


def _in_proj(x, lw, tabs, layer, depth, earlier):
    b, t, d = x.shape
    tm = _pick(t, (512, 256, 128))
    grid = (b, t // tm)
    row = lambda w: pl.BlockSpec((1, tm, w), lambda i, j: (i, j, 0))
    srow = lambda w: pl.BlockSpec((1, 1, tm, w), lambda i, j: (layer, i, j, 0))
    colT = pl.BlockSpec((1, QW, tm), lambda i, j: (i, 0, j))
    in_specs = [
        row(d), _const_spec((1, d)), _const_spec((d, _W_MIX_COLS)),
        _const_spec((1, MLA_Q_RANK)), _const_spec((1, MLA_KV_RANK)), _const_spec((1, LANE)),
        pl.BlockSpec((tm, LANE), lambda i, j: (j, 0)), pl.BlockSpec((tm, LANE), lambda i, j: (j, 0)),
        pl.BlockSpec((QW, tm), lambda i, j: (0, j)), pl.BlockSpec((QW, tm), lambda i, j: (0, j)),
        _const_spec((QW, MLA_Q_RANK)), _const_spec((QW, MLA_Q_RANK)), _const_spec((QW, 256)),
    ]
    sds = jax.ShapeDtypeStruct
    state = lambda w: sds((depth, b, t, w), F32)
    out_shape = [
        sds((b, t, 256), BF16),
        state(256), state(256),
        sds((b, t, 256), F32), sds((b, t, 256), F32),
        state(128), state(LANE), state(LANE),
        sds((b, QW, t), BF16), sds((b, QW, t), BF16),
        state(256), state(256),
    ]
    out_specs = [row(256), srow(256), srow(256), row(256), row(256), srow(128), srow(LANE), srow(LANE),
                 colT, colT, srow(256), srow(256)]
    assert len(in_specs) == _N_IN_PROJ_INPUTS
    aliases = {}
    if earlier is not None:
        in_specs += [pl.BlockSpec(memory_space=pl.ANY)] * len(earlier)
        aliases = {_N_IN_PROJ_INPUTS + n: out for n, out in enumerate(_IN_PROJ_STATE_OUTPUTS)}
    return pl.pallas_call(
        _in_proj_kernel, grid=grid, in_specs=in_specs, out_specs=out_specs, out_shape=out_shape,
        input_output_aliases=aliases, compiler_params=_params("parallel", "parallel"), name="in_proj",
    )(x, lw['ln_mix_pre'], lw['w_mix'], lw['mla_q_norm'], lw['mla_kv_norm'], lw['fox_bf'],
      tabs['ck'], tabs['sk'], tabs['cqT'], tabs['sqT'], lw['wuqT'], lw['wuqswT'], lw['eqT'], *(earlier or ()))


def _cumsum_rows(x):
    n = x.shape[0]
    row = lax.broadcasted_iota(jnp.int32, x.shape, 0)
    d = 1
    while d < n:
        x = x + jnp.where(row >= d, pltpu.roll(x, d, 0), 0.0)
        d *= 2
    return x


def _kv_prep_kernel(fq_ref, fk_ref, fv_ref, logf_ref, ckv_ref, kpe_ref, sk_ref, sv_ref,
                    ek_ref, evT_ref, eqT_ref, fkmap_ref, fqmapT_ref, onesk_ref, onesq_ref,
                    wuk_ref, ekpe_ref, wuvT_ref, seg_ref, segslot_ref,
                    fqT_out, fk_out, fvT_out, mk_out, mvT_out, sbk_out, sbvT_out, cend_out, kn_out, mkn_out,
                    carry_ref, *, tk):
    fk_ref, fv_ref, logf_ref, ckv_ref, kpe_ref, sk_ref, sv_ref = (
        r.at[0] for r in (fk_ref, fv_ref, logf_ref, ckv_ref, kpe_ref, sk_ref, sv_ref))

    @pl.when(pl.program_id(1) == 0)
    def _():
        carry_ref[...] = jnp.zeros_like(carry_ref)
        kn_out[...] = jnp.zeros_like(kn_out)
        mkn_out[...] = jnp.zeros_like(mkn_out)

    ts = fk_ref.shape[1]
    c = _cumsum_rows(logf_ref[0]) + carry_ref[...]
    carry_ref[...] = c[ts - 1:ts, :]
    c = c * LOG2E
    for j in range(ts // tk):
        cend_out[0, j] = c[(j + 1) * tk - 1:(j + 1) * tk, :]

    def norm_bound(x, seg):
        xf = x.astype(F32)
        sums = _dot((xf * xf).astype(BF16), seg) * 1.01
        return jnp.max(sums, axis=0, keepdims=True)

    kn_out[0] = jnp.maximum(kn_out[0], norm_bound(fk_ref[0].astype(BF16), seg_ref[...]))
    c_hi = c.astype(BF16).astype(F32)
    rem = c - c_hi
    c_mid = rem.astype(BF16).astype(F32)
    c_lo = (rem - c_mid).astype(BF16).astype(F32)
    feat = (c_hi + pltpu.roll(c_mid, N_HEADS, 1) + pltpu.roll(c_lo, 2 * N_HEADS, 1)).astype(BF16)

    def store_vt(out, v_t):
        for j in range(ts // tk):
            out[0, j] = v_t[:, j * tk:(j + 1) * tk]

    fk_out[0] = (_dot(fk_ref[0].astype(BF16), ek_ref[...]) + _dot(feat, fkmap_ref[...]) + onesk_ref[...]).astype(BF16)
    fqT_out[0] = (_dot_nt(eqT_ref[...], fq_ref[0]) + _dot_nt(fqmapT_ref[...], feat) + onesq_ref[...]).astype(BF16)
    store_vt(fvT_out, _dot_nt(evT_ref[...], fv_ref[0].astype(BF16)).astype(BF16))

    ckv = ckv_ref[0].astype(BF16)
    mk = (_dot(ckv, wuk_ref[...]) + _dot(kpe_ref[0].astype(BF16), ekpe_ref[...])).astype(BF16)
    mk_out[0] = mk
    mkn_out[0] = jnp.maximum(mkn_out[0], norm_bound(mk, segslot_ref[...]))
    store_vt(mvT_out, _dot_nt(wuvT_ref[...], ckv).astype(BF16))

    sbk_out[0] = _dot(sk_ref[0].astype(BF16), ek_ref[...]).astype(BF16)
    store_vt(sbvT_out, _dot_nt(evT_ref[...], sv_ref[0].astype(BF16)).astype(BF16))


def _kv_prep(fq, fk, fv, logf, ckv, kpe, sk, sv, layer, lw, cst, tk):
    _, b, s, _ = fk.shape
    ts = _pick(s, (1024, 768, 512, 256))
    grid = (b, s // ts)
    row = lambda w: pl.BlockSpec((1, ts, w), lambda i, j: (i, j, 0))
    srow = lambda w: pl.BlockSpec((1, 1, ts, w), lambda i, j: (layer, i, j, 0))
    colT = pl.BlockSpec((1, QW, ts), lambda i, j: (i, 0, j))
    vT = pl.BlockSpec((1, ts // tk, VW, tk), lambda i, j: (i, j, 0, 0))
    in_specs = [row(256), srow(256), srow(256), srow(LANE), srow(128), srow(LANE), srow(256), srow(256),
                _const_spec((256, QW)), _const_spec((VW, VW)), _const_spec((QW, 256)),
                _const_spec((LANE, QW)), _const_spec((QW, LANE)), _const_spec((1, QW)), _const_spec((QW, 1)),
                _const_spec((MLA_KV_RANK, QW)), _const_spec((LANE, QW)), _const_spec((VW, MLA_KV_RANK)),
                _const_spec((VW, LANE)), _const_spec((QW, LANE))]
    sds = jax.ShapeDtypeStruct
    slab = sds((b, s, QW), BF16)
    slab_t = sds((b, s // tk, VW, tk), BF16)
    stat = sds((b, 1, LANE), F32)
    stat_spec = pl.BlockSpec((1, 1, LANE), lambda i, j: (i, 0, 0))
    out_shape = [sds((b, QW, s), BF16), slab, slab_t, slab, slab_t, slab, slab_t,
                 sds((b, s // tk, 1, LANE), F32), stat, stat]
    out_specs = [colT, row(QW), vT, row(QW), vT, row(QW), vT,
                 pl.BlockSpec((1, ts // tk, 1, LANE), lambda i, j: (i, j, 0, 0)), stat_spec, stat_spec]
    return pl.pallas_call(
        functools.partial(_kv_prep_kernel, tk=tk), grid=grid, in_specs=in_specs, out_specs=out_specs,
        out_shape=out_shape, scratch_shapes=[pltpu.VMEM((1, LANE), F32)],
        compiler_params=_params("parallel", "arbitrary"), name="kv_prep",
    )(fq, fk, fv, logf, ckv, kpe, sk, sv, cst['ek'], cst['evT'], lw['eqT'], cst['fkmap'], cst['fqmapT'],
      cst['ones_k'], cst['ones_qT'], lw['wuk'], cst['ekpe'], lw['wuvT'], cst['seg'], cst['seg_slot'])


def _head_rows(h):
    return slice(h * HP, (h + 1) * HP)


def _value_rows(h):
    return slice(h * HEAD_DIM, (h + 1) * HEAD_DIM)


def _load_kv(k_ref, vT_ref, kb, tk, h):
    off = pl.multiple_of(kb * tk, tk)
    return k_ref[0, pl.ds(off, tk), _head_rows(h)], vT_ref[0, kb, _value_rows(h), :]


def _finish(accs, invs, eye_ref, o_ref):
    for pair in range(N_HEADS // 2):
        o_t = jnp.concatenate([accs[2 * pair] * invs[2 * pair], accs[2 * pair + 1] * invs[2 * pair + 1]], axis=0)
        o_ref[0, :, _head_rows(pair)] = _dot_nt(eye_ref[...], o_t.astype(BF16)).astype(BF16)


def _softmax_attn_kernel(qT_ref, k_ref, vT_ref, eye_ref, o_ref, *, mode, past, tq, tk, s_valid):
    qi = pl.program_id(1)
    ltk = int(math.log2(tk))
    q_lo = past + qi * tq
    q_hi = q_lo + (tq - 1)
    if mode == 'fox':
        n_full = (q_lo + 1) >> ltk
        n_blk = (q_hi >> ltk) + 1
    else:
        lim_lo = ((q_lo // CHUNK) + 1) * CHUNK
        lim_hi = jnp.minimum(((q_hi // CHUNK) + 1) * CHUNK, s_valid)
        n_blk = (lim_hi + (tk - 1)) >> ltk
        n_full = jnp.minimum(lim_lo >> ltk, n_blk)
    qpos = q_lo + lax.broadcasted_iota(jnp.int32, (1, tq), 1)

    def step(kb, carry, masked):
        off = pl.multiple_of(kb * tk, tk)
        scores = [_dot(k_ref[0, pl.ds(off, tk), _head_rows(h)], qT_ref[0, _head_rows(h), :]) for h in range(N_HEADS)]
        if masked:
            kpos = kb * tk + lax.broadcasted_iota(jnp.int32, (tk, 1), 0)
            if mode == 'fox':
                vis = kpos <= qpos
            else:
                vis = jnp.logical_and((kpos // CHUNK) <= (qpos // CHUNK), kpos < s_valid)
        out = []
        for h in range(N_HEADS):
            m, l, acc = carry[h]
            s = jnp.where(vis, scores[h], NEG_INF) if masked else scores[h]
            m_new = jnp.maximum(m, jnp.max(s, axis=0, keepdims=True))
            p = jnp.exp2(s - m_new)
            alpha = jnp.exp2(m - m_new)
            l = alpha * l + jnp.sum(p, axis=0, keepdims=True)
            acc = alpha * acc + _dot(vT_ref[0, kb, _value_rows(h), :], p.astype(BF16))
            out.append((m_new, l, acc))
        return tuple(out)

    init = (jnp.full((1, tq), NEG_INF, F32), jnp.zeros((1, tq), F32), jnp.zeros((HEAD_DIM, tq), F32))
    carry = lax.fori_loop(0, n_full, functools.partial(step, masked=False), (init,) * N_HEADS)
    carry = lax.fori_loop(n_full, n_blk, functools.partial(step, masked=True), carry)
    _finish([c[2] for c in carry], [1.0 / c[1] for c in carry], eye_ref, o_ref)


def _first_needed_pair(cend_ref, kn_ref, q_norm2, n_pairs):
    nblk = cend_ref.shape[1]
    cend = cend_ref[0]
    c_tile = cend_ref[0, pl.ds(jnp.maximum(2 * n_pairs - 1, 0), 1), :]
    head_lane = lax.broadcasted_iota(jnp.int32, (1, LANE), 1)
    q_max2 = jnp.zeros((1, LANE), F32)
    for h in range(N_HEADS):
        q_max2 = jnp.where(head_lane == h, jnp.max(q_norm2[h], axis=1, keepdims=True) * 1.01, q_max2)
    qk_bound = 2.0 * jnp.sqrt(q_max2 * kn_ref[0]) + 1.0
    lane = lax.broadcasted_iota(jnp.int32, (nblk, LANE), 1)
    blk = lax.broadcasted_iota(jnp.int32, (nblk, 1), 0)
    worst = jnp.max(jnp.where(lane < N_HEADS, qk_bound + c_tile - cend, -jnp.inf), axis=1, keepdims=True)
    is_pair_end = jnp.logical_and((blk & 1) == 1, blk < 2 * n_pairs)
    skip = jnp.logical_and(is_pair_end, worst <= -FOX_SKIP_LIMIT)
    return jnp.sum(skip.astype(jnp.int32))


def _softmax_attn_pipelined_kernel(qT_ref, k_ref, vT_ref, eye_ref, *rest, mode, past, tq, tk, s_valid):
    o_ref, sa_ref, sb_ref, acc_ref = rest[-4:]
    kn_ref = rest[-5]
    qi = pl.program_id(1)
    q_lo = past + qi * tq
    n_pairs = past // tq + qi
    feature_rows = HEAD_DIM if mode == 'fox' else HP
    q_norm2 = []
    for h in range(N_HEADS):
        q_h = qT_ref[0, h * HP:h * HP + feature_rows, :].astype(F32)
        q_norm2.append(jnp.sum(q_h * q_h, axis=0, keepdims=True))
    first_pair = _first_needed_pair(rest[0], kn_ref, q_norm2, n_pairs) if mode == 'fox' else 0
    qpos = q_lo + lax.broadcasted_iota(jnp.int32, (1, tq), 1)

    def qk_head(kb, dst_ref, h):
        off = pl.multiple_of(kb * tk, tk)
        dst_ref[h] = _dot(k_ref[0, pl.ds(off, tk), _head_rows(h)], qT_ref[0, _head_rows(h), :])

    def qk(kb, dst_ref):
        for h in range(N_HEADS):
            qk_head(kb, dst_ref, h)

    def softmax_pv(kb, src_ref, stats, masked, lanes=slice(None)):
        if masked:
            kpos = kb * tk + lax.broadcasted_iota(jnp.int32, (tk, 1), 0)
            if mode == 'fox':
                vis = kpos <= qpos[:, lanes]
            else:
                vis = (kpos // CHUNK) <= (qpos[:, lanes] // CHUNK)
                if s_valid < k_ref.shape[1]:
                    vis = jnp.logical_and(vis, kpos < s_valid)
        out = []
        for h in range(N_HEADS):
            m, l = stats[h]
            s = src_ref[h, :, lanes]
            if masked:
                s = jnp.where(vis, s, NEG_INF)
            m_new = jnp.maximum(m, jnp.max(s, axis=0, keepdims=True))
            p = jnp.exp2(s - m_new)
            alpha = jnp.exp2(m - m_new)
            l = alpha * l + jnp.sum(p, axis=0, keepdims=True)
            rows = _value_rows(h)
            acc_ref[rows, lanes] = alpha * acc_ref[rows, lanes] + _dot(vT_ref[0, kb, rows, :], p.astype(BF16))
            out.append((m_new, l))
        return tuple(out)

    def fixed_pv_head(kb, src_ref, h, m_fix, total):
        p = jnp.exp2(src_ref[h] - m_fix)
        rows = _value_rows(h)
        acc_ref[rows, :] = acc_ref[rows, :] + _dot(vT_ref[0, kb, rows, :], p.astype(BF16))
        return total + jnp.sum(p, axis=0, keepdims=True)

    def pairs(step, carry):
        def body(j, carry):
            kb = 2 * j
            qk(kb + 1, sb_ref)
            carry = step(kb, sa_ref, carry)
            qk(kb + 2, sa_ref)
            return step(kb + 1, sb_ref, carry)
        return lax.fori_loop(first_pair, n_pairs, body, carry)

    acc_ref[...] = jnp.zeros_like(acc_ref)
    own = 2 * n_pairs
    early, late = slice(0, tk), slice(tk, tq)
    qk(own, sa_ref)
    qk(own + 1, sb_ref)
    init = ((jnp.full((1, tk), NEG_INF, F32), jnp.zeros((1, tk), F32)),) * N_HEADS
    st_early = softmax_pv(own, sa_ref, init, True, early)
    st_late = softmax_pv(own, sa_ref, init, False, late)
    qk(2 * first_pair, sa_ref)
    st_late = softmax_pv(own + 1, sb_ref, st_late, True, late)
    stats = tuple(tuple(jnp.concatenate([a, b], axis=1) for a, b in zip(st_early[h], st_late[h]))
                  for h in range(N_HEADS))

    gap = None
    for h in range(N_HEADS):
        g = jnp.sqrt(q_norm2[h] * kn_ref[0][:, h:h + 1]) * 1.01 + 0.5 - stats[h][0]
        gap = g if gap is None else jnp.maximum(gap, g)
    frozen_ok = jnp.max(gap) <= FIXED_STABILISER_LIMIT

    def run_frozen(stats):
        m_fix = [st[0] for st in stats]

        def block(kb_next, dst_ref, kb, src_ref, sums):
            out = []
            for h in range(N_HEADS):
                qk_head(kb_next, dst_ref, h)
                out.append(fixed_pv_head(kb, src_ref, h, m_fix[h], sums[h]))
            return tuple(out)

        def body(j, sums):
            kb = 2 * j
            sums = block(kb + 1, sb_ref, kb, sa_ref, sums)
            return block(kb + 2, sa_ref, kb + 1, sb_ref, sums)

        sums = lax.fori_loop(first_pair, n_pairs, body, tuple(st[1] for st in stats))
        return tuple((m_fix[h], sums[h]) for h in range(N_HEADS))

    def run_online(stats):
        return pairs(lambda kb, src, st: softmax_pv(kb, src, st, False), stats)

    stats = lax.cond(frozen_ok, run_frozen, run_online, stats)
    _finish([acc_ref[_value_rows(h), :] for h in range(N_HEADS)], [1.0 / st[1] for st in stats], eye_ref, o_ref)


def _stick_attn_kernel(qT_ref, k_ref, vT_ref, eye_ref, later_ref, o_ref, *, past, tq, tk):
    assert tk % tq == 0 and past % tq == 0
    qi = pl.program_id(1)
    ltk = int(math.log2(tk))
    q_lo = past + qi * tq
    own = q_lo >> ltk
    qpos = q_lo + lax.broadcasted_iota(jnp.int32, (1, tq), 1)

    def block_terms(kb, masked):
        kvs = [_load_kv(k_ref, vT_ref, kb, tk, h) for h in range(N_HEADS)]
        zs = [_dot(kvs[h][0], qT_ref[0, _head_rows(h), :]) for h in range(N_HEADS)]
        if masked:
            kpos = kb * tk + lax.broadcasted_iota(jnp.int32, (tk, 1), 0)
            vis = kpos < qpos
        out = []
        for h in range(N_HEADS):
            z = zs[h]
            drop = jnp.where(z > SOFTPLUS_LINEAR_FROM, z, jnp.log2(1.0 + jnp.exp2(z)))
            if masked:
                drop = jnp.where(vis, drop, 0.0)
            later = _dot(later_ref[...], drop.astype(BF16))
            logw = z - drop - later
            if masked:
                logw = jnp.where(vis, logw, NEG_INF)
            out.append((logw, jnp.sum(drop, axis=0, keepdims=True), kvs[h][1]))
        return out

    def accumulate(terms, carry, live=None):
        out = []
        for h in range(N_HEADS):
            logw, total, v_t = terms[h]
            decay, acc = carry[h]
            a = jnp.exp2(logw - decay)
            if live is not None:
                a = jnp.where(live, a, 0.0)
                total = jnp.where(live, total, 0.0)
            out.append((decay + total, acc + _dot(v_t, a.astype(BF16))))
        return tuple(out)

    def min_decay(carry):
        d = carry[0][0]
        for h in range(1, N_HEADS):
            d = jnp.minimum(d, carry[h][0])
        return jnp.min(d)

    init = (jnp.zeros((1, tq), F32), jnp.zeros((HEAD_DIM, tq), F32))
    own_terms = block_terms(own, True)
    prev_terms = block_terms(jnp.maximum(own - 1, 0), False)
    carry = accumulate(own_terms, (init,) * N_HEADS)
    carry = accumulate(prev_terms, carry, live=own >= 1)

    def cond(state):
        kb, dmin, _ = state
        return jnp.logical_and(kb >= 0, dmin < SB_DECAY_LIMIT)

    def body(state):
        kb, _, carry = state
        carry = accumulate(block_terms(kb, False), carry)
        return kb - 1, min_decay(carry), carry

    _, _, carry = lax.while_loop(cond, body, (own - 2, min_decay(carry), carry))
    _finish([c[1] for c in carry], [1.0] * N_HEADS, eye_ref, o_ref)


def _attention(mode, q_t, k, v_t, cst, *, past, s_valid, tq, tk, skip_stats=None):
    b, _, t = q_t.shape
    s = k.shape[1]
    grid = (b, t // tq)
    kv_mode = pl.Buffered(1) if 2 * s * (QW + VW) * 2 > VMEM_LIMIT_BYTES // 2 else None
    in_specs = [pl.BlockSpec((1, QW, tq), lambda i, j: (i, 0, j)),
                pl.BlockSpec((1, s, QW), lambda i, j: (i, 0, 0), pipeline_mode=kv_mode),
                pl.BlockSpec((1, s // tk, VW, tk), lambda i, j: (i, 0, 0, 0), pipeline_mode=kv_mode),
                _const_spec((tq, tq))]
    args = [q_t, k, v_t, jnp.eye(tq, dtype=BF16)]
    scratch = []
    if mode == 'sb':
        kern = functools.partial(_stick_attn_kernel, past=past, tq=tq, tk=tk)
        in_specs.append(_const_spec((tk, tk)))
        args.append(cst['later'][tk])
    elif tq == 2 * tk and past % tq == 0:
        kern = functools.partial(_softmax_attn_pipelined_kernel, mode=mode, past=past, tq=tq, tk=tk, s_valid=s_valid)
        scratch = [pltpu.VMEM((N_HEADS, tk, tq), F32), pltpu.VMEM((N_HEADS, tk, tq), F32), pltpu.VMEM((VW, tq), F32)]
        stat_spec = pl.BlockSpec((1, 1, LANE), lambda i, j: (i, 0, 0))
        if mode == 'fox':
            cend, kn = skip_stats
            in_specs += [pl.BlockSpec((1, s // tk, LANE), lambda i, j: (i, 0, 0)), stat_spec]
            args += [cend.reshape(b, s // tk, LANE), kn]
        else:
            in_specs.append(stat_spec)
            args.append(skip_stats[0])
    else:
        kern = functools.partial(_softmax_attn_kernel, mode=mode, past=past, tq=tq, tk=tk, s_valid=s_valid)
    width = (N_HEADS // 2) * HP
    return pl.pallas_call(
        kern, grid=grid, in_specs=in_specs,
        out_specs=pl.BlockSpec((1, tq, width), lambda i, j: (i, j, 0)),
        out_shape=jax.ShapeDtypeStruct((b, t, width), BF16), scratch_shapes=scratch,
        compiler_params=_params("parallel", "parallel"), name="attn_" + mode,
    )(*args)


def _lru_kernel(lx_ref, lg_ref, cb_ref, h0_ref, cw_ref, cbias_ref, wr_ref, br_ref, wi_ref, bi_ref, lam_ref,
                y_ref, nb_ref, hl_ref, xcat_ref, h_ref, *, past, tc):
    ci = pl.program_id(1)
    keep = LRU_CONV - 1

    @pl.when(ci == 0)
    def _():
        xcat_ref[0:8, :] = jnp.zeros((8, LRU_WIDTH), F32)
        xcat_ref[8 - keep:8, :] = cb_ref[0]
        h_ref[...] = h0_ref[0]

    x = lx_ref[0]
    xcat_ref[8:8 + tc, :] = x
    xc = cbias_ref[...] + x * cw_ref[keep:keep + 1, :]
    for tap in range(keep):
        shift = keep - tap
        xc = xc + xcat_ref[8 - shift:8 - shift + tc, :] * cw_ref[tap:tap + 1, :]
    nb_ref[0] = xcat_ref[8 + tc - keep:8 + tc, :]
    xcat_ref[0:8, :] = x[tc - 8:tc, :]

    xcb = xc.astype(BF16)
    r = _sigmoid(_dot(xcb, wr_ref[...]) + br_ref[...])
    gate_in = _sigmoid(_dot(xcb, wi_ref[...]) + bi_ref[...])
    log_a = (-LRU_C) * r * _softplus(-lam_ref[...])
    row = lax.broadcasted_iota(jnp.int32, (tc, LRU_WIDTH), 0)
    reset = (past + ci * tc + row) == 0
    a = jnp.where(reset, 0.0, jnp.exp(log_a))
    y2 = 2.0 * log_a
    series = -y2 * (1.0 + y2 * (0.5 + y2 * (1.0 / 6.0 + y2 * (1.0 / 24.0 + y2 * (1.0 / 120.0)))))
    one_minus = jnp.where(y2 > -0.05, series, 1.0 - jnp.exp(y2))
    mult = jnp.where(reset, 1.0, jnp.sqrt(one_minus))
    u = mult * gate_in * xc

    d = 1
    while d < tc:
        if d < 8:
            ok = row >= d
            u = u + jnp.where(ok, a * pltpu.roll(u, d, 0), 0.0)
            a = jnp.where(ok, a * pltpu.roll(a, d, 0), a)
        else:
            u = jnp.concatenate([u[:d], u[d:] + a[d:] * u[:-d]], axis=0)
            a = jnp.concatenate([a[:d], a[d:] * a[:-d]], axis=0)
        d *= 2
    hs = a * h_ref[...] + u
    h_last = hs[tc - 1:tc, :]
    h_ref[...] = h_last
    hl_ref[0] = h_last

    g = lg_ref[0]
    gelu = 0.5 * g * (1.0 + jnp.tanh(0.7978845608028654 * (g + 0.044715 * g * g * g)))
    y_ref[0] = (hs * gelu).astype(BF16)


def _lru(lx, lg, conv_buf, h0, lw, past):
    b, t, w = lx.shape
    tc = _pick(t, (256, 128, 64, 32))
    grid = (b, t // tc)
    row = pl.BlockSpec((1, tc, w), lambda i, j: (i, j, 0))
    keep = LRU_CONV - 1
    in_specs = [row, row, pl.BlockSpec((1, keep, w), lambda i, j: (i, 0, 0)), pl.BlockSpec((1, 1, w), lambda i, j: (i, 0, 0)),
                _const_spec((LRU_CONV, w)), _const_spec((1, w)), _const_spec((w, w)), _const_spec((1, w)),
                _const_spec((w, w)), _const_spec((1, w)), _const_spec((1, w))]
    sds = jax.ShapeDtypeStruct
    return pl.pallas_call(
        functools.partial(_lru_kernel, past=past, tc=tc), grid=grid, in_specs=in_specs,
        out_specs=[row, pl.BlockSpec((1, keep, w), lambda i, j: (i, 0, 0)), pl.BlockSpec((1, 1, w), lambda i, j: (i, 0, 0))],
        out_shape=[sds((b, t, w), BF16), sds((b, keep, w), F32), sds((b, 1, w), F32)],
        scratch_shapes=[pltpu.VMEM((tc + 8, w), F32), pltpu.VMEM((1, w), F32)],
        compiler_params=_params("parallel", "arbitrary"), name="rg_lru",
    )(lx, lg, conv_buf, h0, lw['lru_conv_w'], lw['lru_conv_b'], lw['lru_wr'], lw['lru_br'], lw['lru_wi'],
      lw['lru_bi'], lw['lru_lam'])


def _row_chains(tm):
    n = 2 if tm % 512 == 0 else 1
    return [slice(i * (tm // n), (i + 1) * (tm // n)) for i in range(n)]


def _merge_kernel(x_ref, oa_ref, ob_ref, oc_ref, od_ref, g1_ref, g2_ref, wg_ref, wb_ref, wo_ref, out_ref):
    d = x_ref.shape[2]
    for rows in _row_chains(x_ref.shape[1]):
        x = x_ref[0, rows, :]
        hb = _rms(x, g1_ref[...]).astype(BF16)
        merged = None
        for n, o_ref in enumerate((oa_ref, ob_ref, oc_ref, od_ref)):
            gate = _sigmoid(_dot(hb, wg_ref[:, n * d:(n + 1) * d]))
            term = gate * _dot(o_ref[0, rows, :], wb_ref[n])
            merged = term if merged is None else merged + term
        y = _dot(merged.astype(BF16), wo_ref[...])
        out_ref[0, rows, :] = x + _rms(y, g2_ref[...])


def _merge(x, o_a, o_b, o_c, o_d, lw):
    b, t, d = x.shape
    tm = _pick(t, (512, 256, 128))
    grid = (b, t // tm)
    row = lambda w: pl.BlockSpec((1, tm, w), lambda i, j: (i, j, 0))
    bw = BRANCH_WIDTH
    single = pl.Buffered(1)
    in_specs = [row(d), row(bw), row(bw), row(bw), row(bw), _const_spec((1, d)), _const_spec((1, d)),
                pl.BlockSpec((d, N_BRANCH * d), lambda i, j: (0, 0), pipeline_mode=single),
                pl.BlockSpec((N_BRANCH, BRANCH_WIDTH, d), lambda i, j: (0, 0, 0), pipeline_mode=single),
                pl.BlockSpec((d, d), lambda i, j: (0, 0), pipeline_mode=single)]
    return pl.pallas_call(
        _merge_kernel, grid=grid, in_specs=in_specs, out_specs=row(d),
        out_shape=jax.ShapeDtypeStruct((b, t, d), F32),
        compiler_params=_params("parallel", "parallel"), name="merge",
    )(x, o_a, o_b, o_c, o_d, lw['ln_mix_pre'], lw['ln_mix_post'], lw['w_gate'], lw['w_branch'], lw['w_out'])


def _mem_kv_kernel(mem_ref, g_ref, wk_ref, wv_ref, k_ref, v_ref):
    mn = _rms(mem_ref[0], g_ref[...]).astype(BF16)
    k_ref[0] = _dot(mn, wk_ref[...])
    v_ref[0] = _dot(mn, wv_ref[...])


def _mem_kv(mem, lw):
    b, m, d = mem.shape
    w = lw['mem_wk'].shape[1]
    out = jax.ShapeDtypeStruct((b, m, w), F32)
    blk = pl.BlockSpec((1, m, w), lambda i: (i, 0, 0))
    return pl.pallas_call(
        _mem_kv_kernel, grid=(b,),
        in_specs=[pl.BlockSpec((1, m, d), lambda i: (i, 0, 0)), _const_spec((1, d)), _const_spec((d, w)), _const_spec((d, w))],
        out_specs=[blk, blk], out_shape=[out, out], compiler_params=_params("parallel"), name="mem_kv",
    )(mem, lw['mem_norm'], lw['mem_wk'], lw['mem_wv'])


def _mem_attn_kernel(x_ref, mk_ref, mv_ref, g1_ref, g2_ref, wq_ref, wo_ref, out_ref):
    x = x_ref[0]
    hb = _rms(x, g1_ref[...]).astype(BF16)
    q = (_dot(hb, wq_ref[...]) * (MEM_HEAD_DIM ** -0.5)).astype(BF16)
    heads = []
    for h in range(N_HEADS):
        sl = slice(h * MEM_HEAD_DIM, (h + 1) * MEM_HEAD_DIM)
        s = _dot_nt(q[:, sl], mk_ref[0, :, sl])
        p = jnp.exp(s - jnp.max(s, axis=-1, keepdims=True))
        inv = 1.0 / jnp.sum(p, axis=-1, keepdims=True)
        heads.append((_dot(p.astype(BF16), mv_ref[0, :, sl]) * inv).astype(BF16))
    y = _dot(jnp.concatenate(heads, axis=1), wo_ref[...])
    out_ref[0] = x + _rms(y, g2_ref[...])


def _mem_attn(x, mk, mv, lw):
    b, t, d = x.shape
    m, w = mk.shape[1], mk.shape[2]
    tm = _pick(t, (512, 256, 128))
    row = pl.BlockSpec((1, tm, d), lambda i, j: (i, j, 0))
    kv = pl.BlockSpec((1, m, w), lambda i, j: (i, 0, 0))
    return pl.pallas_call(
        _mem_attn_kernel, grid=(b, t // tm),
        in_specs=[row, kv, kv, _const_spec((1, d)), _const_spec((1, d)), _const_spec((d, w)), _const_spec((w, d))],
        out_specs=row, out_shape=jax.ShapeDtypeStruct((b, t, d), F32),
        compiler_params=_params("parallel", "parallel"), name="mem_attn",
    )(x, mk, mv, lw['ln_mem_pre'], lw['ln_mem_post'], lw['mem_wq'], lw['mem_wo'])


def _ffn_kernel(x_ref, g1_ref, g2_ref, wg_ref, wu_ref, wd_ref, out_ref):
    for rows in _row_chains(x_ref.shape[1]):
        x = x_ref[0, rows, :]
        hb = _rms(x, g1_ref[...]).astype(BF16)
        gate = _dot(hb, wg_ref[...])
        act = (gate * _sigmoid(gate) * _dot(hb, wu_ref[...])).astype(BF16)
        y = _dot(act, wd_ref[...])
        out_ref[0, rows, :] = x + _rms(y, g2_ref[...])


def _ffn(x, lw):
    b, t, d = x.shape
    f = lw['ffn_wg'].shape[1]
    tm = _pick(t, (512, 256, 128))
    row = pl.BlockSpec((1, tm, d), lambda i, j: (i, j, 0))
    single = pl.Buffered(1)
    return pl.pallas_call(
        _ffn_kernel, grid=(b, t // tm),
        in_specs=[row, _const_spec((1, d)), _const_spec((1, d)),
                  pl.BlockSpec((d, f), lambda i, j: (0, 0), pipeline_mode=single),
                  pl.BlockSpec((d, f), lambda i, j: (0, 0), pipeline_mode=single),
                  pl.BlockSpec((f, d), lambda i, j: (0, 0), pipeline_mode=single)],
        out_specs=row, out_shape=jax.ShapeDtypeStruct((b, t, d), F32),
        compiler_params=_params("parallel", "parallel"), name="ffn",
    )(x, lw['ln_ffn_pre'], lw['ln_ffn_post'], lw['ffn_wg'], lw['ffn_wu'], lw['ffn_wd'])


def _constants():
    fk, fq, ones_k, ones_q = _fox_feature_maps()
    later = {}
    for tk in (256,):
        idx = np.arange(tk)
        later[tk] = jnp.asarray((idx[None, :] > idx[:, None]).astype(np.float32), BF16)
    return dict(
        ek=jnp.asarray(_place_qk(), BF16), evT=jnp.eye(VW, dtype=BF16),
        fkmap=jnp.asarray(fk, BF16), fqmapT=jnp.asarray(fq.T, BF16),
        ones_k=jnp.asarray(ones_k, F32), ones_qT=jnp.asarray(ones_q.T, F32),
        ekpe=jnp.asarray(_place_kpe(), BF16), later=later,
        seg=jnp.asarray(np.repeat(np.eye(N_HEADS, LANE, dtype=np.float32), HEAD_DIM, axis=0), BF16),
        seg_slot=jnp.asarray(np.repeat(np.eye(N_HEADS, LANE, dtype=np.float32), HP, axis=0), BF16))


def _pad_cols(w, n):
    return jnp.pad(w, ((0, 0), (0, n - w.shape[1])))


_MIX_SIZES = (256, 256, 256, N_HEADS, LRU_WIDTH, LRU_WIDTH, MLA_Q_RANK, MLA_KV_RANK, MLA_ROPE, 256, 256, 256)


def _w_in_plan(d_model):
    offs = np.concatenate([[0], np.cumsum(_MIX_SIZES)])
    fq, fk, fv, ff, lx, lg, cq, ckv, kpe, sq, sk, sv = [np.arange(offs[i], offs[i + 1]) for i in range(len(_MIX_SIZES))]
    half = MLA_ROPE // 2

    def padded(cols):
        return np.concatenate([cols, np.full(LANE - len(cols), -1)])

    mix = [fq, fk, fv, lx, lg, cq, ckv, padded(kpe), padded(np.concatenate([kpe[half:], kpe[:half]])), padded(ff),
           sq, sk, sv]
    pieces, col = [], 0
    for src in mix:
        pieces.append((0, col, src))
        col += len(src)
    assert col == _W_MIX_COLS
    gate0 = offs[-1]
    for i in range(N_BRANCH * d_model // 256):
        pieces.append((1, i * 256, gate0 + i * 256 + np.arange(256)))
    return pieces


def _transpose_kernel(xT_ref, eye_ref, x_ref):
    x_ref[...] = _dot_nt(eye_ref[...], xT_ref[...]).astype(x_ref.dtype)


def _transpose_bf16(x_t):
    r, c = x_t.shape
    rb = _pick(r, (256, 128))
    return pl.pallas_call(
        _transpose_kernel, grid=(r // rb,),
        in_specs=[pl.BlockSpec((rb, c), lambda i: (i, 0)), _const_spec((c, c))],
        out_specs=pl.BlockSpec((c, rb), lambda i: (0, i)), out_shape=jax.ShapeDtypeStruct((c, r), BF16),
        compiler_params=_params("parallel"), name="transpose_w",
    )(x_t, jnp.eye(c, dtype=BF16))


def _repack_w_in(w_in_t):
    d_model = w_in_t.shape[1]
    rows = [[], []]
    for dst, _, src in _w_in_plan(d_model):
        valid = src[src >= 0]
        runs = np.split(valid, np.nonzero(np.diff(valid) != 1)[0] + 1)
        piece = jnp.concatenate([w_in_t[int(r[0]):int(r[-1]) + 1] for r in runs], axis=0)
        rows[dst].append(jnp.pad(piece, ((0, len(src) - len(valid)), (0, 0))))
    return tuple(_transpose_bf16(jnp.concatenate(r, axis=0).astype(BF16)) for r in rows)


def _layer_weights(l, p):
    w_mix, w_gate = _repack_w_in(jnp.transpose(p['w_in'], (2, 0, 1))[:, l, :])
    half = MLA_ROPE // 2
    qk = MLA_NOPE + MLA_ROPE
    wuq = p['mla_w_uq'][l].reshape(MLA_Q_RANK, N_HEADS, qk)
    rope = wuq[:, :, MLA_NOPE:]
    rope_sw = jnp.concatenate([rope[:, :, half:], rope[:, :, :half]], axis=2)
    wuq_p = jnp.pad(wuq, ((0, 0), (0, 0), (0, HP - qk))).reshape(MLA_Q_RANK, QW)
    wuqsw_p = jnp.pad(rope_sw, ((0, 0), (0, 0), (MLA_NOPE, HP - qk))).reshape(MLA_Q_RANK, QW)
    wuk = p['mla_w_uk'][l].reshape(MLA_KV_RANK, N_HEADS, MLA_NOPE)
    wuk_p = jnp.pad(wuk, ((0, 0), (0, 0), (0, HP - MLA_NOPE))).reshape(MLA_KV_RANK, QW)

    def block_diag(w):
        return jax.scipy.linalg.block_diag(*[w[i] for i in range(w.shape[0])]).astype(BF16)

    row = lambda v: v[l].reshape(1, -1).astype(F32)
    return dict(
        w_mix=w_mix, w_gate=w_gate,
        ln_mix_pre=row(p['ln_mix_pre']), ln_mix_post=row(p['ln_mix_post']),
        fox_bf=_pad_cols(row(p['fox_bf']), LANE),
        mla_q_norm=row(p['mla_q_norm']), mla_kv_norm=row(p['mla_kv_norm']),
        wuqT=wuq_p.T.astype(BF16), wuqswT=wuqsw_p.T.astype(BF16), wuk=wuk_p.astype(BF16), wuvT=p['mla_w_uv'][l].T.astype(BF16),
        eqT=jnp.asarray(_place_qk().T, BF16),
        lru_conv_w=p['lru_conv_w'][l].astype(F32), lru_conv_b=row(p['lru_conv_b']),
        lru_wr=block_diag(p['lru_wr'][l]), lru_br=row(p['lru_br']),
        lru_wi=block_diag(p['lru_wi'][l]), lru_bi=row(p['lru_bi']), lru_lam=row(p['lru_lam']),
        w_branch=p['w_branch'][l].astype(BF16), w_out=p['w_out'][l].astype(BF16),
        ln_mem_pre=row(p['ln_mem_pre']), ln_mem_post=row(p['ln_mem_post']), mem_norm=row(p['mem_norm']),
        mem_wq=p['mem_wq'][l].astype(BF16), mem_wk=p['mem_wk'][l].astype(BF16),
        mem_wv=p['mem_wv'][l].astype(BF16), mem_wo=p['mem_wo'][l].astype(BF16),
        ln_ffn_pre=row(p['ln_ffn_pre']), ln_ffn_post=row(p['ln_ffn_post']),
        ffn_wg=p['ffn_wg'][l].astype(BF16), ffn_wu=p['ffn_wu'][l].astype(BF16), ffn_wd=p['ffn_wd'][l].astype(BF16))


def _rope_tables(past, t):
    half = MLA_ROPE // 2
    inv = jnp.power(ROPE_BASE, -jnp.arange(half, dtype=F32) / half)
    ang = (past + jnp.arange(t, dtype=jnp.int32)).astype(F32)[:, None] * inv
    cos, sin = jnp.cos(ang), jnp.sin(ang)
    c32 = jnp.concatenate([cos, cos], axis=1)
    s32 = jnp.concatenate([-sin, sin], axis=1)
    scale = (MLA_NOPE + MLA_ROPE) ** -0.5 * LOG2E
    slot_c = jnp.concatenate([jnp.ones((t, MLA_NOPE), F32), c32, jnp.zeros((t, HP - MLA_NOPE - MLA_ROPE), F32)], axis=1)
    slot_s = jnp.concatenate([jnp.zeros((t, MLA_NOPE), F32), s32, jnp.zeros((t, HP - MLA_NOPE - MLA_ROPE), F32)], axis=1)
    return dict(ck=_pad_cols(c32, LANE), sk=_pad_cols(s32, LANE),
                cqT=(jnp.tile(slot_c, (1, N_HEADS)) * scale).T, sqT=(jnp.tile(slot_s, (1, N_HEADS)) * scale).T)


def _trunk_layer(x, past_state, mem_k, mem_v, lw, cst, layer, depth, earlier):
    b, t, _ = x.shape
    past = 0 if past_state is None else past_state['fox_k'].shape[1]
    tk = 256
    fq, fk, fv, lx, lg, ckv_n, kpe_slab, logf_slab, mla_qT, sb_qT, sk, sv = _in_proj(
        x, lw, _rope_tables(past, t), layer, depth, earlier)
    stacked = (fk, fv, ckv_n, kpe_slab, logf_slab, sk, sv)

    if past_state is None:
        s_valid = t
        keys = (fq, fk, fv, logf_slab, ckv_n, kpe_slab, sk, sv)
        key_layer = layer
        conv_buf = jnp.zeros((b, LRU_CONV - 1, LRU_WIDTH), F32)
        h0 = jnp.zeros((b, 1, LRU_WIDTH), F32)
    else:
        s_valid = past + t
        s_pad = -(-s_valid // tk) * tk

        def cat(old, new):
            old = old.reshape(b, past, -1).astype(new.dtype)
            old = jnp.pad(old, ((0, 0), (0, 0), (0, new.shape[2] - old.shape[2])))
            return jnp.pad(jnp.concatenate([old, new], axis=1), ((0, 0), (0, s_pad - s_valid), (0, 0)))

        def cat_state(old, new):
            return cat(old, new[layer])[None]

        keys = (cat(jnp.zeros((b, past, 256), BF16), fq), cat_state(past_state['fox_k'], fk),
                cat_state(past_state['fox_v'], fv), cat_state(past_state['fox_logf'], logf_slab),
                cat_state(past_state['mla_ckv'], ckv_n), cat_state(past_state['mla_kpe'], kpe_slab),
                cat_state(past_state['sb_k'], sk), cat_state(past_state['sb_v'], sv))
        key_layer = 0
        conv_buf = past_state['lru_conv'].astype(F32)
        h0 = past_state['lru_h'].reshape(b, 1, LRU_WIDTH).astype(F32)

    (fox_qT, fox_k, fox_vT, mla_k, mla_vT, sb_k, sb_vT,
     *fox_skip_stats, mla_key_norm) = _kv_prep(*keys, key_layer, lw, cst, tk)
    if past_state is not None:
        fox_qT = fox_qT[:, :, past:past + t]

    att = functools.partial(_attention, cst=cst, past=past, s_valid=s_valid, tk=tk)
    tq_softmax = _pick(t, (2 * tk, tk, 128))
    o_a = att('fox', fox_qT, fox_k, fox_vT, tq=tq_softmax, skip_stats=fox_skip_stats)
    o_c = att('mla', mla_qT, mla_k, mla_vT, tq=tq_softmax, skip_stats=[mla_key_norm])
    o_d = att('sb', sb_qT, sb_k, sb_vT, tq=_pick(t, (tk, 128)))
    o_b, lru_conv, lru_h = _lru(lx, lg, conv_buf, h0, lw, past)

    x = _merge(x, o_a, o_b, o_c, o_d, lw)
    x = _mem_attn(x, mem_k, mem_v, lw)
    x = _ffn(x, lw)

    return x, stacked, dict(lru_h=lru_h.reshape(b, LRU_WIDTH), lru_conv=lru_conv)


def _state_outputs(stacked, small):
    fk, fv, ckv_n, kpe_slab, logf_slab, sk, sv = stacked
    depth, b, t, _ = fk.shape
    heads = lambda a: a.reshape(depth, b, t, N_HEADS, HEAD_DIM)
    stk = lambda name: jnp.stack([s[name] for s in small])
    return (heads(fk), heads(fv), logf_slab[..., :N_HEADS], stk('lru_h'), stk('lru_conv'),
            ckv_n, kpe_slab[..., :MLA_ROPE], heads(sk), heads(sv))


def kernel(x_prompt, x_sample, cache_fox_k, cache_fox_v, cache_fox_logf, state_lru_h, state_lru_conv, cache_mla_ckv, cache_mla_kpe, cache_sb_k, cache_sb_v, cache_mem_k, cache_mem_v, mem_prompt, ln_mix_pre, ln_mix_post, w_in, fox_bf, lru_conv_w, lru_conv_b, lru_wr, lru_br, lru_wi, lru_bi, lru_lam, mla_q_norm, mla_w_uq, mla_kv_norm, mla_w_uk, mla_w_uv, w_branch, w_out, ln_mem_pre, ln_mem_post, mem_norm, mem_wq, mem_wk, mem_wv, mem_wo, ln_ffn_pre, ln_ffn_post, ffn_wg, ffn_wu, ffn_wd):
    params = dict(ln_mix_pre=ln_mix_pre, ln_mix_post=ln_mix_post, w_in=w_in, fox_bf=fox_bf, lru_conv_w=lru_conv_w,
                  lru_conv_b=lru_conv_b, lru_wr=lru_wr, lru_br=lru_br, lru_wi=lru_wi, lru_bi=lru_bi, lru_lam=lru_lam,
                  mla_q_norm=mla_q_norm, mla_w_uq=mla_w_uq, mla_kv_norm=mla_kv_norm, mla_w_uk=mla_w_uk,
                  mla_w_uv=mla_w_uv, w_branch=w_branch, w_out=w_out, ln_mem_pre=ln_mem_pre, ln_mem_post=ln_mem_post,
                  mem_norm=mem_norm, mem_wq=mem_wq, mem_wk=mem_wk, mem_wv=mem_wv, mem_wo=mem_wo,
                  ln_ffn_pre=ln_ffn_pre, ln_ffn_post=ln_ffn_post, ffn_wg=ffn_wg, ffn_wu=ffn_wu, ffn_wd=ffn_wd)
    depth = w_in.shape[0]
    cst = _constants()
    weights = [_layer_weights(l, params) for l in range(depth)]
    bp, mem_len = mem_prompt.shape[0], mem_prompt.shape[1]

    y_prompt, p_rows, p_small, p_mem = x_prompt, None, [], []
    for l in range(depth):
        mk, mv = _mem_kv(mem_prompt, weights[l])
        y_prompt, p_rows, small = _trunk_layer(y_prompt, None, mk.astype(BF16), mv.astype(BF16), weights[l], cst,
                                               l, depth, p_rows)
        p_small.append(small)
        p_mem.append((mk.reshape(bp, mem_len, N_HEADS, MEM_HEAD_DIM), mv.reshape(bp, mem_len, N_HEADS, MEM_HEAD_DIM)))

    y_sample, s_rows, s_small = x_sample, None, []
    bs = x_sample.shape[0]
    for l in range(depth):
        past = dict(fox_k=cache_fox_k[l], fox_v=cache_fox_v[l], fox_logf=cache_fox_logf[l], lru_h=state_lru_h[l],
                    lru_conv=state_lru_conv[l], mla_ckv=cache_mla_ckv[l], mla_kpe=cache_mla_kpe[l],
                    sb_k=cache_sb_k[l], sb_v=cache_sb_v[l])
        mk = cache_mem_k[l].reshape(bs, mem_len, -1).astype(BF16)
        mv = cache_mem_v[l].reshape(bs, mem_len, -1).astype(BF16)
        y_sample, s_rows, small = _trunk_layer(y_sample, past, mk, mv, weights[l], cst, l, depth, s_rows)
        s_small.append(small)

    mem_out = (jnp.stack([m[0] for m in p_mem]), jnp.stack([m[1] for m in p_mem]))
    return (y_prompt, y_sample) + _state_outputs(p_rows, p_small) + mem_out + _state_outputs(s_rows, s_small)
```

```python
import functools
import math

import numpy as np
import jax
import jax.numpy as jnp
from jax import lax
from jax.experimental import pallas as pl
from jax.experimental.pallas import tpu as pltpu

F32 = jnp.float32
BF16 = jnp.bfloat16

CHUNK = 64
HEAD_DIM = 64
N_HEADS = 4
LRU_WIDTH = 256
LRU_CONV = 4
LRU_C = 8.0
MLA_Q_RANK = 256
MLA_KV_RANK = 128
MLA_NOPE = 64
MLA_ROPE = 32
MLA_V = 64
ROPE_BASE = 10000.0
N_BRANCH = 4
BRANCH_WIDTH = 256
MEM_HEAD_DIM = 128
EPS = 1e-6
NEG_INF = -1e30

HP = 128
QW = N_HEADS * HP
VW = N_HEADS * HEAD_DIM
LANE = 128
VMEM_LIMIT_BYTES = 56 * 1024 * 1024
LOG2E = 1.4426950408889634
SB_DECAY_LIMIT = 127.0
FOX_SKIP_LIMIT = 150.0
FIXED_STABILISER_LIMIT = 60.0
SOFTPLUS_LINEAR_FROM = 60.0

_NT = (((1,), (1,)), ((), ()))


def _dot(a, b):
    return jnp.dot(a, b, preferred_element_type=F32)


def _dot_nt(a, b):
    return lax.dot_general(a, b, _NT, preferred_element_type=F32)


def _rms(x, g):
    ms = jnp.mean(x * x, axis=-1, keepdims=True)
    return x * lax.rsqrt(ms + EPS) * g


def _sigmoid(x):
    return 1.0 / (1.0 + jnp.exp(-x))


def _softplus(x):
    return jnp.maximum(x, 0.0) + jnp.log(1.0 + jnp.exp(-jnp.abs(x)))


def _pick(n, cands):
    for c in cands:
        if n % c == 0:
            return c
    return n


def _params(*sem):
    return pltpu.CompilerParams(dimension_semantics=sem, vmem_limit_bytes=VMEM_LIMIT_BYTES)


def _const_spec(shape):
    nd = len(shape)
    return pl.BlockSpec(shape, lambda *_: (0,) * nd)


def _place_qk():
    e = np.zeros((N_HEADS * HEAD_DIM, QW), np.float32)
    for h in range(N_HEADS):
        for j in range(HEAD_DIM):
            e[h * HEAD_DIM + j, h * HP + j] = 1.0
    return e


def _fox_feature_maps():
    fk = np.zeros((LANE, QW), np.float32)
    fq = np.zeros((LANE, QW), np.float32)
    ones_k = np.zeros((1, QW), np.float32)
    ones_q = np.zeros((1, QW), np.float32)
    for h in range(N_HEADS):
        for part in range(3):
            fq[part * N_HEADS + h, h * HP + HEAD_DIM + part] = 1.0
            fk[part * N_HEADS + h, h * HP + HEAD_DIM + 3 + part] = -1.0
            ones_k[0, h * HP + HEAD_DIM + part] = 1.0
            ones_q[0, h * HP + HEAD_DIM + 3 + part] = 1.0
    return fk, fq, ones_k, ones_q


def _place_kpe():
    e = np.zeros((LANE, QW), np.float32)
    for h in range(N_HEADS):
        for j in range(MLA_ROPE):
            e[j, h * HP + MLA_NOPE + j] = 1.0
    return e


_W_OFF = dict(fq=0, fk=256, fv=512, lx=768, lg=1024, cq=1280, ckv=1536, kpe=1664, kpe_sw=1792, ff=1920,
              sq=2048, sk=2304, sv=2560)
_W_MIX_COLS = 2816


_N_IN_PROJ_INPUTS = 13
_IN_PROJ_STATE_OUTPUTS = (1, 2, 5, 6, 7, 10, 11)


def _in_proj_kernel(*refs):
    (x_ref, g_ref, w_ref, qn_ref, kvn_ref, bf_ref, ck_ref, sk_tab_ref, cqT_ref, sqT_ref,
     wuqT_ref, wuqswT_ref, eqT_ref) = refs[:_N_IN_PROJ_INPUTS]
    (fq_ref, fk_ref, fv_ref, lx_ref, lg_ref, ckv_ref, kpe_ref, logf_ref, mqT_ref, sbqT_ref, sk_ref, sv_ref) = refs[-12:]
    fk_ref, fv_ref, ckv_ref, kpe_ref, logf_ref, sk_ref, sv_ref = (
        r.at[0] for r in (fk_ref, fv_ref, ckv_ref, kpe_ref, logf_ref, sk_ref, sv_ref))
    chains = _row_chains(x_ref.shape[1])
    late = []
    for rows in chains:
        hb = _rms(x_ref[0, rows, :], g_ref[...]).astype(BF16)

        def proj(name, width):
            a = _W_OFF[name]
            return _dot(hb, w_ref[:, a:a + width])

        fq_ref[0, rows, :] = (proj('fq', 256) * (HEAD_DIM ** -0.5 * LOG2E)).astype(BF16)
        fk_ref[0, rows, :] = proj('fk', 256)
        fv_ref[0, rows, :] = proj('fv', 256)
        lx_ref[0, rows, :] = proj('lx', 256)
        lg_ref[0, rows, :] = proj('lg', 256)
        cqn = _rms(proj('cq', 256), qn_ref[...]).astype(BF16)
        ckv_ref[0, rows, :] = _rms(proj('ckv', 128), kvn_ref[...])
        kpe_ref[0, rows, :] = proj('kpe', 128) * ck_ref[rows, :] + proj('kpe_sw', 128) * sk_tab_ref[rows, :]

        ff = proj('ff', 128) + bf_ref[...]
        log_sig = jnp.minimum(ff, 0.0) - jnp.log(1.0 + jnp.exp(-jnp.abs(ff)))
        lane = lax.broadcasted_iota(jnp.int32, ff.shape, 1)
        logf_ref[0, rows, :] = jnp.where(lane < N_HEADS, log_sig, 0.0)

        sq = (proj('sq', 256) * (HEAD_DIM ** -0.5 * LOG2E)).astype(BF16)
        sk_ref[0, rows, :] = proj('sk', 256)
        sv_ref[0, rows, :] = proj('sv', 256)
        late.append((rows, cqn, sq))

    for rows, cqn, sq in late:
        q_t = _dot_nt(wuqT_ref[...], cqn)
        qsw_t = _dot_nt(wuqswT_ref[...], cqn)
        mqT_ref[0, :, rows] = (q_t * cqT_ref[:, rows] + qsw_t * sqT_ref[:, rows]).astype(BF16)
        sbqT_ref[0, :, rows] = _dot_nt(eqT_ref[...], sq).astype(BF16)


def _in_proj(x, lw, tabs, layer, depth, earlier):
    b, t, d = x.shape
    tm = _pick(t, (512, 256, 128))
    grid = (b, t // tm)
    row = lambda w: pl.BlockSpec((1, tm, w), lambda i, j: (i, j, 0))
    srow = lambda w: pl.BlockSpec((1, 1, tm, w), lambda i, j: (layer, i, j, 0))
    colT = pl.BlockSpec((1, QW, tm), lambda i, j: (i, 0, j))
    in_specs = [
        row(d), _const_spec((1, d)), _const_spec((d, _W_MIX_COLS)),
        _const_spec((1, MLA_Q_RANK)), _const_spec((1, MLA_KV_RANK)), _const_spec((1, LANE)),
        pl.BlockSpec((tm, LANE), lambda i, j: (j, 0)), pl.BlockSpec((tm, LANE), lambda i, j: (j, 0)),
        pl.BlockSpec((QW, tm), lambda i, j: (0, j)), pl.BlockSpec((QW, tm), lambda i, j: (0, j)),
        _const_spec((QW, MLA_Q_RANK)), _const_spec((QW, MLA_Q_RANK)), _const_spec((QW, 256)),
    ]
    sds = jax.ShapeDtypeStruct
    state = lambda w: sds((depth, b, t, w), F32)
    out_shape = [
        sds((b, t, 256), BF16),
        state(256), state(256),
        sds((b, t, 256), F32), sds((b, t, 256), F32),
        state(128), state(LANE), state(LANE),
        sds((b, QW, t), BF16), sds((b, QW, t), BF16),
        state(256), state(256),
    ]
    out_specs = [row(256), srow(256), srow(256), row(256), row(256), srow(128), srow(LANE), srow(LANE),
                 colT, colT, srow(256), srow(256)]
    assert len(in_specs) == _N_IN_PROJ_INPUTS
    aliases = {}
    if earlier is not None:
        in_specs += [pl.BlockSpec(memory_space=pl.ANY)] * len(earlier)
        aliases = {_N_IN_PROJ_INPUTS + n: out for n, out in enumerate(_IN_PROJ_STATE_OUTPUTS)}
    return pl.pallas_call(
        _in_proj_kernel, grid=grid, in_specs=in_specs, out_specs=out_specs, out_shape=out_shape,
        input_output_aliases=aliases, compiler_params=_params("parallel", "parallel"), name="in_proj",
    )(x, lw['ln_mix_pre'], lw['w_mix'], lw['mla_q_norm'], lw['mla_kv_norm'], lw['fox_bf'],
      tabs['ck'], tabs['sk'], tabs['cqT'], tabs['sqT'], lw['wuqT'], lw['wuqswT'], lw['eqT'], *(earlier or ()))


def _cumsum_rows(x):
    n = x.shape[0]
    row = lax.broadcasted_iota(jnp.int32, x.shape, 0)
    d = 1
    while d < n:
        x = x + jnp.where(row >= d, pltpu.roll(x, d, 0), 0.0)
        d *= 2
    return x


def _kv_prep_kernel(fq_ref, fk_ref, fv_ref, logf_ref, ckv_ref, kpe_ref, sk_ref, sv_ref,
                    ek_ref, evT_ref, eqT_ref, fkmap_ref, fqmapT_ref, onesk_ref, onesq_ref,
                    wuk_ref, ekpe_ref, wuvT_ref, seg_ref, segslot_ref,
                    fqT_out, fk_out, fvT_out, mk_out, mvT_out, sbk_out, sbvT_out, cend_out, kn_out, mkn_out,
                    carry_ref, *, tk):
    fk_ref, fv_ref, logf_ref, ckv_ref, kpe_ref, sk_ref, sv_ref = (
        r.at[0] for r in (fk_ref, fv_ref, logf_ref, ckv_ref, kpe_ref, sk_ref, sv_ref))

    @pl.when(pl.program_id(1) == 0)
    def _():
        carry_ref[...] = jnp.zeros_like(carry_ref)
        kn_out[...] = jnp.zeros_like(kn_out)
        mkn_out[...] = jnp.zeros_like(mkn_out)

    ts = fk_ref.shape[1]
    c = _cumsum_rows(logf_ref[0]) + carry_ref[...]
    carry_ref[...] = c[ts - 1:ts, :]
    c = c * LOG2E
    for j in range(ts // tk):
        cend_out[0, j] = c[(j + 1) * tk - 1:(j + 1) * tk, :]

    def norm_bound(x, seg):
        xf = x.astype(F32)
        sums = _dot((xf * xf).astype(BF16), seg) * 1.01
        return jnp.max(sums, axis=0, keepdims=True)

    kn_out[0] = jnp.maximum(kn_out[0], norm_bound(fk_ref[0].astype(BF16), seg_ref[...]))
    c_hi = c.astype(BF16).astype(F32)
    rem = c - c_hi
    c_mid = rem.astype(BF16).astype(F32)
    c_lo = (rem - c_mid).astype(BF16).astype(F32)
    feat = (c_hi + pltpu.roll(c_mid, N_HEADS, 1) + pltpu.roll(c_lo, 2 * N_HEADS, 1)).astype(BF16)

    def store_vt(out, v_t):
        for j in range(ts // tk):
            out[0, j] = v_t[:, j * tk:(j + 1) * tk]

    fk_out[0] = (_dot(fk_ref[0].astype(BF16), ek_ref[...]) + _dot(feat, fkmap_ref[...]) + onesk_ref[...]).astype(BF16)
    fqT_out[0] = (_dot_nt(eqT_ref[...], fq_ref[0]) + _dot_nt(fqmapT_ref[...], feat) + onesq_ref[...]).astype(BF16)
    store_vt(fvT_out, _dot_nt(evT_ref[...], fv_ref[0].astype(BF16)).astype(BF16))

    ckv = ckv_ref[0].astype(BF16)
    mk = (_dot(ckv, wuk_ref[...]) + _dot(kpe_ref[0].astype(BF16), ekpe_ref[...])).astype(BF16)
    mk_out[0] = mk
    mkn_out[0] = jnp.maximum(mkn_out[0], norm_bound(mk, segslot_ref[...]))
    store_vt(mvT_out, _dot_nt(wuvT_ref[...], ckv).astype(BF16))

    sbk_out[0] = _dot(sk_ref[0].astype(BF16), ek_ref[...]).astype(BF16)
    store_vt(sbvT_out, _dot_nt(evT_ref[...], sv_ref[0].astype(BF16)).astype(BF16))


def _kv_prep(fq, fk, fv, logf, ckv, kpe, sk, sv, layer, lw, cst, tk):
    _, b, s, _ = fk.shape
    ts = _pick(s, (1024, 768, 512, 256))
    grid = (b, s // ts)
    row = lambda w: pl.BlockSpec((1, ts, w), lambda i, j: (i, j, 0))
    srow = lambda w: pl.BlockSpec((1, 1, ts, w), lambda i, j: (layer, i, j, 0))
    colT = pl.BlockSpec((1, QW, ts), lambda i, j: (i, 0, j))
    vT = pl.BlockSpec((1, ts // tk, VW, tk), lambda i, j: (i, j, 0, 0))
    in_specs = [row(256), srow(256), srow(256), srow(LANE), srow(128), srow(LANE), srow(256), srow(256),
                _const_spec((256, QW)), _const_spec((VW, VW)), _const_spec((QW, 256)),
                _const_spec((LANE, QW)), _const_spec((QW, LANE)), _const_spec((1, QW)), _const_spec((QW, 1)),
                _const_spec((MLA_KV_RANK, QW)), _const_spec((LANE, QW)), _const_spec((VW, MLA_KV_RANK)),
                _const_spec((VW, LANE)), _const_spec((QW, LANE))]
    sds = jax.ShapeDtypeStruct
    slab = sds((b, s, QW), BF16)
    slab_t = sds((b, s // tk, VW, tk), BF16)
    stat = sds((b, 1, LANE), F32)
    stat_spec = pl.BlockSpec((1, 1, LANE), lambda i, j: (i, 0, 0))
    out_shape = [sds((b, QW, s), BF16), slab, slab_t, slab, slab_t, slab, slab_t,
                 sds((b, s // tk, 1, LANE), F32), stat, stat]
    out_specs = [colT, row(QW), vT, row(QW), vT, row(QW), vT,
                 pl.BlockSpec((1, ts // tk, 1, LANE), lambda i, j: (i, j, 0, 0)), stat_spec, stat_spec]
    return pl.pallas_call(
        functools.partial(_kv_prep_kernel, tk=tk), grid=grid, in_specs=in_specs, out_specs=out_specs,
        out_shape=out_shape, scratch_shapes=[pltpu.VMEM((1, LANE), F32)],
        compiler_params=_params("parallel", "arbitrary"), name="kv_prep",
    )(fq, fk, fv, logf, ckv, kpe, sk, sv, cst['ek'], cst['evT'], lw['eqT'], cst['fkmap'], cst['fqmapT'],
      cst['ones_k'], cst['ones_qT'], lw['wuk'], cst['ekpe'], lw['wuvT'], cst['seg'], cst['seg_slot'])


def _head_rows(h):
    return slice(h * HP, (h + 1) * HP)


def _value_rows(h):
    return slice(h * HEAD_DIM, (h + 1) * HEAD_DIM)


def _finish(accs, invs, eye_ref, o_ref):
    for pair in range(N_HEADS // 2):
        o_t = jnp.concatenate([accs[2 * pair] * invs[2 * pair], accs[2 * pair + 1] * invs[2 * pair + 1]], axis=0)
        o_ref[0, :, _head_rows(pair)] = _dot_nt(eye_ref[...], o_t.astype(BF16)).astype(BF16)


def _softmax_attn_kernel(qT_ref, k_ref, vT_ref, eye_ref, o_ref, *, mode, past, tq, tk, s_valid):
    qi = pl.program_id(1)
    ltk = int(math.log2(tk))
    q_lo = past + qi * tq
    q_hi = q_lo + (tq - 1)
    if mode == 'fox':
        n_full = (q_lo + 1) >> ltk
        n_blk = (q_hi >> ltk) + 1
    else:
        lim_lo = ((q_lo // CHUNK) + 1) * CHUNK
        lim_hi = jnp.minimum(((q_hi // CHUNK) + 1) * CHUNK, s_valid)
        n_blk = (lim_hi + (tk - 1)) >> ltk
        n_full = jnp.minimum(lim_lo >> ltk, n_blk)
    qpos = q_lo + lax.broadcasted_iota(jnp.int32, (1, tq), 1)

    def step(kb, carry, masked):
        off = pl.multiple_of(kb * tk, tk)
        scores = [_dot(k_ref[0, pl.ds(off, tk), _head_rows(h)], qT_ref[0, _head_rows(h), :]) for h in range(N_HEADS)]
        if masked:
            kpos = kb * tk + lax.broadcasted_iota(jnp.int32, (tk, 1), 0)
            if mode == 'fox':
                vis = kpos <= qpos
            else:
                vis = jnp.logical_and((kpos // CHUNK) <= (qpos // CHUNK), kpos < s_valid)
        out = []
        for h in range(N_HEADS):
            m, l, acc = carry[h]
            s = jnp.where(vis, scores[h], NEG_INF) if masked else scores[h]
            m_new = jnp.maximum(m, jnp.max(s, axis=0, keepdims=True))
            p = jnp.exp2(s - m_new)
            alpha = jnp.exp2(m - m_new)
            l = alpha * l + jnp.sum(p, axis=0, keepdims=True)
            acc = alpha * acc + _dot(vT_ref[0, kb, _value_rows(h), :], p.astype(BF16))
            out.append((m_new, l, acc))
        return tuple(out)

    init = (jnp.full((1, tq), NEG_INF, F32), jnp.zeros((1, tq), F32), jnp.zeros((HEAD_DIM, tq), F32))
    carry = lax.fori_loop(0, n_full, functools.partial(step, masked=False), (init,) * N_HEADS)
    carry = lax.fori_loop(n_full, n_blk, functools.partial(step, masked=True), carry)
    _finish([c[2] for c in carry], [1.0 / c[1] for c in carry], eye_ref, o_ref)


def _first_needed_pair(cend_ref, kn_ref, q_norm2, n_pairs):
    nblk = cend_ref.shape[1]
    cend = cend_ref[0]
    c_tile = cend_ref[0, pl.ds(jnp.maximum(2 * n_pairs - 1, 0), 1), :]
    head_lane = lax.broadcasted_iota(jnp.int32, (1, LANE), 1)
    q_max2 = jnp.zeros((1, LANE), F32)
    for h in range(N_HEADS):
        q_max2 = jnp.where(head_lane == h, jnp.max(q_norm2[h], axis=1, keepdims=True) * 1.01, q_max2)
    qk_bound = 2.0 * jnp.sqrt(q_max2 * kn_ref[0]) + 1.0
    lane = lax.broadcasted_iota(jnp.int32, (nblk, LANE), 1)
    blk = lax.broadcasted_iota(jnp.int32, (nblk, 1), 0)
    worst = jnp.max(jnp.where(lane < N_HEADS, qk_bound + c_tile - cend, -jnp.inf), axis=1, keepdims=True)
    is_pair_end = jnp.logical_and((blk & 1) == 1, blk < 2 * n_pairs)
    skip = jnp.logical_and(is_pair_end, worst <= -FOX_SKIP_LIMIT)
    return jnp.sum(skip.astype(jnp.int32))


def _softmax_attn_pipelined_kernel(qT_ref, k_ref, vT_ref, eye_ref, *rest, mode, past, tq, tk, s_valid):
    o_ref, sa_ref, sb_ref, acc_ref = rest[-4:]
    kn_ref = rest[-5]
    qi = pl.program_id(1)
    q_lo = past + qi * tq
    n_pairs = past // tq + qi
    feature_rows = HEAD_DIM if mode == 'fox' else HP
    q_norm2 = []
    for h in range(N_HEADS):
        q_h = qT_ref[0, h * HP:h * HP + feature_rows, :].astype(F32)
        q_norm2.append(jnp.sum(q_h * q_h, axis=0, keepdims=True))
    first_pair = _first_needed_pair(rest[0], kn_ref, q_norm2, n_pairs) if mode == 'fox' else 0
    qpos = q_lo + lax.broadcasted_iota(jnp.int32, (1, tq), 1)

    def qk_head(kb, dst_ref, h):
        off = pl.multiple_of(kb * tk, tk)
        dst_ref[h] = _dot(k_ref[0, pl.ds(off, tk), _head_rows(h)], qT_ref[0, _head_rows(h), :])

    def qk(kb, dst_ref):
        for h in range(N_HEADS):
            qk_head(kb, dst_ref, h)

    def visible(kb, lanes):
        kpos = kb * tk + lax.broadcasted_iota(jnp.int32, (tk, 1), 0)
        if mode == 'fox':
            return kpos <= qpos[:, lanes]
        vis = (kpos // CHUNK) <= (qpos[:, lanes] // CHUNK)
        if s_valid < k_ref.shape[1]:
            vis = jnp.logical_and(vis, kpos < s_valid)
        return vis

    def softmax_pv_head(kb, src_ref, h, stat, vis=None, lanes=slice(None)):
        m, l = stat
        s = src_ref[h, :, lanes]
        if vis is not None:
            s = jnp.where(vis, s, NEG_INF)
        m_new = jnp.maximum(m, jnp.max(s, axis=0, keepdims=True))
        p = jnp.exp2(s - m_new)
        alpha = jnp.exp2(m - m_new)
        l = alpha * l + jnp.sum(p, axis=0, keepdims=True)
        rows = _value_rows(h)
        acc_ref[rows, lanes] = alpha * acc_ref[rows, lanes] + _dot(vT_ref[0, kb, rows, :], p.astype(BF16))
        return m_new, l

    def softmax_pv(kb, src_ref, stats):
        return tuple(softmax_pv_head(kb, src_ref, h, stats[h]) for h in range(N_HEADS))

    def fixed_pv_head(kb, src_ref, h, m_fix, total):
        p = jnp.exp2(src_ref[h] - m_fix)
        rows = _value_rows(h)
        acc_ref[rows, :] = acc_ref[rows, :] + _dot(vT_ref[0, kb, rows, :], p.astype(BF16))
        return total + jnp.sum(p, axis=0, keepdims=True)

    def pairs(step, carry):
        def body(j, carry):
            kb = 2 * j
            qk(kb + 1, sb_ref)
            carry = step(kb, sa_ref, carry)
            qk(kb + 2, sa_ref)
            return step(kb + 1, sb_ref, carry)
        return lax.fori_loop(first_pair, n_pairs, body, carry)

    acc_ref[...] = jnp.zeros_like(acc_ref)
    own = 2 * n_pairs
    early, late = slice(0, tk), slice(tk, tq)
    init = (jnp.full((1, tk), NEG_INF, F32), jnp.zeros((1, tk), F32))
    score_order = [(own, sa_ref, h) for h in range(N_HEADS)] + [(own + 1, sb_ref, h) for h in range(N_HEADS)]
    for kb, dst_ref, h in score_order[:2]:
        qk_head(kb, dst_ref, h)
    vis = visible(own, early)
    st_early, st_late = [], []
    for h in range(N_HEADS):
        st_early.append(softmax_pv_head(own, sa_ref, h, init, vis, early))
        qk_head(*score_order[2 + h])
    for h in range(N_HEADS):
        st_late.append(softmax_pv_head(own, sa_ref, h, init, None, late))
        if 6 + h < len(score_order):
            qk_head(*score_order[6 + h])
    vis = visible(own + 1, late)
    for h in range(N_HEADS):
        qk_head(2 * first_pair, sa_ref, h)
        st_late[h] = softmax_pv_head(own + 1, sb_ref, h, st_late[h], vis, late)
    stats = tuple(tuple(jnp.concatenate([a, b], axis=1) for a, b in zip(st_early[h], st_late[h]))
                  for h in range(N_HEADS))

    gap = None
    for h in range(N_HEADS):
        g = jnp.sqrt(q_norm2[h] * kn_ref[0][:, h:h + 1]) * 1.01 + 0.5 - stats[h][0]
        gap = g if gap is None else jnp.maximum(gap, g)
    frozen_ok = jnp.max(gap) <= FIXED_STABILISER_LIMIT

    def run_frozen(stats):
        m_fix = [st[0] for st in stats]

        def block(kb_next, dst_ref, kb, src_ref, sums):
            out = []
            for h in range(N_HEADS):
                qk_head(kb_next, dst_ref, h)
                out.append(fixed_pv_head(kb, src_ref, h, m_fix[h], sums[h]))
            return tuple(out)

        def body(j, sums):
            kb = 2 * j
            sums = block(kb + 1, sb_ref, kb, sa_ref, sums)
            return block(kb + 2, sa_ref, kb + 1, sb_ref, sums)

        sums = lax.fori_loop(first_pair, n_pairs, body, tuple(st[1] for st in stats))
        return tuple((m_fix[h], sums[h]) for h in range(N_HEADS))

    def run_online(stats):
        return pairs(softmax_pv, stats)

    stats = lax.cond(frozen_ok, run_frozen, run_online, stats)
    _finish([acc_ref[_value_rows(h), :] for h in range(N_HEADS)], [1.0 / st[1] for st in stats], eye_ref, o_ref)


def _stick_attn_kernel(qT_ref, k_ref, vT_ref, eye_ref, later_ref, o_ref, *, past, tq, tk):
    assert tk % tq == 0 and past % tq == 0
    qi = pl.program_id(1)
    ltk = int(math.log2(tk))
    q_lo = past + qi * tq
    own = q_lo >> ltk
    qpos = q_lo + lax.broadcasted_iota(jnp.int32, (1, tq), 1)

    def scores(kb):
        off = pl.multiple_of(kb * tk, tk)
        return [_dot(k_ref[0, pl.ds(off, tk), _head_rows(h)], qT_ref[0, _head_rows(h), :])
                for h in range(N_HEADS)]

    def block_terms(kb, masked, zs=None):
        zs = scores(kb) if zs is None else zs
        kvs = [(None, vT_ref[0, kb, _value_rows(h), :]) for h in range(N_HEADS)]
        if masked:
            kpos = kb * tk + lax.broadcasted_iota(jnp.int32, (tk, 1), 0)
            vis = kpos < qpos
        out = []
        for h in range(N_HEADS):
            z = zs[h]
            drop = jnp.where(z > SOFTPLUS_LINEAR_FROM, z, jnp.log2(1.0 + jnp.exp2(z)))
            if masked:
                drop = jnp.where(vis, drop, 0.0)
            later = _dot(later_ref[...], drop.astype(BF16))
            logw = z - drop - later
            if masked:
                logw = jnp.where(vis, logw, NEG_INF)
            out.append((logw, jnp.sum(drop, axis=0, keepdims=True), kvs[h][1]))
        return out

    def accumulate(terms, carry, live=None):
        out = []
        for h in range(N_HEADS):
            logw, total, v_t = terms[h]
            decay, acc = carry[h]
            a = jnp.exp2(logw - decay)
            if live is not None:
                a = jnp.where(live, a, 0.0)
                total = jnp.where(live, total, 0.0)
            out.append((decay + total, acc + _dot(v_t, a.astype(BF16))))
        return tuple(out)

    def min_decay(carry):
        d = carry[0][0]
        for h in range(1, N_HEADS):
            d = jnp.minimum(d, carry[h][0])
        return jnp.min(d)

    init = (jnp.zeros((1, tq), F32), jnp.zeros((HEAD_DIM, tq), F32))
    prev = jnp.maximum(own - 1, 0)
    z_own, z_prev = scores(own), scores(prev)
    own_terms = block_terms(own, True, z_own)
    prev_terms = block_terms(prev, False, z_prev)
    carry = accumulate(own_terms, (init,) * N_HEADS)
    carry = accumulate(prev_terms, carry, live=own >= 1)

    def cond(state):
        kb, dmin, _ = state
        return jnp.logical_and(kb >= 0, dmin < SB_DECAY_LIMIT)

    def body(state):
        kb, _, carry = state
        carry = accumulate(block_terms(kb, False), carry)
        return kb - 1, min_decay(carry), carry

    _, _, carry = lax.while_loop(cond, body, (own - 2, min_decay(carry), carry))
    _finish([c[1] for c in carry], [1.0] * N_HEADS, eye_ref, o_ref)


def _attention(mode, q_t, k, v_t, cst, *, past, s_valid, tq, tk, skip_stats=None):
    b, _, t = q_t.shape
    s = k.shape[1]
    grid = (b, t // tq)
    kv_mode = pl.Buffered(1) if 2 * s * (QW + VW) * 2 > VMEM_LIMIT_BYTES // 2 else None
    in_specs = [pl.BlockSpec((1, QW, tq), lambda i, j: (i, 0, j)),
                pl.BlockSpec((1, s, QW), lambda i, j: (i, 0, 0), pipeline_mode=kv_mode),
                pl.BlockSpec((1, s // tk, VW, tk), lambda i, j: (i, 0, 0, 0), pipeline_mode=kv_mode),
                _const_spec((tq, tq))]
    args = [q_t, k, v_t, jnp.eye(tq, dtype=BF16)]
    scratch = []
    if mode == 'sb':
        kern = functools.partial(_stick_attn_kernel, past=past, tq=tq, tk=tk)
        in_specs.append(_const_spec((tk, tk)))
        args.append(cst['later'][tk])
    elif tq == 2 * tk and past % tq == 0:
        kern = functools.partial(_softmax_attn_pipelined_kernel, mode=mode, past=past, tq=tq, tk=tk, s_valid=s_valid)
        scratch = [pltpu.VMEM((N_HEADS, tk, tq), F32), pltpu.VMEM((N_HEADS, tk, tq), F32), pltpu.VMEM((VW, tq), F32)]
        stat_spec = pl.BlockSpec((1, 1, LANE), lambda i, j: (i, 0, 0))
        if mode == 'fox':
            cend, kn = skip_stats
            in_specs += [pl.BlockSpec((1, s // tk, LANE), lambda i, j: (i, 0, 0)), stat_spec]
            args += [cend.reshape(b, s // tk, LANE), kn]
        else:
            in_specs.append(stat_spec)
            args.append(skip_stats[0])
    else:
        kern = functools.partial(_softmax_attn_kernel, mode=mode, past=past, tq=tq, tk=tk, s_valid=s_valid)
    width = (N_HEADS // 2) * HP
    return pl.pallas_call(
        kern, grid=grid, in_specs=in_specs,
        out_specs=pl.BlockSpec((1, tq, width), lambda i, j: (i, j, 0)),
        out_shape=jax.ShapeDtypeStruct((b, t, width), BF16), scratch_shapes=scratch,
        compiler_params=_params("parallel", "parallel"), name="attn_" + mode,
    )(*args)


def _lru_kernel(lx_ref, lg_ref, cb_ref, h0_ref, cw_ref, cbias_ref, wr_ref, br_ref, wi_ref, bi_ref, lam_ref,
                y_ref, nb_ref, hl_ref, xcat_ref, h_ref, *, past, tc):
    ci = pl.program_id(1)
    keep = LRU_CONV - 1

    @pl.when(ci == 0)
    def _():
        xcat_ref[0:8, :] = jnp.zeros((8, LRU_WIDTH), F32)
        xcat_ref[8 - keep:8, :] = cb_ref[0]
        h_ref[...] = h0_ref[0]

    x = lx_ref[0]
    xcat_ref[8:8 + tc, :] = x
    xc = cbias_ref[...] + x * cw_ref[keep:keep + 1, :]
    for tap in range(keep):
        shift = keep - tap
        xc = xc + xcat_ref[8 - shift:8 - shift + tc, :] * cw_ref[tap:tap + 1, :]
    nb_ref[0] = xcat_ref[8 + tc - keep:8 + tc, :]
    xcat_ref[0:8, :] = x[tc - 8:tc, :]

    xcb = xc.astype(BF16)
    r = _sigmoid(_dot(xcb, wr_ref[...]) + br_ref[...])
    gate_in = _sigmoid(_dot(xcb, wi_ref[...]) + bi_ref[...])
    log_a = (-LRU_C) * r * _softplus(-lam_ref[...])
    row = lax.broadcasted_iota(jnp.int32, (tc, LRU_WIDTH), 0)
    reset = (past + ci * tc + row) == 0
    a = jnp.where(reset, 0.0, jnp.exp(log_a))
    y2 = 2.0 * log_a
    series = -y2 * (1.0 + y2 * (0.5 + y2 * (1.0 / 6.0 + y2 * (1.0 / 24.0 + y2 * (1.0 / 120.0)))))
    one_minus = jnp.where(y2 > -0.05, series, 1.0 - jnp.exp(y2))
    mult = jnp.where(reset, 1.0, jnp.sqrt(one_minus))
    u = mult * gate_in * xc

    d = 1
    while d < tc:
        if d < 8:
            ok = row >= d
            u = u + jnp.where(ok, a * pltpu.roll(u, d, 0), 0.0)
            a = jnp.where(ok, a * pltpu.roll(a, d, 0), a)
        else:
            u = jnp.concatenate([u[:d], u[d:] + a[d:] * u[:-d]], axis=0)
            a = jnp.concatenate([a[:d], a[d:] * a[:-d]], axis=0)
        d *= 2
    hs = a * h_ref[...] + u
    h_last = hs[tc - 1:tc, :]
    h_ref[...] = h_last
    hl_ref[0] = h_last

    g = lg_ref[0]
    gelu = 0.5 * g * (1.0 + jnp.tanh(0.7978845608028654 * (g + 0.044715 * g * g * g)))
    y_ref[0] = (hs * gelu).astype(BF16)


def _lru(lx, lg, conv_buf, h0, lw, past):
    b, t, w = lx.shape
    tc = _pick(t, (256, 128, 64, 32))
    grid = (b, t // tc)
    row = pl.BlockSpec((1, tc, w), lambda i, j: (i, j, 0))
    keep = LRU_CONV - 1
    in_specs = [row, row, pl.BlockSpec((1, keep, w), lambda i, j: (i, 0, 0)), pl.BlockSpec((1, 1, w), lambda i, j: (i, 0, 0)),
                _const_spec((LRU_CONV, w)), _const_spec((1, w)), _const_spec((w, w)), _const_spec((1, w)),
                _const_spec((w, w)), _const_spec((1, w)), _const_spec((1, w))]
    sds = jax.ShapeDtypeStruct
    return pl.pallas_call(
        functools.partial(_lru_kernel, past=past, tc=tc), grid=grid, in_specs=in_specs,
        out_specs=[row, pl.BlockSpec((1, keep, w), lambda i, j: (i, 0, 0)), pl.BlockSpec((1, 1, w), lambda i, j: (i, 0, 0))],
        out_shape=[sds((b, t, w), BF16), sds((b, keep, w), F32), sds((b, 1, w), F32)],
        scratch_shapes=[pltpu.VMEM((tc + 8, w), F32), pltpu.VMEM((1, w), F32)],
        compiler_params=_params("parallel", "arbitrary"), name="rg_lru",
    )(lx, lg, conv_buf, h0, lw['lru_conv_w'], lw['lru_conv_b'], lw['lru_wr'], lw['lru_br'], lw['lru_wi'],
      lw['lru_bi'], lw['lru_lam'])


def _row_chains(tm):
    n = 2 if tm % 512 == 0 else 1
    return [slice(i * (tm // n), (i + 1) * (tm // n)) for i in range(n)]


def _merge_kernel(x_ref, oa_ref, ob_ref, oc_ref, od_ref, g1_ref, g2_ref, wg_ref, wb_ref, wo_ref, out_ref):
    d = x_ref.shape[2]
    for rows in _row_chains(x_ref.shape[1]):
        x = x_ref[0, rows, :]
        hb = _rms(x, g1_ref[...]).astype(BF16)
        merged = None
        for n, o_ref in enumerate((oa_ref, ob_ref, oc_ref, od_ref)):
            gate = _sigmoid(_dot(hb, wg_ref[:, n * d:(n + 1) * d]))
            term = gate * _dot(o_ref[0, rows, :], wb_ref[n])
            merged = term if merged is None else merged + term
        y = _dot(merged.astype(BF16), wo_ref[...])
        out_ref[0, rows, :] = x + _rms(y, g2_ref[...])


def _merge(x, o_a, o_b, o_c, o_d, lw):
    b, t, d = x.shape
    tm = _pick(t, (512, 256, 128))
    grid = (b, t // tm)
    row = lambda w: pl.BlockSpec((1, tm, w), lambda i, j: (i, j, 0))
    bw = BRANCH_WIDTH
    single = pl.Buffered(1)
    in_specs = [row(d), row(bw), row(bw), row(bw), row(bw), _const_spec((1, d)), _const_spec((1, d)),
                pl.BlockSpec((d, N_BRANCH * d), lambda i, j: (0, 0), pipeline_mode=single),
                pl.BlockSpec((N_BRANCH, BRANCH_WIDTH, d), lambda i, j: (0, 0, 0), pipeline_mode=single),
                pl.BlockSpec((d, d), lambda i, j: (0, 0), pipeline_mode=single)]
    return pl.pallas_call(
        _merge_kernel, grid=grid, in_specs=in_specs, out_specs=row(d),
        out_shape=jax.ShapeDtypeStruct((b, t, d), F32),
        compiler_params=_params("parallel", "parallel"), name="merge",
    )(x, o_a, o_b, o_c, o_d, lw['ln_mix_pre'], lw['ln_mix_post'], lw['w_gate'], lw['w_branch'], lw['w_out'])


def _mem_kv_kernel(mem_ref, g_ref, wk_ref, wv_ref, k_ref, v_ref):
    mn = _rms(mem_ref[0], g_ref[...]).astype(BF16)
    k_ref[0] = _dot(mn, wk_ref[...])
    v_ref[0] = _dot(mn, wv_ref[...])


def _mem_kv(mem, lw):
    b, m, d = mem.shape
    w = lw['mem_wk'].shape[1]
    out = jax.ShapeDtypeStruct((b, m, w), F32)
    blk = pl.BlockSpec((1, m, w), lambda i: (i, 0, 0))
    return pl.pallas_call(
        _mem_kv_kernel, grid=(b,),
        in_specs=[pl.BlockSpec((1, m, d), lambda i: (i, 0, 0)), _const_spec((1, d)), _const_spec((d, w)), _const_spec((d, w))],
        out_specs=[blk, blk], out_shape=[out, out], compiler_params=_params("parallel"), name="mem_kv",
    )(mem, lw['mem_norm'], lw['mem_wk'], lw['mem_wv'])


def _mem_attn_kernel(x_ref, mk_ref, mv_ref, g1_ref, g2_ref, wq_ref, wo_ref, out_ref):
    x = x_ref[0]
    hb = _rms(x, g1_ref[...]).astype(BF16)
    q = (_dot(hb, wq_ref[...]) * (MEM_HEAD_DIM ** -0.5)).astype(BF16)
    heads = []
    for h in range(N_HEADS):
        sl = slice(h * MEM_HEAD_DIM, (h + 1) * MEM_HEAD_DIM)
        s = _dot_nt(q[:, sl], mk_ref[0, :, sl])
        p = jnp.exp(s - jnp.max(s, axis=-1, keepdims=True))
        inv = 1.0 / jnp.sum(p, axis=-1, keepdims=True)
        heads.append((_dot(p.astype(BF16), mv_ref[0, :, sl]) * inv).astype(BF16))
    y = _dot(jnp.concatenate(heads, axis=1), wo_ref[...])
    out_ref[0] = x + _rms(y, g2_ref[...])


def _mem_attn(x, mk, mv, lw):
    b, t, d = x.shape
    m, w = mk.shape[1], mk.shape[2]
    tm = _pick(t, (512, 256, 128))
    row = pl.BlockSpec((1, tm, d), lambda i, j: (i, j, 0))
    kv = pl.BlockSpec((1, m, w), lambda i, j: (i, 0, 0))
    return pl.pallas_call(
        _mem_attn_kernel, grid=(b, t // tm),
        in_specs=[row, kv, kv, _const_spec((1, d)), _const_spec((1, d)), _const_spec((d, w)), _const_spec((w, d))],
        out_specs=row, out_shape=jax.ShapeDtypeStruct((b, t, d), F32),
        compiler_params=_params("parallel", "parallel"), name="mem_attn",
    )(x, mk, mv, lw['ln_mem_pre'], lw['ln_mem_post'], lw['mem_wq'], lw['mem_wo'])


def _ffn_kernel(x_ref, g1_ref, g2_ref, wg_ref, wu_ref, wd_ref, out_ref):
    for rows in _row_chains(x_ref.shape[1]):
        x = x_ref[0, rows, :]
        hb = _rms(x, g1_ref[...]).astype(BF16)
        gate = _dot(hb, wg_ref[...])
        act = (gate * _sigmoid(gate) * _dot(hb, wu_ref[...])).astype(BF16)
        y = _dot(act, wd_ref[...])
        out_ref[0, rows, :] = x + _rms(y, g2_ref[...])


def _ffn(x, lw):
    b, t, d = x.shape
    f = lw['ffn_wg'].shape[1]
    tm = _pick(t, (512, 256, 128))
    row = pl.BlockSpec((1, tm, d), lambda i, j: (i, j, 0))
    single = pl.Buffered(1)
    return pl.pallas_call(
        _ffn_kernel, grid=(b, t // tm),
        in_specs=[row, _const_spec((1, d)), _const_spec((1, d)),
                  pl.BlockSpec((d, f), lambda i, j: (0, 0), pipeline_mode=single),
                  pl.BlockSpec((d, f), lambda i, j: (0, 0), pipeline_mode=single),
                  pl.BlockSpec((f, d), lambda i, j: (0, 0), pipeline_mode=single)],
        out_specs=row, out_shape=jax.ShapeDtypeStruct((b, t, d), F32),
        compiler_params=_params("parallel", "parallel"), name="ffn",
    )(x, lw['ln_ffn_pre'], lw['ln_ffn_post'], lw['ffn_wg'], lw['ffn_wu'], lw['ffn_wd'])


def _constants():
    fk, fq, ones_k, ones_q = _fox_feature_maps()
    later = {}
    for tk in (256,):
        idx = np.arange(tk)
        later[tk] = jnp.asarray((idx[None, :] > idx[:, None]).astype(np.float32), BF16)
    return dict(
        ek=jnp.asarray(_place_qk(), BF16), evT=jnp.eye(VW, dtype=BF16),
        fkmap=jnp.asarray(fk, BF16), fqmapT=jnp.asarray(fq.T, BF16),
        ones_k=jnp.asarray(ones_k, F32), ones_qT=jnp.asarray(ones_q.T, F32),
        ekpe=jnp.asarray(_place_kpe(), BF16), later=later,
        seg=jnp.asarray(np.repeat(np.eye(N_HEADS, LANE, dtype=np.float32), HEAD_DIM, axis=0), BF16),
        seg_slot=jnp.asarray(np.repeat(np.eye(N_HEADS, LANE, dtype=np.float32), HP, axis=0), BF16))


def _pad_cols(w, n):
    return jnp.pad(w, ((0, 0), (0, n - w.shape[1])))


_MIX_SIZES = (256, 256, 256, N_HEADS, LRU_WIDTH, LRU_WIDTH, MLA_Q_RANK, MLA_KV_RANK, MLA_ROPE, 256, 256, 256)


def _w_in_plan(d_model):
    offs = np.concatenate([[0], np.cumsum(_MIX_SIZES)])
    fq, fk, fv, ff, lx, lg, cq, ckv, kpe, sq, sk, sv = [np.arange(offs[i], offs[i + 1]) for i in range(len(_MIX_SIZES))]
    half = MLA_ROPE // 2

    def padded(cols):
        return np.concatenate([cols, np.full(LANE - len(cols), -1)])

    mix = [fq, fk, fv, lx, lg, cq, ckv, padded(kpe), padded(np.concatenate([kpe[half:], kpe[:half]])), padded(ff),
           sq, sk, sv]
    pieces, col = [], 0
    for src in mix:
        pieces.append((0, col, src))
        col += len(src)
    assert col == _W_MIX_COLS
    gate0 = offs[-1]
    for i in range(N_BRANCH * d_model // 256):
        pieces.append((1, i * 256, gate0 + i * 256 + np.arange(256)))
    return pieces


def _transpose_kernel(xT_ref, eye_ref, x_ref):
    x_ref[...] = _dot_nt(eye_ref[...], xT_ref[...]).astype(x_ref.dtype)


def _transpose_bf16(x_t):
    r, c = x_t.shape
    rb = _pick(r, (256, 128))
    return pl.pallas_call(
        _transpose_kernel, grid=(r // rb,),
        in_specs=[pl.BlockSpec((rb, c), lambda i: (i, 0)), _const_spec((c, c))],
        out_specs=pl.BlockSpec((c, rb), lambda i: (0, i)), out_shape=jax.ShapeDtypeStruct((c, r), BF16),
        compiler_params=_params("parallel"), name="transpose_w",
    )(x_t, jnp.eye(c, dtype=BF16))


def _repack_w_in(w_in_t):
    d_model = w_in_t.shape[1]
    rows = [[], []]
    for dst, _, src in _w_in_plan(d_model):
        valid = src[src >= 0]
        runs = np.split(valid, np.nonzero(np.diff(valid) != 1)[0] + 1)
        piece = jnp.concatenate([w_in_t[int(r[0]):int(r[-1]) + 1] for r in runs], axis=0)
        rows[dst].append(jnp.pad(piece, ((0, len(src) - len(valid)), (0, 0))))
    return tuple(_transpose_bf16(jnp.concatenate(r, axis=0).astype(BF16)) for r in rows)


def _layer_weights(l, p):
    w_mix, w_gate = _repack_w_in(jnp.transpose(p['w_in'], (2, 0, 1))[:, l, :])
    half = MLA_ROPE // 2
    qk = MLA_NOPE + MLA_ROPE
    wuq = p['mla_w_uq'][l].reshape(MLA_Q_RANK, N_HEADS, qk)
    rope = wuq[:, :, MLA_NOPE:]
    rope_sw = jnp.concatenate([rope[:, :, half:], rope[:, :, :half]], axis=2)
    wuq_p = jnp.pad(wuq, ((0, 0), (0, 0), (0, HP - qk))).reshape(MLA_Q_RANK, QW)
    wuqsw_p = jnp.pad(rope_sw, ((0, 0), (0, 0), (MLA_NOPE, HP - qk))).reshape(MLA_Q_RANK, QW)
    wuk = p['mla_w_uk'][l].reshape(MLA_KV_RANK, N_HEADS, MLA_NOPE)
    wuk_p = jnp.pad(wuk, ((0, 0), (0, 0), (0, HP - MLA_NOPE))).reshape(MLA_KV_RANK, QW)

    def block_diag(w):
        return jax.scipy.linalg.block_diag(*[w[i] for i in range(w.shape[0])]).astype(BF16)

    row = lambda v: v[l].reshape(1, -1).astype(F32)
    return dict(
        w_mix=w_mix, w_gate=w_gate,
        ln_mix_pre=row(p['ln_mix_pre']), ln_mix_post=row(p['ln_mix_post']),
        fox_bf=_pad_cols(row(p['fox_bf']), LANE),
        mla_q_norm=row(p['mla_q_norm']), mla_kv_norm=row(p['mla_kv_norm']),
        wuqT=wuq_p.T.astype(BF16), wuqswT=wuqsw_p.T.astype(BF16), wuk=wuk_p.astype(BF16), wuvT=p['mla_w_uv'][l].T.astype(BF16),
        eqT=jnp.asarray(_place_qk().T, BF16),
        lru_conv_w=p['lru_conv_w'][l].astype(F32), lru_conv_b=row(p['lru_conv_b']),
        lru_wr=block_diag(p['lru_wr'][l]), lru_br=row(p['lru_br']),
        lru_wi=block_diag(p['lru_wi'][l]), lru_bi=row(p['lru_bi']), lru_lam=row(p['lru_lam']),
        w_branch=p['w_branch'][l].astype(BF16), w_out=p['w_out'][l].astype(BF16),
        ln_mem_pre=row(p['ln_mem_pre']), ln_mem_post=row(p['ln_mem_post']), mem_norm=row(p['mem_norm']),
        mem_wq=p['mem_wq'][l].astype(BF16), mem_wk=p['mem_wk'][l].astype(BF16),
        mem_wv=p['mem_wv'][l].astype(BF16), mem_wo=p['mem_wo'][l].astype(BF16),
        ln_ffn_pre=row(p['ln_ffn_pre']), ln_ffn_post=row(p['ln_ffn_post']),
        ffn_wg=p['ffn_wg'][l].astype(BF16), ffn_wu=p['ffn_wu'][l].astype(BF16), ffn_wd=p['ffn_wd'][l].astype(BF16))


def _rope_tables(past, t):
    half = MLA_ROPE // 2
    inv = jnp.power(ROPE_BASE, -jnp.arange(half, dtype=F32) / half)
    ang = (past + jnp.arange(t, dtype=jnp.int32)).astype(F32)[:, None] * inv
    cos, sin = jnp.cos(ang), jnp.sin(ang)
    c32 = jnp.concatenate([cos, cos], axis=1)
    s32 = jnp.concatenate([-sin, sin], axis=1)
    scale = (MLA_NOPE + MLA_ROPE) ** -0.5 * LOG2E
    slot_c = jnp.concatenate([jnp.ones((t, MLA_NOPE), F32), c32, jnp.zeros((t, HP - MLA_NOPE - MLA_ROPE), F32)], axis=1)
    slot_s = jnp.concatenate([jnp.zeros((t, MLA_NOPE), F32), s32, jnp.zeros((t, HP - MLA_NOPE - MLA_ROPE), F32)], axis=1)
    return dict(ck=_pad_cols(c32, LANE), sk=_pad_cols(s32, LANE),
                cqT=(jnp.tile(slot_c, (1, N_HEADS)) * scale).T, sqT=(jnp.tile(slot_s, (1, N_HEADS)) * scale).T)


def _trunk_layer(x, past_state, mem_k, mem_v, lw, cst, layer, depth, earlier):
    b, t, _ = x.shape
    past = 0 if past_state is None else past_state['fox_k'].shape[1]
    tk = 256
    fq, fk, fv, lx, lg, ckv_n, kpe_slab, logf_slab, mla_qT, sb_qT, sk, sv = _in_proj(
        x, lw, _rope_tables(past, t), layer, depth, earlier)
    stacked = (fk, fv, ckv_n, kpe_slab, logf_slab, sk, sv)

    if past_state is None:
        s_valid = t
        keys = (fq, fk, fv, logf_slab, ckv_n, kpe_slab, sk, sv)
        key_layer = layer
        conv_buf = jnp.zeros((b, LRU_CONV - 1, LRU_WIDTH), F32)
        h0 = jnp.zeros((b, 1, LRU_WIDTH), F32)
    else:
        s_valid = past + t
        s_pad = -(-s_valid // tk) * tk

        def cat(old, new):
            old = old.reshape(b, past, -1).astype(new.dtype)
            old = jnp.pad(old, ((0, 0), (0, 0), (0, new.shape[2] - old.shape[2])))
            return jnp.pad(jnp.concatenate([old, new], axis=1), ((0, 0), (0, s_pad - s_valid), (0, 0)))

        def cat_state(old, new):
            return cat(old, new[layer])[None]

        keys = (cat(jnp.zeros((b, past, 256), BF16), fq), cat_state(past_state['fox_k'], fk),
                cat_state(past_state['fox_v'], fv), cat_state(past_state['fox_logf'], logf_slab),
                cat_state(past_state['mla_ckv'], ckv_n), cat_state(past_state['mla_kpe'], kpe_slab),
                cat_state(past_state['sb_k'], sk), cat_state(past_state['sb_v'], sv))
        key_layer = 0
        conv_buf = past_state['lru_conv'].astype(F32)
        h0 = past_state['lru_h'].reshape(b, 1, LRU_WIDTH).astype(F32)

    (fox_qT, fox_k, fox_vT, mla_k, mla_vT, sb_k, sb_vT,
     *fox_skip_stats, mla_key_norm) = _kv_prep(*keys, key_layer, lw, cst, tk)
    if past_state is not None:
        fox_qT = fox_qT[:, :, past:past + t]

    att = functools.partial(_attention, cst=cst, past=past, s_valid=s_valid, tk=tk)
    tq_softmax = _pick(t, (2 * tk, tk, 128))
    o_a = att('fox', fox_qT, fox_k, fox_vT, tq=tq_softmax, skip_stats=fox_skip_stats)
    o_c = att('mla', mla_qT, mla_k, mla_vT, tq=tq_softmax, skip_stats=[mla_key_norm])
    o_d = att('sb', sb_qT, sb_k, sb_vT, tq=_pick(t, (tk, 128)))
    o_b, lru_conv, lru_h = _lru(lx, lg, conv_buf, h0, lw, past)

    x = _merge(x, o_a, o_b, o_c, o_d, lw)
    x = _mem_attn(x, mem_k, mem_v, lw)
    x = _ffn(x, lw)

    return x, stacked, dict(lru_h=lru_h.reshape(b, LRU_WIDTH), lru_conv=lru_conv)


def _state_outputs(stacked, small):
    fk, fv, ckv_n, kpe_slab, logf_slab, sk, sv = stacked
    depth, b, t, _ = fk.shape
    heads = lambda a: a.reshape(depth, b, t, N_HEADS, HEAD_DIM)
    stk = lambda name: jnp.stack([s[name] for s in small])
    return (heads(fk), heads(fv), logf_slab[..., :N_HEADS], stk('lru_h'), stk('lru_conv'),
            ckv_n, kpe_slab[..., :MLA_ROPE], heads(sk), heads(sv))


def kernel(x_prompt, x_sample, cache_fox_k, cache_fox_v, cache_fox_logf, state_lru_h, state_lru_conv, cache_mla_ckv, cache_mla_kpe, cache_sb_k, cache_sb_v, cache_mem_k, cache_mem_v, mem_prompt, ln_mix_pre, ln_mix_post, w_in, fox_bf, lru_conv_w, lru_conv_b, lru_wr, lru_br, lru_wi, lru_bi, lru_lam, mla_q_norm, mla_w_uq, mla_kv_norm, mla_w_uk, mla_w_uv, w_branch, w_out, ln_mem_pre, ln_mem_post, mem_norm, mem_wq, mem_wk, mem_wv, mem_wo, ln_ffn_pre, ln_ffn_post, ffn_wg, ffn_wu, ffn_wd):
    params = dict(ln_mix_pre=ln_mix_pre, ln_mix_post=ln_mix_post, w_in=w_in, fox_bf=fox_bf, lru_conv_w=lru_conv_w,
                  lru_conv_b=lru_conv_b, lru_wr=lru_wr, lru_br=lru_br, lru_wi=lru_wi, lru_bi=lru_bi, lru_lam=lru_lam,
                  mla_q_norm=mla_q_norm, mla_w_uq=mla_w_uq, mla_kv_norm=mla_kv_norm, mla_w_uk=mla_w_uk,
                  mla_w_uv=mla_w_uv, w_branch=w_branch, w_out=w_out, ln_mem_pre=ln_mem_pre, ln_mem_post=ln_mem_post,
                  mem_norm=mem_norm, mem_wq=mem_wq, mem_wk=mem_wk, mem_wv=mem_wv, mem_wo=mem_wo,
                  ln_ffn_pre=ln_ffn_pre, ln_ffn_post=ln_ffn_post, ffn_wg=ffn_wg, ffn_wu=ffn_wu, ffn_wd=ffn_wd)
    depth = w_in.shape[0]
    cst = _constants()
    weights = [_layer_weights(l, params) for l in range(depth)]
    bp, mem_len = mem_prompt.shape[0], mem_prompt.shape[1]

    y_prompt, p_rows, p_small, p_mem = x_prompt, None, [], []
    for l in range(depth):
        mk, mv = _mem_kv(mem_prompt, weights[l])
        y_prompt, p_rows, small = _trunk_layer(y_prompt, None, mk.astype(BF16), mv.astype(BF16), weights[l], cst,
                                               l, depth, p_rows)
        p_small.append(small)
        p_mem.append((mk.reshape(bp, mem_len, N_HEADS, MEM_HEAD_DIM), mv.reshape(bp, mem_len, N_HEADS, MEM_HEAD_DIM)))

    y_sample, s_rows, s_small = x_sample, None, []
    bs = x_sample.shape[0]
    for l in range(depth):
        past = dict(fox_k=cache_fox_k[l], fox_v=cache_fox_v[l], fox_logf=cache_fox_logf[l], lru_h=state_lru_h[l],
                    lru_conv=state_lru_conv[l], mla_ckv=cache_mla_ckv[l], mla_kpe=cache_mla_kpe[l],
                    sb_k=cache_sb_k[l], sb_v=cache_sb_v[l])
        mk = cache_mem_k[l].reshape(bs, mem_len, -1).astype(BF16)
        mv = cache_mem_v[l].reshape(bs, mem_len, -1).astype(BF16)
        y_sample, s_rows, small = _trunk_layer(y_sample, past, mk, mv, weights[l], cst, l, depth, s_rows)
        s_small.append(small)

    mem_out = (jnp.stack([m[0] for m in p_mem]), jnp.stack([m[1] for m in p_mem]))
    return (y_prompt, y_sample) + _state_outputs(p_rows, p_small) + mem_out + _state_outputs(s_rows, s_small)
```

```python
import functools
import math

import numpy as np
import jax
import jax.numpy as jnp
from jax import lax
from jax.experimental import pallas as pl
from jax.experimental.pallas import tpu as pltpu

F32 = jnp.float32
BF16 = jnp.bfloat16

CHUNK = 64
HEAD_DIM = 64
N_HEADS = 4
LRU_WIDTH = 256
LRU_CONV = 4
LRU_C = 8.0
MLA_Q_RANK = 256
MLA_KV_RANK = 128
MLA_NOPE = 64
MLA_ROPE = 32
MLA_V = 64
ROPE_BASE = 10000.0
N_BRANCH = 4
BRANCH_WIDTH = 256
MEM_HEAD_DIM = 128
EPS = 1e-6
NEG_INF = -1e30

HP = 128
QW = N_HEADS * HP
VW = N_HEADS * HEAD_DIM
LANE = 128
VMEM_LIMIT_BYTES = 56 * 1024 * 1024
LOG2E = 1.4426950408889634
SB_DECAY_LIMIT = 127.0
FOX_SKIP_LIMIT = 150.0
FIXED_STABILISER_LIMIT = 60.0
SOFTPLUS_LINEAR_FROM = 60.0

_NT = (((1,), (1,)), ((), ()))


def _dot(a, b):
    return jnp.dot(a, b, preferred_element_type=F32)


def _dot_nt(a, b):
    return lax.dot_general(a, b, _NT, preferred_element_type=F32)


def _rms(x, g):
    ms = jnp.mean(x * x, axis=-1, keepdims=True)
    return x * lax.rsqrt(ms + EPS) * g


def _sigmoid(x):
    return 1.0 / (1.0 + jnp.exp(-x))


def _softplus(x):
    return jnp.maximum(x, 0.0) + jnp.log(1.0 + jnp.exp(-jnp.abs(x)))


def _pick(n, cands):
    for c in cands:
        if n % c == 0:
            return c
    return n


def _params(*sem):
    return pltpu.CompilerParams(dimension_semantics=sem, vmem_limit_bytes=VMEM_LIMIT_BYTES)


def _const_spec(shape):
    nd = len(shape)
    return pl.BlockSpec(shape, lambda *_: (0,) * nd)


def _place_qk():
    e = np.zeros((N_HEADS * HEAD_DIM, QW), np.float32)
    for h in range(N_HEADS):
        for j in range(HEAD_DIM):
            e[h * HEAD_DIM + j, h * HP + j] = 1.0
    return e


def _fox_feature_maps():
    fk = np.zeros((LANE, QW), np.float32)
    fq = np.zeros((LANE, QW), np.float32)
    ones_k = np.zeros((1, QW), np.float32)
    ones_q = np.zeros((1, QW), np.float32)
    for h in range(N_HEADS):
        for part in range(3):
            fq[part * N_HEADS + h, h * HP + HEAD_DIM + part] = 1.0
            fk[part * N_HEADS + h, h * HP + HEAD_DIM + 3 + part] = -1.0
            ones_k[0, h * HP + HEAD_DIM + part] = 1.0
            ones_q[0, h * HP + HEAD_DIM + 3 + part] = 1.0
    return fk, fq, ones_k, ones_q


def _place_kpe():
    e = np.zeros((LANE, QW), np.float32)
    for h in range(N_HEADS):
        for j in range(MLA_ROPE):
            e[j, h * HP + MLA_NOPE + j] = 1.0
    return e


_W_OFF = dict(fq=0, fk=256, fv=512, lx=768, lg=1024, cq=1280, ckv=1536, kpe=1664, kpe_sw=1792, ff=1920,
              sq=2048, sk=2304, sv=2560)
_W_MIX_COLS = 2816


_N_IN_PROJ_INPUTS = 13
_IN_PROJ_STATE_OUTPUTS = (1, 2, 5, 6, 7, 10, 11)


def _in_proj_kernel(*refs):
    (x_ref, g_ref, w_ref, qn_ref, kvn_ref, bf_ref, ck_ref, sk_tab_ref, cqT_ref, sqT_ref,
     wuqT_ref, wuqswT_ref, eqT_ref) = refs[:_N_IN_PROJ_INPUTS]
    (fq_ref, fk_ref, fv_ref, lx_ref, lg_ref, ckv_ref, kpe_ref, logf_ref, mqT_ref, sbqT_ref, sk_ref, sv_ref) = refs[-12:]
    fk_ref, fv_ref, ckv_ref, kpe_ref, logf_ref, sk_ref, sv_ref = (
        r.at[0] for r in (fk_ref, fv_ref, ckv_ref, kpe_ref, logf_ref, sk_ref, sv_ref))
    chains = _row_chains(x_ref.shape[1])
    late = []
    for rows in chains:
        hb = _rms(x_ref[0, rows, :], g_ref[...]).astype(BF16)

        def proj(name, width):
            a = _W_OFF[name]
            return _dot(hb, w_ref[:, a:a + width])

        fq_ref[0, rows, :] = (proj('fq', 256) * (HEAD_DIM ** -0.5 * LOG2E)).astype(BF16)
        fk_ref[0, rows, :] = proj('fk', 256)
        fv_ref[0, rows, :] = proj('fv', 256)
        lx_ref[0, rows, :] = proj('lx', 256)
        lg_ref[0, rows, :] = proj('lg', 256)
        cqn = _rms(proj('cq', 256), qn_ref[...]).astype(BF16)
        ckv_ref[0, rows, :] = _rms(proj('ckv', 128), kvn_ref[...])
        kpe_ref[0, rows, :] = proj('kpe', 128) * ck_ref[rows, :] + proj('kpe_sw', 128) * sk_tab_ref[rows, :]

        ff = proj('ff', 128) + bf_ref[...]
        log_sig = jnp.minimum(ff, 0.0) - jnp.log(1.0 + jnp.exp(-jnp.abs(ff)))
        lane = lax.broadcasted_iota(jnp.int32, ff.shape, 1)
        logf_ref[0, rows, :] = jnp.where(lane < N_HEADS, log_sig, 0.0)

        sq = (proj('sq', 256) * (HEAD_DIM ** -0.5 * LOG2E)).astype(BF16)
        sk_ref[0, rows, :] = proj('sk', 256)
        sv_ref[0, rows, :] = proj('sv', 256)
        late.append((rows, cqn, sq))

    for rows, cqn, sq in late:
        q_t = _dot_nt(wuqT_ref[...], cqn)
        qsw_t = _dot_nt(wuqswT_ref[...], cqn)
        mqT_ref[0, :, rows] = (q_t * cqT_ref[:, rows] + qsw_t * sqT_ref[:, rows]).astype(BF16)
        sbqT_ref[0, :, rows] = _dot_nt(eqT_ref[...], sq).astype(BF16)


def _in_proj(x, lw, tabs, layer, depth, earlier):
    b, t, d = x.shape
    tm = _pick(t, (512, 256, 128))
    grid = (b, t // tm)
    row = lambda w: pl.BlockSpec((1, tm, w), lambda i, j: (i, j, 0))
    srow = lambda w: pl.BlockSpec((1, 1, tm, w), lambda i, j: (layer, i, j, 0))
    colT = pl.BlockSpec((1, QW, tm), lambda i, j: (i, 0, j))
    in_specs = [
        row(d), _const_spec((1, d)), _const_spec((d, _W_MIX_COLS)),
        _const_spec((1, MLA_Q_RANK)), _const_spec((1, MLA_KV_RANK)), _const_spec((1, LANE)),
        pl.BlockSpec((tm, LANE), lambda i, j: (j, 0)), pl.BlockSpec((tm, LANE), lambda i, j: (j, 0)),
        pl.BlockSpec((QW, tm), lambda i, j: (0, j)), pl.BlockSpec((QW, tm), lambda i, j: (0, j)),
        _const_spec((QW, MLA_Q_RANK)), _const_spec((QW, MLA_Q_RANK)), _const_spec((QW, 256)),
    ]
    sds = jax.ShapeDtypeStruct
    state = lambda w: sds((depth, b, t, w), F32)
    out_shape = [
        sds((b, t, 256), BF16),
        state(256), state(256),
        sds((b, t, 256), F32), sds((b, t, 256), F32),
        state(128), state(LANE), state(LANE),
        sds((b, QW, t), BF16), sds((b, QW, t), BF16),
        state(256), state(256),
    ]
    out_specs = [row(256), srow(256), srow(256), row(256), row(256), srow(128), srow(LANE), srow(LANE),
                 colT, colT, srow(256), srow(256)]
    assert len(in_specs) == _N_IN_PROJ_INPUTS
    aliases = {}
    if earlier is not None:
        in_specs += [pl.BlockSpec(memory_space=pl.ANY)] * len(earlier)
        aliases = {_N_IN_PROJ_INPUTS + n: out for n, out in enumerate(_IN_PROJ_STATE_OUTPUTS)}
    return pl.pallas_call(
        _in_proj_kernel, grid=grid, in_specs=in_specs, out_specs=out_specs, out_shape=out_shape,
        input_output_aliases=aliases, compiler_params=_params("parallel", "parallel"), name="in_proj",
    )(x, lw['ln_mix_pre'], lw['w_mix'], lw['mla_q_norm'], lw['mla_kv_norm'], lw['fox_bf'],
      tabs['ck'], tabs['sk'], tabs['cqT'], tabs['sqT'], lw['wuqT'], lw['wuqswT'], lw['eqT'], *(earlier or ()))


def _cumsum_rows(x):
    n = x.shape[0]
    row = lax.broadcasted_iota(jnp.int32, x.shape, 0)
    d = 1
    while d < n:
        x = x + jnp.where(row >= d, pltpu.roll(x, d, 0), 0.0)
        d *= 2
    return x


def _kv_prep_kernel(fq_ref, fk_ref, fv_ref, logf_ref, ckv_ref, kpe_ref, sk_ref, sv_ref,
                    ek_ref, evT_ref, eqT_ref, fkmap_ref, fqmapT_ref, onesk_ref, onesq_ref,
                    wuk_ref, ekpe_ref, wuvT_ref, seg_ref, segslot_ref,
                    fqT_out, fk_out, fvT_out, mk_out, mvT_out, sbk_out, sbvT_out, cend_out, kn_out, mkn_out,
                    carry_ref, *, tk):
    fk_ref, fv_ref, logf_ref, ckv_ref, kpe_ref, sk_ref, sv_ref = (
        r.at[0] for r in (fk_ref, fv_ref, logf_ref, ckv_ref, kpe_ref, sk_ref, sv_ref))

    @pl.when(pl.program_id(1) == 0)
    def _():
        carry_ref[...] = jnp.zeros_like(carry_ref)
        kn_out[...] = jnp.zeros_like(kn_out)
        mkn_out[...] = jnp.zeros_like(mkn_out)

    ts = fk_ref.shape[1]
    c = _cumsum_rows(logf_ref[0]) + carry_ref[...]
    carry_ref[...] = c[ts - 1:ts, :]
    c = c * LOG2E
    for j in range(ts // tk):
        cend_out[0, j] = c[(j + 1) * tk - 1:(j + 1) * tk, :]

    def norm_bound(x, seg):
        xf = x.astype(F32)
        sums = _dot((xf * xf).astype(BF16), seg) * 1.01
        return jnp.max(sums, axis=0, keepdims=True)

    kn_out[0] = jnp.maximum(kn_out[0], norm_bound(fk_ref[0].astype(BF16), seg_ref[...]))
    c_hi = c.astype(BF16).astype(F32)
    rem = c - c_hi
    c_mid = rem.astype(BF16).astype(F32)
    c_lo = (rem - c_mid).astype(BF16).astype(F32)
    feat = (c_hi + pltpu.roll(c_mid, N_HEADS, 1) + pltpu.roll(c_lo, 2 * N_HEADS, 1)).astype(BF16)

    def store_vt(out, v_t):
        for j in range(ts // tk):
            out[0, j] = v_t[:, j * tk:(j + 1) * tk]

    fk_out[0] = (_dot(fk_ref[0].astype(BF16), ek_ref[...]) + _dot(feat, fkmap_ref[...]) + onesk_ref[...]).astype(BF16)
    fqT_out[0] = (_dot_nt(eqT_ref[...], fq_ref[0]) + _dot_nt(fqmapT_ref[...], feat) + onesq_ref[...]).astype(BF16)
    store_vt(fvT_out, _dot_nt(evT_ref[...], fv_ref[0].astype(BF16)).astype(BF16))

    ckv = ckv_ref[0].astype(BF16)
    mk = (_dot(ckv, wuk_ref[...]) + _dot(kpe_ref[0].astype(BF16), ekpe_ref[...])).astype(BF16)
    mk_out[0] = mk
    mkn_out[0] = jnp.maximum(mkn_out[0], norm_bound(mk, segslot_ref[...]))
    store_vt(mvT_out, _dot_nt(wuvT_ref[...], ckv).astype(BF16))

    sbk_out[0] = _dot(sk_ref[0].astype(BF16), ek_ref[...]).astype(BF16)
    store_vt(sbvT_out, _dot_nt(evT_ref[...], sv_ref[0].astype(BF16)).astype(BF16))


def _kv_prep(fq, fk, fv, logf, ckv, kpe, sk, sv, layer, lw, cst, tk):
    _, b, s, _ = fk.shape
    ts = _pick(s, (1024, 768, 512, 256))
    grid = (b, s // ts)
    row = lambda w: pl.BlockSpec((1, ts, w), lambda i, j: (i, j, 0))
    srow = lambda w: pl.BlockSpec((1, 1, ts, w), lambda i, j: (layer, i, j, 0))
    colT = pl.BlockSpec((1, QW, ts), lambda i, j: (i, 0, j))
    vT = pl.BlockSpec((1, ts // tk, VW, tk), lambda i, j: (i, j, 0, 0))
    in_specs = [row(256), srow(256), srow(256), srow(LANE), srow(128), srow(LANE), srow(256), srow(256),
                _const_spec((256, QW)), _const_spec((VW, VW)), _const_spec((QW, 256)),
                _const_spec((LANE, QW)), _const_spec((QW, LANE)), _const_spec((1, QW)), _const_spec((QW, 1)),
                _const_spec((MLA_KV_RANK, QW)), _const_spec((LANE, QW)), _const_spec((VW, MLA_KV_RANK)),
                _const_spec((VW, LANE)), _const_spec((QW, LANE))]
    sds = jax.ShapeDtypeStruct
    slab = sds((b, s, QW), BF16)
    slab_t = sds((b, s // tk, VW, tk), BF16)
    stat = sds((b, 1, LANE), F32)
    stat_spec = pl.BlockSpec((1, 1, LANE), lambda i, j: (i, 0, 0))
    out_shape = [sds((b, QW, s), BF16), slab, slab_t, slab, slab_t, slab, slab_t,
                 sds((b, s // tk, 1, LANE), F32), stat, stat]
    out_specs = [colT, row(QW), vT, row(QW), vT, row(QW), vT,
                 pl.BlockSpec((1, ts // tk, 1, LANE), lambda i, j: (i, j, 0, 0)), stat_spec, stat_spec]
    return pl.pallas_call(
        functools.partial(_kv_prep_kernel, tk=tk), grid=grid, in_specs=in_specs, out_specs=out_specs,
        out_shape=out_shape, scratch_shapes=[pltpu.VMEM((1, LANE), F32)],
        compiler_params=_params("parallel", "arbitrary"), name="kv_prep",
    )(fq, fk, fv, logf, ckv, kpe, sk, sv, cst['ek'], cst['evT'], lw['eqT'], cst['fkmap'], cst['fqmapT'],
      cst['ones_k'], cst['ones_qT'], lw['wuk'], cst['ekpe'], lw['wuvT'], cst['seg'], cst['seg_slot'])


def _head_rows(h):
    return slice(h * HP, (h + 1) * HP)


def _value_rows(h):
    return slice(h * HEAD_DIM, (h + 1) * HEAD_DIM)


def _finish(accs, invs, eye_ref, o_ref):
    for pair in range(N_HEADS // 2):
        o_t = jnp.concatenate([accs[2 * pair] * invs[2 * pair], accs[2 * pair + 1] * invs[2 * pair + 1]], axis=0)
        o_ref[0, :, _head_rows(pair)] = _dot_nt(eye_ref[...], o_t.astype(BF16)).astype(BF16)


def _softmax_attn_kernel(qT_ref, k_ref, vT_ref, eye_ref, o_ref, *, mode, past, tq, tk, s_valid):
    qi = pl.program_id(1)
    ltk = int(math.log2(tk))
    q_lo = past + qi * tq
    q_hi = q_lo + (tq - 1)
    if mode == 'fox':
        n_full = (q_lo + 1) >> ltk
        n_blk = (q_hi >> ltk) + 1
    else:
        lim_lo = ((q_lo // CHUNK) + 1) * CHUNK
        lim_hi = jnp.minimum(((q_hi // CHUNK) + 1) * CHUNK, s_valid)
        n_blk = (lim_hi + (tk - 1)) >> ltk
        n_full = jnp.minimum(lim_lo >> ltk, n_blk)
    qpos = q_lo + lax.broadcasted_iota(jnp.int32, (1, tq), 1)

    def step(kb, carry, masked):
        off = pl.multiple_of(kb * tk, tk)
        scores = [_dot(k_ref[0, pl.ds(off, tk), _head_rows(h)], qT_ref[0, _head_rows(h), :]) for h in range(N_HEADS)]
        if masked:
            kpos = kb * tk + lax.broadcasted_iota(jnp.int32, (tk, 1), 0)
            if mode == 'fox':
                vis = kpos <= qpos
            else:
                vis = jnp.logical_and((kpos // CHUNK) <= (qpos // CHUNK), kpos < s_valid)
        out = []
        for h in range(N_HEADS):
            m, l, acc = carry[h]
            s = jnp.where(vis, scores[h], NEG_INF) if masked else scores[h]
            m_new = jnp.maximum(m, jnp.max(s, axis=0, keepdims=True))
            p = jnp.exp2(s - m_new)
            alpha = jnp.exp2(m - m_new)
            l = alpha * l + jnp.sum(p, axis=0, keepdims=True)
            acc = alpha * acc + _dot(vT_ref[0, kb, _value_rows(h), :], p.astype(BF16))
            out.append((m_new, l, acc))
        return tuple(out)

    init = (jnp.full((1, tq), NEG_INF, F32), jnp.zeros((1, tq), F32), jnp.zeros((HEAD_DIM, tq), F32))
    carry = lax.fori_loop(0, n_full, functools.partial(step, masked=False), (init,) * N_HEADS)
    carry = lax.fori_loop(n_full, n_blk, functools.partial(step, masked=True), carry)
    _finish([c[2] for c in carry], [1.0 / c[1] for c in carry], eye_ref, o_ref)


def _first_needed_pair(cend_ref, kn_ref, q_norm2, n_pairs):
    nblk = cend_ref.shape[1]
    cend = cend_ref[0]
    c_tile = cend_ref[0, pl.ds(jnp.maximum(2 * n_pairs - 1, 0), 1), :]
    head_lane = lax.broadcasted_iota(jnp.int32, (1, LANE), 1)
    q_max2 = jnp.zeros((1, LANE), F32)
    for h in range(N_HEADS):
        q_max2 = jnp.where(head_lane == h, jnp.max(q_norm2[h], axis=1, keepdims=True) * 1.01, q_max2)
    qk_bound = 2.0 * jnp.sqrt(q_max2 * kn_ref[0]) + 1.0
    lane = lax.broadcasted_iota(jnp.int32, (nblk, LANE), 1)
    blk = lax.broadcasted_iota(jnp.int32, (nblk, 1), 0)
    worst = jnp.max(jnp.where(lane < N_HEADS, qk_bound + c_tile - cend, -jnp.inf), axis=1, keepdims=True)
    is_pair_end = jnp.logical_and((blk & 1) == 1, blk < 2 * n_pairs)
    skip = jnp.logical_and(is_pair_end, worst <= -FOX_SKIP_LIMIT)
    return jnp.sum(skip.astype(jnp.int32))


def _softmax_attn_pipelined_kernel(qT_ref, k_ref, vT_ref, eye_ref, *rest, mode, past, tq, tk, s_valid):
    o_ref, sa_ref, sb_ref, acc_ref = rest[-4:]
    kn_ref = rest[-5]
    qi = pl.program_id(1)
    q_lo = past + qi * tq
    n_pairs = past // tq + qi
    feature_rows = HEAD_DIM if mode == 'fox' else HP
    q_norm2 = []
    for h in range(N_HEADS):
        q_h = qT_ref[0, h * HP:h * HP + feature_rows, :].astype(F32)
        q_norm2.append(jnp.sum(q_h * q_h, axis=0, keepdims=True))
    first_pair = _first_needed_pair(rest[0], kn_ref, q_norm2, n_pairs) if mode == 'fox' else 0
    qpos = q_lo + lax.broadcasted_iota(jnp.int32, (1, tq), 1)

    def qk_head(kb, dst_ref, h):
        off = pl.multiple_of(kb * tk, tk)
        dst_ref[h] = _dot(k_ref[0, pl.ds(off, tk), _head_rows(h)], qT_ref[0, _head_rows(h), :])

    def qk(kb, dst_ref):
        for h in range(N_HEADS):
            qk_head(kb, dst_ref, h)

    def softmax_pv(kb, src_ref, stats, masked, lanes=slice(None)):
        if masked:
            kpos = kb * tk + lax.broadcasted_iota(jnp.int32, (tk, 1), 0)
            if mode == 'fox':
                vis = kpos <= qpos[:, lanes]
            else:
                vis = (kpos // CHUNK) <= (qpos[:, lanes] // CHUNK)
                if s_valid < k_ref.shape[1]:
                    vis = jnp.logical_and(vis, kpos < s_valid)
        out = []
        for h in range(N_HEADS):
            m, l = stats[h]
            s = src_ref[h, :, lanes]
            if masked:
                s = jnp.where(vis, s, NEG_INF)
            m_new = jnp.maximum(m, jnp.max(s, axis=0, keepdims=True))
            p = jnp.exp2(s - m_new)
            alpha = jnp.exp2(m - m_new)
            l = alpha * l + jnp.sum(p, axis=0, keepdims=True)
            rows = _value_rows(h)
            acc_ref[rows, lanes] = alpha * acc_ref[rows, lanes] + _dot(vT_ref[0, kb, rows, :], p.astype(BF16))
            out.append((m_new, l))
        return tuple(out)

    def fixed_pv_head(kb, src_ref, h, m_fix, total):
        p = jnp.exp2(src_ref[h] - m_fix)
        rows = _value_rows(h)
        acc_ref[rows, :] = acc_ref[rows, :] + _dot(vT_ref[0, kb, rows, :], p.astype(BF16))
        return total + jnp.sum(p, axis=0, keepdims=True)

    def pairs(step, carry):
        def body(j, carry):
            kb = 2 * j
            qk(kb + 1, sb_ref)
            carry = step(kb, sa_ref, carry)
            qk(kb + 2, sa_ref)
            return step(kb + 1, sb_ref, carry)
        return lax.fori_loop(first_pair, n_pairs, body, carry)

    acc_ref[...] = jnp.zeros_like(acc_ref)
    own = 2 * n_pairs
    early, late = slice(0, tk), slice(tk, tq)
    qk(own, sa_ref)
    qk(own + 1, sb_ref)
    init = ((jnp.full((1, tk), NEG_INF, F32), jnp.zeros((1, tk), F32)),) * N_HEADS
    st_early = softmax_pv(own, sa_ref, init, True, early)
    st_late = softmax_pv(own, sa_ref, init, False, late)
    qk(2 * first_pair, sa_ref)
    st_late = softmax_pv(own + 1, sb_ref, st_late, True, late)
    stats = tuple(tuple(jnp.concatenate([a, b], axis=1) for a, b in zip(st_early[h], st_late[h]))
                  for h in range(N_HEADS))

    gap = None
    for h in range(N_HEADS):
        g = jnp.sqrt(q_norm2[h] * kn_ref[0][:, h:h + 1]) * 1.01 + 0.5 - stats[h][0]
        gap = g if gap is None else jnp.maximum(gap, g)
    frozen_ok = jnp.max(gap) <= FIXED_STABILISER_LIMIT

    def run_frozen(stats):
        m_fix = [st[0] for st in stats]

        def block(kb_next, dst_ref, kb, src_ref, sums):
            out = []
            for h in range(N_HEADS):
                qk_head(kb_next, dst_ref, h)
                out.append(fixed_pv_head(kb, src_ref, h, m_fix[h], sums[h]))
            return tuple(out)

        def body(j, sums):
            kb = 2 * j
            sums = block(kb + 1, sb_ref, kb, sa_ref, sums)
            return block(kb + 2, sa_ref, kb + 1, sb_ref, sums)

        sums = lax.fori_loop(first_pair, n_pairs, body, tuple(st[1] for st in stats))
        return tuple((m_fix[h], sums[h]) for h in range(N_HEADS))

    def run_online(stats):
        return pairs(lambda kb, src, st: softmax_pv(kb, src, st, False), stats)

    stats = lax.cond(frozen_ok, run_frozen, run_online, stats)
    _finish([acc_ref[_value_rows(h), :] for h in range(N_HEADS)], [1.0 / st[1] for st in stats], eye_ref, o_ref)


def _stick_attn_kernel(qT_ref, k_ref, vT_ref, eye_ref, later_ref, o_ref, *, past, tq, tk):
    assert tk % tq == 0 and past % tq == 0
    qi = pl.program_id(1)
    ltk = int(math.log2(tk))
    q_lo = past + qi * tq
    own = q_lo >> ltk
    qpos = q_lo + lax.broadcasted_iota(jnp.int32, (1, tq), 1)

    def scores(kb):
        off = pl.multiple_of(kb * tk, tk)
        return [_dot(k_ref[0, pl.ds(off, tk), _head_rows(h)], qT_ref[0, _head_rows(h), :])
                for h in range(N_HEADS)]

    def block_terms(kb, masked, zs=None):
        zs = scores(kb) if zs is None else zs
        kvs = [(None, vT_ref[0, kb, _value_rows(h), :]) for h in range(N_HEADS)]
        if masked:
            kpos = kb * tk + lax.broadcasted_iota(jnp.int32, (tk, 1), 0)
            vis = kpos < qpos
        out = []
        for h in range(N_HEADS):
            z = zs[h]
            drop = jnp.where(z > SOFTPLUS_LINEAR_FROM, z, jnp.log2(1.0 + jnp.exp2(z)))
            if masked:
                drop = jnp.where(vis, drop, 0.0)
            later = _dot(later_ref[...], drop.astype(BF16))
            logw = z - drop - later
            if masked:
                logw = jnp.where(vis, logw, NEG_INF)
            out.append((logw, jnp.sum(drop, axis=0, keepdims=True), kvs[h][1]))
        return out

    def accumulate(terms, carry, live=None):
        out = []
        for h in range(N_HEADS):
            logw, total, v_t = terms[h]
            decay, acc = carry[h]
            a = jnp.exp2(logw - decay)
            if live is not None:
                a = jnp.where(live, a, 0.0)
                total = jnp.where(live, total, 0.0)
            out.append((decay + total, acc + _dot(v_t, a.astype(BF16))))
        return tuple(out)

    def min_decay(carry):
        d = carry[0][0]
        for h in range(1, N_HEADS):
            d = jnp.minimum(d, carry[h][0])
        return jnp.min(d)

    init = (jnp.zeros((1, tq), F32), jnp.zeros((HEAD_DIM, tq), F32))
    prev = jnp.maximum(own - 1, 0)
    z_own, z_prev = scores(own), scores(prev)
    own_terms = block_terms(own, True, z_own)
    prev_terms = block_terms(prev, False, z_prev)
    carry = accumulate(own_terms, (init,) * N_HEADS)
    carry = accumulate(prev_terms, carry, live=own >= 1)

    def cond(state):
        kb, dmin, _ = state
        return jnp.logical_and(kb >= 0, dmin < SB_DECAY_LIMIT)

    def body(state):
        kb, _, carry = state
        carry = accumulate(block_terms(kb, False), carry)
        return kb - 1, min_decay(carry), carry

    _, _, carry = lax.while_loop(cond, body, (own - 2, min_decay(carry), carry))
    _finish([c[1] for c in carry], [1.0] * N_HEADS, eye_ref, o_ref)


def _attention(mode, q_t, k, v_t, cst, *, past, s_valid, tq, tk, skip_stats=None):
    b, _, t = q_t.shape
    s = k.shape[1]
    grid = (b, t // tq)
    kv_mode = pl.Buffered(1) if 2 * s * (QW + VW) * 2 > VMEM_LIMIT_BYTES // 2 else None
    in_specs = [pl.BlockSpec((1, QW, tq), lambda i, j: (i, 0, j)),
                pl.BlockSpec((1, s, QW), lambda i, j: (i, 0, 0), pipeline_mode=kv_mode),
                pl.BlockSpec((1, s // tk, VW, tk), lambda i, j: (i, 0, 0, 0), pipeline_mode=kv_mode),
                _const_spec((tq, tq))]
    args = [q_t, k, v_t, jnp.eye(tq, dtype=BF16)]
    scratch = []
    if mode == 'sb':
        kern = functools.partial(_stick_attn_kernel, past=past, tq=tq, tk=tk)
        in_specs.append(_const_spec((tk, tk)))
        args.append(cst['later'][tk])
    elif tq == 2 * tk and past % tq == 0:
        kern = functools.partial(_softmax_attn_pipelined_kernel, mode=mode, past=past, tq=tq, tk=tk, s_valid=s_valid)
        scratch = [pltpu.VMEM((N_HEADS, tk, tq), F32), pltpu.VMEM((N_HEADS, tk, tq), F32), pltpu.VMEM((VW, tq), F32)]
        stat_spec = pl.BlockSpec((1, 1, LANE), lambda i, j: (i, 0, 0))
        if mode == 'fox':
            cend, kn = skip_stats
            in_specs += [pl.BlockSpec((1, s // tk, LANE), lambda i, j: (i, 0, 0)), stat_spec]
            args += [cend.reshape(b, s // tk, LANE), kn]
        else:
            in_specs.append(stat_spec)
            args.append(skip_stats[0])
    else:
        kern = functools.partial(_softmax_attn_kernel, mode=mode, past=past, tq=tq, tk=tk, s_valid=s_valid)
    width = (N_HEADS // 2) * HP
    return pl.pallas_call(
        kern, grid=grid, in_specs=in_specs,
        out_specs=pl.BlockSpec((1, tq, width), lambda i, j: (i, j, 0)),
        out_shape=jax.ShapeDtypeStruct((b, t, width), BF16), scratch_shapes=scratch,
        compiler_params=_params("parallel", "parallel"), name="attn_" + mode,
    )(*args)


def _lru_kernel(lx_ref, lg_ref, cb_ref, h0_ref, cw_ref, cbias_ref, wr_ref, br_ref, wi_ref, bi_ref, lam_ref,
                y_ref, nb_ref, hl_ref, xcat_ref, h_ref, *, past, tc):
    ci = pl.program_id(1)
    keep = LRU_CONV - 1

    @pl.when(ci == 0)
    def _():
        xcat_ref[0:8, :] = jnp.zeros((8, LRU_WIDTH), F32)
        xcat_ref[8 - keep:8, :] = cb_ref[0]
        h_ref[...] = h0_ref[0]

    x = lx_ref[0]
    xcat_ref[8:8 + tc, :] = x
    xc = cbias_ref[...] + x * cw_ref[keep:keep + 1, :]
    for tap in range(keep):
        shift = keep - tap
        xc = xc + xcat_ref[8 - shift:8 - shift + tc, :] * cw_ref[tap:tap + 1, :]
    nb_ref[0] = xcat_ref[8 + tc - keep:8 + tc, :]
    xcat_ref[0:8, :] = x[tc - 8:tc, :]

    xcb = xc.astype(BF16)
    r = _sigmoid(_dot(xcb, wr_ref[...]) + br_ref[...])
    gate_in = _sigmoid(_dot(xcb, wi_ref[...]) + bi_ref[...])
    log_a = (-LRU_C) * r * _softplus(-lam_ref[...])
    row = lax.broadcasted_iota(jnp.int32, (tc, LRU_WIDTH), 0)
    reset = (past + ci * tc + row) == 0
    a = jnp.where(reset, 0.0, jnp.exp(log_a))
    y2 = 2.0 * log_a
    series = -y2 * (1.0 + y2 * (0.5 + y2 * (1.0 / 6.0 + y2 * (1.0 / 24.0 + y2 * (1.0 / 120.0)))))
    one_minus = jnp.where(y2 > -0.05, series, 1.0 - jnp.exp(y2))
    mult = jnp.where(reset, 1.0, jnp.sqrt(one_minus))
    u = mult * gate_in * xc

    d = 1
    while d < tc:
        if d < 8:
            ok = row >= d
            u = u + jnp.where(ok, a * pltpu.roll(u, d, 0), 0.0)
            a = jnp.where(ok, a * pltpu.roll(a, d, 0), a)
        else:
            u = jnp.concatenate([u[:d], u[d:] + a[d:] * u[:-d]], axis=0)
            a = jnp.concatenate([a[:d], a[d:] * a[:-d]], axis=0)
        d *= 2
    hs = a * h_ref[...] + u
    h_last = hs[tc - 1:tc, :]
    h_ref[...] = h_last
    hl_ref[0] = h_last

    g = lg_ref[0]
    gelu = 0.5 * g * (1.0 + jnp.tanh(0.7978845608028654 * (g + 0.044715 * g * g * g)))
    y_ref[0] = (hs * gelu).astype(BF16)


def _lru(lx, lg, conv_buf, h0, lw, past):
    b, t, w = lx.shape
    tc = _pick(t, (256, 128, 64, 32))
    grid = (b, t // tc)
    row = pl.BlockSpec((1, tc, w), lambda i, j: (i, j, 0))
    keep = LRU_CONV - 1
    in_specs = [row, row, pl.BlockSpec((1, keep, w), lambda i, j: (i, 0, 0)), pl.BlockSpec((1, 1, w), lambda i, j: (i, 0, 0)),
                _const_spec((LRU_CONV, w)), _const_spec((1, w)), _const_spec((w, w)), _const_spec((1, w)),
                _const_spec((w, w)), _const_spec((1, w)), _const_spec((1, w))]
    sds = jax.ShapeDtypeStruct
    return pl.pallas_call(
        functools.partial(_lru_kernel, past=past, tc=tc), grid=grid, in_specs=in_specs,
        out_specs=[row, pl.BlockSpec((1, keep, w), lambda i, j: (i, 0, 0)), pl.BlockSpec((1, 1, w), lambda i, j: (i, 0, 0))],
        out_shape=[sds((b, t, w), BF16), sds((b, keep, w), F32), sds((b, 1, w), F32)],
        scratch_shapes=[pltpu.VMEM((tc + 8, w), F32), pltpu.VMEM((1, w), F32)],
        compiler_params=_params("parallel", "arbitrary"), name="rg_lru",
    )(lx, lg, conv_buf, h0, lw['lru_conv_w'], lw['lru_conv_b'], lw['lru_wr'], lw['lru_br'], lw['lru_wi'],
      lw['lru_bi'], lw['lru_lam'])


def _row_chains(tm):
    n = 2 if tm % 512 == 0 else 1
    return [slice(i * (tm // n), (i + 1) * (tm // n)) for i in range(n)]


def _merge_kernel(x_ref, oa_ref, ob_ref, oc_ref, od_ref, g1_ref, g2_ref, wg_ref, wb_ref, wo_ref, out_ref):
    d = x_ref.shape[2]
    for rows in _row_chains(x_ref.shape[1]):
        x = x_ref[0, rows, :]
        hb = _rms(x, g1_ref[...]).astype(BF16)
        merged = None
        for n, o_ref in enumerate((oa_ref, ob_ref, oc_ref, od_ref)):
            gate = _sigmoid(_dot(hb, wg_ref[:, n * d:(n + 1) * d]))
            term = gate * _dot(o_ref[0, rows, :], wb_ref[n])
            merged = term if merged is None else merged + term
        y = _dot(merged.astype(BF16), wo_ref[...])
        out_ref[0, rows, :] = x + _rms(y, g2_ref[...])


def _merge(x, o_a, o_b, o_c, o_d, lw):
    b, t, d = x.shape
    tm = _pick(t, (512, 256, 128))
    grid = (b, t // tm)
    row = lambda w: pl.BlockSpec((1, tm, w), lambda i, j: (i, j, 0))
    bw = BRANCH_WIDTH
    single = pl.Buffered(1)
    in_specs = [row(d), row(bw), row(bw), row(bw), row(bw), _const_spec((1, d)), _const_spec((1, d)),
                pl.BlockSpec((d, N_BRANCH * d), lambda i, j: (0, 0), pipeline_mode=single),
                pl.BlockSpec((N_BRANCH, BRANCH_WIDTH, d), lambda i, j: (0, 0, 0), pipeline_mode=single),
                pl.BlockSpec((d, d), lambda i, j: (0, 0), pipeline_mode=single)]
    return pl.pallas_call(
        _merge_kernel, grid=grid, in_specs=in_specs, out_specs=row(d),
        out_shape=jax.ShapeDtypeStruct((b, t, d), F32),
        compiler_params=_params("parallel", "parallel"), name="merge",
    )(x, o_a, o_b, o_c, o_d, lw['ln_mix_pre'], lw['ln_mix_post'], lw['w_gate'], lw['w_branch'], lw['w_out'])


def _mem_kv_kernel(mem_ref, g_ref, wk_ref, wv_ref, k_ref, v_ref):
    mn = _rms(mem_ref[0], g_ref[...]).astype(BF16)
    k_ref[0] = _dot(mn, wk_ref[...])
    v_ref[0] = _dot(mn, wv_ref[...])


def _mem_kv(mem, lw):
    b, m, d = mem.shape
    w = lw['mem_wk'].shape[1]
    out = jax.ShapeDtypeStruct((b, m, w), F32)
    blk = pl.BlockSpec((1, m, w), lambda i: (i, 0, 0))
    return pl.pallas_call(
        _mem_kv_kernel, grid=(b,),
        in_specs=[pl.BlockSpec((1, m, d), lambda i: (i, 0, 0)), _const_spec((1, d)), _const_spec((d, w)), _const_spec((d, w))],
        out_specs=[blk, blk], out_shape=[out, out], compiler_params=_params("parallel"), name="mem_kv",
    )(mem, lw['mem_norm'], lw['mem_wk'], lw['mem_wv'])


def _mem_attn_kernel(x_ref, mk_ref, mv_ref, g1_ref, g2_ref, wq_ref, wo_ref, out_ref):
    x = x_ref[0]
    hb = _rms(x, g1_ref[...]).astype(BF16)
    q = (_dot(hb, wq_ref[...]) * (MEM_HEAD_DIM ** -0.5)).astype(BF16)
    heads = []
    for h in range(N_HEADS):
        sl = slice(h * MEM_HEAD_DIM, (h + 1) * MEM_HEAD_DIM)
        s = _dot_nt(q[:, sl], mk_ref[0, :, sl])
        p = jnp.exp(s - jnp.max(s, axis=-1, keepdims=True))
        inv = 1.0 / jnp.sum(p, axis=-1, keepdims=True)
        heads.append((_dot(p.astype(BF16), mv_ref[0, :, sl]) * inv).astype(BF16))
    y = _dot(jnp.concatenate(heads, axis=1), wo_ref[...])
    out_ref[0] = x + _rms(y, g2_ref[...])


def _mem_attn(x, mk, mv, lw):
    b, t, d = x.shape
    m, w = mk.shape[1], mk.shape[2]
    tm = _pick(t, (512, 256, 128))
    row = pl.BlockSpec((1, tm, d), lambda i, j: (i, j, 0))
    kv = pl.BlockSpec((1, m, w), lambda i, j: (i, 0, 0))
    return pl.pallas_call(
        _mem_attn_kernel, grid=(b, t // tm),
        in_specs=[row, kv, kv, _const_spec((1, d)), _const_spec((1, d)), _const_spec((d, w)), _const_spec((w, d))],
        out_specs=row, out_shape=jax.ShapeDtypeStruct((b, t, d), F32),
        compiler_params=_params("parallel", "parallel"), name="mem_attn",
    )(x, mk, mv, lw['ln_mem_pre'], lw['ln_mem_post'], lw['mem_wq'], lw['mem_wo'])


def _ffn_kernel(x_ref, g1_ref, g2_ref, wg_ref, wu_ref, wd_ref, out_ref):
    for rows in _row_chains(x_ref.shape[1]):
        x = x_ref[0, rows, :]
        hb = _rms(x, g1_ref[...]).astype(BF16)
        gate = _dot(hb, wg_ref[...])
        act = (gate * _sigmoid(gate) * _dot(hb, wu_ref[...])).astype(BF16)
        y = _dot(act, wd_ref[...])
        out_ref[0, rows, :] = x + _rms(y, g2_ref[...])


def _ffn(x, lw):
    b, t, d = x.shape
    f = lw['ffn_wg'].shape[1]
    tm = _pick(t, (512, 256, 128))
    row = pl.BlockSpec((1, tm, d), lambda i, j: (i, j, 0))
    single = pl.Buffered(1)
    return pl.pallas_call(
        _ffn_kernel, grid=(b, t // tm),
        in_specs=[row, _const_spec((1, d)), _const_spec((1, d)),
                  pl.BlockSpec((d, f), lambda i, j: (0, 0), pipeline_mode=single),
                  pl.BlockSpec((d, f), lambda i, j: (0, 0), pipeline_mode=single),
                  pl.BlockSpec((f, d), lambda i, j: (0, 0), pipeline_mode=single)],
        out_specs=row, out_shape=jax.ShapeDtypeStruct((b, t, d), F32),
        compiler_params=_params("parallel", "parallel"), name="ffn",
    )(x, lw['ln_ffn_pre'], lw['ln_ffn_post'], lw['ffn_wg'], lw['ffn_wu'], lw['ffn_wd'])


def _constants():
    fk, fq, ones_k, ones_q = _fox_feature_maps()
    later = {}
    for tk in (256,):
        idx = np.arange(tk)
        later[tk] = jnp.asarray((idx[None, :] > idx[:, None]).astype(np.float32), BF16)
    return dict(
        ek=jnp.asarray(_place_qk(), BF16), evT=jnp.eye(VW, dtype=BF16),
        fkmap=jnp.asarray(fk, BF16), fqmapT=jnp.asarray(fq.T, BF16),
        ones_k=jnp.asarray(ones_k, F32), ones_qT=jnp.asarray(ones_q.T, F32),
        ekpe=jnp.asarray(_place_kpe(), BF16), later=later,
        seg=jnp.asarray(np.repeat(np.eye(N_HEADS, LANE, dtype=np.float32), HEAD_DIM, axis=0), BF16),
        seg_slot=jnp.asarray(np.repeat(np.eye(N_HEADS, LANE, dtype=np.float32), HP, axis=0), BF16))


def _pad_cols(w, n):
    return jnp.pad(w, ((0, 0), (0, n - w.shape[1])))


_MIX_SIZES = (256, 256, 256, N_HEADS, LRU_WIDTH, LRU_WIDTH, MLA_Q_RANK, MLA_KV_RANK, MLA_ROPE, 256, 256, 256)


def _w_in_plan(d_model):
    offs = np.concatenate([[0], np.cumsum(_MIX_SIZES)])
    fq, fk, fv, ff, lx, lg, cq, ckv, kpe, sq, sk, sv = [np.arange(offs[i], offs[i + 1]) for i in range(len(_MIX_SIZES))]
    half = MLA_ROPE // 2

    def padded(cols):
        return np.concatenate([cols, np.full(LANE - len(cols), -1)])

    mix = [fq, fk, fv, lx, lg, cq, ckv, padded(kpe), padded(np.concatenate([kpe[half:], kpe[:half]])), padded(ff),
           sq, sk, sv]
    pieces, col = [], 0
    for src in mix:
        pieces.append((0, col, src))
        col += len(src)
    assert col == _W_MIX_COLS
    gate0 = offs[-1]
    for i in range(N_BRANCH * d_model // 256):
        pieces.append((1, i * 256, gate0 + i * 256 + np.arange(256)))
    return pieces


def _transpose_kernel(xT_ref, eye_ref, x_ref):
    x_ref[...] = _dot_nt(eye_ref[...], xT_ref[...]).astype(x_ref.dtype)


def _transpose_bf16(x_t):
    r, c = x_t.shape
    rb = _pick(r, (256, 128))
    return pl.pallas_call(
        _transpose_kernel, grid=(r // rb,),
        in_specs=[pl.BlockSpec((rb, c), lambda i: (i, 0)), _const_spec((c, c))],
        out_specs=pl.BlockSpec((c, rb), lambda i: (0, i)), out_shape=jax.ShapeDtypeStruct((c, r), BF16),
        compiler_params=_params("parallel"), name="transpose_w",
    )(x_t, jnp.eye(c, dtype=BF16))


def _repack_w_in(w_in_t):
    d_model = w_in_t.shape[1]
    rows = [[], []]
    for dst, _, src in _w_in_plan(d_model):
        valid = src[src >= 0]
        runs = np.split(valid, np.nonzero(np.diff(valid) != 1)[0] + 1)
        piece = jnp.concatenate([w_in_t[int(r[0]):int(r[-1]) + 1] for r in runs], axis=0)
        rows[dst].append(jnp.pad(piece, ((0, len(src) - len(valid)), (0, 0))))
    return tuple(_transpose_bf16(jnp.concatenate(r, axis=0).astype(BF16)) for r in rows)


def _layer_weights(l, p):
    w_mix, w_gate = _repack_w_in(jnp.transpose(p['w_in'], (2, 0, 1))[:, l, :])
    half = MLA_ROPE // 2
    qk = MLA_NOPE + MLA_ROPE
    wuq = p['mla_w_uq'][l].reshape(MLA_Q_RANK, N_HEADS, qk)
    rope = wuq[:, :, MLA_NOPE:]
    rope_sw = jnp.concatenate([rope[:, :, half:], rope[:, :, :half]], axis=2)
    wuq_p = jnp.pad(wuq, ((0, 0), (0, 0), (0, HP - qk))).reshape(MLA_Q_RANK, QW)
    wuqsw_p = jnp.pad(rope_sw, ((0, 0), (0, 0), (MLA_NOPE, HP - qk))).reshape(MLA_Q_RANK, QW)
    wuk = p['mla_w_uk'][l].reshape(MLA_KV_RANK, N_HEADS, MLA_NOPE)
    wuk_p = jnp.pad(wuk, ((0, 0), (0, 0), (0, HP - MLA_NOPE))).reshape(MLA_KV_RANK, QW)

    def block_diag(w):
        return jax.scipy.linalg.block_diag(*[w[i] for i in range(w.shape[0])]).astype(BF16)

    row = lambda v: v[l].reshape(1, -1).astype(F32)
    return dict(
        w_mix=w_mix, w_gate=w_gate,
        ln_mix_pre=row(p['ln_mix_pre']), ln_mix_post=row(p['ln_mix_post']),
        fox_bf=_pad_cols(row(p['fox_bf']), LANE),
        mla_q_norm=row(p['mla_q_norm']), mla_kv_norm=row(p['mla_kv_norm']),
        wuqT=wuq_p.T.astype(BF16), wuqswT=wuqsw_p.T.astype(BF16), wuk=wuk_p.astype(BF16), wuvT=p['mla_w_uv'][l].T.astype(BF16),
        eqT=jnp.asarray(_place_qk().T, BF16),
        lru_conv_w=p['lru_conv_w'][l].astype(F32), lru_conv_b=row(p['lru_conv_b']),
        lru_wr=block_diag(p['lru_wr'][l]), lru_br=row(p['lru_br']),
        lru_wi=block_diag(p['lru_wi'][l]), lru_bi=row(p['lru_bi']), lru_lam=row(p['lru_lam']),
        w_branch=p['w_branch'][l].astype(BF16), w_out=p['w_out'][l].astype(BF16),
        ln_mem_pre=row(p['ln_mem_pre']), ln_mem_post=row(p['ln_mem_post']), mem_norm=row(p['mem_norm']),
        mem_wq=p['mem_wq'][l].astype(BF16), mem_wk=p['mem_wk'][l].astype(BF16),
        mem_wv=p['mem_wv'][l].astype(BF16), mem_wo=p['mem_wo'][l].astype(BF16),
        ln_ffn_pre=row(p['ln_ffn_pre']), ln_ffn_post=row(p['ln_ffn_post']),
        ffn_wg=p['ffn_wg'][l].astype(BF16), ffn_wu=p['ffn_wu'][l].astype(BF16), ffn_wd=p['ffn_wd'][l].astype(BF16))


def _rope_tables(past, t):
    half = MLA_ROPE // 2
    inv = jnp.power(ROPE_BASE, -jnp.arange(half, dtype=F32) / half)
    ang = (past + jnp.arange(t, dtype=jnp.int32)).astype(F32)[:, None] * inv
    cos, sin = jnp.cos(ang), jnp.sin(ang)
    c32 = jnp.concatenate([cos, cos], axis=1)
    s32 = jnp.concatenate([-sin, sin], axis=1)
    scale = (MLA_NOPE + MLA_ROPE) ** -0.5 * LOG2E
    slot_c = jnp.concatenate([jnp.ones((t, MLA_NOPE), F32), c32, jnp.zeros((t, HP - MLA_NOPE - MLA_ROPE), F32)], axis=1)
    slot_s = jnp.concatenate([jnp.zeros((t, MLA_NOPE), F32), s32, jnp.zeros((t, HP - MLA_NOPE - MLA_ROPE), F32)], axis=1)
    return dict(ck=_pad_cols(c32, LANE), sk=_pad_cols(s32, LANE),
                cqT=(jnp.tile(slot_c, (1, N_HEADS)) * scale).T, sqT=(jnp.tile(slot_s, (1, N_HEADS)) * scale).T)


def _trunk_layer(x, past_state, mem_k, mem_v, lw, cst, layer, depth, earlier):
    b, t, _ = x.shape
    past = 0 if past_state is None else past_state['fox_k'].shape[1]
    tk = 256
    fq, fk, fv, lx, lg, ckv_n, kpe_slab, logf_slab, mla_qT, sb_qT, sk, sv = _in_proj(
        x, lw, _rope_tables(past, t), layer, depth, earlier)
    stacked = (fk, fv, ckv_n, kpe_slab, logf_slab, sk, sv)

    if past_state is None:
        s_valid = t
        keys = (fq, fk, fv, logf_slab, ckv_n, kpe_slab, sk, sv)
        key_layer = layer
        conv_buf = jnp.zeros((b, LRU_CONV - 1, LRU_WIDTH), F32)
        h0 = jnp.zeros((b, 1, LRU_WIDTH), F32)
    else:
        s_valid = past + t
        s_pad = -(-s_valid // tk) * tk

        def cat(old, new):
            old = old.reshape(b, past, -1).astype(new.dtype)
            old = jnp.pad(old, ((0, 0), (0, 0), (0, new.shape[2] - old.shape[2])))
            return jnp.pad(jnp.concatenate([old, new], axis=1), ((0, 0), (0, s_pad - s_valid), (0, 0)))

        def cat_state(old, new):
            return cat(old, new[layer])[None]

        keys = (cat(jnp.zeros((b, past, 256), BF16), fq), cat_state(past_state['fox_k'], fk),
                cat_state(past_state['fox_v'], fv), cat_state(past_state['fox_logf'], logf_slab),
                cat_state(past_state['mla_ckv'], ckv_n), cat_state(past_state['mla_kpe'], kpe_slab),
                cat_state(past_state['sb_k'], sk), cat_state(past_state['sb_v'], sv))
        key_layer = 0
        conv_buf = past_state['lru_conv'].astype(F32)
        h0 = past_state['lru_h'].reshape(b, 1, LRU_WIDTH).astype(F32)

    (fox_qT, fox_k, fox_vT, mla_k, mla_vT, sb_k, sb_vT,
     *fox_skip_stats, mla_key_norm) = _kv_prep(*keys, key_layer, lw, cst, tk)
    if past_state is not None:
        fox_qT = fox_qT[:, :, past:past + t]

    att = functools.partial(_attention, cst=cst, past=past, s_valid=s_valid, tk=tk)
    tq_softmax = _pick(t, (2 * tk, tk, 128))
    o_a = att('fox', fox_qT, fox_k, fox_vT, tq=tq_softmax, skip_stats=fox_skip_stats)
    o_c = att('mla', mla_qT, mla_k, mla_vT, tq=tq_softmax, skip_stats=[mla_key_norm])
    o_d = att('sb', sb_qT, sb_k, sb_vT, tq=_pick(t, (tk, 128)))
    o_b, lru_conv, lru_h = _lru(lx, lg, conv_buf, h0, lw, past)

    x = _merge(x, o_a, o_b, o_c, o_d, lw)
    x = _mem_attn(x, mem_k, mem_v, lw)
    x = _ffn(x, lw)

    return x, stacked, dict(lru_h=lru_h.reshape(b, LRU_WIDTH), lru_conv=lru_conv)


def _state_outputs(stacked, small):
    fk, fv, ckv_n, kpe_slab, logf_slab, sk, sv = stacked
    depth, b, t, _ = fk.shape
    heads = lambda a: a.reshape(depth, b, t, N_HEADS, HEAD_DIM)
    stk = lambda name: jnp.stack([s[name] for s in small])
    return (heads(fk), heads(fv), logf_slab[..., :N_HEADS], stk('lru_h'), stk('lru_conv'),
            ckv_n, kpe_slab[..., :MLA_ROPE], heads(sk), heads(sv))


def kernel(x_prompt, x_sample, cache_fox_k, cache_fox_v, cache_fox_logf, state_lru_h, state_lru_conv, cache_mla_ckv, cache_mla_kpe, cache_sb_k, cache_sb_v, cache_mem_k, cache_mem_v, mem_prompt, ln_mix_pre, ln_mix_post, w_in, fox_bf, lru_conv_w, lru_conv_b, lru_wr, lru_br, lru_wi, lru_bi, lru_lam, mla_q_norm, mla_w_uq, mla_kv_norm, mla_w_uk, mla_w_uv, w_branch, w_out, ln_mem_pre, ln_mem_post, mem_norm, mem_wq, mem_wk, mem_wv, mem_wo, ln_ffn_pre, ln_ffn_post, ffn_wg, ffn_wu, ffn_wd):
    params = dict(ln_mix_pre=ln_mix_pre, ln_mix_post=ln_mix_post, w_in=w_in, fox_bf=fox_bf, lru_conv_w=lru_conv_w,
                  lru_conv_b=lru_conv_b, lru_wr=lru_wr, lru_br=lru_br, lru_wi=lru_wi, lru_bi=lru_bi, lru_lam=lru_lam,
                  mla_q_norm=mla_q_norm, mla_w_uq=mla_w_uq, mla_kv_norm=mla_kv_norm, mla_w_uk=mla_w_uk,
                  mla_w_uv=mla_w_uv, w_branch=w_branch, w_out=w_out, ln_mem_pre=ln_mem_pre, ln_mem_post=ln_mem_post,
                  mem_norm=mem_norm, mem_wq=mem_wq, mem_wk=mem_wk, mem_wv=mem_wv, mem_wo=mem_wo,
                  ln_ffn_pre=ln_ffn_pre, ln_ffn_post=ln_ffn_post, ffn_wg=ffn_wg, ffn_wu=ffn_wu, ffn_wd=ffn_wd)
    depth = w_in.shape[0]
    cst = _constants()
    weights = [_layer_weights(l, params) for l in range(depth)]
    bp, mem_len = mem_prompt.shape[0], mem_prompt.shape[1]

    y_prompt, p_rows, p_small, p_mem = x_prompt, None, [], []
    for l in range(depth):
        mk, mv = _mem_kv(mem_prompt, weights[l])
        y_prompt, p_rows, small = _trunk_layer(y_prompt, None, mk.astype(BF16), mv.astype(BF16), weights[l], cst,
                                               l, depth, p_rows)
        p_small.append(small)
        p_mem.append((mk.reshape(bp, mem_len, N_HEADS, MEM_HEAD_DIM), mv.reshape(bp, mem_len, N_HEADS, MEM_HEAD_DIM)))

    y_sample, s_rows, s_small = x_sample, None, []
    bs = x_sample.shape[0]
    for l in range(depth):
        past = dict(fox_k=cache_fox_k[l], fox_v=cache_fox_v[l], fox_logf=cache_fox_logf[l], lru_h=state_lru_h[l],
                    lru_conv=state_lru_conv[l], mla_ckv=cache_mla_ckv[l], mla_kpe=cache_mla_kpe[l],
                    sb_k=cache_sb_k[l], sb_v=cache_sb_v[l])
        mk = cache_mem_k[l].reshape(bs, mem_len, -1).astype(BF16)
        mv = cache_mem_v[l].reshape(bs, mem_len, -1).astype(BF16)
        y_sample, s_rows, small = _trunk_layer(y_sample, past, mk, mv, weights[l], cst, l, depth, s_rows)
        s_small.append(small)

    mem_out = (jnp.stack([m[0] for m in p_mem]), jnp.stack([m[1] for m in p_mem]))
    return (y_prompt, y_sample) + _state_outputs(p_rows, p_small) + mem_out + _state_outputs(s_rows, s_small)
```
